```python
import math
import jax, jax.numpy as jnp
from jax import lax
import numpy as np

D_MODEL = 1024
BATCH = 16
SEQ = 256
DEPTH = 4
DEC_BATCH = 8
DEC_SEQ = 2048
PAST_LEN = 256

GRID_W = 64
N_EVEN = (DEPTH + 1) // 2
N_ODD = DEPTH // 2
D_MIX = D_MODEL
CONV_WIDTH = D_MIX // 2
CONV_K = 3
ATTN_HEADS = 8
ATTN_KV_HEADS = 2
HEAD_DIM = 64
S5_WIDTH = D_MIX // 2
S5_GROUP = 16
S5_GROUPS = S5_WIDTH // S5_GROUP
S5_STATE = 64
MLA_HEADS = 8
MLA_Q_RANK = 256
MLA_KV_RANK = 128
MLA_NOPE_DIM = 64
MLA_ROPE_DIM = 32
MLA_V_DIM = 64
D_FF = 2816
N_EXPERTS = 8
TOP_K = 2
EXPERT_FF = 3584

ROPE_THETA = 10000.0
Q_BLOCK = 128
LN_EPS = 1e-6
RMS_EPS = 1e-6
DEEPNORM_ALPHA = (2 * DEPTH) ** 0.25
DEEPNORM_BETA = (8 * DEPTH) ** -0.25
EVEN_IN = 3 * CONV_WIDTH + (ATTN_HEADS + 2 * ATTN_KV_HEADS) * HEAD_DIM
ODD_IN = S5_WIDTH + MLA_Q_RANK + MLA_KV_RANK + MLA_ROPE_DIM

kernel_name = 'hybrid_diffusion_prefix_trunk_step'


def layer_norm(x, g, b):
    xf = x.astype(jnp.float32)
    mu = jnp.mean(xf, axis=-1, keepdims=True)
    var = jnp.mean(jnp.square(xf - mu), axis=-1, keepdims=True)
    return ((xf - mu) * lax.rsqrt(var + LN_EPS)).astype(x.dtype) * g + b


def rms_norm(x, g):
    xf = x.astype(jnp.float32)
    return (xf * lax.rsqrt(jnp.mean(xf * xf, axis=-1, keepdims=True) + RMS_EPS)).astype(x.dtype) * g


def axial_rope_tables(n_tokens, rot_dim):
    rows = n_tokens // GRID_W
    row_pos = jnp.repeat(jnp.arange(rows, dtype=jnp.float32), GRID_W)
    col_pos = jnp.tile(jnp.arange(GRID_W, dtype=jnp.float32), rows)
    half = rot_dim // 2
    inv_freq = ROPE_THETA ** (-jnp.arange(0, half, 2, dtype=jnp.float32) / half)
    ang_r = row_pos[:, None] * inv_freq
    ang_c = col_pos[:, None] * inv_freq
    ang = jnp.concatenate([ang_r, ang_r, ang_c, ang_c], axis=-1)
    return jnp.cos(ang), jnp.sin(ang)


def rotate_axial(x, cos, sin):
    half = x.shape[-1] // 2
    qtr = half // 2
    def rot(v):
        return jnp.concatenate([-v[..., qtr:], v[..., :qtr]], axis=-1)
    rotated = jnp.concatenate([rot(x[..., :half]), rot(x[..., half:])], axis=-1)
    return x * cos[:, None, :].astype(x.dtype) + rotated * sin[:, None, :].astype(x.dtype)


def blocked_attention(q, k, v, scale):
    b, s, h, dk = q.shape
    kvh = k.shape[2]
    g = h // kvh
    nblk = s // Q_BLOCK
    qb = q.reshape(b, nblk, Q_BLOCK, kvh, g, dk).transpose(1, 0, 2, 3, 4, 5)
    def one_block(q_blk):
        sc = jnp.einsum('bqkgd,btkd->bkgqt', q_blk, k).astype(jnp.float32) * scale
        p = jax.nn.softmax(sc, axis=-1).astype(v.dtype)
        return jnp.einsum('bkgqt,btkd->bqkgd', p, v)
    out = lax.map(one_block, qb)
    return out.transpose(1, 0, 2, 3, 4, 5).reshape(b, s, h, v.shape[-1])


def centred_conv3(x, w):
    xp = jnp.pad(x, ((0, 0), (1, 1), (0, 0)))
    return w[0] * xp[:, :-2] + w[1] * xp[:, 1:-1] + w[2] * xp[:, 2:]


def even_mixer(h, P, i, ctx):
    b, s, _ = h.shape
    a_end = 3 * CONV_WIDTH
    q_end = a_end + ATTN_HEADS * HEAD_DIM
    k_end = q_end + ATTN_KV_HEADS * HEAD_DIM
    proj = h @ P['ev_w_in'][i]
    gate_b = proj[..., :CONV_WIDTH]
    gate_c = proj[..., CONV_WIDTH:2 * CONV_WIDTH]
    x_in = proj[..., 2 * CONV_WIDTH:a_end]
    y_conv = gate_b * centred_conv3(gate_c * x_in, P['ev_conv_w'][i])
    q = rms_norm(proj[..., a_end:q_end].reshape(b, s, ATTN_HEADS, HEAD_DIM), P['ev_q_gain'][i])
    k = rms_norm(proj[..., q_end:k_end].reshape(b, s, ATTN_KV_HEADS, HEAD_DIM), P['ev_k_gain'][i])
    v = proj[..., k_end:].reshape(b, s, ATTN_KV_HEADS, HEAD_DIM)
    if ctx is None:
        q_r, k_all, v_all = q, k, v
    else:
        cos, sin = axial_rope_tables(s, HEAD_DIM)
        q_r = rotate_axial(q, cos, sin)
        k_all = jnp.concatenate([ctx[0], rotate_axial(k, cos, sin)], axis=1)
        v_all = jnp.concatenate([ctx[1], v], axis=1)
    y_attn = blocked_attention(q_r, k_all, v_all, HEAD_DIM ** -0.5).reshape(b, s, ATTN_HEADS * HEAD_DIM)
    out = jnp.concatenate([y_conv, y_attn], axis=-1) @ P['ev_w_out'][i]
    return out, (k, v)


def s5_discretize(a_re, a_im, log_dt, b_re, b_im):
    dt = jnp.exp(log_dt)[..., None]
    mag = jnp.exp(a_re * dt)
    ab_re = mag * jnp.cos(a_im * dt)
    ab_im = mag * jnp.sin(a_im * dt)
    num_re = ab_re - 1.0
    num_im = ab_im
    den = a_re * a_re + a_im * a_im
    f_re = (num_re * a_re + num_im * a_im) / den
    f_im = (num_im * a_re - num_re * a_im) / den
    bb_re = f_re[..., None] * b_re - f_im[..., None] * b_im
    bb_im = f_re[..., None] * b_im + f_im[..., None] * b_re
    return ab_re, ab_im, bb_re, bb_im


def diag_complex_scan(ab_re, ab_im, bu_re, bu_im, h0):
    a_re = jnp.broadcast_to(ab_re, bu_re.shape)
    a_im = jnp.broadcast_to(ab_im, bu_im.shape)
    def combine(e1, e2):
        a1r, a1i, b1r, b1i = e1
        a2r, a2i, b2r, b2i = e2
        return (a1r * a2r - a1i * a2i, a1r * a2i + a1i * a2r,
                a2r * b1r - a2i * b1i + b2r, a2r * b1i + a2i * b1r + b2i)
    cr, ci, hr, hi = lax.associative_scan(combine, (a_re, a_im, bu_re, bu_im), axis=1)
    if h0 is None:
        return hr, hi
    h0r, h0i = h0[0][:, None], h0[1][:, None]
    return hr + cr * h0r - ci * h0i, hi + cr * h0i + ci * h0r


def s5_mixer(u, P, i, h0):
    b, s, _ = u.shape
    ug = u.reshape(b, s, S5_GROUPS, S5_GROUP)
    ab_re, ab_im, bb_re, bb_im = s5_discretize(P['s5_a_re'][i], P['s5_a_im'][i], P['s5_log_dt'][i],
                                               P['s5_b_re'][i], P['s5_b_im'][i])
    c_re, c_im = P['s5_c_re'][i], P['s5_c_im'][i]
    y = u * P['s5_d'][i]
    fins_re, fins_im = [], []
    for d in range(2):
        ud = ug if d == 0 else jnp.flip(ug, axis=1)
        bu_re = jnp.einsum('bsgp,gnp->bsgn', ud, bb_re[d])
        bu_im = jnp.einsum('bsgp,gnp->bsgn', ud, bb_im[d])
        h0d = None if h0 is None else (h0[0][:, d], h0[1][:, d])
        h_re, h_im = diag_complex_scan(ab_re[d], ab_im[d], bu_re, bu_im, h0d)
        fins_re.append(h_re[:, -1])
        fins_im.append(h_im[:, -1])
        yd = jnp.einsum('bsgn,gpn->bsgp', h_re, c_re[d]) - jnp.einsum('bsgn,gpn->bsgp', h_im, c_im[d])
        if d == 1:
            yd = jnp.flip(yd, axis=1)
        y = y + yd.reshape(b, s, S5_WIDTH)
    y = jax.nn.gelu(y)
    y = y * jax.nn.sigmoid(y @ P['s5_w_glu'][i] + P['s5_b_glu'][i])
    return y, (jnp.stack(fins_re, axis=1), jnp.stack(fins_im, axis=1))


def mla_mixer(q_c, kv_c, k_pe, P, i, ctx):
    b, s, _ = q_c.shape
    q = (rms_norm(q_c, P['mla_q_gain'][i]) @ P['mla_w_uq'][i]).reshape(b, s, MLA_HEADS, MLA_NOPE_DIM + MLA_ROPE_DIM)
    q_nope, q_pe = q[..., :MLA_NOPE_DIM], q[..., MLA_NOPE_DIM:]
    ckv = rms_norm(kv_c, P['mla_kv_gain'][i])
    if ctx is None:
        ckv_all, kpe_all = ckv, k_pe
    else:
        cos, sin = axial_rope_tables(s, MLA_ROPE_DIM)
        q_pe = rotate_axial(q_pe, cos, sin)
        kpe_lat = rotate_axial(k_pe[:, :, None, :], cos, sin)[:, :, 0]
        ckv_all = jnp.concatenate([ctx[0], ckv], axis=1)
        kpe_all = jnp.concatenate([ctx[1], kpe_lat], axis=1)
    t = ckv_all.shape[1]
    kv = (ckv_all @ P['mla_w_ukv'][i]).reshape(b, t, MLA_HEADS, MLA_NOPE_DIM + MLA_V_DIM)
    k_nope, v = kv[..., :MLA_NOPE_DIM], kv[..., MLA_NOPE_DIM:]
    k = jnp.concatenate([k_nope, jnp.broadcast_to(kpe_all[:, :, None, :], (b, t, MLA_HEADS, MLA_ROPE_DIM))], axis=-1)
    qf = jnp.concatenate([q_nope, q_pe], axis=-1)
    out = blocked_attention(qf, k, v, (MLA_NOPE_DIM + MLA_ROPE_DIM) ** -0.5)
    return out.reshape(b, s, MLA_HEADS * MLA_V_DIM), (ckv, k_pe)


def odd_mixer(h, P, i, ctx):
    proj = h @ P['od_w_in'][i]
    q_end = S5_WIDTH + MLA_Q_RANK
    kv_end = q_end + MLA_KV_RANK
    u = proj[..., :S5_WIDTH]
    y_s5, fins = s5_mixer(u, P, i, None if ctx is None else (ctx[2], ctx[3]))
    y_mla, (ckv, kpe) = mla_mixer(proj[..., S5_WIDTH:q_end], proj[..., q_end:kv_end], proj[..., kv_end:],
                                  P, i, None if ctx is None else (ctx[0], ctx[1]))
    out = jnp.concatenate([y_s5, y_mla], axis=-1) @ P['od_w_out'][i]
    return out, (ckv, kpe, fins[0], fins[1])


def swiglu(h, wg, wu, wd):
    return (jax.nn.silu(h @ wg) * (h @ wu)) @ wd


def moe_swiglu(h, router, wg, wu, wd):
    logits = (h @ router).astype(jnp.float32)
    top_val, top_idx = lax.top_k(logits, TOP_K)
    gates = jax.nn.softmax(top_val, axis=-1)
    dense_gate = jnp.sum(jax.nn.one_hot(top_idx, N_EXPERTS, dtype=jnp.float32) * gates[..., None], axis=-2).astype(h.dtype)
    out = jnp.zeros_like(h)
    for e in range(N_EXPERTS):
        out = out + dense_gate[..., e:e + 1] * swiglu(h, wg[e], wu[e], wd[e])
    return out


def modulation(cond, ada_w, ada_b):
    m = jax.nn.silu(cond) @ ada_w + ada_b
    return jnp.split(m[..., None, :], 6, axis=-1)


def trunk(x, cond, P, cache):
    out_k, out_v, out_ckv, out_kpe, out_sre, out_sim = [], [], [], [], [], []
    for l in range(DEPTH):
        i = l // 2
        sh1, sc1, g1, sh2, sc2, g2 = modulation(cond, P['ada_w'][l], P['ada_b'][l])
        h = x * (1.0 + sc1) + sh1
        if l % 2 == 0:
            ctx = None if cache is None else (cache['k'][:, i], cache['v'][:, i])
            y, (k, v) = even_mixer(h, P, i, ctx)
            out_k.append(k)
            out_v.append(v)
        else:
            ctx = None if cache is None else (cache['ckv'][:, i], cache['kpe'][:, i], cache['sre'][:, i], cache['sim'][:, i])
            y, (ckv, kpe, sre, sim) = odd_mixer(h, P, i, ctx)
            out_ckv.append(ckv)
            out_kpe.append(kpe)
            out_sre.append(sre)
            out_sim.append(sim)
        x = layer_norm(DEEPNORM_ALPHA * x + g1 * y, P['ln_g'][l, 0], P['ln_b'][l, 0])
        h = x * (1.0 + sc2) + sh2
        if l % 2 == 0:
            f = swiglu(h, P['ffn_w_gate'][i], P['ffn_w_up'][i], P['ffn_w_down'][i])
        else:
            f = moe_swiglu(h, P['moe_router'][i], P['moe_w_gate'][i], P['moe_w_up'][i], P['moe_w_down'][i])
        x = layer_norm(DEEPNORM_ALPHA * x + g2 * f, P['ln_g'][l, 1], P['ln_b'][l, 1])
    return x, (out_k, out_v, out_ckv, out_kpe, out_sre, out_sim)


def setup_inputs(seed: int = 0) -> dict:
    key = jax.random.key(seed)
    ks = iter(jax.random.split(key, 48))
    def nrm(shape, scale=1.0):
        return jax.random.normal(next(ks), shape, dtype=jnp.float32) * scale
    d = D_MODEL
    inp = {}
    inp['x_prompt'] = nrm((BATCH, SEQ, d))
    inp['x_sample'] = nrm((DEC_BATCH, DEC_SEQ, d))
    inp['cache_attn_k'] = nrm((DEC_BATCH, N_EVEN, PAST_LEN, ATTN_KV_HEADS, HEAD_DIM))
    inp['cache_attn_v'] = nrm((DEC_BATCH, N_EVEN, PAST_LEN, ATTN_KV_HEADS, HEAD_DIM))
    inp['cache_mla_ckv'] = nrm((DEC_BATCH, N_ODD, PAST_LEN, MLA_KV_RANK))
    inp['cache_mla_kpe'] = nrm((DEC_BATCH, N_ODD, PAST_LEN, MLA_ROPE_DIM))
    inp['state_ssm_re'] = nrm((DEC_BATCH, N_ODD, 2, S5_GROUPS, S5_STATE), 0.3)
    inp['state_ssm_im'] = nrm((DEC_BATCH, N_ODD, 2, S5_GROUPS, S5_STATE), 0.3)
    inp['c'] = nrm((DEC_BATCH, d))
    inp['c_ctx'] = nrm((d,))
    inp['ada_w'] = nrm((DEPTH, d, 6 * d), d ** -0.5)
    inp['ada_b'] = nrm((DEPTH, 6 * d), 0.02)
    inp['ln_g'] = 1.0 + nrm((DEPTH, 2, d), 0.05)
    inp['ln_b'] = nrm((DEPTH, 2, d), 0.02)
    inp['ev_w_in'] = nrm((N_EVEN, d, EVEN_IN), d ** -0.5)
    inp['ev_conv_w'] = nrm((N_EVEN, CONV_K, CONV_WIDTH), CONV_K ** -0.5)
    inp['ev_q_gain'] = 1.0 + nrm((N_EVEN, HEAD_DIM), 0.05)
    inp['ev_k_gain'] = 1.0 + nrm((N_EVEN, HEAD_DIM), 0.05)
    inp['ev_w_out'] = nrm((N_EVEN, D_MIX, d), D_MIX ** -0.5 * DEEPNORM_BETA)
    inp['ffn_w_gate'] = nrm((N_EVEN, d, D_FF), d ** -0.5)
    inp['ffn_w_up'] = nrm((N_EVEN, d, D_FF), d ** -0.5)
    inp['ffn_w_down'] = nrm((N_EVEN, D_FF, d), D_FF ** -0.5 * DEEPNORM_BETA)
    inp['od_w_in'] = nrm((N_ODD, d, ODD_IN), d ** -0.5)
    inp['s5_a_re'] = -0.5 + nrm((N_ODD, 2, S5_GROUPS, S5_STATE), 0.01)
    inp['s5_a_im'] = math.pi * jnp.arange(S5_STATE, dtype=jnp.float32) + nrm((N_ODD, 2, S5_GROUPS, S5_STATE), 0.01)
    inp['s5_log_dt'] = jax.random.uniform(next(ks), (N_ODD, 2, S5_GROUPS), dtype=jnp.float32,
                                          minval=math.log(1e-3), maxval=math.log(1e-1))
    inp['s5_b_re'] = nrm((N_ODD, 2, S5_GROUPS, S5_STATE, S5_GROUP), (2 * S5_GROUP) ** -0.5)
    inp['s5_b_im'] = nrm((N_ODD, 2, S5_GROUPS, S5_STATE, S5_GROUP), (2 * S5_GROUP) ** -0.5)
    inp['s5_c_re'] = nrm((N_ODD, 2, S5_GROUPS, S5_GROUP, S5_STATE), S5_STATE ** -0.5)
    inp['s5_c_im'] = nrm((N_ODD, 2, S5_GROUPS, S5_GROUP, S5_STATE), S5_STATE ** -0.5)
    inp['s5_d'] = nrm((N_ODD, S5_WIDTH))
    inp['s5_w_glu'] = nrm((N_ODD, S5_WIDTH, S5_WIDTH), S5_WIDTH ** -0.5)
    inp['s5_b_glu'] = nrm((N_ODD, S5_WIDTH), 0.02)
    inp['mla_q_gain'] = 1.0 + nrm((N_ODD, MLA_Q_RANK), 0.05)
    inp['mla_w_uq'] = nrm((N_ODD, MLA_Q_RANK, MLA_HEADS * (MLA_NOPE_DIM + MLA_ROPE_DIM)), MLA_Q_RANK ** -0.5)
    inp['mla_kv_gain'] = 1.0 + nrm((N_ODD, MLA_KV_RANK), 0.05)
    inp['mla_w_ukv'] = nrm((N_ODD, MLA_KV_RANK, MLA_HEADS * (MLA_NOPE_DIM + MLA_V_DIM)), MLA_KV_RANK ** -0.5)
    inp['od_w_out'] = nrm((N_ODD, D_MIX, d), D_MIX ** -0.5 * DEEPNORM_BETA)
    inp['moe_router'] = nrm((N_ODD, d, N_EXPERTS), d ** -0.5)
    inp['moe_w_gate'] = nrm((N_ODD, N_EXPERTS, d, EXPERT_FF), d ** -0.5)
    inp['moe_w_up'] = nrm((N_ODD, N_EXPERTS, d, EXPERT_FF), d ** -0.5)
    inp['moe_w_down'] = nrm((N_ODD, N_EXPERTS, EXPERT_FF, d), EXPERT_FF ** -0.5 * DEEPNORM_BETA)
    return inp


def reference(x_prompt, x_sample, cache_attn_k, cache_attn_v, cache_mla_ckv, cache_mla_kpe,
              state_ssm_re, state_ssm_im, c, c_ctx, ada_w, ada_b, ln_g, ln_b,
              ev_w_in, ev_conv_w, ev_q_gain, ev_k_gain, ev_w_out, ffn_w_gate, ffn_w_up, ffn_w_down,
              od_w_in, s5_a_re, s5_a_im, s5_log_dt, s5_b_re, s5_b_im, s5_c_re, s5_c_im, s5_d,
              s5_w_glu, s5_b_glu, mla_q_gain, mla_w_uq, mla_kv_gain, mla_w_ukv, od_w_out,
              moe_router, moe_w_gate, moe_w_up, moe_w_down):
    P = dict(ada_w=ada_w, ada_b=ada_b, ln_g=ln_g, ln_b=ln_b,
             ev_w_in=ev_w_in, ev_conv_w=ev_conv_w, ev_q_gain=ev_q_gain, ev_k_gain=ev_k_gain, ev_w_out=ev_w_out,
             ffn_w_gate=ffn_w_gate, ffn_w_up=ffn_w_up, ffn_w_down=ffn_w_down,
             od_w_in=od_w_in, s5_a_re=s5_a_re, s5_a_im=s5_a_im, s5_log_dt=s5_log_dt,
             s5_b_re=s5_b_re, s5_b_im=s5_b_im, s5_c_re=s5_c_re, s5_c_im=s5_c_im, s5_d=s5_d,
             s5_w_glu=s5_w_glu, s5_b_glu=s5_b_glu, mla_q_gain=mla_q_gain, mla_w_uq=mla_w_uq,
             mla_kv_gain=mla_kv_gain, mla_w_ukv=mla_w_ukv, od_w_out=od_w_out,
             moe_router=moe_router, moe_w_gate=moe_w_gate, moe_w_up=moe_w_up, moe_w_down=moe_w_down)
    y_prompt, (ks, vs, ckvs, kpes, sres, sims) = trunk(x_prompt, c_ctx, P, None)
    new_attn_k = jnp.stack(ks, axis=1)
    new_attn_v = jnp.stack(vs, axis=1)
    new_mla_ckv = jnp.stack(ckvs, axis=1)
    new_mla_kpe = jnp.stack(kpes, axis=1)
    new_ssm_re = jnp.stack(sres, axis=1)
    new_ssm_im = jnp.stack(sims, axis=1)
    cache = dict(k=cache_attn_k, v=cache_attn_v, ckv=cache_mla_ckv, kpe=cache_mla_kpe,
                 sre=state_ssm_re, sim=state_ssm_im)
    y_sample, _ = trunk(x_sample, c, P, cache)
    return (y_prompt, y_sample, new_attn_k, new_attn_v, new_mla_ckv, new_mla_kpe, new_ssm_re, new_ssm_im)
```

```python
import functools
import math

import jax
import jax.numpy as jnp
import numpy as np
from jax import lax
from jax.experimental import pallas as pl
from jax.experimental.pallas import tpu as pltpu

F32 = jnp.float32
BF16 = jnp.bfloat16

LANE = 128
SUBLANE = 8
MIB = 1024 * 1024

GRID_W = 64
ROPE_THETA = 10000.0
LN_EPS = 1e-6
RMS_EPS = 1e-6
HEAD_DIM = 64
ATTN_HEADS = 8
ATTN_KV_HEADS = 2
CONV_WIDTH = 512
S5_WIDTH = 512
S5_GROUP = 16
S5_GROUPS = 32
S5_STATE = 64
S5_CHUNK = 16
MLA_HEADS = 8
MLA_Q_RANK = 256
MLA_KV_RANK = 128
MLA_NOPE = 64
MLA_ROPE = 32
MLA_V = 64
N_EXPERTS = 8

TM_PROJ = 256
TM_FFN = 512
FFN_CHUNK = 1408
T_MOE = 1024
ROW_MOE = 128
MOE_CHUNK = 896
TQ_ATTN = 256


def _params(sem, vmem_mib):
    return pltpu.CompilerParams(dimension_semantics=sem, vmem_limit_bytes=vmem_mib * MIB)


def _dot(a, b):
    return jnp.dot(a, b, preferred_element_type=F32)


def _dot_nt(a, b):
    return lax.dot_general(a, b, (((1,), (1,)), ((), ())), preferred_element_type=F32)


def _split(a):
    hi = a.astype(BF16)
    lo = (a - hi.astype(F32)).astype(BF16)
    return hi, lo


def _silu(x):
    return x * jax.nn.sigmoid(x)


def _layer_norm(r, g, b):
    mu = jnp.mean(r, axis=-1, keepdims=True)
    d = r - mu
    var = jnp.mean(d * d, axis=-1, keepdims=True)
    return d * lax.rsqrt(var + LN_EPS) * g + b


def _rms(x, g, n):
    ms = jnp.sum(x * x, axis=-1, keepdims=True) * (1.0 / n)
    return x * lax.rsqrt(ms + RMS_EPS) * g


def _rope(x, cos, sa, sb, q):
    w = x.shape[-1]
    return x * cos + pltpu.roll(x, w - q, 1) * sa + pltpu.roll(x, q, 1) * sb


def _ada_kernel(c_ref, w_ref, b_ref, o_ref):
    c = c_ref[...]
    a_hi, a_lo = _split(_silu(c))
    w_hi, w_lo = _split(w_ref[0])
    o_ref[0] = _dot(a_hi, w_hi) + _dot(a_lo, w_hi) + _dot(a_hi, w_lo) + b_ref[0]


def _modulation(cond, ada_w, ada_b):
    depth, d, d6 = ada_w.shape
    r = cond.shape[0]
    tn = 1536
    return pl.pallas_call(
        _ada_kernel,
        out_shape=jax.ShapeDtypeStruct((depth, r, d6), F32),
        grid=(depth, d6 // tn),
        in_specs=[pl.BlockSpec((r, d), lambda l, j: (0, 0)),
                  pl.BlockSpec((1, d, tn), lambda l, j: (l, 0, j)),
                  pl.BlockSpec((1, 1, tn), lambda l, j: (l, 0, j))],
        out_specs=pl.BlockSpec((1, r, tn), lambda l, j: (l, 0, j)),
        compiler_params=_params(("arbitrary", "arbitrary"), 40),
        name="ada_modulation",
    )(cond, ada_w, ada_b.reshape(depth, 1, d6))


def _even_in_kernel(rows_ref, rblk_ref, x_ref, sc_ref, sh_ref, wc_ref, wq_ref, wkv_ref, qg_ref, kg_ref,
                    cos_ref, sa_ref, sb_ref, conv_ref, q_ref, k_ref, v_ref, kn_ref, vf_ref):
    del rows_ref, rblk_ref
    h = (x_ref[...] * (1.0 + sc_ref[0]) + sh_ref[0]).astype(BF16)
    conv_ref[...] = _dot(h, wc_ref[...])
    cos, sa, sb = cos_ref[...], sa_ref[...], sb_ref[...]
    quarter = HEAD_DIM // 4
    q = _dot(h, wq_ref[...])
    for hd in range(ATTN_HEADS):
        sl = slice(hd * LANE, (hd + 1) * LANE)
        qn = _rms(q[:, sl], qg_ref[...], HEAD_DIM)
        q_ref[:, sl] = (_rope(qn, cos, sa, sb, quarter) * (HEAD_DIM ** -0.5)).astype(BF16)
    kv = _dot(h, wkv_ref[...])
    kw = ATTN_KV_HEADS * LANE
    for hd in range(ATTN_KV_HEADS):
        sl = slice(hd * LANE, (hd + 1) * LANE)
        kn = _rms(kv[:, sl], kg_ref[...], HEAD_DIM)
        kn_ref[:, sl] = kn
        k_ref[:, sl] = _rope(kn, cos, sa, sb, quarter).astype(BF16)
    v = kv[:, kw:]
    vf_ref[...] = v
    v_ref[...] = v.astype(BF16)


def _even_in(x, sc, sh, wc, wq, wkv, qg, kg, tables, rows, rblk):
    n, d = x.shape
    tm = TM_PROJ
    cw, qw, kvw = wc.shape[1], wq.shape[1], wkv.shape[1]
    kw = kvw // 2
    cos, sa, sb = tables
    row3 = lambda i, rows, rblk: (rows[i], 0, 0)
    tok = lambda i, rows, rblk: (i, 0)
    const = lambda i, rows, rblk: (0, 0)
    tab = lambda i, rows, rblk: (rblk[i], 0)
    return pl.pallas_call(
        _even_in_kernel,
        out_shape=(jax.ShapeDtypeStruct((n, cw), F32), jax.ShapeDtypeStruct((n, qw), BF16),
                   jax.ShapeDtypeStruct((n, kw), BF16), jax.ShapeDtypeStruct((n, kw), BF16),
                   jax.ShapeDtypeStruct((n, kw), F32), jax.ShapeDtypeStruct((n, kw), F32)),
        grid_spec=pltpu.PrefetchScalarGridSpec(
            num_scalar_prefetch=2, grid=(n // tm,),
            in_specs=[pl.BlockSpec((tm, d), tok),
                      pl.BlockSpec((1, 1, d), row3), pl.BlockSpec((1, 1, d), row3),
                      pl.BlockSpec((d, cw), const), pl.BlockSpec((d, qw), const), pl.BlockSpec((d, kvw), const),
                      pl.BlockSpec((1, LANE), const), pl.BlockSpec((1, LANE), const),
                      pl.BlockSpec((tm, LANE), tab), pl.BlockSpec((tm, LANE), tab), pl.BlockSpec((tm, LANE), tab)],
            out_specs=[pl.BlockSpec((tm, cw), tok), pl.BlockSpec((tm, qw), tok),
                       pl.BlockSpec((tm, kw), tok), pl.BlockSpec((tm, kw), tok),
                       pl.BlockSpec((tm, kw), tok), pl.BlockSpec((tm, kw), tok)]),
        compiler_params=_params(("arbitrary",), 40),
        name="even_in_proj",
    )(rows, rblk, x, sc, sh, wc, wq, wkv, qg, kg, cos, sa, sb)


def _attn_kernel(*refs, n_heads, group, n_seg):
    q_ref = refs[0]
    seg = refs[1:1 + 2 * n_seg]
    o_ref = refs[1 + 2 * n_seg]
    for hd in range(n_heads):
        sl = slice(hd * LANE, (hd + 1) * LANE)
        ks = slice((hd // group) * LANE, (hd // group + 1) * LANE)
        qh = q_ref[:, sl]
        scores = [_dot_nt(qh, seg[2 * s][:, ks]) for s in range(n_seg)]
        m = jnp.max(scores[0], axis=-1, keepdims=True)
        for s in range(1, n_seg):
            m = jnp.maximum(m, jnp.max(scores[s], axis=-1, keepdims=True))
        den = None
        acc = None
        for s in range(n_seg):
            p = jnp.exp(scores[s] - m)
            ps = jnp.sum(p, axis=-1, keepdims=True)
            pv = _dot(p.astype(BF16), seg[2 * s + 1][:, ks])
            den = ps if den is None else den + ps
            acc = pv if acc is None else acc + pv
        o_ref[:, sl] = (acc * (1.0 / den)).astype(BF16)


def _attention(q, segments, n_heads, group):
    b, s, qw = q.shape
    tq = min(TQ_ATTN, s)
    in_specs = [pl.BlockSpec((None, tq, qw), lambda i, j: (i, j, 0))]
    args = [q]
    for k, v in segments:
        t, kw = k.shape[1], k.shape[2]
        in_specs += [pl.BlockSpec((None, t, kw), lambda i, j: (i, 0, 0)),
                     pl.BlockSpec((None, t, kw), lambda i, j: (i, 0, 0))]
        args += [k, v]
    return pl.pallas_call(
        functools.partial(_attn_kernel, n_heads=n_heads, group=group, n_seg=len(segments)),
        out_shape=jax.ShapeDtypeStruct((b, s, qw), BF16),
        grid=(b, s // tq),
        in_specs=in_specs,
        out_specs=pl.BlockSpec((None, tq, qw), lambda i, j: (i, j, 0)),
        compiler_params=_params(("arbitrary", "arbitrary"), 48),
        name="attention",
    )(*args)


def _even_out_kernel(first_ref, last_ref, rows_ref, c_ref, cp_ref, cn_ref, a_ref, x_ref, g_ref, cw_ref, w_ref,
                     lg_ref, lb_ref, o_ref, *, alpha):
    del rows_ref
    i = pl.program_id(0)
    tm = c_ref.shape[0]
    cwid = CONV_WIDTH
    c = c_ref[...]
    gate_b, z = c[:, :cwid], c[:, cwid:2 * cwid] * c[:, 2 * cwid:]
    cp = cp_ref[...]
    cn = cn_ref[...]
    zp = cp[SUBLANE - 1:SUBLANE, cwid:2 * cwid] * cp[SUBLANE - 1:SUBLANE, 2 * cwid:]
    zn = cn[0:1, cwid:2 * cwid] * cn[0:1, 2 * cwid:]
    zp = zp * (1 - first_ref[i]).astype(F32)
    zn = zn * (1 - last_ref[i]).astype(F32)
    row = lax.broadcasted_iota(jnp.int32, (tm, cwid), 0)
    z_prev = jnp.where(row == 0, zp, pltpu.roll(z, 1, 0))
    z_next = jnp.where(row == tm - 1, zn, pltpu.roll(z, tm - 1, 0))
    cw = cw_ref[...]
    y = gate_b * (cw[0:1] * z_prev + cw[1:2] * z + cw[2:3] * z_next)
    out = _dot(y.astype(BF16), w_ref[:cwid, :]) + _dot(a_ref[...], w_ref[cwid:, :])
    o_ref[...] = _layer_norm(alpha * x_ref[...] + g_ref[0] * out, lg_ref[...], lb_ref[...])


def _even_out(conv, attn, x, g1, conv_w, w_out, ln_g, ln_b, first, last, rows, alpha):
    n, d = x.shape
    tm = TM_PROJ
    cw3 = conv.shape[1]
    aw = attn.shape[1]
    hb = tm // SUBLANE
    nblk8 = n // SUBLANE
    tok = lambda i, f, l, r: (i, 0)
    const = lambda i, f, l, r: (0, 0)
    return pl.pallas_call(
        functools.partial(_even_out_kernel, alpha=alpha),
        out_shape=jax.ShapeDtypeStruct((n, d), F32),
        grid_spec=pltpu.PrefetchScalarGridSpec(
            num_scalar_prefetch=3, grid=(n // tm,),
            in_specs=[pl.BlockSpec((tm, cw3), tok),
                      pl.BlockSpec((SUBLANE, cw3), lambda i, f, l, r: (jnp.maximum(i * hb - 1, 0), 0)),
                      pl.BlockSpec((SUBLANE, cw3), lambda i, f, l, r: (jnp.minimum((i + 1) * hb, nblk8 - 1), 0)),
                      pl.BlockSpec((tm, aw), tok), pl.BlockSpec((tm, d), tok),
                      pl.BlockSpec((1, 1, d), lambda i, f, l, r: (r[i], 0, 0)),
                      pl.BlockSpec((SUBLANE, CONV_WIDTH), const),
                      pl.BlockSpec((CONV_WIDTH + aw, d), const),
                      pl.BlockSpec((1, d), const), pl.BlockSpec((1, d), const)],
            out_specs=pl.BlockSpec((tm, d), tok)),
        compiler_params=_params(("arbitrary",), 40),
        name="even_out_proj",
    )(first, last, rows, conv, conv, conv, attn, x, g1, conv_w, w_out, ln_g, ln_b)


def _ffn_kernel(rows_ref, x_ref, sc_ref, sh_ref, g_ref, wg_ref, wu_ref, wd_ref, lg_ref, lb_ref, o_ref,
                h_s, acc_s, *, alpha):
    del rows_ref
    f = pl.program_id(1)

    @pl.when(f == 0)
    def _():
        h_s[...] = (x_ref[...] * (1.0 + sc_ref[0]) + sh_ref[0]).astype(BF16)

    h = h_s[...]
    a = (_silu(_dot(h, wg_ref[...])) * _dot(h, wu_ref[...])).astype(BF16)
    y = _dot(a, wd_ref[...])

    @pl.when(f == 0)
    def _():
        acc_s[...] = y

    @pl.when(f > 0)
    def _():
        acc_s[...] += y

    @pl.when(f == pl.num_programs(1) - 1)
    def _():
        o_ref[...] = _layer_norm(alpha * x_ref[...] + g_ref[0] * acc_s[...], lg_ref[...], lb_ref[...])


def _ffn(x, sc, sh, g2, wg, wu, wd, ln_g, ln_b, rows, alpha):
    n, d = x.shape
    tm = TM_FFN
    ff = wg.shape[1]
    fc = FFN_CHUNK
    tok = lambda i, f, r: (i, 0)
    row3 = lambda i, f, r: (r[i], 0, 0)
    const = lambda i, f, r: (0, 0)
    return pl.pallas_call(
        functools.partial(_ffn_kernel, alpha=alpha),
        out_shape=jax.ShapeDtypeStruct((n, d), F32),
        grid_spec=pltpu.PrefetchScalarGridSpec(
            num_scalar_prefetch=1, grid=(n // tm, ff // fc),
            in_specs=[pl.BlockSpec((tm, d), tok),
                      pl.BlockSpec((1, 1, d), row3), pl.BlockSpec((1, 1, d), row3), pl.BlockSpec((1, 1, d), row3),
                      pl.BlockSpec((d, fc), lambda i, f, r: (0, f)), pl.BlockSpec((d, fc), lambda i, f, r: (0, f)),
                      pl.BlockSpec((fc, d), lambda i, f, r: (f, 0)),
                      pl.BlockSpec((1, d), const), pl.BlockSpec((1, d), const)],
            out_specs=pl.BlockSpec((tm, d), tok),
            scratch_shapes=[pltpu.VMEM((tm, d), BF16), pltpu.VMEM((tm, d), F32)]),
        compiler_params=_params(("arbitrary", "arbitrary"), 48),
        name="ffn_swiglu",
    )(rows, x, sc, sh, g2, wg, wu, wd, ln_g, ln_b)


def _odd_in_kernel(rows_ref, x_ref, sc_ref, sh_ref, w_ref, o_ref):
    del rows_ref
    h = (x_ref[...] * (1.0 + sc_ref[0]) + sh_ref[0]).astype(BF16)
    o_ref[...] = _dot(h, w_ref[...])


def _odd_in(x, sc, sh, w, rows):
    n, d = x.shape
    tm = TM_PROJ
    nw = w.shape[1]
    tok = lambda i, r: (i, 0)
    row3 = lambda i, r: (r[i], 0, 0)
    return pl.pallas_call(
        _odd_in_kernel,
        out_shape=jax.ShapeDtypeStruct((n, nw), F32),
        grid_spec=pltpu.PrefetchScalarGridSpec(
            num_scalar_prefetch=1, grid=(n // tm,),
            in_specs=[pl.BlockSpec((tm, d), tok), pl.BlockSpec((1, 1, d), row3), pl.BlockSpec((1, 1, d), row3),
                      pl.BlockSpec((d, nw), lambda i, r: (0, 0))],
            out_specs=pl.BlockSpec((tm, nw), tok)),
        compiler_params=_params(("arbitrary",), 32),
        name="odd_in_proj",
    )(rows, x, sc, sh, w)


def _s5_kernel(u_ref, zin_ref, t_ref, zout_ref, al_ref, h0_ref, y_ref, fin_ref, z_s, hf_s, hb_s, *, nb, nc):
    u = u_ref[0]
    z_s[...] = _dot(u, zin_ref[0])
    ar = al_ref[0, 0:1, :]
    ai = al_ref[0, 1:2, :]
    fwd = lax.broadcasted_iota(jnp.int32, (nb, LANE), 1) < S5_STATE

    def step(k, carry):
        re, im = carry
        rf = pl.multiple_of(k * nb, SUBLANE)
        rb = pl.multiple_of((nc - 1 - k) * nb, SUBLANE)
        hf_s[pl.ds(rf, nb), :LANE] = re
        hf_s[pl.ds(rf, nb), LANE:] = im
        hb_s[pl.ds(rb, nb), :LANE] = re
        hb_s[pl.ds(rb, nb), LANE:] = im
        zr = jnp.where(fwd, z_s[pl.ds(rf, nb), :LANE], z_s[pl.ds(rb, nb), :LANE])
        zi = jnp.where(fwd, z_s[pl.ds(rf, nb), LANE:], z_s[pl.ds(rb, nb), LANE:])
        return ar * re - ai * im + zr, ar * im + ai * re + zi

    h0 = h0_ref[0]
    re, im = lax.fori_loop(0, nc, step, (h0[:, :LANE], h0[:, LANE:]))
    fin_ref[0, :, :LANE] = re
    fin_ref[0, :, LANE:] = im
    m = u.shape[0]
    col = lax.broadcasted_iota(jnp.int32, (m, 2 * LANE), 1)
    is_fwd = (col % LANE) < S5_STATE
    h_in = jnp.where(is_fwd, hf_s[...], hb_s[...]).astype(BF16)
    y_ref[0] = _dot(u, t_ref[0]) + _dot(h_in, zout_ref[0])


def _s5(u, mats, h0, nb, nc):
    zin, tmat, zout, al = mats
    g, m, w = u.shape
    blk = lambda i: (i, 0, 0)
    return pl.pallas_call(
        functools.partial(_s5_kernel, nb=nb, nc=nc),
        out_shape=(jax.ShapeDtypeStruct((g, m, w), F32), jax.ShapeDtypeStruct((g, nb, w), F32)),
        grid=(g,),
        in_specs=[pl.BlockSpec((1, m, w), blk), pl.BlockSpec((1, w, w), blk), pl.BlockSpec((1, w, w), blk),
                  pl.BlockSpec((1, w, w), blk), pl.BlockSpec((1, 2, LANE), blk), pl.BlockSpec((1, nb, w), blk)],
        out_specs=(pl.BlockSpec((1, m, w), blk), pl.BlockSpec((1, nb, w), blk)),
        scratch_shapes=[pltpu.VMEM((m, w), F32), pltpu.VMEM((m, w), F32), pltpu.VMEM((m, w), F32)],
        compiler_params=_params(("arbitrary",), 32),
        name="s5_scan",
    )(u, zin, tmat, zout, al, h0)


def _s5_matrices(a_re, a_im, log_dt, b_re, b_im, c_re, c_im):
    hp = lax.Precision.HIGHEST
    L = S5_CHUNK
    dt = jnp.exp(log_dt)[..., None]
    lam_re, lam_im = a_re * dt, a_im * dt

    def power(k):
        k = k[:, None, None, None]
        mag = jnp.exp(lam_re * k)
        return mag * jnp.cos(lam_im * k), mag * jnp.sin(lam_im * k)

    ab_re, ab_im = power(jnp.ones((1,), F32))
    ab_re, ab_im = ab_re[0], ab_im[0]
    num_re, num_im = ab_re - 1.0, ab_im
    den = a_re * a_re + a_im * a_im
    f_re = (num_re * a_re + num_im * a_im) / den
    f_im = (num_im * a_re - num_re * a_im) / den
    bb_re = f_re[..., None] * b_re - f_im[..., None] * b_im
    bb_im = f_re[..., None] * b_im + f_im[..., None] * b_re

    ks = jnp.arange(L + 1, dtype=F32)
    pw_re, pw_im = power(ks)

    def zin_dir(d, exps):
        pr, pi = pw_re[exps, d], pw_im[exps, d]
        w_re = pr[..., None] * bb_re[d][None] - pi[..., None] * bb_im[d][None]
        w_im = pr[..., None] * bb_im[d][None] + pi[..., None] * bb_re[d][None]
        to = lambda w: jnp.transpose(w, (1, 0, 3, 2)).reshape(S5_GROUPS, L * S5_GROUP, S5_STATE)
        return to(w_re), to(w_im)

    steps = np.arange(L)
    zf_re, zf_im = zin_dir(0, L - 1 - steps)
    zb_re, zb_im = zin_dir(1, steps)
    zin = jnp.concatenate([zf_re, zb_re, zf_im, zb_im], axis=-1)

    def zout_dir(d, exps):
        pr, pi = pw_re[exps, d], pw_im[exps, d]
        cr, ci = c_re[d], c_im[d]
        e_re = cr[None] * pr[:, :, None, :] - ci[None] * pi[:, :, None, :]
        e_im = cr[None] * pi[:, :, None, :] + ci[None] * pr[:, :, None, :]
        to = lambda e: jnp.transpose(e, (1, 3, 0, 2)).reshape(S5_GROUPS, S5_STATE, L * S5_GROUP)
        return to(e_re), to(-e_im)

    of_re, of_im = zout_dir(0, steps + 1)
    ob_re, ob_im = zout_dir(1, L - steps)
    zout = jnp.concatenate([of_re, ob_re, of_im, ob_im], axis=1)

    def taps(d):
        pr, pi = pw_re[:L, d], pw_im[:L, d]
        m_re = pr[..., None] * bb_re[d][None] - pi[..., None] * bb_im[d][None]
        m_im = pr[..., None] * bb_im[d][None] + pi[..., None] * bb_re[d][None]
        return (jnp.einsum('gpn,lgnq->lgpq', c_re[d], m_re, precision=hp)
                - jnp.einsum('gpn,lgnq->lgpq', c_im[d], m_im, precision=hp))

    kf, kb = taps(0), taps(1)
    s_idx = steps[:, None]
    t_idx = steps[None, :]
    lag_f = np.clip(t_idx - s_idx, 0, L - 1)
    lag_b = np.clip(s_idx - t_idx, 0, L - 1)
    m_f = jnp.asarray((t_idx >= s_idx).astype(np.float32))[:, :, None, None, None]
    m_b = jnp.asarray((s_idx >= t_idx).astype(np.float32))[:, :, None, None, None]
    tm = kf[lag_f] * m_f + kb[lag_b] * m_b
    tmat = jnp.transpose(tm, (2, 0, 4, 1, 3)).reshape(S5_GROUPS, L * S5_GROUP, L * S5_GROUP)

    al = jnp.stack([jnp.concatenate([pw_re[L, 0], pw_re[L, 1]], axis=-1),
                    jnp.concatenate([pw_im[L, 0], pw_im[L, 1]], axis=-1)], axis=1)
    return zin.astype(BF16), tmat.astype(BF16), zout.astype(BF16), al


def _s5_pass(u, mats, h0, b, s):
    L = S5_CHUNK
    nc = s // L
    nb = -(-b // SUBLANE) * SUBLANE
    ug = u.reshape(b, nc, L, S5_GROUPS, S5_GROUP).transpose(3, 1, 0, 2, 4)
    if nb != b:
        ug = jnp.pad(ug, ((0, 0), (0, 0), (0, nb - b), (0, 0), (0, 0)))
        h0 = jnp.pad(h0, ((0, 0), (0, nb - b), (0, 0)))
    ug = ug.reshape(S5_GROUPS, nc * nb, L * S5_GROUP).astype(BF16)
    y, fin = _s5(ug, mats, h0, nb, nc)
    y = y.reshape(S5_GROUPS, nc, nb, L, S5_GROUP)[:, :, :b].transpose(2, 1, 3, 0, 4).reshape(b * s, S5_WIDTH)
    return y, fin[:, :b]


def _mla_kv(ckv, kpe, wk_ref, wv_ref, pl_ref, k_ref, v_ref):
    cb = ckv.astype(BF16)
    k_ref[...] = (_dot(cb, wk_ref[...]) + _dot(kpe.astype(BF16), pl_ref[...])).astype(BF16)
    v_ref[...] = _dot(cb, wv_ref[...]).astype(BF16)


def _mla_prep_kernel(rblk_ref, p_ref, qg_ref, kvg_ref, wuq_ref, wk_ref, wv_ref, pl_ref, cos_ref, sa_ref, sb_ref,
                     q_ref, k_ref, v_ref, ckv_ref):
    del rblk_ref
    pr = p_ref[...]
    cos, sa, sb = cos_ref[...], sa_ref[...], sb_ref[...]
    quarter = MLA_ROPE // 4
    scale = (MLA_NOPE + MLA_ROPE) ** -0.5
    qn = _rms(pr[:, :MLA_Q_RANK], qg_ref[...], MLA_Q_RANK).astype(BF16)
    q = _dot(qn, wuq_ref[...])
    for hd in range(MLA_HEADS):
        sl = slice(hd * LANE, (hd + 1) * LANE)
        q_ref[:, sl] = (_rope(q[:, sl], cos, sa, sb, quarter) * scale).astype(BF16)
    ckv = _rms(pr[:, MLA_Q_RANK:MLA_Q_RANK + MLA_KV_RANK], kvg_ref[...], MLA_KV_RANK)
    ckv_ref[...] = ckv
    kpe = _rope(pr[:, MLA_Q_RANK + MLA_KV_RANK:], cos, sa, sb, quarter)
    _mla_kv(ckv, kpe, wk_ref, wv_ref, pl_ref, k_ref, v_ref)


def _mla_prep(proj, qg, kvg, wuq, wk, wv, place, tables, rblk):
    n = proj.shape[0]
    tm = TM_PROJ
    hw = MLA_HEADS * LANE
    pw = MLA_Q_RANK + MLA_KV_RANK + LANE
    cos, sa, sb = tables
    tok = lambda i, r: (i, 0)
    const = lambda i, r: (0, 0)
    tab = lambda i, r: (r[i], 0)
    return pl.pallas_call(
        _mla_prep_kernel,
        out_shape=(jax.ShapeDtypeStruct((n, hw), BF16), jax.ShapeDtypeStruct((n, hw), BF16),
                   jax.ShapeDtypeStruct((n, hw), BF16), jax.ShapeDtypeStruct((n, MLA_KV_RANK), F32)),
        grid_spec=pltpu.PrefetchScalarGridSpec(
            num_scalar_prefetch=1, grid=(n // tm,),
            in_specs=[pl.BlockSpec((tm, pw), lambda i, r: (i, 1)),
                      pl.BlockSpec((1, MLA_Q_RANK), const), pl.BlockSpec((1, MLA_KV_RANK), const),
                      pl.BlockSpec((MLA_Q_RANK, hw), const), pl.BlockSpec((MLA_KV_RANK, hw), const),
                      pl.BlockSpec((MLA_KV_RANK, hw), const), pl.BlockSpec((LANE, hw), const),
                      pl.BlockSpec((tm, LANE), tab), pl.BlockSpec((tm, LANE), tab), pl.BlockSpec((tm, LANE), tab)],
            out_specs=[pl.BlockSpec((tm, hw), tok), pl.BlockSpec((tm, hw), tok), pl.BlockSpec((tm, hw), tok),
                       pl.BlockSpec((tm, MLA_KV_RANK), tok)]),
        compiler_params=_params(("arbitrary",), 32),
        name="mla_prep",
    )(rblk, proj, qg, kvg, wuq, wk, wv, place, cos, sa, sb)


def _mla_cache_kernel(c_ref, p_ref, wk_ref, wv_ref, pl_ref, k_ref, v_ref):
    _mla_kv(c_ref[...], p_ref[...], wk_ref, wv_ref, pl_ref, k_ref, v_ref)


def _mla_cache(ckv, kpe, wk, wv, place):
    n = ckv.shape[0]
    tm = TM_PROJ
    hw = MLA_HEADS * LANE
    tok = lambda i: (i, 0)
    const = lambda i: (0, 0)
    return pl.pallas_call(
        _mla_cache_kernel,
        out_shape=(jax.ShapeDtypeStruct((n, hw), BF16), jax.ShapeDtypeStruct((n, hw), BF16)),
        grid=(n // tm,),
        in_specs=[pl.BlockSpec((tm, MLA_KV_RANK), tok), pl.BlockSpec((tm, LANE), tok),
                  pl.BlockSpec((MLA_KV_RANK, hw), const), pl.BlockSpec((MLA_KV_RANK, hw), const),
                  pl.BlockSpec((LANE, hw), const)],
        out_specs=[pl.BlockSpec((tm, hw), tok), pl.BlockSpec((tm, hw), tok)],
        compiler_params=_params(("arbitrary",), 32),
        name="mla_cache_kv",
    )(ckv, kpe, wk, wv, place)


def _gelu_tanh(x):
    return 0.5 * x * (1.0 + jnp.tanh(math.sqrt(2.0 / math.pi) * (x + 0.044715 * (x * x * x))))


def _odd_out_kernel(rows_ref, u_ref, ys_ref, a_ref, x_ref, g_ref, d_ref, wglu_ref, bglu_ref, w_ref, lg_ref, lb_ref,
                    o_ref, *, alpha):
    del rows_ref
    y = _gelu_tanh(u_ref[...] * d_ref[...] + ys_ref[...])
    y = y * jax.nn.sigmoid(_dot(y.astype(BF16), wglu_ref[...]) + bglu_ref[...])
    out = _dot(y.astype(BF16), w_ref[:S5_WIDTH, :]) + _dot(a_ref[...], w_ref[S5_WIDTH:, :])
    o_ref[...] = _layer_norm(alpha * x_ref[...] + g_ref[0] * out, lg_ref[...], lb_ref[...])


def _odd_out(proj, y_ssm, attn, x, g1, s5_d, w_glu, b_glu, w_out, ln_g, ln_b, rows, alpha):
    n, d = x.shape
    tm = TM_PROJ
    aw = attn.shape[1]
    tok = lambda i, r: (i, 0)
    const = lambda i, r: (0, 0)
    return pl.pallas_call(
        functools.partial(_odd_out_kernel, alpha=alpha),
        out_shape=jax.ShapeDtypeStruct((n, d), F32),
        grid_spec=pltpu.PrefetchScalarGridSpec(
            num_scalar_prefetch=1, grid=(n // tm,),
            in_specs=[pl.BlockSpec((tm, S5_WIDTH), tok), pl.BlockSpec((tm, S5_WIDTH), tok),
                      pl.BlockSpec((tm, aw), tok), pl.BlockSpec((tm, d), tok),
                      pl.BlockSpec((1, 1, d), lambda i, r: (r[i], 0, 0)),
                      pl.BlockSpec((1, S5_WIDTH), const), pl.BlockSpec((S5_WIDTH, S5_WIDTH), const),
                      pl.BlockSpec((1, S5_WIDTH), const), pl.BlockSpec((S5_WIDTH + aw, d), const),
                      pl.BlockSpec((1, d), const), pl.BlockSpec((1, d), const)],
            out_specs=pl.BlockSpec((tm, d), tok)),
        compiler_params=_params(("arbitrary",), 40),
        name="odd_out_proj",
    )(rows, proj, y_ssm, attn, x, g1, s5_d, w_glu, b_glu, w_out, ln_g, ln_b)


def _router_kernel(rows_ref, x_ref, sc_ref, sh_ref, rt_ref, hb_ref, pos_ref, pt_ref, cnt_ref, tri_s):
    del rows_ref
    t = x_ref.shape[0]
    ne = N_EXPERTS

    @pl.when(pl.program_id(0) == 0)
    def _():
        before = lax.broadcasted_iota(jnp.int32, (t, t), 0) < lax.broadcasted_iota(jnp.int32, (t, t), 1)
        tri_s[...] = jnp.where(before, 1.0, 0.0).astype(BF16)

    h = x_ref[...] * (1.0 + sc_ref[0]) + sh_ref[0]
    h_hi, h_lo = _split(h)
    hb_ref[...] = h_hi
    r_hi, r_lo = _split(rt_ref[...])
    logits = _dot_nt(r_hi, h_hi) + _dot_nt(r_lo, h_hi) + _dot_nt(r_hi, h_lo)
    eid = lax.broadcasted_iota(jnp.int32, (ne, t), 0).astype(F32)
    m0 = jnp.max(logits, axis=0, keepdims=True)
    i0 = jnp.min(jnp.where(logits == m0, eid, float(ne)), axis=0, keepdims=True)
    rest = jnp.where(eid == i0, -jnp.inf, logits)
    m1 = jnp.max(rest, axis=0, keepdims=True)
    i1 = jnp.min(jnp.where(rest == m1, eid, float(ne)), axis=0, keepdims=True)
    ex = jnp.exp(m1 - m0)
    g0 = 1.0 / (1.0 + ex)
    g1 = ex / (1.0 + ex)
    sel0 = eid == i0
    sel1 = eid == i1
    member = jnp.where(sel0, 1.0, jnp.where(sel1, 1.0, 0.0))
    gate = jnp.where(sel0, g0, jnp.where(sel1, g1, 0.0))
    rank = _dot(member.astype(BF16), tri_s[...])
    pos = jnp.where(member > 0.0, rank, -1.0)
    pos_ref[0] = pos.astype(jnp.int32)
    cnt = jnp.sum(member, axis=1, keepdims=True)
    cnt_ref[0] = jnp.broadcast_to(cnt, (ne, LANE)).astype(jnp.int32)
    packed = jnp.concatenate([pos, gate, jnp.zeros((LANE - 2 * ne, t), F32)], axis=0)
    pt_ref[...] = packed.T


def _router(x, sc, sh, router_t, rows):
    n, d = x.shape
    t = T_MOE
    nb = n // t
    ne = N_EXPERTS
    tok = lambda i, r: (i, 0)
    row3 = lambda i, r: (r[i], 0, 0)
    blk3 = lambda i, r: (i, 0, 0)
    return pl.pallas_call(
        _router_kernel,
        out_shape=(jax.ShapeDtypeStruct((n, d), BF16), jax.ShapeDtypeStruct((nb, ne, t), jnp.int32),
                   jax.ShapeDtypeStruct((n, LANE), F32), jax.ShapeDtypeStruct((nb, ne, LANE), jnp.int32)),
        grid_spec=pltpu.PrefetchScalarGridSpec(
            num_scalar_prefetch=1, grid=(nb,),
            in_specs=[pl.BlockSpec((t, d), tok), pl.BlockSpec((1, 1, d), row3), pl.BlockSpec((1, 1, d), row3),
                      pl.BlockSpec((ne, d), lambda i, r: (0, 0))],
            out_specs=[pl.BlockSpec((t, d), tok), pl.BlockSpec((1, ne, t), blk3),
                       pl.BlockSpec((t, LANE), tok), pl.BlockSpec((1, ne, LANE), blk3)],
            scratch_shapes=[pltpu.VMEM((t, t), BF16)]),
        compiler_params=_params(("arbitrary",), 48),
        name="moe_router",
    )(rows, x, sc, sh, router_t)


def _moe_kernel(cnt_ref, h_ref, pos_ref, pt_ref, wg_ref, wu_ref, wd_ref, o_ref, xe_s, ye_s):
    b, e, f = pl.program_id(0), pl.program_id(1), pl.program_id(2)
    t = h_ref.shape[0]
    rt = ROW_MOE
    ntile = (cnt_ref[b * N_EXPERTS + e] + rt - 1) // rt

    @pl.when((e == 0) & (f == 0))
    def _():
        o_ref[...] = jnp.zeros_like(o_ref)

    @pl.when(f == 0)
    def _():
        pos_row = pos_ref[0, pl.ds(e, 1), :]

        def gather(r, c):
            r0 = pl.multiple_of(r * rt, rt)
            ids = lax.broadcasted_iota(jnp.int32, (rt, t), 0) + r0
            onehot = jnp.where(ids == pos_row, 1.0, 0.0).astype(BF16)
            xe_s[pl.ds(r0, rt), :] = _dot(onehot, h_ref[...]).astype(BF16)
            return c

        lax.fori_loop(0, ntile, gather, 0)

    def mlp(r, c):
        r0 = pl.multiple_of(r * rt, rt)
        x = xe_s[pl.ds(r0, rt), :]
        a = (_silu(_dot(x, wg_ref[0])) * _dot(x, wu_ref[0])).astype(BF16)
        y = _dot(a, wd_ref[0])

        @pl.when(f == 0)
        def _():
            ye_s[pl.ds(r0, rt), :] = y

        @pl.when(f > 0)
        def _():
            ye_s[pl.ds(r0, rt), :] += y

        return c

    lax.fori_loop(0, ntile, mlp, 0)

    @pl.when(f == pl.num_programs(2) - 1)
    def _():
        pt = pt_ref[...]
        lane = lax.broadcasted_iota(jnp.int32, (t, LANE), 1)
        pos_col = jnp.sum(jnp.where(lane == e, pt, 0.0), axis=1, keepdims=True)
        gate_col = jnp.sum(jnp.where(lane == e + N_EXPERTS, pt, 0.0), axis=1, keepdims=True)

        def combine(r, c):
            r0 = pl.multiple_of(r * rt, rt)
            onehot = jnp.where(pos_col == (lane + r0).astype(F32), 1.0, 0.0).astype(BF16)
            o_ref[...] += gate_col * _dot(onehot, ye_s[pl.ds(r0, rt), :].astype(BF16))
            return c

        lax.fori_loop(0, ntile, combine, 0)


def _moe(h, pos, pt, counts, wg, wu, wd):
    n, d = h.shape
    t = T_MOE
    nb = n // t
    ne, _, ff = wg.shape
    fc = MOE_CHUNK
    tok = lambda b, e, f, c: (b, 0)
    return pl.pallas_call(
        _moe_kernel,
        out_shape=jax.ShapeDtypeStruct((n, d), F32),
        grid_spec=pltpu.PrefetchScalarGridSpec(
            num_scalar_prefetch=1, grid=(nb, ne, ff // fc),
            in_specs=[pl.BlockSpec((t, d), tok),
                      pl.BlockSpec((1, ne, t), lambda b, e, f, c: (b, 0, 0)),
                      pl.BlockSpec((t, LANE), tok),
                      pl.BlockSpec((1, d, fc), lambda b, e, f, c: (e, 0, f)),
                      pl.BlockSpec((1, d, fc), lambda b, e, f, c: (e, 0, f)),
                      pl.BlockSpec((1, fc, d), lambda b, e, f, c: (e, f, 0))],
            out_specs=pl.BlockSpec((t, d), tok),
            scratch_shapes=[pltpu.VMEM((t, d), BF16), pltpu.VMEM((t, d), F32)]),
        compiler_params=_params(("arbitrary", "arbitrary", "arbitrary"), 52),
        name="moe_experts",
    )(counts, h, pos, pt, wg, wu, wd)


def _add_ln_kernel(rows_ref, x_ref, f_ref, g_ref, lg_ref, lb_ref, o_ref, *, alpha):
    del rows_ref
    o_ref[...] = _layer_norm(alpha * x_ref[...] + g_ref[0] * f_ref[...], lg_ref[...], lb_ref[...])


def _add_ln(x, f, g2, ln_g, ln_b, rows, alpha):
    n, d = x.shape
    tm = TM_FFN
    tok = lambda i, r: (i, 0)
    const = lambda i, r: (0, 0)
    return pl.pallas_call(
        functools.partial(_add_ln_kernel, alpha=alpha),
        out_shape=jax.ShapeDtypeStruct((n, d), F32),
        grid_spec=pltpu.PrefetchScalarGridSpec(
            num_scalar_prefetch=1, grid=(n // tm,),
            in_specs=[pl.BlockSpec((tm, d), tok), pl.BlockSpec((tm, d), tok),
                      pl.BlockSpec((1, 1, d), lambda i, r: (r[i], 0, 0)),
                      pl.BlockSpec((1, d), const), pl.BlockSpec((1, d), const)],
            out_specs=pl.BlockSpec((tm, d), tok)),
        compiler_params=_params(("arbitrary",), 32),
        name="residual_layer_norm",
    )(rows, x, f, g2, ln_g, ln_b)


def _tile_rows(n, p, ss, tm):
    starts = np.arange(n // tm) * tm
    return jnp.asarray(np.where(starts < p, 0, 1 + (starts - p) // ss).astype(np.int32))


def _tile_rope_blocks(n, p, ss, tm):
    starts = np.arange(n // tm) * tm
    return jnp.asarray(np.where(starts < p, 0, 1 + ((starts - p) % ss) // tm).astype(np.int32))


def _tile_edges(n, p, sp, ss, tm):
    starts = np.arange(n // tm) * tm
    pos = np.where(starts < p, starts % sp, (starts - p) % ss)
    seq = np.where(starts < p, sp, ss)
    return jnp.asarray((pos == 0).astype(np.int32)), jnp.asarray((pos + tm == seq).astype(np.int32))


def _rope_tables(n_tokens, rot_dim, lane0, tm):
    rows = n_tokens // GRID_W
    row_pos = jnp.repeat(jnp.arange(rows, dtype=F32), GRID_W)
    col_pos = jnp.tile(jnp.arange(GRID_W, dtype=F32), rows)
    half = rot_dim // 2
    qtr = half // 2
    inv_freq = ROPE_THETA ** (-jnp.arange(0, half, 2, dtype=F32) / half)
    ang_r = row_pos[:, None] * inv_freq
    ang_c = col_pos[:, None] * inv_freq
    ang = jnp.concatenate([ang_r, ang_r, ang_c, ang_c], axis=-1)
    cos, sin = jnp.cos(ang), jnp.sin(ang)
    first = ((np.arange(rot_dim) % half) < qtr).astype(np.float32)
    sa = -sin * first
    sb = sin * (1.0 - first)

    def place(tbl, fill):
        full = jnp.full((n_tokens, LANE), fill, F32).at[:, lane0:lane0 + rot_dim].set(tbl)
        return jnp.concatenate([jnp.full((tm, LANE), fill, F32), full], axis=0)

    return place(cos, 1.0), place(sa, 0.0), place(sb, 0.0)


def _pad_heads(w, n_heads, width):
    lead = w.shape[:-1]
    w = w.reshape(lead + (n_heads, width))
    w = jnp.pad(w, [(0, 0)] * len(lead) + [(0, 0), (0, LANE - width)])
    return w.reshape(lead + (n_heads * LANE,))


def _pad_head_rows(w, n_heads, width):
    d = w.shape[-1]
    w = w.reshape(n_heads, width, d)
    return jnp.pad(w, ((0, 0), (0, LANE - width), (0, 0))).reshape(n_heads * LANE, d)


def kernel(x_prompt, x_sample, cache_attn_k, cache_attn_v, cache_mla_ckv, cache_mla_kpe, state_ssm_re, state_ssm_im, c, c_ctx, ada_w, ada_b, ln_g, ln_b, ev_w_in, ev_conv_w, ev_q_gain, ev_k_gain, ev_w_out, ffn_w_gate, ffn_w_up, ffn_w_down, od_w_in, s5_a_re, s5_a_im, s5_log_dt, s5_b_re, s5_b_im, s5_c_re, s5_c_im, s5_d, s5_w_glu, s5_b_glu, mla_q_gain, mla_w_uq, mla_kv_gain, mla_w_ukv, od_w_out, moe_router, moe_w_gate, moe_w_up, moe_w_down):
    bp, sp, d = x_prompt.shape
    bs, ss, _ = x_sample.shape
    depth = ada_w.shape[0]
    alpha = (2 * depth) ** 0.25
    p = bp * sp
    n = p + bs * ss
    past = cache_attn_k.shape[2]
    for tm in (TM_PROJ, TM_FFN, T_MOE):
        assert p % tm == 0 and ss % tm == 0 and (sp % tm == 0 or tm % sp == 0)
    assert sp % TM_PROJ == 0 and sp % S5_CHUNK == 0 and ss % S5_CHUNK == 0

    x = jnp.concatenate([x_prompt.reshape(p, d), x_sample.reshape(bs * ss, d)], axis=0)

    nrow = -(-(1 + bs) // SUBLANE) * SUBLANE
    cond = jnp.zeros((nrow, d), F32).at[0].set(c_ctx).at[1:1 + bs].set(c)
    mod = _modulation(cond, ada_w, ada_b)

    def mod_part(l, k):
        return mod[l, :, k * d:(k + 1) * d].reshape(nrow, 1, d)

    rows_proj = _tile_rows(n, p, ss, TM_PROJ)
    rows_ffn = _tile_rows(n, p, ss, TM_FFN)
    rows_moe = _tile_rows(n, p, ss, T_MOE)
    rblk = _tile_rope_blocks(n, p, ss, TM_PROJ)
    first, last = _tile_edges(n, p, sp, ss, TM_PROJ)
    tables_even = _rope_tables(ss, HEAD_DIM, 0, TM_PROJ)
    tables_mla = _rope_tables(ss, MLA_ROPE, MLA_NOPE, TM_PROJ)

    out_k, out_v, out_ckv, out_kpe, out_sre, out_sim = [], [], [], [], [], []
    cw = CONV_WIDTH
    hq = ATTN_HEADS * HEAD_DIM
    hkv = ATTN_KV_HEADS * HEAD_DIM
    for l in range(depth):
        i = l // 2
        sh1, sc1, g1, sh2, sc2, g2 = [mod_part(l, k) for k in range(6)]
        lg = ln_g[l].reshape(2, 1, d)
        lb = ln_b[l].reshape(2, 1, d)
        if l % 2 == 0:
            w_in = ev_w_in[i]
            wc = w_in[:, :3 * cw].astype(BF16)
            wq = _pad_heads(w_in[:, 3 * cw:3 * cw + hq], ATTN_HEADS, HEAD_DIM).astype(BF16)
            wkv = jnp.concatenate([_pad_heads(w_in[:, 3 * cw + hq:3 * cw + hq + hkv], ATTN_KV_HEADS, HEAD_DIM),
                                   _pad_heads(w_in[:, 3 * cw + hq + hkv:], ATTN_KV_HEADS, HEAD_DIM)],
                                  axis=1).astype(BF16)
            qg = jnp.pad(ev_q_gain[i], (0, LANE - HEAD_DIM)).reshape(1, LANE)
            kg = jnp.pad(ev_k_gain[i], (0, LANE - HEAD_DIM)).reshape(1, LANE)
            conv, q, k, v, kn, vf = _even_in(x, sc1, sh1, wc, wq, wkv, qg, kg, tables_even, rows_proj, rblk)
            kvw = ATTN_KV_HEADS * LANE
            out_k.append(kn[:p].reshape(bp, sp, ATTN_KV_HEADS, LANE)[..., :HEAD_DIM])
            out_v.append(vf[:p].reshape(bp, sp, ATTN_KV_HEADS, LANE)[..., :HEAD_DIM])
            group = ATTN_HEADS // ATTN_KV_HEADS
            qw = ATTN_HEADS * LANE
            a_p = _attention(q[:p].reshape(bp, sp, qw),
                             [(k[:p].reshape(bp, sp, kvw), v[:p].reshape(bp, sp, kvw))], ATTN_HEADS, group)
            kc = _pad_heads(cache_attn_k[:, i].reshape(bs, past, hkv), ATTN_KV_HEADS, HEAD_DIM).astype(BF16)
            vc = _pad_heads(cache_attn_v[:, i].reshape(bs, past, hkv), ATTN_KV_HEADS, HEAD_DIM).astype(BF16)
            a_s = _attention(q[p:].reshape(bs, ss, qw),
                             [(kc, vc), (k[p:].reshape(bs, ss, kvw), v[p:].reshape(bs, ss, kvw))],
                             ATTN_HEADS, group)
            attn = jnp.concatenate([a_p.reshape(p, qw), a_s.reshape(bs * ss, qw)], axis=0)
            w_out = jnp.concatenate([ev_w_out[i][:cw], _pad_head_rows(ev_w_out[i][cw:], ATTN_HEADS, HEAD_DIM)],
                                    axis=0).astype(BF16)
            conv_w = jnp.pad(ev_conv_w[i], ((0, SUBLANE - ev_conv_w.shape[1]), (0, 0)))
            x = _even_out(conv, attn, x, g1, conv_w, w_out, lg[0], lb[0], first, last, rows_proj, alpha)
            x = _ffn(x, sc2, sh2, g2, ffn_w_gate[i].astype(BF16), ffn_w_up[i].astype(BF16),
                     ffn_w_down[i].astype(BF16), lg[1], lb[1], rows_ffn, alpha)
        else:
            q_end = S5_WIDTH + MLA_Q_RANK
            kv_end = q_end + MLA_KV_RANK
            w_in = od_w_in[i]
            w_in = jnp.concatenate([w_in[:, :kv_end], jnp.zeros((d, MLA_NOPE), F32), w_in[:, kv_end:],
                                    jnp.zeros((d, LANE - MLA_NOPE - MLA_ROPE), F32)], axis=1).astype(BF16)
            proj = _odd_in(x, sc1, sh1, w_in, rows_proj)
            out_kpe.append(proj[:p, kv_end + MLA_NOPE:kv_end + MLA_NOPE + MLA_ROPE].reshape(bp, sp, MLA_ROPE))

            mats = _s5_matrices(s5_a_re[i], s5_a_im[i], s5_log_dt[i], s5_b_re[i], s5_b_im[i], s5_c_re[i], s5_c_im[i])
            u = proj[:, :S5_WIDTH]
            h0_p = jnp.zeros((S5_GROUPS, bp, 4 * S5_STATE), F32)
            y_p, fin_p = _s5_pass(u[:p], mats, h0_p, bp, sp)

            def pack_state(re, im):
                to = lambda a: a.transpose(2, 0, 1, 3).reshape(S5_GROUPS, a.shape[0], 2 * S5_STATE)
                return jnp.concatenate([to(re), to(im)], axis=-1)

            y_s, _ = _s5_pass(u[p:], mats, pack_state(state_ssm_re[:, i], state_ssm_im[:, i]), bs, ss)
            y_ssm = jnp.concatenate([y_p, y_s], axis=0)
            fin = fin_p.reshape(S5_GROUPS, bp, 2, 2, S5_STATE)
            out_sre.append(fin[:, :, 0].transpose(1, 2, 0, 3))
            out_sim.append(fin[:, :, 1].transpose(1, 2, 0, 3))

            qk = MLA_NOPE + MLA_ROPE
            wuq = _pad_heads(mla_w_uq[i], MLA_HEADS, qk).astype(BF16)
            wukv = mla_w_ukv[i].reshape(MLA_KV_RANK, MLA_HEADS, MLA_NOPE + MLA_V)
            wk = _pad_heads(wukv[..., :MLA_NOPE].reshape(MLA_KV_RANK, -1), MLA_HEADS, MLA_NOPE).astype(BF16)
            wv = _pad_heads(wukv[..., MLA_NOPE:].reshape(MLA_KV_RANK, -1), MLA_HEADS, MLA_V).astype(BF16)
            place = np.zeros((LANE, MLA_HEADS * LANE), np.float32)
            for hd in range(MLA_HEADS):
                for j in range(MLA_ROPE):
                    place[MLA_NOPE + j, hd * LANE + MLA_NOPE + j] = 1.0
            place = jnp.asarray(place, BF16)
            q, k, v, ckv = _mla_prep(proj, mla_q_gain[i].reshape(1, -1), mla_kv_gain[i].reshape(1, -1),
                                     wuq, wk, wv, place, tables_mla, rblk)
            out_ckv.append(ckv[:p].reshape(bp, sp, MLA_KV_RANK))
            hw = MLA_HEADS * LANE
            a_p = _attention(q[:p].reshape(bp, sp, hw), [(k[:p].reshape(bp, sp, hw), v[:p].reshape(bp, sp, hw))],
                             MLA_HEADS, 1)
            kpe_c = jnp.pad(cache_mla_kpe[:, i].reshape(bs * past, MLA_ROPE),
                            ((0, 0), (MLA_NOPE, LANE - MLA_NOPE - MLA_ROPE)))
            kc, vc = _mla_cache(cache_mla_ckv[:, i].reshape(bs * past, MLA_KV_RANK), kpe_c, wk, wv, place)
            a_s = _attention(q[p:].reshape(bs, ss, hw),
                             [(kc.reshape(bs, past, hw), vc.reshape(bs, past, hw)),
                              (k[p:].reshape(bs, ss, hw), v[p:].reshape(bs, ss, hw))], MLA_HEADS, 1)
            attn = jnp.concatenate([a_p.reshape(p, hw), a_s.reshape(bs * ss, hw)], axis=0)
            w_out = jnp.concatenate([od_w_out[i][:S5_WIDTH], _pad_head_rows(od_w_out[i][S5_WIDTH:], MLA_HEADS, MLA_V)],
                                    axis=0).astype(BF16)
            x = _odd_out(proj, y_ssm, attn, x, g1, s5_d[i].reshape(1, -1), s5_w_glu[i].astype(BF16),
                         s5_b_glu[i].reshape(1, -1), w_out, lg[0], lb[0], rows_proj, alpha)

            hb, pos, pt, cnt = _router(x, sc2, sh2, moe_router[i].T, rows_moe)
            f = _moe(hb, pos, pt, cnt[:, :, 0].reshape(-1), moe_w_gate[i].astype(BF16), moe_w_up[i].astype(BF16),
                     moe_w_down[i].astype(BF16))
            x = _add_ln(x, f, g2, lg[1], lb[1], rows_ffn, alpha)

    y_prompt = x[:p].reshape(bp, sp, d)
    y_sample = x[p:].reshape(bs, ss, d)
    return (y_prompt, y_sample, jnp.stack(out_k, axis=1), jnp.stack(out_v, axis=1),
            jnp.stack(out_ckv, axis=1), jnp.stack(out_kpe, axis=1),
            jnp.stack(out_sre, axis=1), jnp.stack(out_sim, axis=1))
```

```python
import functools
import math

import jax
import jax.numpy as jnp
import numpy as np
from jax import lax
from jax.experimental import pallas as pl
from jax.experimental.pallas import tpu as pltpu

F32 = jnp.float32
BF16 = jnp.bfloat16

LANE = 128
SUBLANE = 8
MIB = 1024 * 1024

GRID_W = 64
ROPE_THETA = 10000.0
LN_EPS = 1e-6
RMS_EPS = 1e-6
HEAD_DIM = 64
ATTN_HEADS = 8
ATTN_KV_HEADS = 2
CONV_WIDTH = 512
S5_WIDTH = 512
S5_GROUP = 16
S5_GROUPS = 32
S5_STATE = 64
S5_CHUNK = 16
MLA_HEADS = 8
MLA_Q_RANK = 256
MLA_KV_RANK = 128
MLA_NOPE = 64
MLA_ROPE = 32
MLA_V = 64
N_EXPERTS = 8

TM_PROJ = 256
TM_FFN = 512
FFN_CHUNK = 1408
T_MOE = 1024
ROW_MOE = 128
RT_GMM = 512
MOE_CHUNK = 896
SEG_ALIGN = 16
SEG_SIZES = tuple(T_MOE >> k for k in range(T_MOE.bit_length()) if (T_MOE >> k) >= SEG_ALIGN)
TQ_ATTN = 256


def _params(sem, vmem_mib):
    return pltpu.CompilerParams(dimension_semantics=sem, vmem_limit_bytes=vmem_mib * MIB)


def _dot(a, b):
    return jnp.dot(a, b, preferred_element_type=F32)


def _dot_nt(a, b):
    return lax.dot_general(a, b, (((1,), (1,)), ((), ())), preferred_element_type=F32)


def _split(a):
    hi = a.astype(BF16)
    lo = (a - hi.astype(F32)).astype(BF16)
    return hi, lo


def _silu(x):
    return x * jax.nn.sigmoid(x)


def _layer_norm(r, g, b):
    mu = jnp.mean(r, axis=-1, keepdims=True)
    d = r - mu
    var = jnp.mean(d * d, axis=-1, keepdims=True)
    return d * lax.rsqrt(var + LN_EPS) * g + b


def _rms(x, g, n):
    ms = jnp.sum(x * x, axis=-1, keepdims=True) * (1.0 / n)
    return x * lax.rsqrt(ms + RMS_EPS) * g


def _rope(x, cos, sa, sb, q):
    w = x.shape[-1]
    return x * cos + pltpu.roll(x, w - q, 1) * sa + pltpu.roll(x, q, 1) * sb


def _ada_kernel(c_ref, w_ref, b_ref, o_ref):
    c = c_ref[...]
    a_hi, a_lo = _split(_silu(c))
    w_hi, w_lo = _split(w_ref[0])
    o_ref[0] = _dot(a_hi, w_hi) + _dot(a_lo, w_hi) + _dot(a_hi, w_lo) + b_ref[0]


def _modulation(cond, ada_w, ada_b):
    depth, d, d6 = ada_w.shape
    r = cond.shape[0]
    tn = 1536
    return pl.pallas_call(
        _ada_kernel,
        out_shape=jax.ShapeDtypeStruct((depth, r, d6), F32),
        grid=(depth, d6 // tn),
        in_specs=[pl.BlockSpec((r, d), lambda l, j: (0, 0)),
                  pl.BlockSpec((1, d, tn), lambda l, j: (l, 0, j)),
                  pl.BlockSpec((1, 1, tn), lambda l, j: (l, 0, j))],
        out_specs=pl.BlockSpec((1, r, tn), lambda l, j: (l, 0, j)),
        compiler_params=_params(("arbitrary", "arbitrary"), 40),
        name="ada_modulation",
    )(cond, ada_w, ada_b.reshape(depth, 1, d6))


def _even_in_kernel(rows_ref, rblk_ref, x_ref, sc_ref, sh_ref, wc_ref, wq_ref, wkv_ref, qg_ref, kg_ref,
                    cos_ref, sa_ref, sb_ref, conv_ref, q_ref, k_ref, v_ref, kn_ref, vf_ref):
    del rows_ref, rblk_ref
    h = (x_ref[...] * (1.0 + sc_ref[0]) + sh_ref[0]).astype(BF16)
    conv_ref[...] = _dot(h, wc_ref[...])
    cos, sa, sb = cos_ref[...], sa_ref[...], sb_ref[...]
    quarter = HEAD_DIM // 4
    q = _dot(h, wq_ref[...])
    for hd in range(ATTN_HEADS):
        sl = slice(hd * LANE, (hd + 1) * LANE)
        qn = _rms(q[:, sl], qg_ref[...], HEAD_DIM)
        q_ref[:, sl] = (_rope(qn, cos, sa, sb, quarter) * (HEAD_DIM ** -0.5)).astype(BF16)
    kv = _dot(h, wkv_ref[...])
    kw = ATTN_KV_HEADS * LANE
    for hd in range(ATTN_KV_HEADS):
        sl = slice(hd * LANE, (hd + 1) * LANE)
        kn = _rms(kv[:, sl], kg_ref[...], HEAD_DIM)
        kn_ref[:, sl] = kn
        k_ref[:, sl] = _rope(kn, cos, sa, sb, quarter).astype(BF16)
    v = kv[:, kw:]
    vf_ref[...] = v
    v_ref[...] = v.astype(BF16)


def _even_in(x, sc, sh, wc, wq, wkv, qg, kg, tables, rows, rblk):
    n, d = x.shape
    tm = TM_PROJ
    cw, qw, kvw = wc.shape[1], wq.shape[1], wkv.shape[1]
    kw = kvw // 2
    cos, sa, sb = tables
    row3 = lambda i, rows, rblk: (rows[i], 0, 0)
    tok = lambda i, rows, rblk: (i, 0)
    const = lambda i, rows, rblk: (0, 0)
    tab = lambda i, rows, rblk: (rblk[i], 0)
    return pl.pallas_call(
        _even_in_kernel,
        out_shape=(jax.ShapeDtypeStruct((n, cw), F32), jax.ShapeDtypeStruct((n, qw), BF16),
                   jax.ShapeDtypeStruct((n, kw), BF16), jax.ShapeDtypeStruct((n, kw), BF16),
                   jax.ShapeDtypeStruct((n, kw), F32), jax.ShapeDtypeStruct((n, kw), F32)),
        grid_spec=pltpu.PrefetchScalarGridSpec(
            num_scalar_prefetch=2, grid=(n // tm,),
            in_specs=[pl.BlockSpec((tm, d), tok),
                      pl.BlockSpec((1, 1, d), row3), pl.BlockSpec((1, 1, d), row3),
                      pl.BlockSpec((d, cw), const), pl.BlockSpec((d, qw), const), pl.BlockSpec((d, kvw), const),
                      pl.BlockSpec((1, LANE), const), pl.BlockSpec((1, LANE), const),
                      pl.BlockSpec((tm, LANE), tab), pl.BlockSpec((tm, LANE), tab), pl.BlockSpec((tm, LANE), tab)],
            out_specs=[pl.BlockSpec((tm, cw), tok), pl.BlockSpec((tm, qw), tok),
                       pl.BlockSpec((tm, kw), tok), pl.BlockSpec((tm, kw), tok),
                       pl.BlockSpec((tm, kw), tok), pl.BlockSpec((tm, kw), tok)]),
        compiler_params=_params(("arbitrary",), 40),
        name="even_in_proj",
    )(rows, rblk, x, sc, sh, wc, wq, wkv, qg, kg, cos, sa, sb)


def _attn_kernel(*refs, n_heads, group, n_seg):
    q_ref = refs[0]
    seg = refs[1:1 + 2 * n_seg]
    o_ref = refs[1 + 2 * n_seg]
    for hd in range(n_heads):
        sl = slice(hd * LANE, (hd + 1) * LANE)
        ks = slice((hd // group) * LANE, (hd // group + 1) * LANE)
        qh = q_ref[:, sl]
        scores = [_dot_nt(qh, seg[2 * s][:, ks]) for s in range(n_seg)]
        m = jnp.max(scores[0], axis=-1, keepdims=True)
        for s in range(1, n_seg):
            m = jnp.maximum(m, jnp.max(scores[s], axis=-1, keepdims=True))
        den = None
        acc = None
        for s in range(n_seg):
            p = jnp.exp(scores[s] - m)
            ps = jnp.sum(p, axis=-1, keepdims=True)
            pv = _dot(p.astype(BF16), seg[2 * s + 1][:, ks])
            den = ps if den is None else den + ps
            acc = pv if acc is None else acc + pv
        o_ref[:, sl] = (acc * (1.0 / den)).astype(BF16)


def _attention(q, segments, n_heads, group):
    b, s, qw = q.shape
    tq = min(TQ_ATTN, s)
    in_specs = [pl.BlockSpec((None, tq, qw), lambda i, j: (i, j, 0))]
    args = [q]
    for k, v in segments:
        t, kw = k.shape[1], k.shape[2]
        in_specs += [pl.BlockSpec((None, t, kw), lambda i, j: (i, 0, 0)),
                     pl.BlockSpec((None, t, kw), lambda i, j: (i, 0, 0))]
        args += [k, v]
    return pl.pallas_call(
        functools.partial(_attn_kernel, n_heads=n_heads, group=group, n_seg=len(segments)),
        out_shape=jax.ShapeDtypeStruct((b, s, qw), BF16),
        grid=(b, s // tq),
        in_specs=in_specs,
        out_specs=pl.BlockSpec((None, tq, qw), lambda i, j: (i, j, 0)),
        compiler_params=_params(("arbitrary", "arbitrary"), 48),
        name="attention",
    )(*args)


def _even_out_kernel(first_ref, last_ref, rows_ref, c_ref, cp_ref, cn_ref, a_ref, x_ref, g_ref, cw_ref, w_ref,
                     lg_ref, lb_ref, o_ref, *, alpha):
    del rows_ref
    i = pl.program_id(0)
    tm = c_ref.shape[0]
    cwid = CONV_WIDTH
    c = c_ref[...]
    gate_b, z = c[:, :cwid], c[:, cwid:2 * cwid] * c[:, 2 * cwid:]
    cp = cp_ref[...]
    cn = cn_ref[...]
    zp = cp[SUBLANE - 1:SUBLANE, cwid:2 * cwid] * cp[SUBLANE - 1:SUBLANE, 2 * cwid:]
    zn = cn[0:1, cwid:2 * cwid] * cn[0:1, 2 * cwid:]
    zp = zp * (1 - first_ref[i]).astype(F32)
    zn = zn * (1 - last_ref[i]).astype(F32)
    row = lax.broadcasted_iota(jnp.int32, (tm, cwid), 0)
    z_prev = jnp.where(row == 0, zp, pltpu.roll(z, 1, 0))
    z_next = jnp.where(row == tm - 1, zn, pltpu.roll(z, tm - 1, 0))
    cw = cw_ref[...]
    y = gate_b * (cw[0:1] * z_prev + cw[1:2] * z + cw[2:3] * z_next)
    out = _dot(y.astype(BF16), w_ref[:cwid, :]) + _dot(a_ref[...], w_ref[cwid:, :])
    o_ref[...] = _layer_norm(alpha * x_ref[...] + g_ref[0] * out, lg_ref[...], lb_ref[...])


def _even_out(conv, attn, x, g1, conv_w, w_out, ln_g, ln_b, first, last, rows, alpha):
    n, d = x.shape
    tm = TM_PROJ
    cw3 = conv.shape[1]
    aw = attn.shape[1]
    hb = tm // SUBLANE
    nblk8 = n // SUBLANE
    tok = lambda i, f, l, r: (i, 0)
    const = lambda i, f, l, r: (0, 0)
    return pl.pallas_call(
        functools.partial(_even_out_kernel, alpha=alpha),
        out_shape=jax.ShapeDtypeStruct((n, d), F32),
        grid_spec=pltpu.PrefetchScalarGridSpec(
            num_scalar_prefetch=3, grid=(n // tm,),
            in_specs=[pl.BlockSpec((tm, cw3), tok),
                      pl.BlockSpec((SUBLANE, cw3), lambda i, f, l, r: (jnp.maximum(i * hb - 1, 0), 0)),
                      pl.BlockSpec((SUBLANE, cw3), lambda i, f, l, r: (jnp.minimum((i + 1) * hb, nblk8 - 1), 0)),
                      pl.BlockSpec((tm, aw), tok), pl.BlockSpec((tm, d), tok),
                      pl.BlockSpec((1, 1, d), lambda i, f, l, r: (r[i], 0, 0)),
                      pl.BlockSpec((SUBLANE, CONV_WIDTH), const),
                      pl.BlockSpec((CONV_WIDTH + aw, d), const),
                      pl.BlockSpec((1, d), const), pl.BlockSpec((1, d), const)],
            out_specs=pl.BlockSpec((tm, d), tok)),
        compiler_params=_params(("arbitrary",), 40),
        name="even_out_proj",
    )(first, last, rows, conv, conv, conv, attn, x, g1, conv_w, w_out, ln_g, ln_b)


def _ffn_kernel(rows_ref, x_ref, sc_ref, sh_ref, g_ref, wg_ref, wu_ref, wd_ref, lg_ref, lb_ref, o_ref,
                h_s, acc_s, *, alpha):
    del rows_ref
    f = pl.program_id(1)

    @pl.when(f == 0)
    def _():
        h_s[...] = (x_ref[...] * (1.0 + sc_ref[0]) + sh_ref[0]).astype(BF16)

    h = h_s[...]
    a = (_silu(_dot(h, wg_ref[...])) * _dot(h, wu_ref[...])).astype(BF16)
    y = _dot(a, wd_ref[...])

    @pl.when(f == 0)
    def _():
        acc_s[...] = y

    @pl.when(f > 0)
    def _():
        acc_s[...] += y

    @pl.when(f == pl.num_programs(1) - 1)
    def _():
        o_ref[...] = _layer_norm(alpha * x_ref[...] + g_ref[0] * acc_s[...], lg_ref[...], lb_ref[...])


def _ffn(x, sc, sh, g2, wg, wu, wd, ln_g, ln_b, rows, alpha):
    n, d = x.shape
    tm = TM_FFN
    ff = wg.shape[1]
    fc = FFN_CHUNK
    tok = lambda i, f, r: (i, 0)
    row3 = lambda i, f, r: (r[i], 0, 0)
    const = lambda i, f, r: (0, 0)
    return pl.pallas_call(
        functools.partial(_ffn_kernel, alpha=alpha),
        out_shape=jax.ShapeDtypeStruct((n, d), F32),
        grid_spec=pltpu.PrefetchScalarGridSpec(
            num_scalar_prefetch=1, grid=(n // tm, ff // fc),
            in_specs=[pl.BlockSpec((tm, d), tok),
                      pl.BlockSpec((1, 1, d), row3), pl.BlockSpec((1, 1, d), row3), pl.BlockSpec((1, 1, d), row3),
                      pl.BlockSpec((d, fc), lambda i, f, r: (0, f)), pl.BlockSpec((d, fc), lambda i, f, r: (0, f)),
                      pl.BlockSpec((fc, d), lambda i, f, r: (f, 0)),
                      pl.BlockSpec((1, d), const), pl.BlockSpec((1, d), const)],
            out_specs=pl.BlockSpec((tm, d), tok),
            scratch_shapes=[pltpu.VMEM((tm, d), BF16), pltpu.VMEM((tm, d), F32)]),
        compiler_params=_params(("arbitrary", "arbitrary"), 48),
        name="ffn_swiglu",
    )(rows, x, sc, sh, g2, wg, wu, wd, ln_g, ln_b)


def _odd_in_kernel(rows_ref, x_ref, sc_ref, sh_ref, w_ref, o_ref):
    del rows_ref
    h = (x_ref[...] * (1.0 + sc_ref[0]) + sh_ref[0]).astype(BF16)
    o_ref[...] = _dot(h, w_ref[...])


def _odd_in(x, sc, sh, w, rows):
    n, d = x.shape
    tm = TM_PROJ
    nw = w.shape[1]
    tok = lambda i, r: (i, 0)
    row3 = lambda i, r: (r[i], 0, 0)
    return pl.pallas_call(
        _odd_in_kernel,
        out_shape=jax.ShapeDtypeStruct((n, nw), F32),
        grid_spec=pltpu.PrefetchScalarGridSpec(
            num_scalar_prefetch=1, grid=(n // tm,),
            in_specs=[pl.BlockSpec((tm, d), tok), pl.BlockSpec((1, 1, d), row3), pl.BlockSpec((1, 1, d), row3),
                      pl.BlockSpec((d, nw), lambda i, r: (0, 0))],
            out_specs=pl.BlockSpec((tm, nw), tok)),
        compiler_params=_params(("arbitrary",), 32),
        name="odd_in_proj",
    )(rows, x, sc, sh, w)


def _s5_kernel(u_ref, zin_ref, t_ref, zout_ref, al_ref, h0_ref, y_ref, fin_ref, z_s, hf_s, hb_s, *, nb, nc):
    u = u_ref[0]
    z_s[...] = _dot(u, zin_ref[0])
    ar = al_ref[0, 0:1, :]
    ai = al_ref[0, 1:2, :]
    fwd = lax.broadcasted_iota(jnp.int32, (nb, LANE), 1) < S5_STATE

    def step(k, carry):
        re, im = carry
        rf = pl.multiple_of(k * nb, SUBLANE)
        rb = pl.multiple_of((nc - 1 - k) * nb, SUBLANE)
        hf_s[pl.ds(rf, nb), :LANE] = re
        hf_s[pl.ds(rf, nb), LANE:] = im
        hb_s[pl.ds(rb, nb), :LANE] = re
        hb_s[pl.ds(rb, nb), LANE:] = im
        zr = jnp.where(fwd, z_s[pl.ds(rf, nb), :LANE], z_s[pl.ds(rb, nb), :LANE])
        zi = jnp.where(fwd, z_s[pl.ds(rf, nb), LANE:], z_s[pl.ds(rb, nb), LANE:])
        return ar * re - ai * im + zr, ar * im + ai * re + zi

    h0 = h0_ref[0]
    re, im = lax.fori_loop(0, nc, step, (h0[:, :LANE], h0[:, LANE:]))
    fin_ref[0, :, :LANE] = re
    fin_ref[0, :, LANE:] = im
    m = u.shape[0]
    col = lax.broadcasted_iota(jnp.int32, (m, 2 * LANE), 1)
    is_fwd = (col % LANE) < S5_STATE
    h_in = jnp.where(is_fwd, hf_s[...], hb_s[...]).astype(BF16)
    y_ref[0] = _dot(u, t_ref[0]) + _dot(h_in, zout_ref[0])


def _s5(u, mats, h0, nb, nc):
    zin, tmat, zout, al = mats
    g, m, w = u.shape
    blk = lambda i: (i, 0, 0)
    return pl.pallas_call(
        functools.partial(_s5_kernel, nb=nb, nc=nc),
        out_shape=(jax.ShapeDtypeStruct((g, m, w), F32), jax.ShapeDtypeStruct((g, nb, w), F32)),
        grid=(g,),
        in_specs=[pl.BlockSpec((1, m, w), blk), pl.BlockSpec((1, w, w), blk), pl.BlockSpec((1, w, w), blk),
                  pl.BlockSpec((1, w, w), blk), pl.BlockSpec((1, 2, LANE), blk), pl.BlockSpec((1, nb, w), blk)],
        out_specs=(pl.BlockSpec((1, m, w), blk), pl.BlockSpec((1, nb, w), blk)),
        scratch_shapes=[pltpu.VMEM((m, w), F32), pltpu.VMEM((m, w), F32), pltpu.VMEM((m, w), F32)],
        compiler_params=_params(("arbitrary",), 32),
        name="s5_scan",
    )(u, zin, tmat, zout, al, h0)


def _s5_matrices(a_re, a_im, log_dt, b_re, b_im, c_re, c_im):
    hp = lax.Precision.HIGHEST
    L = S5_CHUNK
    dt = jnp.exp(log_dt)[..., None]
    lam_re, lam_im = a_re * dt, a_im * dt

    def power(k):
        k = k[:, None, None, None]
        mag = jnp.exp(lam_re * k)
        return mag * jnp.cos(lam_im * k), mag * jnp.sin(lam_im * k)

    ab_re, ab_im = power(jnp.ones((1,), F32))
    ab_re, ab_im = ab_re[0], ab_im[0]
    num_re, num_im = ab_re - 1.0, ab_im
    den = a_re * a_re + a_im * a_im
    f_re = (num_re * a_re + num_im * a_im) / den
    f_im = (num_im * a_re - num_re * a_im) / den
    bb_re = f_re[..., None] * b_re - f_im[..., None] * b_im
    bb_im = f_re[..., None] * b_im + f_im[..., None] * b_re

    ks = jnp.arange(L + 1, dtype=F32)
    pw_re, pw_im = power(ks)

    def zin_dir(d, exps):
        pr, pi = pw_re[exps, d], pw_im[exps, d]
        w_re = pr[..., None] * bb_re[d][None] - pi[..., None] * bb_im[d][None]
        w_im = pr[..., None] * bb_im[d][None] + pi[..., None] * bb_re[d][None]
        to = lambda w: jnp.transpose(w, (1, 0, 3, 2)).reshape(S5_GROUPS, L * S5_GROUP, S5_STATE)
        return to(w_re), to(w_im)

    steps = np.arange(L)
    zf_re, zf_im = zin_dir(0, L - 1 - steps)
    zb_re, zb_im = zin_dir(1, steps)
    zin = jnp.concatenate([zf_re, zb_re, zf_im, zb_im], axis=-1)

    def zout_dir(d, exps):
        pr, pi = pw_re[exps, d], pw_im[exps, d]
        cr, ci = c_re[d], c_im[d]
        e_re = cr[None] * pr[:, :, None, :] - ci[None] * pi[:, :, None, :]
        e_im = cr[None] * pi[:, :, None, :] + ci[None] * pr[:, :, None, :]
        to = lambda e: jnp.transpose(e, (1, 3, 0, 2)).reshape(S5_GROUPS, S5_STATE, L * S5_GROUP)
        return to(e_re), to(-e_im)

    of_re, of_im = zout_dir(0, steps + 1)
    ob_re, ob_im = zout_dir(1, L - steps)
    zout = jnp.concatenate([of_re, ob_re, of_im, ob_im], axis=1)

    def taps(d):
        pr, pi = pw_re[:L, d], pw_im[:L, d]
        m_re = pr[..., None] * bb_re[d][None] - pi[..., None] * bb_im[d][None]
        m_im = pr[..., None] * bb_im[d][None] + pi[..., None] * bb_re[d][None]
        return (jnp.einsum('gpn,lgnq->lgpq', c_re[d], m_re, precision=hp)
                - jnp.einsum('gpn,lgnq->lgpq', c_im[d], m_im, precision=hp))

    kf, kb = taps(0), taps(1)
    s_idx = steps[:, None]
    t_idx = steps[None, :]
    lag_f = np.clip(t_idx - s_idx, 0, L - 1)
    lag_b = np.clip(s_idx - t_idx, 0, L - 1)
    m_f = jnp.asarray((t_idx >= s_idx).astype(np.float32))[:, :, None, None, None]
    m_b = jnp.asarray((s_idx >= t_idx).astype(np.float32))[:, :, None, None, None]
    tm = kf[lag_f] * m_f + kb[lag_b] * m_b
    tmat = jnp.transpose(tm, (2, 0, 4, 1, 3)).reshape(S5_GROUPS, L * S5_GROUP, L * S5_GROUP)

    al = jnp.stack([jnp.concatenate([pw_re[L, 0], pw_re[L, 1]], axis=-1),
                    jnp.concatenate([pw_im[L, 0], pw_im[L, 1]], axis=-1)], axis=1)
    return zin.astype(BF16), tmat.astype(BF16), zout.astype(BF16), al


def _s5_pass(u, mats, h0, b, s):
    L = S5_CHUNK
    nc = s // L
    nb = -(-b // SUBLANE) * SUBLANE
    ug = u.reshape(b, nc, L, S5_GROUPS, S5_GROUP).transpose(3, 1, 0, 2, 4)
    if nb != b:
        ug = jnp.pad(ug, ((0, 0), (0, 0), (0, nb - b), (0, 0), (0, 0)))
        h0 = jnp.pad(h0, ((0, 0), (0, nb - b), (0, 0)))
    ug = ug.reshape(S5_GROUPS, nc * nb, L * S5_GROUP).astype(BF16)
    y, fin = _s5(ug, mats, h0, nb, nc)
    y = y.reshape(S5_GROUPS, nc, nb, L, S5_GROUP)[:, :, :b].transpose(2, 1, 3, 0, 4).reshape(b * s, S5_WIDTH)
    return y, fin[:, :b]


def _mla_kv(ckv, kpe, wk_ref, wv_ref, pl_ref, k_ref, v_ref):
    cb = ckv.astype(BF16)
    k_ref[...] = (_dot(cb, wk_ref[...]) + _dot(kpe.astype(BF16), pl_ref[...])).astype(BF16)
    v_ref[...] = _dot(cb, wv_ref[...]).astype(BF16)


def _mla_prep_kernel(rblk_ref, p_ref, qg_ref, kvg_ref, wuq_ref, wk_ref, wv_ref, pl_ref, cos_ref, sa_ref, sb_ref,
                     q_ref, k_ref, v_ref, ckv_ref):
    del rblk_ref
    pr = p_ref[...]
    cos, sa, sb = cos_ref[...], sa_ref[...], sb_ref[...]
    quarter = MLA_ROPE // 4
    scale = (MLA_NOPE + MLA_ROPE) ** -0.5
    qn = _rms(pr[:, :MLA_Q_RANK], qg_ref[...], MLA_Q_RANK).astype(BF16)
    q = _dot(qn, wuq_ref[...])
    for hd in range(MLA_HEADS):
        sl = slice(hd * LANE, (hd + 1) * LANE)
        q_ref[:, sl] = (_rope(q[:, sl], cos, sa, sb, quarter) * scale).astype(BF16)
    ckv = _rms(pr[:, MLA_Q_RANK:MLA_Q_RANK + MLA_KV_RANK], kvg_ref[...], MLA_KV_RANK)
    ckv_ref[...] = ckv
    kpe = _rope(pr[:, MLA_Q_RANK + MLA_KV_RANK:], cos, sa, sb, quarter)
    _mla_kv(ckv, kpe, wk_ref, wv_ref, pl_ref, k_ref, v_ref)


def _mla_prep(proj, qg, kvg, wuq, wk, wv, place, tables, rblk):
    n = proj.shape[0]
    tm = TM_PROJ
    hw = MLA_HEADS * LANE
    pw = MLA_Q_RANK + MLA_KV_RANK + LANE
    cos, sa, sb = tables
    tok = lambda i, r: (i, 0)
    const = lambda i, r: (0, 0)
    tab = lambda i, r: (r[i], 0)
    return pl.pallas_call(
        _mla_prep_kernel,
        out_shape=(jax.ShapeDtypeStruct((n, hw), BF16), jax.ShapeDtypeStruct((n, hw), BF16),
                   jax.ShapeDtypeStruct((n, hw), BF16), jax.ShapeDtypeStruct((n, MLA_KV_RANK), F32)),
        grid_spec=pltpu.PrefetchScalarGridSpec(
            num_scalar_prefetch=1, grid=(n // tm,),
            in_specs=[pl.BlockSpec((tm, pw), lambda i, r: (i, 1)),
                      pl.BlockSpec((1, MLA_Q_RANK), const), pl.BlockSpec((1, MLA_KV_RANK), const),
                      pl.BlockSpec((MLA_Q_RANK, hw), const), pl.BlockSpec((MLA_KV_RANK, hw), const),
                      pl.BlockSpec((MLA_KV_RANK, hw), const), pl.BlockSpec((LANE, hw), const),
                      pl.BlockSpec((tm, LANE), tab), pl.BlockSpec((tm, LANE), tab), pl.BlockSpec((tm, LANE), tab)],
            out_specs=[pl.BlockSpec((tm, hw), tok), pl.BlockSpec((tm, hw), tok), pl.BlockSpec((tm, hw), tok),
                       pl.BlockSpec((tm, MLA_KV_RANK), tok)]),
        compiler_params=_params(("arbitrary",), 32),
        name="mla_prep",
    )(rblk, proj, qg, kvg, wuq, wk, wv, place, cos, sa, sb)


def _mla_cache_kernel(c_ref, p_ref, wk_ref, wv_ref, pl_ref, k_ref, v_ref):
    _mla_kv(c_ref[...], p_ref[...], wk_ref, wv_ref, pl_ref, k_ref, v_ref)


def _mla_cache(ckv, kpe, wk, wv, place):
    n = ckv.shape[0]
    tm = TM_PROJ
    hw = MLA_HEADS * LANE
    tok = lambda i: (i, 0)
    const = lambda i: (0, 0)
    return pl.pallas_call(
        _mla_cache_kernel,
        out_shape=(jax.ShapeDtypeStruct((n, hw), BF16), jax.ShapeDtypeStruct((n, hw), BF16)),
        grid=(n // tm,),
        in_specs=[pl.BlockSpec((tm, MLA_KV_RANK), tok), pl.BlockSpec((tm, LANE), tok),
                  pl.BlockSpec((MLA_KV_RANK, hw), const), pl.BlockSpec((MLA_KV_RANK, hw), const),
                  pl.BlockSpec((LANE, hw), const)],
        out_specs=[pl.BlockSpec((tm, hw), tok), pl.BlockSpec((tm, hw), tok)],
        compiler_params=_params(("arbitrary",), 32),
        name="mla_cache_kv",
    )(ckv, kpe, wk, wv, place)


def _gelu_tanh(x):
    return 0.5 * x * (1.0 + jnp.tanh(math.sqrt(2.0 / math.pi) * (x + 0.044715 * (x * x * x))))


def _odd_out_kernel(rows_ref, u_ref, ys_ref, a_ref, x_ref, g_ref, d_ref, wglu_ref, bglu_ref, w_ref, lg_ref, lb_ref,
                    o_ref, *, alpha):
    del rows_ref
    y = _gelu_tanh(u_ref[...] * d_ref[...] + ys_ref[...])
    y = y * jax.nn.sigmoid(_dot(y.astype(BF16), wglu_ref[...]) + bglu_ref[...])
    out = _dot(y.astype(BF16), w_ref[:S5_WIDTH, :]) + _dot(a_ref[...], w_ref[S5_WIDTH:, :])
    o_ref[...] = _layer_norm(alpha * x_ref[...] + g_ref[0] * out, lg_ref[...], lb_ref[...])


def _odd_out(proj, y_ssm, attn, x, g1, s5_d, w_glu, b_glu, w_out, ln_g, ln_b, rows, alpha):
    n, d = x.shape
    tm = TM_PROJ
    aw = attn.shape[1]
    tok = lambda i, r: (i, 0)
    const = lambda i, r: (0, 0)
    return pl.pallas_call(
        functools.partial(_odd_out_kernel, alpha=alpha),
        out_shape=jax.ShapeDtypeStruct((n, d), F32),
        grid_spec=pltpu.PrefetchScalarGridSpec(
            num_scalar_prefetch=1, grid=(n // tm,),
            in_specs=[pl.BlockSpec((tm, S5_WIDTH), tok), pl.BlockSpec((tm, S5_WIDTH), tok),
                      pl.BlockSpec((tm, aw), tok), pl.BlockSpec((tm, d), tok),
                      pl.BlockSpec((1, 1, d), lambda i, r: (r[i], 0, 0)),
                      pl.BlockSpec((1, S5_WIDTH), const), pl.BlockSpec((S5_WIDTH, S5_WIDTH), const),
                      pl.BlockSpec((1, S5_WIDTH), const), pl.BlockSpec((S5_WIDTH + aw, d), const),
                      pl.BlockSpec((1, d), const), pl.BlockSpec((1, d), const)],
            out_specs=pl.BlockSpec((tm, d), tok)),
        compiler_params=_params(("arbitrary",), 40),
        name="odd_out_proj",
    )(rows, proj, y_ssm, attn, x, g1, s5_d, w_glu, b_glu, w_out, ln_g, ln_b)


def _router_kernel(rows_ref, x_ref, sc_ref, sh_ref, rt_ref, hb_ref, pos_ref, gate_ref, pt_ref, cnt_ref, tri_s):
    del rows_ref
    t = x_ref.shape[0]
    ne = N_EXPERTS

    @pl.when(pl.program_id(0) == 0)
    def _():
        before = lax.broadcasted_iota(jnp.int32, (t, t), 0) < lax.broadcasted_iota(jnp.int32, (t, t), 1)
        tri_s[...] = jnp.where(before, 1.0, 0.0).astype(BF16)

    h = x_ref[...] * (1.0 + sc_ref[0]) + sh_ref[0]
    h_hi, h_lo = _split(h)
    hb_ref[...] = h_hi
    r_hi, r_lo = _split(rt_ref[...])
    logits = _dot_nt(r_hi, h_hi) + _dot_nt(r_lo, h_hi) + _dot_nt(r_hi, h_lo)
    eid = lax.broadcasted_iota(jnp.int32, (ne, t), 0).astype(F32)
    m0 = jnp.max(logits, axis=0, keepdims=True)
    i0 = jnp.min(jnp.where(logits == m0, eid, float(ne)), axis=0, keepdims=True)
    rest = jnp.where(eid == i0, -jnp.inf, logits)
    m1 = jnp.max(rest, axis=0, keepdims=True)
    i1 = jnp.min(jnp.where(rest == m1, eid, float(ne)), axis=0, keepdims=True)
    ex = jnp.exp(m1 - m0)
    g0 = 1.0 / (1.0 + ex)
    g1 = ex / (1.0 + ex)
    sel0 = eid == i0
    sel1 = eid == i1
    member = jnp.where(sel0, 1.0, jnp.where(sel1, 1.0, 0.0))
    gate = jnp.where(sel0, g0, jnp.where(sel1, g1, 0.0))
    rank = _dot(member.astype(BF16), tri_s[...])
    pos = jnp.where(member > 0.0, rank, -1.0)
    pos_ref[0] = pos.astype(jnp.int32)
    gate_ref[0] = gate
    cnt = jnp.sum(member, axis=1, keepdims=True)
    cnt_ref[0] = jnp.broadcast_to(cnt, (ne, LANE)).astype(jnp.int32)
    packed = jnp.concatenate([pos, gate, jnp.zeros((LANE - 2 * ne, t), F32)], axis=0)
    pt_ref[...] = packed.T


def _router(x, sc, sh, router_t, rows):
    n, d = x.shape
    t = T_MOE
    nb = n // t
    ne = N_EXPERTS
    tok = lambda i, r: (i, 0)
    row3 = lambda i, r: (r[i], 0, 0)
    blk3 = lambda i, r: (i, 0, 0)
    return pl.pallas_call(
        _router_kernel,
        out_shape=(jax.ShapeDtypeStruct((n, d), BF16), jax.ShapeDtypeStruct((nb, ne, t), jnp.int32),
                   jax.ShapeDtypeStruct((nb, ne, t), F32),
                   jax.ShapeDtypeStruct((n, LANE), F32), jax.ShapeDtypeStruct((nb, ne, LANE), jnp.int32)),
        grid_spec=pltpu.PrefetchScalarGridSpec(
            num_scalar_prefetch=1, grid=(nb,),
            in_specs=[pl.BlockSpec((t, d), tok), pl.BlockSpec((1, 1, d), row3), pl.BlockSpec((1, 1, d), row3),
                      pl.BlockSpec((ne, d), lambda i, r: (0, 0))],
            out_specs=[pl.BlockSpec((t, d), tok), pl.BlockSpec((1, ne, t), blk3), pl.BlockSpec((1, ne, t), blk3),
                       pl.BlockSpec((t, LANE), tok), pl.BlockSpec((1, ne, LANE), blk3)],
            scratch_shapes=[pltpu.VMEM((t, t), BF16)]),
        compiler_params=_params(("arbitrary",), 48),
        name="moe_router",
    )(rows, x, sc, sh, router_t)


def _segment_copies(rows, src, dst, src0, dst0, sems):
    out = []
    for k, sz in enumerate(SEG_SIZES):
        off = (rows // (2 * sz)) * (2 * sz)
        s0 = pl.multiple_of(src0 + off, SEG_ALIGN)
        d0 = pl.multiple_of(dst0 + off, SEG_ALIGN)
        out.append(((rows & sz) != 0,
                    pltpu.make_async_copy(src.at[pl.ds(s0, sz)], dst.at[pl.ds(d0, sz)], sems.at[k])))
    return out


def _run_copies(copies):
    for pred, cp in copies:
        pl.when(pred)(cp.start)
    for pred, cp in copies:
        pl.when(pred)(cp.wait)


def _moe_gather_kernel(cnt_ref, base_ref, h_ref, pos_ref, gate_ref, xs_in, gs_in, xs_ref, gs_ref,
                       xe_s, ge_s, sem_x, sem_g):
    del xs_in, gs_in
    b, e = pl.program_id(0), pl.program_id(1)
    t = h_ref.shape[0]
    rt = ROW_MOE
    cnt = cnt_ref[b * N_EXPERTS + e]
    base = base_ref[b * N_EXPERTS + e]
    pos_row = pos_ref[0, pl.ds(e, 1), :]
    gate_row = gate_ref[0, pl.ds(e, 1), :]

    def gather(r, c):
        r0 = pl.multiple_of(r * rt, rt)
        hit = (lax.broadcasted_iota(jnp.int32, (rt, t), 0) + r0) == pos_row
        onehot = jnp.where(hit, 1.0, 0.0).astype(BF16)
        xe_s[pl.ds(r0, rt), :] = _dot(onehot, h_ref[...]).astype(BF16)
        g = jnp.sum(jnp.where(hit, gate_row, 0.0), axis=1, keepdims=True)
        ge_s[pl.ds(r0, rt), :] = jnp.broadcast_to(g, (rt, LANE))
        return c

    lax.fori_loop(0, (cnt + rt - 1) // rt, gather, 0)
    rows = ((cnt + SEG_ALIGN - 1) // SEG_ALIGN) * SEG_ALIGN
    _run_copies(_segment_copies(rows, xe_s, xs_ref, 0, base, sem_x)
                + _segment_copies(rows, ge_s, gs_ref, 0, base, sem_g))


def _moe_gather(h, pos, gate, counts, base, n_rows):
    n, d = h.shape
    t = T_MOE
    nb = n // t
    ne = N_EXPERTS
    tok = lambda b, e, c, o: (b, 0)
    blk3 = lambda b, e, c, o: (b, 0, 0)
    any_spec = pl.BlockSpec(memory_space=pl.ANY)
    xs0 = jnp.zeros((n_rows, d), BF16)
    gs0 = jnp.zeros((n_rows, LANE), F32)
    return pl.pallas_call(
        _moe_gather_kernel,
        out_shape=(jax.ShapeDtypeStruct((n_rows, d), BF16), jax.ShapeDtypeStruct((n_rows, LANE), F32)),
        grid_spec=pltpu.PrefetchScalarGridSpec(
            num_scalar_prefetch=2, grid=(nb, ne),
            in_specs=[pl.BlockSpec((t, d), tok), pl.BlockSpec((1, ne, t), blk3), pl.BlockSpec((1, ne, t), blk3),
                      any_spec, any_spec],
            out_specs=[any_spec, any_spec],
            scratch_shapes=[pltpu.VMEM((t, d), BF16), pltpu.VMEM((t, LANE), F32),
                            pltpu.SemaphoreType.DMA((len(SEG_SIZES),)), pltpu.SemaphoreType.DMA((len(SEG_SIZES),))]),
        input_output_aliases={5: 0, 6: 1},
        compiler_params=_params(("arbitrary", "arbitrary"), 32),
        name="moe_gather",
    )(counts, base, h, pos, gate, xs0, gs0)


def _moe_gmm_kernel(te_ref, valid_ref, x_ref, gs_ref, wg_ref, wu_ref, wd_ref, y_ref, acc_s):
    del te_ref
    i, f = pl.program_id(0), pl.program_id(1)
    last = pl.num_programs(1) - 1
    valid = valid_ref[i] != 0

    @pl.when(valid)
    def _():
        x = x_ref[...]
        a = (_silu(_dot(x, wg_ref[0])) * _dot(x, wu_ref[0])).astype(BF16)
        y = _dot(a, wd_ref[0])

        @pl.when(f == 0)
        def _():
            acc_s[...] = y

        @pl.when(f > 0)
        def _():
            acc_s[...] += y

        @pl.when(f == last)
        def _():
            y_ref[...] = (acc_s[...] * gs_ref[:, 0:1]).astype(BF16)

    @pl.when(jnp.logical_not(valid) & (f == last))
    def _():
        y_ref[...] = jnp.zeros_like(y_ref)


def _moe_gmm(xs, gs, tile_expert, tile_valid, wg, wu, wd):
    n_rows, d = xs.shape
    rt = RT_GMM
    ne, _, ff = wg.shape
    fc = MOE_CHUNK
    tok = lambda i, f, te, tv: (i, 0)
    return pl.pallas_call(
        _moe_gmm_kernel,
        out_shape=jax.ShapeDtypeStruct((n_rows, d), BF16),
        grid_spec=pltpu.PrefetchScalarGridSpec(
            num_scalar_prefetch=2, grid=(n_rows // rt, ff // fc),
            in_specs=[pl.BlockSpec((rt, d), tok), pl.BlockSpec((rt, LANE), tok),
                      pl.BlockSpec((1, d, fc), lambda i, f, te, tv: (te[i], 0, f)),
                      pl.BlockSpec((1, d, fc), lambda i, f, te, tv: (te[i], 0, f)),
                      pl.BlockSpec((1, fc, d), lambda i, f, te, tv: (te[i], f, 0))],
            out_specs=pl.BlockSpec((rt, d), tok),
            scratch_shapes=[pltpu.VMEM((rt, d), F32)]),
        compiler_params=_params(("arbitrary", "arbitrary"), 40),
        name="moe_gmm",
    )(tile_expert, tile_valid, xs, gs, wg, wu, wd)


def _moe_combine_kernel(cnt_ref, base_ref, rows_ref, ys_ref, pt_ref, x_ref, g_ref, lg_ref, lb_ref, o_ref,
                        ycat_s, sems, *, alpha):
    del rows_ref
    b = pl.program_id(0)
    t = x_ref.shape[0]
    cap = ycat_s.shape[0]
    ne = N_EXPERTS

    @pl.when(b == 0)
    def _():
        ycat_s[...] = jnp.zeros_like(ycat_s)

    copies = []
    offs = []
    off = 0
    for e in range(ne):
        cnt = cnt_ref[b * ne + e]
        rows = ((cnt + SEG_ALIGN - 1) // SEG_ALIGN) * SEG_ALIGN
        copies += _segment_copies(rows, ys_ref, ycat_s, base_ref[b * ne + e], off, sems.at[e])
        offs.append(off)
        off = off + rows
    _run_copies(copies)

    pt = pt_ref[...]
    lane = lax.broadcasted_iota(jnp.int32, (1, LANE), 1)
    off_lane = jnp.zeros((1, LANE), F32)
    for e in range(1, ne):
        off_lane = jnp.where(lane == e, offs[e].astype(F32), off_lane)
    chosen = (pt >= 0.0) & (lax.broadcasted_iota(jnp.int32, (t, LANE), 1) < ne)
    slot = pt + off_lane
    s_lo = jnp.min(jnp.where(chosen, slot, float(cap)), axis=1, keepdims=True)
    s_hi = jnp.max(jnp.where(chosen, slot, -1.0), axis=1, keepdims=True)
    n_piece = 3
    wp = cap // n_piece
    f = None
    for j in range(n_piece):
        col = (lax.broadcasted_iota(jnp.int32, (t, wp), 1) + j * wp).astype(F32)
        sel = jnp.where(col == s_lo, 1.0, jnp.where(col == s_hi, 1.0, 0.0)).astype(BF16)
        part = _dot(sel, ycat_s[j * wp:(j + 1) * wp, :])
        f = part if f is None else f + part
    o_ref[...] = _layer_norm(alpha * x_ref[...] + g_ref[0] * f, lg_ref[...], lb_ref[...])


def _moe_combine(ys, pt, x, g2, ln_g, ln_b, counts, base, rows, alpha):
    n, d = x.shape
    t = T_MOE
    ne = N_EXPERTS
    cap = -(-(2 * t + ne * (SEG_ALIGN - 1)) // (6 * LANE)) * (6 * LANE)
    tok = lambda b, c, o, r: (b, 0)
    const = lambda b, c, o, r: (0, 0)
    return pl.pallas_call(
        functools.partial(_moe_combine_kernel, alpha=alpha),
        out_shape=jax.ShapeDtypeStruct((n, d), F32),
        grid_spec=pltpu.PrefetchScalarGridSpec(
            num_scalar_prefetch=3, grid=(n // t,),
            in_specs=[pl.BlockSpec(memory_space=pl.ANY), pl.BlockSpec((t, LANE), tok), pl.BlockSpec((t, d), tok),
                      pl.BlockSpec((1, 1, d), lambda b, c, o, r: (r[b], 0, 0)),
                      pl.BlockSpec((1, d), const), pl.BlockSpec((1, d), const)],
            out_specs=pl.BlockSpec((t, d), tok),
            scratch_shapes=[pltpu.VMEM((cap, d), BF16), pltpu.SemaphoreType.DMA((ne, len(SEG_SIZES)))]),
        compiler_params=_params(("arbitrary",), 48),
        name="moe_combine",
    )(counts, base, rows, ys, pt, x, g2, ln_g, ln_b)


def _moe_layout(cnt, n_tiles):
    cnt16 = (cnt + SEG_ALIGN - 1) // SEG_ALIGN * SEG_ALIGN
    rows_e = jnp.sum(cnt16, axis=0)
    rows_e = (rows_e + RT_GMM - 1) // RT_GMM * RT_GMM
    ends = jnp.cumsum(rows_e)
    base = (ends - rows_e)[None, :] + jnp.cumsum(cnt16, axis=0) - cnt16
    tile_end = ends // RT_GMM
    idx = jnp.arange(n_tiles, dtype=jnp.int32)
    valid = idx < tile_end[-1]
    te = jnp.sum((idx[:, None] >= tile_end[None, :]).astype(jnp.int32), axis=1)
    last = jnp.sum(((tile_end[-1] - 1) >= tile_end).astype(jnp.int32))
    te = jnp.where(valid, te, last)
    return base.reshape(-1).astype(jnp.int32), te.astype(jnp.int32), valid.astype(jnp.int32)


def _tile_rows(n, p, ss, tm):
    starts = np.arange(n // tm) * tm
    return jnp.asarray(np.where(starts < p, 0, 1 + (starts - p) // ss).astype(np.int32))


def _tile_rope_blocks(n, p, ss, tm):
    starts = np.arange(n // tm) * tm
    return jnp.asarray(np.where(starts < p, 0, 1 + ((starts - p) % ss) // tm).astype(np.int32))


def _tile_edges(n, p, sp, ss, tm):
    starts = np.arange(n // tm) * tm
    pos = np.where(starts < p, starts % sp, (starts - p) % ss)
    seq = np.where(starts < p, sp, ss)
    return jnp.asarray((pos == 0).astype(np.int32)), jnp.asarray((pos + tm == seq).astype(np.int32))


def _rope_tables(n_tokens, rot_dim, lane0, tm):
    rows = n_tokens // GRID_W
    row_pos = jnp.repeat(jnp.arange(rows, dtype=F32), GRID_W)
    col_pos = jnp.tile(jnp.arange(GRID_W, dtype=F32), rows)
    half = rot_dim // 2
    qtr = half // 2
    inv_freq = ROPE_THETA ** (-jnp.arange(0, half, 2, dtype=F32) / half)
    ang_r = row_pos[:, None] * inv_freq
    ang_c = col_pos[:, None] * inv_freq
    ang = jnp.concatenate([ang_r, ang_r, ang_c, ang_c], axis=-1)
    cos, sin = jnp.cos(ang), jnp.sin(ang)
    first = ((np.arange(rot_dim) % half) < qtr).astype(np.float32)
    sa = -sin * first
    sb = sin * (1.0 - first)

    def place(tbl, fill):
        full = jnp.full((n_tokens, LANE), fill, F32).at[:, lane0:lane0 + rot_dim].set(tbl)
        return jnp.concatenate([jnp.full((tm, LANE), fill, F32), full], axis=0)

    return place(cos, 1.0), place(sa, 0.0), place(sb, 0.0)


def _pad_heads(w, n_heads, width):
    lead = w.shape[:-1]
    w = w.reshape(lead + (n_heads, width))
    w = jnp.pad(w, [(0, 0)] * len(lead) + [(0, 0), (0, LANE - width)])
    return w.reshape(lead + (n_heads * LANE,))


def _pad_head_rows(w, n_heads, width):
    d = w.shape[-1]
    w = w.reshape(n_heads, width, d)
    return jnp.pad(w, ((0, 0), (0, LANE - width), (0, 0))).reshape(n_heads * LANE, d)


def kernel(x_prompt, x_sample, cache_attn_k, cache_attn_v, cache_mla_ckv, cache_mla_kpe, state_ssm_re, state_ssm_im, c, c_ctx, ada_w, ada_b, ln_g, ln_b, ev_w_in, ev_conv_w, ev_q_gain, ev_k_gain, ev_w_out, ffn_w_gate, ffn_w_up, ffn_w_down, od_w_in, s5_a_re, s5_a_im, s5_log_dt, s5_b_re, s5_b_im, s5_c_re, s5_c_im, s5_d, s5_w_glu, s5_b_glu, mla_q_gain, mla_w_uq, mla_kv_gain, mla_w_ukv, od_w_out, moe_router, moe_w_gate, moe_w_up, moe_w_down):
    bp, sp, d = x_prompt.shape
    bs, ss, _ = x_sample.shape
    depth = ada_w.shape[0]
    alpha = (2 * depth) ** 0.25
    p = bp * sp
    n = p + bs * ss
    past = cache_attn_k.shape[2]
    for tm in (TM_PROJ, TM_FFN, T_MOE):
        assert p % tm == 0 and ss % tm == 0 and (sp % tm == 0 or tm % sp == 0)
    assert sp % TM_PROJ == 0 and sp % S5_CHUNK == 0 and ss % S5_CHUNK == 0

    x = jnp.concatenate([x_prompt.reshape(p, d), x_sample.reshape(bs * ss, d)], axis=0)

    nrow = -(-(1 + bs) // SUBLANE) * SUBLANE
    cond = jnp.zeros((nrow, d), F32).at[0].set(c_ctx).at[1:1 + bs].set(c)
    mod = _modulation(cond, ada_w, ada_b)

    def mod_part(l, k):
        return mod[l, :, k * d:(k + 1) * d].reshape(nrow, 1, d)

    rows_proj = _tile_rows(n, p, ss, TM_PROJ)
    rows_ffn = _tile_rows(n, p, ss, TM_FFN)
    rows_moe = _tile_rows(n, p, ss, T_MOE)
    rblk = _tile_rope_blocks(n, p, ss, TM_PROJ)
    first, last = _tile_edges(n, p, sp, ss, TM_PROJ)
    tables_even = _rope_tables(ss, HEAD_DIM, 0, TM_PROJ)
    tables_mla = _rope_tables(ss, MLA_ROPE, MLA_NOPE, TM_PROJ)

    out_k, out_v, out_ckv, out_kpe, out_sre, out_sim = [], [], [], [], [], []
    cw = CONV_WIDTH
    hq = ATTN_HEADS * HEAD_DIM
    hkv = ATTN_KV_HEADS * HEAD_DIM
    for l in range(depth):
        i = l // 2
        sh1, sc1, g1, sh2, sc2, g2 = [mod_part(l, k) for k in range(6)]
        lg = ln_g[l].reshape(2, 1, d)
        lb = ln_b[l].reshape(2, 1, d)
        if l % 2 == 0:
            w_in = ev_w_in[i]
            wc = w_in[:, :3 * cw].astype(BF16)
            wq = _pad_heads(w_in[:, 3 * cw:3 * cw + hq], ATTN_HEADS, HEAD_DIM).astype(BF16)
            wkv = jnp.concatenate([_pad_heads(w_in[:, 3 * cw + hq:3 * cw + hq + hkv], ATTN_KV_HEADS, HEAD_DIM),
                                   _pad_heads(w_in[:, 3 * cw + hq + hkv:], ATTN_KV_HEADS, HEAD_DIM)],
                                  axis=1).astype(BF16)
            qg = jnp.pad(ev_q_gain[i], (0, LANE - HEAD_DIM)).reshape(1, LANE)
            kg = jnp.pad(ev_k_gain[i], (0, LANE - HEAD_DIM)).reshape(1, LANE)
            conv, q, k, v, kn, vf = _even_in(x, sc1, sh1, wc, wq, wkv, qg, kg, tables_even, rows_proj, rblk)
            kvw = ATTN_KV_HEADS * LANE
            out_k.append(kn[:p].reshape(bp, sp, ATTN_KV_HEADS, LANE)[..., :HEAD_DIM])
            out_v.append(vf[:p].reshape(bp, sp, ATTN_KV_HEADS, LANE)[..., :HEAD_DIM])
            group = ATTN_HEADS // ATTN_KV_HEADS
            qw = ATTN_HEADS * LANE
            a_p = _attention(q[:p].reshape(bp, sp, qw),
                             [(k[:p].reshape(bp, sp, kvw), v[:p].reshape(bp, sp, kvw))], ATTN_HEADS, group)
            kc = _pad_heads(cache_attn_k[:, i].reshape(bs, past, hkv), ATTN_KV_HEADS, HEAD_DIM).astype(BF16)
            vc = _pad_heads(cache_attn_v[:, i].reshape(bs, past, hkv), ATTN_KV_HEADS, HEAD_DIM).astype(BF16)
            a_s = _attention(q[p:].reshape(bs, ss, qw),
                             [(kc, vc), (k[p:].reshape(bs, ss, kvw), v[p:].reshape(bs, ss, kvw))],
                             ATTN_HEADS, group)
            attn = jnp.concatenate([a_p.reshape(p, qw), a_s.reshape(bs * ss, qw)], axis=0)
            w_out = jnp.concatenate([ev_w_out[i][:cw], _pad_head_rows(ev_w_out[i][cw:], ATTN_HEADS, HEAD_DIM)],
                                    axis=0).astype(BF16)
            conv_w = jnp.pad(ev_conv_w[i], ((0, SUBLANE - ev_conv_w.shape[1]), (0, 0)))
            x = _even_out(conv, attn, x, g1, conv_w, w_out, lg[0], lb[0], first, last, rows_proj, alpha)
            x = _ffn(x, sc2, sh2, g2, ffn_w_gate[i].astype(BF16), ffn_w_up[i].astype(BF16),
                     ffn_w_down[i].astype(BF16), lg[1], lb[1], rows_ffn, alpha)
        else:
            q_end = S5_WIDTH + MLA_Q_RANK
            kv_end = q_end + MLA_KV_RANK
            w_in = od_w_in[i]
            w_in = jnp.concatenate([w_in[:, :kv_end], jnp.zeros((d, MLA_NOPE), F32), w_in[:, kv_end:],
                                    jnp.zeros((d, LANE - MLA_NOPE - MLA_ROPE), F32)], axis=1).astype(BF16)
            proj = _odd_in(x, sc1, sh1, w_in, rows_proj)
            out_kpe.append(proj[:p, kv_end + MLA_NOPE:kv_end + MLA_NOPE + MLA_ROPE].reshape(bp, sp, MLA_ROPE))

            mats = _s5_matrices(s5_a_re[i], s5_a_im[i], s5_log_dt[i], s5_b_re[i], s5_b_im[i], s5_c_re[i], s5_c_im[i])
            u = proj[:, :S5_WIDTH]
            h0_p = jnp.zeros((S5_GROUPS, bp, 4 * S5_STATE), F32)
            y_p, fin_p = _s5_pass(u[:p], mats, h0_p, bp, sp)

            def pack_state(re, im):
                to = lambda a: a.transpose(2, 0, 1, 3).reshape(S5_GROUPS, a.shape[0], 2 * S5_STATE)
                return jnp.concatenate([to(re), to(im)], axis=-1)

            y_s, _ = _s5_pass(u[p:], mats, pack_state(state_ssm_re[:, i], state_ssm_im[:, i]), bs, ss)
            y_ssm = jnp.concatenate([y_p, y_s], axis=0)
            fin = fin_p.reshape(S5_GROUPS, bp, 2, 2, S5_STATE)
            out_sre.append(fin[:, :, 0].transpose(1, 2, 0, 3))
            out_sim.append(fin[:, :, 1].transpose(1, 2, 0, 3))

            qk = MLA_NOPE + MLA_ROPE
            wuq = _pad_heads(mla_w_uq[i], MLA_HEADS, qk).astype(BF16)
            wukv = mla_w_ukv[i].reshape(MLA_KV_RANK, MLA_HEADS, MLA_NOPE + MLA_V)
            wk = _pad_heads(wukv[..., :MLA_NOPE].reshape(MLA_KV_RANK, -1), MLA_HEADS, MLA_NOPE).astype(BF16)
            wv = _pad_heads(wukv[..., MLA_NOPE:].reshape(MLA_KV_RANK, -1), MLA_HEADS, MLA_V).astype(BF16)
            place = np.zeros((LANE, MLA_HEADS * LANE), np.float32)
            for hd in range(MLA_HEADS):
                for j in range(MLA_ROPE):
                    place[MLA_NOPE + j, hd * LANE + MLA_NOPE + j] = 1.0
            place = jnp.asarray(place, BF16)
            q, k, v, ckv = _mla_prep(proj, mla_q_gain[i].reshape(1, -1), mla_kv_gain[i].reshape(1, -1),
                                     wuq, wk, wv, place, tables_mla, rblk)
            out_ckv.append(ckv[:p].reshape(bp, sp, MLA_KV_RANK))
            hw = MLA_HEADS * LANE
            a_p = _attention(q[:p].reshape(bp, sp, hw), [(k[:p].reshape(bp, sp, hw), v[:p].reshape(bp, sp, hw))],
                             MLA_HEADS, 1)
            kpe_c = jnp.pad(cache_mla_kpe[:, i].reshape(bs * past, MLA_ROPE),
                            ((0, 0), (MLA_NOPE, LANE - MLA_NOPE - MLA_ROPE)))
            kc, vc = _mla_cache(cache_mla_ckv[:, i].reshape(bs * past, MLA_KV_RANK), kpe_c, wk, wv, place)
            a_s = _attention(q[p:].reshape(bs, ss, hw),
                             [(kc.reshape(bs, past, hw), vc.reshape(bs, past, hw)),
                              (k[p:].reshape(bs, ss, hw), v[p:].reshape(bs, ss, hw))], MLA_HEADS, 1)
            attn = jnp.concatenate([a_p.reshape(p, hw), a_s.reshape(bs * ss, hw)], axis=0)
            w_out = jnp.concatenate([od_w_out[i][:S5_WIDTH], _pad_head_rows(od_w_out[i][S5_WIDTH:], MLA_HEADS, MLA_V)],
                                    axis=0).astype(BF16)
            x = _odd_out(proj, y_ssm, attn, x, g1, s5_d[i].reshape(1, -1), s5_w_glu[i].astype(BF16),
                         s5_b_glu[i].reshape(1, -1), w_out, lg[0], lb[0], rows_proj, alpha)

            hb, pos, gate, pt, cnt = _router(x, sc2, sh2, moe_router[i].T, rows_moe)
            cnt = cnt[:, :, 0]
            nblk = n // T_MOE
            n_tiles = -(-(2 * n + nblk * N_EXPERTS * (SEG_ALIGN - 1)) // RT_GMM) + N_EXPERTS
            base, tile_expert, tile_valid = _moe_layout(cnt, n_tiles)
            counts = cnt.reshape(-1)
            xs, gs = _moe_gather(hb, pos, gate, counts, base, n_tiles * RT_GMM)
            ys = _moe_gmm(xs, gs, tile_expert, tile_valid, moe_w_gate[i].astype(BF16), moe_w_up[i].astype(BF16),
                          moe_w_down[i].astype(BF16))
            x = _moe_combine(ys, pt, x, g2, lg[1], lb[1], counts, base, rows_moe, alpha)

    y_prompt = x[:p].reshape(bp, sp, d)
    y_sample = x[p:].reshape(bs, ss, d)
    return (y_prompt, y_sample, jnp.stack(out_k, axis=1), jnp.stack(out_v, axis=1),
            jnp.stack(out_ckv, axis=1), jnp.stack(out_kpe, axis=1),
            jnp.stack(out_sre, axis=1), jnp.stack(out_sim, axis=1))
```

```python
import functools
import math

import jax
import jax.numpy as jnp
import numpy as np
from jax import lax
from jax.experimental import pallas as pl
from jax.experimental.pallas import tpu as pltpu

F32 = jnp.float32
BF16 = jnp.bfloat16

LANE = 128
SUBLANE = 8
MIB = 1024 * 1024

GRID_W = 64
ROPE_THETA = 10000.0
LN_EPS = 1e-6
RMS_EPS = 1e-6
HEAD_DIM = 64
ATTN_HEADS = 8
ATTN_KV_HEADS = 2
CONV_WIDTH = 512
S5_WIDTH = 512
S5_GROUP = 16
S5_GROUPS = 32
S5_STATE = 64
S5_CHUNK = 16
MLA_HEADS = 8
MLA_Q_RANK = 256
MLA_KV_RANK = 128
MLA_NOPE = 64
MLA_ROPE = 32
MLA_V = 64
N_EXPERTS = 8

TM_PROJ = 256
TM_FFN = 512
FFN_CHUNK = 2816
T_MOE = 1024
ROW_MOE = 128
RT_GMM = 512
MOE_CHUNK = 1792
SEG_ALIGN = 16
SEG_SIZES = tuple(T_MOE >> k for k in range(T_MOE.bit_length()) if (T_MOE >> k) >= SEG_ALIGN)
TQ_ATTN = 256


def _params(sem, vmem_mib):
    return pltpu.CompilerParams(dimension_semantics=sem, vmem_limit_bytes=vmem_mib * MIB)


def _dot(a, b):
    return jnp.dot(a, b, preferred_element_type=F32)


def _dot_nt(a, b):
    return lax.dot_general(a, b, (((1,), (1,)), ((), ())), preferred_element_type=F32)


def _split(a):
    hi = a.astype(BF16)
    lo = (a - hi.astype(F32)).astype(BF16)
    return hi, lo


def _silu(x):
    return x * jax.nn.sigmoid(x)


def _layer_norm(r, g, b):
    mu = jnp.mean(r, axis=-1, keepdims=True)
    d = r - mu
    var = jnp.mean(d * d, axis=-1, keepdims=True)
    return d * lax.rsqrt(var + LN_EPS) * g + b


def _rms(x, g, n):
    ms = jnp.sum(x * x, axis=-1, keepdims=True) * (1.0 / n)
    return x * lax.rsqrt(ms + RMS_EPS) * g


def _rope(x, cos, sa, sb, q):
    w = x.shape[-1]
    return x * cos + pltpu.roll(x, w - q, 1) * sa + pltpu.roll(x, q, 1) * sb


def _ada_kernel(c_ref, w_ref, b_ref, o_ref):
    c = c_ref[...]
    a_hi, a_lo = _split(_silu(c))
    w_hi, w_lo = _split(w_ref[0])
    o_ref[0] = _dot(a_hi, w_hi) + _dot(a_lo, w_hi) + _dot(a_hi, w_lo) + b_ref[0]


def _modulation(cond, ada_w, ada_b):
    depth, d, d6 = ada_w.shape
    r = cond.shape[0]
    tn = 1536
    return pl.pallas_call(
        _ada_kernel,
        out_shape=jax.ShapeDtypeStruct((depth, r, d6), F32),
        grid=(depth, d6 // tn),
        in_specs=[pl.BlockSpec((r, d), lambda l, j: (0, 0)),
                  pl.BlockSpec((1, d, tn), lambda l, j: (l, 0, j)),
                  pl.BlockSpec((1, 1, tn), lambda l, j: (l, 0, j))],
        out_specs=pl.BlockSpec((1, r, tn), lambda l, j: (l, 0, j)),
        compiler_params=_params(("arbitrary", "arbitrary"), 40),
        name="ada_modulation",
    )(cond, ada_w, ada_b.reshape(depth, 1, d6))


def _even_in_kernel(rows_ref, rblk_ref, x_ref, sc_ref, sh_ref, wc_ref, wq_ref, wkv_ref, qg_ref, kg_ref,
                    cos_ref, sa_ref, sb_ref, conv_ref, q_ref, k_ref, v_ref, kn_ref, vf_ref):
    del rows_ref, rblk_ref
    h = (x_ref[...] * (1.0 + sc_ref[0]) + sh_ref[0]).astype(BF16)
    conv_ref[...] = _dot(h, wc_ref[...])
    cos, sa, sb = cos_ref[...], sa_ref[...], sb_ref[...]
    quarter = HEAD_DIM // 4
    q = _dot(h, wq_ref[...])
    for hd in range(ATTN_HEADS):
        sl = slice(hd * LANE, (hd + 1) * LANE)
        qn = _rms(q[:, sl], qg_ref[...], HEAD_DIM)
        q_ref[:, sl] = (_rope(qn, cos, sa, sb, quarter) * (HEAD_DIM ** -0.5)).astype(BF16)
    kv = _dot(h, wkv_ref[...])
    kw = ATTN_KV_HEADS * LANE
    for hd in range(ATTN_KV_HEADS):
        sl = slice(hd * LANE, (hd + 1) * LANE)
        kn = _rms(kv[:, sl], kg_ref[...], HEAD_DIM)
        kn_ref[:, sl] = kn
        k_ref[:, sl] = _rope(kn, cos, sa, sb, quarter).astype(BF16)
    v = kv[:, kw:]
    vf_ref[...] = v
    v_ref[...] = v.astype(BF16)


def _even_in(x, sc, sh, wc, wq, wkv, qg, kg, tables, rows, rblk):
    n, d = x.shape
    tm = TM_PROJ
    cw, qw, kvw = wc.shape[1], wq.shape[1], wkv.shape[1]
    kw = kvw // 2
    cos, sa, sb = tables
    row3 = lambda i, rows, rblk: (rows[i], 0, 0)
    tok = lambda i, rows, rblk: (i, 0)
    const = lambda i, rows, rblk: (0, 0)
    tab = lambda i, rows, rblk: (rblk[i], 0)
    return pl.pallas_call(
        _even_in_kernel,
        out_shape=(jax.ShapeDtypeStruct((n, cw), F32), jax.ShapeDtypeStruct((n, qw), BF16),
                   jax.ShapeDtypeStruct((n, kw), BF16), jax.ShapeDtypeStruct((n, kw), BF16),
                   jax.ShapeDtypeStruct((n, kw), F32), jax.ShapeDtypeStruct((n, kw), F32)),
        grid_spec=pltpu.PrefetchScalarGridSpec(
            num_scalar_prefetch=2, grid=(n // tm,),
            in_specs=[pl.BlockSpec((tm, d), tok),
                      pl.BlockSpec((1, 1, d), row3), pl.BlockSpec((1, 1, d), row3),
                      pl.BlockSpec((d, cw), const), pl.BlockSpec((d, qw), const), pl.BlockSpec((d, kvw), const),
                      pl.BlockSpec((1, LANE), const), pl.BlockSpec((1, LANE), const),
                      pl.BlockSpec((tm, LANE), tab), pl.BlockSpec((tm, LANE), tab), pl.BlockSpec((tm, LANE), tab)],
            out_specs=[pl.BlockSpec((tm, cw), tok), pl.BlockSpec((tm, qw), tok),
                       pl.BlockSpec((tm, kw), tok), pl.BlockSpec((tm, kw), tok),
                       pl.BlockSpec((tm, kw), tok), pl.BlockSpec((tm, kw), tok)]),
        compiler_params=_params(("arbitrary",), 40),
        name="even_in_proj",
    )(rows, rblk, x, sc, sh, wc, wq, wkv, qg, kg, cos, sa, sb)


def _attn_kernel(*refs, n_heads, group, n_seg):
    q_ref = refs[0]
    seg = refs[1:1 + 2 * n_seg]
    o_ref = refs[1 + 2 * n_seg]
    for hd in range(n_heads):
        sl = slice(hd * LANE, (hd + 1) * LANE)
        ks = slice((hd // group) * LANE, (hd // group + 1) * LANE)
        qh = q_ref[:, sl]
        scores = [_dot_nt(qh, seg[2 * s][:, ks]) for s in range(n_seg)]
        m = jnp.max(scores[0], axis=-1, keepdims=True)
        for s in range(1, n_seg):
            m = jnp.maximum(m, jnp.max(scores[s], axis=-1, keepdims=True))
        den = None
        acc = None
        for s in range(n_seg):
            p = jnp.exp(scores[s] - m)
            ps = jnp.sum(p, axis=-1, keepdims=True)
            pv = _dot(p.astype(BF16), seg[2 * s + 1][:, ks])
            den = ps if den is None else den + ps
            acc = pv if acc is None else acc + pv
        o_ref[:, sl] = (acc * (1.0 / den)).astype(BF16)


def _attention(q, segments, n_heads, group, b, s, row0):
    n, qw = q.shape
    assert n % s == 0 and row0 % s == 0
    b0 = row0 // s
    tq = min(TQ_ATTN, s)
    in_specs = [pl.BlockSpec((None, tq, qw), lambda i, j: (i + b0, j, 0))]
    args = [q.reshape(n // s, s, qw)]
    for k, v, own in segments:
        kw = k.shape[-1]
        if own:
            k, v = k.reshape(n // s, s, kw), v.reshape(n // s, s, kw)
            kmap = lambda i, j: (i + b0, 0, 0)
        else:
            kmap = lambda i, j: (i, 0, 0)
        t = k.shape[1]
        in_specs += [pl.BlockSpec((None, t, kw), kmap), pl.BlockSpec((None, t, kw), kmap)]
        args += [k, v]
    return pl.pallas_call(
        functools.partial(_attn_kernel, n_heads=n_heads, group=group, n_seg=len(segments)),
        out_shape=jax.ShapeDtypeStruct((b, s, qw), BF16),
        grid=(b, s // tq),
        in_specs=in_specs,
        out_specs=pl.BlockSpec((None, tq, qw), lambda i, j: (i, j, 0)),
        compiler_params=_params(("arbitrary", "arbitrary"), 48),
        name="attention",
    )(*args)


def _pick_pass(i, n_ctx_tiles, ctx_ref, lat_ref):
    return jnp.where(i < n_ctx_tiles, ctx_ref[...], lat_ref[...])


def _pass_specs(block, n_ctx_tiles, n_lat_tiles):
    lead = (0,) * (len(block) - 2)

    def ctx(i, *_):
        return lead + (jnp.minimum(i, n_ctx_tiles - 1), 0)

    def lat(i, *_):
        return lead + (jnp.clip(i - n_ctx_tiles, 0, n_lat_tiles - 1), 0)

    return pl.BlockSpec(block, ctx), pl.BlockSpec(block, lat)


def _even_out_kernel(first_ref, last_ref, rows_ref, c_ref, cp_ref, cn_ref, ap_ref, as_ref, x_ref, g_ref, cw_ref,
                     w_ref, lg_ref, lb_ref, o_ref, *, alpha, n_ctx_tiles):
    del rows_ref
    i = pl.program_id(0)
    attn = _pick_pass(i, n_ctx_tiles, ap_ref, as_ref)
    tm = c_ref.shape[0]
    cwid = CONV_WIDTH
    c = c_ref[...]
    gate_b, z = c[:, :cwid], c[:, cwid:2 * cwid] * c[:, 2 * cwid:]
    cp = cp_ref[...]
    cn = cn_ref[...]
    zp = cp[SUBLANE - 1:SUBLANE, cwid:2 * cwid] * cp[SUBLANE - 1:SUBLANE, 2 * cwid:]
    zn = cn[0:1, cwid:2 * cwid] * cn[0:1, 2 * cwid:]
    zp = zp * (1 - first_ref[i]).astype(F32)
    zn = zn * (1 - last_ref[i]).astype(F32)
    row = lax.broadcasted_iota(jnp.int32, (tm, cwid), 0)
    z_prev = jnp.where(row == 0, zp, pltpu.roll(z, 1, 0))
    z_next = jnp.where(row == tm - 1, zn, pltpu.roll(z, tm - 1, 0))
    cw = cw_ref[...]
    y = gate_b * (cw[0:1] * z_prev + cw[1:2] * z + cw[2:3] * z_next)
    out = _dot(y.astype(BF16), w_ref[:cwid, :]) + _dot(attn, w_ref[cwid:, :])
    o_ref[...] = _layer_norm(alpha * x_ref[...] + g_ref[0] * out, lg_ref[...], lb_ref[...])


def _even_out(conv, attn_p, attn_s, x, g1, conv_w, w_out, ln_g, ln_b, first, last, rows, alpha):
    n, d = x.shape
    tm = TM_PROJ
    cw3 = conv.shape[1]
    aw = attn_p.shape[1]
    hb = tm // SUBLANE
    nblk8 = n // SUBLANE
    n_ctx, n_lat = attn_p.shape[0] // tm, attn_s.shape[0] // tm
    ap_spec, as_spec = _pass_specs((tm, aw), n_ctx, n_lat)
    tok = lambda i, f, l, r: (i, 0)
    const = lambda i, f, l, r: (0, 0)
    return pl.pallas_call(
        functools.partial(_even_out_kernel, alpha=alpha, n_ctx_tiles=n_ctx),
        out_shape=jax.ShapeDtypeStruct((n, d), F32),
        grid_spec=pltpu.PrefetchScalarGridSpec(
            num_scalar_prefetch=3, grid=(n // tm,),
            in_specs=[pl.BlockSpec((tm, cw3), tok),
                      pl.BlockSpec((SUBLANE, cw3), lambda i, f, l, r: (jnp.maximum(i * hb - 1, 0), 0)),
                      pl.BlockSpec((SUBLANE, cw3), lambda i, f, l, r: (jnp.minimum((i + 1) * hb, nblk8 - 1), 0)),
                      ap_spec, as_spec, pl.BlockSpec((tm, d), tok),
                      pl.BlockSpec((1, 1, d), lambda i, f, l, r: (r[i], 0, 0)),
                      pl.BlockSpec((SUBLANE, CONV_WIDTH), const),
                      pl.BlockSpec((CONV_WIDTH + aw, d), const),
                      pl.BlockSpec((1, d), const), pl.BlockSpec((1, d), const)],
            out_specs=pl.BlockSpec((tm, d), tok)),
        compiler_params=_params(("arbitrary",), 40),
        name="even_out_proj",
    )(first, last, rows, conv, conv, conv, attn_p, attn_s, x, g1, conv_w, w_out, ln_g, ln_b)


def _ffn_kernel(rows_ref, x_ref, sc_ref, sh_ref, g_ref, wg_ref, wu_ref, wd_ref, lg_ref, lb_ref, o_ref,
                h_s, acc_s, *, alpha):
    del rows_ref
    f = pl.program_id(1)

    @pl.when(f == 0)
    def _():
        h_s[...] = (x_ref[...] * (1.0 + sc_ref[0]) + sh_ref[0]).astype(BF16)

    h = h_s[...]
    a = (_silu(_dot(h, wg_ref[...])) * _dot(h, wu_ref[...])).astype(BF16)
    y = _dot(a, wd_ref[...])

    @pl.when(f == 0)
    def _():
        acc_s[...] = y

    @pl.when(f > 0)
    def _():
        acc_s[...] += y

    @pl.when(f == pl.num_programs(1) - 1)
    def _():
        o_ref[...] = _layer_norm(alpha * x_ref[...] + g_ref[0] * acc_s[...], lg_ref[...], lb_ref[...])


def _ffn(x, sc, sh, g2, wg, wu, wd, ln_g, ln_b, rows, alpha):
    n, d = x.shape
    tm = TM_FFN
    ff = wg.shape[1]
    fc = FFN_CHUNK
    tok = lambda i, f, r: (i, 0)
    row3 = lambda i, f, r: (r[i], 0, 0)
    const = lambda i, f, r: (0, 0)
    wmode = dict(pipeline_mode=pl.Buffered(1)) if fc == ff else {}
    return pl.pallas_call(
        functools.partial(_ffn_kernel, alpha=alpha),
        out_shape=jax.ShapeDtypeStruct((n, d), F32),
        grid_spec=pltpu.PrefetchScalarGridSpec(
            num_scalar_prefetch=1, grid=(n // tm, ff // fc),
            in_specs=[pl.BlockSpec((tm, d), tok),
                      pl.BlockSpec((1, 1, d), row3), pl.BlockSpec((1, 1, d), row3), pl.BlockSpec((1, 1, d), row3),
                      pl.BlockSpec((d, fc), lambda i, f, r: (0, f), **wmode),
                      pl.BlockSpec((d, fc), lambda i, f, r: (0, f), **wmode),
                      pl.BlockSpec((fc, d), lambda i, f, r: (f, 0), **wmode),
                      pl.BlockSpec((1, d), const), pl.BlockSpec((1, d), const)],
            out_specs=pl.BlockSpec((tm, d), tok),
            scratch_shapes=[pltpu.VMEM((tm, d), BF16), pltpu.VMEM((tm, d), F32)]),
        compiler_params=_params(("arbitrary", "arbitrary"), 56),
        name="ffn_swiglu",
    )(rows, x, sc, sh, g2, wg, wu, wd, ln_g, ln_b)


def _odd_in_kernel(rows_ref, x_ref, sc_ref, sh_ref, w_ref, o_ref, u_ref, u_s):
    del rows_ref
    h = (x_ref[...] * (1.0 + sc_ref[0]) + sh_ref[0]).astype(BF16)
    proj = _dot(h, w_ref[...])
    o_ref[...] = proj
    n_tile = S5_WIDTH // LANE
    gpt = LANE // S5_GROUP
    for j in range(n_tile):
        u_s[j] = proj[:, j * LANE:(j + 1) * LANE]
    n_chunk = u_s.shape[1] // S5_CHUNK
    steps = [[u_s[j, pl.ds(s, n_chunk, stride=S5_CHUNK), :] for j in range(n_tile)] for s in range(S5_CHUNK)]
    for g in range(S5_GROUPS):
        sl = slice((g % gpt) * S5_GROUP, (g % gpt + 1) * S5_GROUP)
        u_ref[g] = jnp.concatenate([st[g // gpt][:, sl] for st in steps], axis=1).astype(BF16)


def _odd_in(x, sc, sh, w, rows, n_ctx_tiles, ctx_chunk0, lat_chunk0):
    n, d = x.shape
    tm = TM_PROJ
    nw = w.shape[1]
    cpt = tm // S5_CHUNK
    u_rows = n // S5_CHUNK
    tok = lambda i, r: (i, 0)
    row3 = lambda i, r: (r[i], 0, 0)
    ublk = lambda i, r: (0, jnp.where(i < n_ctx_tiles, i + ctx_chunk0 // cpt, i - n_ctx_tiles + lat_chunk0 // cpt), 0)
    return pl.pallas_call(
        _odd_in_kernel,
        out_shape=(jax.ShapeDtypeStruct((n, nw), F32),
                   jax.ShapeDtypeStruct((S5_GROUPS, u_rows, S5_CHUNK * S5_GROUP), BF16)),
        grid_spec=pltpu.PrefetchScalarGridSpec(
            num_scalar_prefetch=1, grid=(n // tm,),
            in_specs=[pl.BlockSpec((tm, d), tok), pl.BlockSpec((1, 1, d), row3), pl.BlockSpec((1, 1, d), row3),
                      pl.BlockSpec((d, nw), lambda i, r: (0, 0))],
            out_specs=[pl.BlockSpec((tm, nw), tok), pl.BlockSpec((S5_GROUPS, cpt, S5_CHUNK * S5_GROUP), ublk)],
            scratch_shapes=[pltpu.VMEM((S5_WIDTH // LANE, tm, LANE), F32)]),
        compiler_params=_params(("arbitrary",), 32),
        name="odd_in_proj",
    )(rows, x, sc, sh, w)


def _s5_kernel(u_ref, zin_ref, t_ref, zout_ref, al_ref, h0_ref, y_ref, fin_ref, z_s, hf_s, hb_s, *, nb, nc):
    u = u_ref[0]
    z = _dot(u, zin_ref[0])
    for part in range(2):
        for b in range(nb):
            z_s[part, pl.ds(b, nc, stride=nb), :] = z[b * nc:(b + 1) * nc, part * LANE:(part + 1) * LANE]
    ar = al_ref[0, 0:1, :]
    ai = al_ref[0, 1:2, :]
    fwd = lax.broadcasted_iota(jnp.int32, (nb, LANE), 1) < S5_STATE
    aligned = (lambda r: pl.multiple_of(r, SUBLANE)) if nb % SUBLANE == 0 else (lambda r: r)

    def step(k, carry):
        re, im = carry
        rf = aligned(k * nb)
        rb = aligned((nc - 1 - k) * nb)
        hf_s[0, pl.ds(rf, nb), :] = re
        hf_s[1, pl.ds(rf, nb), :] = im
        hb_s[0, pl.ds(rb, nb), :] = re
        hb_s[1, pl.ds(rb, nb), :] = im
        zr = jnp.where(fwd, z_s[0, pl.ds(rf, nb), :], z_s[0, pl.ds(rb, nb), :])
        zi = jnp.where(fwd, z_s[1, pl.ds(rf, nb), :], z_s[1, pl.ds(rb, nb), :])
        return ar * re - ai * im + zr, ar * im + ai * re + zi

    h0 = h0_ref[0]
    re, im = lax.fori_loop(0, nc, step, (h0[:, :LANE], h0[:, LANE:]))
    fin_ref[0, :, :LANE] = re
    fin_ref[0, :, LANE:] = im
    m = u.shape[0]
    is_fwd = lax.broadcasted_iota(jnp.int32, (m, LANE), 1) < S5_STATE
    halves = []
    for part in range(2):
        hf_s[part] = jnp.where(is_fwd, hf_s[part], hb_s[part])
        halves.append(jnp.concatenate([hf_s[part, pl.ds(b, nc, stride=nb), :] for b in range(nb)], axis=0))
    h_in = jnp.concatenate(halves, axis=1).astype(BF16)
    y_ref[0] = _dot(u, t_ref[0]) + _dot(h_in, zout_ref[0])


def _s5(u, mats, h0, nb, nc, row_block):
    zin, tmat, zout, al = mats
    g, _, w = u.shape
    m = nb * nc
    blk = lambda i: (i, 0, 0)
    return pl.pallas_call(
        functools.partial(_s5_kernel, nb=nb, nc=nc),
        out_shape=(jax.ShapeDtypeStruct((g, m, w), F32), jax.ShapeDtypeStruct((g, nb, w), F32)),
        grid=(g,),
        in_specs=[pl.BlockSpec((1, m, w), lambda i: (i, row_block, 0)),
                  pl.BlockSpec((1, w, w), blk), pl.BlockSpec((1, w, w), blk),
                  pl.BlockSpec((1, w, w), blk), pl.BlockSpec((1, 2, LANE), blk), pl.BlockSpec((1, nb, w), blk)],
        out_specs=(pl.BlockSpec((1, m, w), blk), pl.BlockSpec((1, nb, w), blk)),
        scratch_shapes=[pltpu.VMEM((2, m, LANE), F32), pltpu.VMEM((2, m, LANE), F32), pltpu.VMEM((2, m, LANE), F32)],
        compiler_params=_params(("arbitrary",), 32),
        name="s5_scan",
    )(u, zin, tmat, zout, al, h0)


def _s5_matrices(a_re, a_im, log_dt, b_re, b_im, c_re, c_im):
    hp = lax.Precision.HIGHEST
    L = S5_CHUNK
    dt = jnp.exp(log_dt)[..., None]
    lam_re, lam_im = a_re * dt, a_im * dt

    def power(k):
        k = k[:, None, None, None]
        mag = jnp.exp(lam_re * k)
        return mag * jnp.cos(lam_im * k), mag * jnp.sin(lam_im * k)

    ab_re, ab_im = power(jnp.ones((1,), F32))
    ab_re, ab_im = ab_re[0], ab_im[0]
    num_re, num_im = ab_re - 1.0, ab_im
    den = a_re * a_re + a_im * a_im
    f_re = (num_re * a_re + num_im * a_im) / den
    f_im = (num_im * a_re - num_re * a_im) / den
    bb_re = f_re[..., None] * b_re - f_im[..., None] * b_im
    bb_im = f_re[..., None] * b_im + f_im[..., None] * b_re

    ks = jnp.arange(L + 1, dtype=F32)
    pw_re, pw_im = power(ks)

    def zin_dir(d, exps):
        pr, pi = pw_re[exps, d], pw_im[exps, d]
        w_re = pr[..., None] * bb_re[d][None] - pi[..., None] * bb_im[d][None]
        w_im = pr[..., None] * bb_im[d][None] + pi[..., None] * bb_re[d][None]
        to = lambda w: jnp.transpose(w, (1, 0, 3, 2)).reshape(S5_GROUPS, L * S5_GROUP, S5_STATE)
        return to(w_re), to(w_im)

    steps = np.arange(L)
    zf_re, zf_im = zin_dir(0, L - 1 - steps)
    zb_re, zb_im = zin_dir(1, steps)
    zin = jnp.concatenate([zf_re, zb_re, zf_im, zb_im], axis=-1)

    def zout_dir(d, exps):
        pr, pi = pw_re[exps, d], pw_im[exps, d]
        cr, ci = c_re[d], c_im[d]
        e_re = cr[None] * pr[:, :, None, :] - ci[None] * pi[:, :, None, :]
        e_im = cr[None] * pi[:, :, None, :] + ci[None] * pr[:, :, None, :]
        to = lambda e: jnp.transpose(e, (1, 3, 0, 2)).reshape(S5_GROUPS, S5_STATE, L * S5_GROUP)
        return to(e_re), to(-e_im)

    of_re, of_im = zout_dir(0, steps + 1)
    ob_re, ob_im = zout_dir(1, L - steps)
    zout = jnp.concatenate([of_re, ob_re, of_im, ob_im], axis=1)

    def taps(d):
        pr, pi = pw_re[:L, d], pw_im[:L, d]
        m_re = pr[..., None] * bb_re[d][None] - pi[..., None] * bb_im[d][None]
        m_im = pr[..., None] * bb_im[d][None] + pi[..., None] * bb_re[d][None]
        return (jnp.einsum('gpn,lgnq->lgpq', c_re[d], m_re, precision=hp)
                - jnp.einsum('gpn,lgnq->lgpq', c_im[d], m_im, precision=hp))

    kf, kb = taps(0), taps(1)
    pad4 = ((0, 0),) * 3
    tm = jnp.stack([jnp.pad(kf[:L - s], ((s, 0),) + pad4) + jnp.pad(kb[:s + 1][::-1], ((0, L - 1 - s),) + pad4)
                    for s in range(L)])
    tmat = jnp.transpose(tm, (2, 0, 4, 1, 3)).reshape(S5_GROUPS, L * S5_GROUP, L * S5_GROUP)

    al = jnp.stack([jnp.concatenate([pw_re[L, 0], pw_re[L, 1]], axis=-1),
                    jnp.concatenate([pw_im[L, 0], pw_im[L, 1]], axis=-1)], axis=1)
    return zin.astype(BF16), tmat.astype(BF16), zout.astype(BF16), al


def _mla_kv(ckv, kpe, wk_ref, wv_ref, pl_ref, k_ref, v_ref):
    cb = ckv.astype(BF16)
    k_ref[...] = (_dot(cb, wk_ref[...]) + _dot(kpe.astype(BF16), pl_ref[...])).astype(BF16)
    v_ref[...] = _dot(cb, wv_ref[...]).astype(BF16)


def _mla_prep_kernel(rblk_ref, p_ref, qg_ref, kvg_ref, wuq_ref, wk_ref, wv_ref, pl_ref, cos_ref, sa_ref, sb_ref,
                     q_ref, k_ref, v_ref, ckv_ref):
    del rblk_ref
    pr = p_ref[...]
    cos, sa, sb = cos_ref[...], sa_ref[...], sb_ref[...]
    quarter = MLA_ROPE // 4
    scale = (MLA_NOPE + MLA_ROPE) ** -0.5
    qn = _rms(pr[:, :MLA_Q_RANK], qg_ref[...], MLA_Q_RANK).astype(BF16)
    q = _dot(qn, wuq_ref[...])
    for hd in range(MLA_HEADS):
        sl = slice(hd * LANE, (hd + 1) * LANE)
        q_ref[:, sl] = (_rope(q[:, sl], cos, sa, sb, quarter) * scale).astype(BF16)
    ckv = _rms(pr[:, MLA_Q_RANK:MLA_Q_RANK + MLA_KV_RANK], kvg_ref[...], MLA_KV_RANK)
    ckv_ref[...] = ckv
    kpe = _rope(pr[:, MLA_Q_RANK + MLA_KV_RANK:], cos, sa, sb, quarter)
    _mla_kv(ckv, kpe, wk_ref, wv_ref, pl_ref, k_ref, v_ref)


def _mla_prep(proj, qg, kvg, wuq, wk, wv, place, tables, rblk):
    n = proj.shape[0]
    tm = TM_PROJ
    hw = MLA_HEADS * LANE
    pw = MLA_Q_RANK + MLA_KV_RANK + LANE
    cos, sa, sb = tables
    tok = lambda i, r: (i, 0)
    const = lambda i, r: (0, 0)
    tab = lambda i, r: (r[i], 0)
    return pl.pallas_call(
        _mla_prep_kernel,
        out_shape=(jax.ShapeDtypeStruct((n, hw), BF16), jax.ShapeDtypeStruct((n, hw), BF16),
                   jax.ShapeDtypeStruct((n, hw), BF16), jax.ShapeDtypeStruct((n, MLA_KV_RANK), F32)),
        grid_spec=pltpu.PrefetchScalarGridSpec(
            num_scalar_prefetch=1, grid=(n // tm,),
            in_specs=[pl.BlockSpec((tm, pw), lambda i, r: (i, 1)),
                      pl.BlockSpec((1, MLA_Q_RANK), const), pl.BlockSpec((1, MLA_KV_RANK), const),
                      pl.BlockSpec((MLA_Q_RANK, hw), const), pl.BlockSpec((MLA_KV_RANK, hw), const),
                      pl.BlockSpec((MLA_KV_RANK, hw), const), pl.BlockSpec((LANE, hw), const),
                      pl.BlockSpec((tm, LANE), tab), pl.BlockSpec((tm, LANE), tab), pl.BlockSpec((tm, LANE), tab)],
            out_specs=[pl.BlockSpec((tm, hw), tok), pl.BlockSpec((tm, hw), tok), pl.BlockSpec((tm, hw), tok),
                       pl.BlockSpec((tm, MLA_KV_RANK), tok)]),
        compiler_params=_params(("arbitrary",), 32),
        name="mla_prep",
    )(rblk, proj, qg, kvg, wuq, wk, wv, place, cos, sa, sb)


def _mla_cache_kernel(c_ref, p_ref, wk_ref, wv_ref, pl_ref, k_ref, v_ref):
    _mla_kv(c_ref[...], p_ref[...], wk_ref, wv_ref, pl_ref, k_ref, v_ref)


def _mla_cache(ckv, kpe, wk, wv, place):
    n = ckv.shape[0]
    tm = TM_PROJ
    hw = MLA_HEADS * LANE
    tok = lambda i: (i, 0)
    const = lambda i: (0, 0)
    return pl.pallas_call(
        _mla_cache_kernel,
        out_shape=(jax.ShapeDtypeStruct((n, hw), BF16), jax.ShapeDtypeStruct((n, hw), BF16)),
        grid=(n // tm,),
        in_specs=[pl.BlockSpec((tm, MLA_KV_RANK), tok), pl.BlockSpec((tm, LANE), tok),
                  pl.BlockSpec((MLA_KV_RANK, hw), const), pl.BlockSpec((MLA_KV_RANK, hw), const),
                  pl.BlockSpec((LANE, hw), const)],
        out_specs=[pl.BlockSpec((tm, hw), tok), pl.BlockSpec((tm, hw), tok)],
        compiler_params=_params(("arbitrary",), 32),
        name="mla_cache_kv",
    )(ckv, kpe, wk, wv, place)


def _gelu_tanh(x):
    return 0.5 * x * (1.0 + jnp.tanh(math.sqrt(2.0 / math.pi) * (x + 0.044715 * (x * x * x))))


def _odd_out_kernel(rows_ref, u_ref, yp_ref, ys_ref, ap_ref, as_ref, x_ref, g_ref, d_ref, wglu_ref, bglu_ref, w_ref,
                    lg_ref, lb_ref, o_ref, y_s, *, alpha, n_ctx_tiles):
    del rows_ref
    i = pl.program_id(0)
    yg = _pick_pass(i, n_ctx_tiles, yp_ref, ys_ref)
    n_chunk = yg.shape[1]
    n_tile = S5_WIDTH // LANE
    gpt = LANE // S5_GROUP
    for s in range(S5_CHUNK):
        sl = slice(s * S5_GROUP, (s + 1) * S5_GROUP)
        for j in range(n_tile):
            y_s[j, pl.ds(s, n_chunk, stride=S5_CHUNK), :] = jnp.concatenate(
                [yg[g][:, sl] for g in range(j * gpt, (j + 1) * gpt)], axis=1)
    y_ssm = jnp.concatenate([y_s[j] for j in range(n_tile)], axis=1)
    y = _gelu_tanh(u_ref[...] * d_ref[...] + y_ssm)
    y = y * jax.nn.sigmoid(_dot(y.astype(BF16), wglu_ref[...]) + bglu_ref[...])
    attn = _pick_pass(i, n_ctx_tiles, ap_ref, as_ref)
    out = _dot(y.astype(BF16), w_ref[:S5_WIDTH, :]) + _dot(attn, w_ref[S5_WIDTH:, :])
    o_ref[...] = _layer_norm(alpha * x_ref[...] + g_ref[0] * out, lg_ref[...], lb_ref[...])


def _odd_out(proj, y_p, y_s, attn_p, attn_s, x, g1, s5_d, w_glu, b_glu, w_out, ln_g, ln_b, rows, alpha):
    n, d = x.shape
    tm = TM_PROJ
    aw = attn_p.shape[1]
    cpt = tm // S5_CHUNK
    n_ctx, n_lat = attn_p.shape[0] // tm, attn_s.shape[0] // tm
    ap_spec, as_spec = _pass_specs((tm, aw), n_ctx, n_lat)
    yp_spec, ys_spec = _pass_specs((S5_GROUPS, cpt, S5_CHUNK * S5_GROUP), n_ctx, n_lat)
    tok = lambda i, r: (i, 0)
    const = lambda i, r: (0, 0)
    return pl.pallas_call(
        functools.partial(_odd_out_kernel, alpha=alpha, n_ctx_tiles=n_ctx),
        out_shape=jax.ShapeDtypeStruct((n, d), F32),
        grid_spec=pltpu.PrefetchScalarGridSpec(
            num_scalar_prefetch=1, grid=(n // tm,),
            in_specs=[pl.BlockSpec((tm, S5_WIDTH), tok), yp_spec, ys_spec,
                      ap_spec, as_spec, pl.BlockSpec((tm, d), tok),
                      pl.BlockSpec((1, 1, d), lambda i, r: (r[i], 0, 0)),
                      pl.BlockSpec((1, S5_WIDTH), const), pl.BlockSpec((S5_WIDTH, S5_WIDTH), const),
                      pl.BlockSpec((1, S5_WIDTH), const), pl.BlockSpec((S5_WIDTH + aw, d), const),
                      pl.BlockSpec((1, d), const), pl.BlockSpec((1, d), const)],
            out_specs=pl.BlockSpec((tm, d), tok),
            scratch_shapes=[pltpu.VMEM((S5_WIDTH // LANE, tm, LANE), F32)]),
        compiler_params=_params(("arbitrary",), 40),
        name="odd_out_proj",
    )(rows, proj, y_p, y_s, attn_p, attn_s, x, g1, s5_d, w_glu, b_glu, w_out, ln_g, ln_b)


def _router_kernel(rows_ref, x_ref, sc_ref, sh_ref, rt_ref, hb_ref, pos_ref, gate_ref, pt_ref, cnt_ref, tri_s):
    del rows_ref
    t = x_ref.shape[0]
    ne = N_EXPERTS

    @pl.when(pl.program_id(0) == 0)
    def _():
        before = lax.broadcasted_iota(jnp.int32, (t, t), 0) < lax.broadcasted_iota(jnp.int32, (t, t), 1)
        tri_s[...] = jnp.where(before, 1.0, 0.0).astype(BF16)

    h = x_ref[...] * (1.0 + sc_ref[0]) + sh_ref[0]
    h_hi, h_lo = _split(h)
    hb_ref[...] = h_hi
    r_hi, r_lo = _split(rt_ref[...])
    logits = _dot_nt(r_hi, h_hi) + _dot_nt(r_lo, h_hi) + _dot_nt(r_hi, h_lo)
    eid = lax.broadcasted_iota(jnp.int32, (ne, t), 0).astype(F32)
    m0 = jnp.max(logits, axis=0, keepdims=True)
    i0 = jnp.min(jnp.where(logits == m0, eid, float(ne)), axis=0, keepdims=True)
    rest = jnp.where(eid == i0, -jnp.inf, logits)
    m1 = jnp.max(rest, axis=0, keepdims=True)
    i1 = jnp.min(jnp.where(rest == m1, eid, float(ne)), axis=0, keepdims=True)
    ex = jnp.exp(m1 - m0)
    g0 = 1.0 / (1.0 + ex)
    g1 = ex / (1.0 + ex)
    sel0 = eid == i0
    sel1 = eid == i1
    member = jnp.where(sel0, 1.0, jnp.where(sel1, 1.0, 0.0))
    gate = jnp.where(sel0, g0, jnp.where(sel1, g1, 0.0))
    rank = _dot(member.astype(BF16), tri_s[...])
    pos = jnp.where(member > 0.0, rank, -1.0)
    pos_ref[0] = pos.astype(jnp.int32)
    gate_ref[0] = gate
    cnt = jnp.sum(member, axis=1, keepdims=True)
    cnt_ref[0] = jnp.broadcast_to(cnt, (ne, LANE)).astype(jnp.int32)
    packed = jnp.concatenate([pos, gate, jnp.zeros((LANE - 2 * ne, t), F32)], axis=0)
    pt_ref[...] = packed.T


def _router(x, sc, sh, router_t, rows):
    n, d = x.shape
    t = T_MOE
    nb = n // t
    ne = N_EXPERTS
    tok = lambda i, r: (i, 0)
    row3 = lambda i, r: (r[i], 0, 0)
    blk3 = lambda i, r: (i, 0, 0)
    return pl.pallas_call(
        _router_kernel,
        out_shape=(jax.ShapeDtypeStruct((n, d), BF16), jax.ShapeDtypeStruct((nb, ne, t), jnp.int32),
                   jax.ShapeDtypeStruct((nb, ne, t), F32),
                   jax.ShapeDtypeStruct((n, LANE), F32), jax.ShapeDtypeStruct((nb, ne, LANE), jnp.int32)),
        grid_spec=pltpu.PrefetchScalarGridSpec(
            num_scalar_prefetch=1, grid=(nb,),
            in_specs=[pl.BlockSpec((t, d), tok), pl.BlockSpec((1, 1, d), row3), pl.BlockSpec((1, 1, d), row3),
                      pl.BlockSpec((ne, d), lambda i, r: (0, 0))],
            out_specs=[pl.BlockSpec((t, d), tok), pl.BlockSpec((1, ne, t), blk3), pl.BlockSpec((1, ne, t), blk3),
                       pl.BlockSpec((t, LANE), tok), pl.BlockSpec((1, ne, LANE), blk3)],
            scratch_shapes=[pltpu.VMEM((t, t), BF16)]),
        compiler_params=_params(("arbitrary",), 48),
        name="moe_router",
    )(rows, x, sc, sh, router_t)


def _segment_copies(rows, src, dst, src0, dst0, sems):
    out = []
    for k, sz in enumerate(SEG_SIZES):
        off = (rows // (2 * sz)) * (2 * sz)
        s0 = pl.multiple_of(src0 + off, SEG_ALIGN)
        d0 = pl.multiple_of(dst0 + off, SEG_ALIGN)
        out.append(((rows & sz) != 0,
                    pltpu.make_async_copy(src.at[pl.ds(s0, sz)], dst.at[pl.ds(d0, sz)], sems.at[k])))
    return out


def _run_copies(copies):
    for pred, cp in copies:
        pl.when(pred)(cp.start)
    for pred, cp in copies:
        pl.when(pred)(cp.wait)


def _moe_gather_kernel(cnt_ref, base_ref, h_ref, pos_ref, gate_ref, xs_in, gs_in, xs_ref, gs_ref,
                       xe_s, ge_s, sem_x, sem_g):
    del xs_in, gs_in
    b, e = pl.program_id(0), pl.program_id(1)
    t = h_ref.shape[0]
    rt = ROW_MOE
    cnt = cnt_ref[b * N_EXPERTS + e]
    base = base_ref[b * N_EXPERTS + e]
    pos_row = pos_ref[0, pl.ds(e, 1), :]
    gate_row = gate_ref[0, pl.ds(e, 1), :]

    def gather(r, c):
        r0 = pl.multiple_of(r * rt, rt)
        hit = (lax.broadcasted_iota(jnp.int32, (rt, t), 0) + r0) == pos_row
        onehot = jnp.where(hit, 1.0, 0.0).astype(BF16)
        xe_s[pl.ds(r0, rt), :] = _dot(onehot, h_ref[...]).astype(BF16)
        g = jnp.sum(jnp.where(hit, gate_row, 0.0), axis=1, keepdims=True)
        ge_s[pl.ds(r0, rt), :] = jnp.broadcast_to(g, (rt, LANE))
        return c

    lax.fori_loop(0, (cnt + rt - 1) // rt, gather, 0)
    rows = ((cnt + SEG_ALIGN - 1) // SEG_ALIGN) * SEG_ALIGN
    _run_copies(_segment_copies(rows, xe_s, xs_ref, 0, base, sem_x)
                + _segment_copies(rows, ge_s, gs_ref, 0, base, sem_g))


def _moe_gather(h, pos, gate, counts, base, n_rows):
    n, d = h.shape
    t = T_MOE
    nb = n // t
    ne = N_EXPERTS
    tok = lambda b, e, c, o: (b, 0)
    blk3 = lambda b, e, c, o: (b, 0, 0)
    any_spec = pl.BlockSpec(memory_space=pl.ANY)
    xs0 = jnp.zeros((n_rows, d), BF16)
    gs0 = jnp.zeros((n_rows, LANE), F32)
    return pl.pallas_call(
        _moe_gather_kernel,
        out_shape=(jax.ShapeDtypeStruct((n_rows, d), BF16), jax.ShapeDtypeStruct((n_rows, LANE), F32)),
        grid_spec=pltpu.PrefetchScalarGridSpec(
            num_scalar_prefetch=2, grid=(nb, ne),
            in_specs=[pl.BlockSpec((t, d), tok), pl.BlockSpec((1, ne, t), blk3), pl.BlockSpec((1, ne, t), blk3),
                      any_spec, any_spec],
            out_specs=[any_spec, any_spec],
            scratch_shapes=[pltpu.VMEM((t, d), BF16), pltpu.VMEM((t, LANE), F32),
                            pltpu.SemaphoreType.DMA((len(SEG_SIZES),)), pltpu.SemaphoreType.DMA((len(SEG_SIZES),))]),
        input_output_aliases={5: 0, 6: 1},
        compiler_params=_params(("arbitrary", "arbitrary"), 32),
        name="moe_gather",
    )(counts, base, h, pos, gate, xs0, gs0)


def _moe_gmm_kernel(te_ref, valid_ref, x_ref, gs_ref, wg_ref, wu_ref, wd_ref, y_ref, acc_s):
    del te_ref
    i, f = pl.program_id(0), pl.program_id(1)
    last = pl.num_programs(1) - 1
    valid = valid_ref[i] != 0

    @pl.when(valid)
    def _():
        x = x_ref[...]
        a = (_silu(_dot(x, wg_ref[0])) * _dot(x, wu_ref[0])).astype(BF16)
        y = _dot(a, wd_ref[0])

        @pl.when(f == 0)
        def _():
            acc_s[...] = y

        @pl.when(f > 0)
        def _():
            acc_s[...] += y

        @pl.when(f == last)
        def _():
            y_ref[...] = (acc_s[...] * gs_ref[:, 0:1]).astype(BF16)

    @pl.when(jnp.logical_not(valid) & (f == last))
    def _():
        y_ref[...] = jnp.zeros_like(y_ref)


def _moe_gmm(xs, gs, tile_expert, tile_valid, wg, wu, wd):
    n_rows, d = xs.shape
    rt = RT_GMM
    ne, _, ff = wg.shape
    fc = MOE_CHUNK
    tok = lambda i, f, te, tv: (i, 0)
    return pl.pallas_call(
        _moe_gmm_kernel,
        out_shape=jax.ShapeDtypeStruct((n_rows, d), BF16),
        grid_spec=pltpu.PrefetchScalarGridSpec(
            num_scalar_prefetch=2, grid=(n_rows // rt, ff // fc),
            in_specs=[pl.BlockSpec((rt, d), tok), pl.BlockSpec((rt, LANE), tok),
                      pl.BlockSpec((1, d, fc), lambda i, f, te, tv: (te[i], 0, f)),
                      pl.BlockSpec((1, d, fc), lambda i, f, te, tv: (te[i], 0, f)),
                      pl.BlockSpec((1, fc, d), lambda i, f, te, tv: (te[i], f, 0))],
            out_specs=pl.BlockSpec((rt, d), tok),
            scratch_shapes=[pltpu.VMEM((rt, d), F32)]),
        compiler_params=_params(("arbitrary", "arbitrary"), 48),
        name="moe_gmm",
    )(tile_expert, tile_valid, xs, gs, wg, wu, wd)


def _moe_combine_kernel(cnt_ref, base_ref, rows_ref, ys_ref, pt_ref, x_ref, g_ref, lg_ref, lb_ref, o_ref,
                        ycat_s, sems, *, alpha):
    del rows_ref
    b = pl.program_id(0)
    t = x_ref.shape[0]
    cap = ycat_s.shape[0]
    ne = N_EXPERTS

    @pl.when(b == 0)
    def _():
        ycat_s[...] = jnp.zeros_like(ycat_s)

    copies = []
    offs = []
    off = 0
    for e in range(ne):
        cnt = cnt_ref[b * ne + e]
        rows = ((cnt + SEG_ALIGN - 1) // SEG_ALIGN) * SEG_ALIGN
        copies += _segment_copies(rows, ys_ref, ycat_s, base_ref[b * ne + e], off, sems.at[e])
        offs.append(off)
        off = off + rows
    _run_copies(copies)

    pt = pt_ref[...]
    lane = lax.broadcasted_iota(jnp.int32, (1, LANE), 1)
    off_lane = jnp.zeros((1, LANE), F32)
    for e in range(1, ne):
        off_lane = jnp.where(lane == e, offs[e].astype(F32), off_lane)
    chosen = (pt >= 0.0) & (lax.broadcasted_iota(jnp.int32, (t, LANE), 1) < ne)
    slot = pt + off_lane
    s_lo = jnp.min(jnp.where(chosen, slot, float(cap)), axis=1, keepdims=True)
    s_hi = jnp.max(jnp.where(chosen, slot, -1.0), axis=1, keepdims=True)
    n_piece = 3
    wp = cap // n_piece
    f = None
    for j in range(n_piece):
        col = (lax.broadcasted_iota(jnp.int32, (t, wp), 1) + j * wp).astype(F32)
        sel = jnp.where(col == s_lo, 1.0, jnp.where(col == s_hi, 1.0, 0.0)).astype(BF16)
        part = _dot(sel, ycat_s[j * wp:(j + 1) * wp, :])
        f = part if f is None else f + part
    o_ref[...] = _layer_norm(alpha * x_ref[...] + g_ref[0] * f, lg_ref[...], lb_ref[...])


def _moe_combine(ys, pt, x, g2, ln_g, ln_b, counts, base, rows, alpha):
    n, d = x.shape
    t = T_MOE
    ne = N_EXPERTS
    cap = -(-(2 * t + ne * (SEG_ALIGN - 1)) // (6 * LANE)) * (6 * LANE)
    tok = lambda b, c, o, r: (b, 0)
    const = lambda b, c, o, r: (0, 0)
    return pl.pallas_call(
        functools.partial(_moe_combine_kernel, alpha=alpha),
        out_shape=jax.ShapeDtypeStruct((n, d), F32),
        grid_spec=pltpu.PrefetchScalarGridSpec(
            num_scalar_prefetch=3, grid=(n // t,),
            in_specs=[pl.BlockSpec(memory_space=pl.ANY), pl.BlockSpec((t, LANE), tok), pl.BlockSpec((t, d), tok),
                      pl.BlockSpec((1, 1, d), lambda b, c, o, r: (r[b], 0, 0)),
                      pl.BlockSpec((1, d), const), pl.BlockSpec((1, d), const)],
            out_specs=pl.BlockSpec((t, d), tok),
            scratch_shapes=[pltpu.VMEM((cap, d), BF16), pltpu.SemaphoreType.DMA((ne, len(SEG_SIZES)))]),
        compiler_params=_params(("arbitrary",), 48),
        name="moe_combine",
    )(counts, base, rows, ys, pt, x, g2, ln_g, ln_b)


def _moe_layout(cnt, n_tiles):
    cnt16 = (cnt + SEG_ALIGN - 1) // SEG_ALIGN * SEG_ALIGN
    rows_e = jnp.sum(cnt16, axis=0)
    rows_e = (rows_e + RT_GMM - 1) // RT_GMM * RT_GMM
    ends = jnp.cumsum(rows_e)
    base = (ends - rows_e)[None, :] + jnp.cumsum(cnt16, axis=0) - cnt16
    tile_end = ends // RT_GMM
    idx = jnp.arange(n_tiles, dtype=jnp.int32)
    valid = idx < tile_end[-1]
    te = jnp.sum((idx[:, None] >= tile_end[None, :]).astype(jnp.int32), axis=1)
    last = jnp.sum(((tile_end[-1] - 1) >= tile_end).astype(jnp.int32))
    te = jnp.where(valid, te, last)
    return base.reshape(-1).astype(jnp.int32), te.astype(jnp.int32), valid.astype(jnp.int32)


def _tile_rows(n, p, ss, tm):
    starts = np.arange(n // tm) * tm
    return jnp.asarray(np.where(starts < p, 0, 1 + (starts - p) // ss).astype(np.int32))


def _tile_rope_blocks(n, p, ss, tm):
    starts = np.arange(n // tm) * tm
    return jnp.asarray(np.where(starts < p, 0, 1 + ((starts - p) % ss) // tm).astype(np.int32))


def _tile_edges(n, p, sp, ss, tm):
    starts = np.arange(n // tm) * tm
    pos = np.where(starts < p, starts % sp, (starts - p) % ss)
    seq = np.where(starts < p, sp, ss)
    return jnp.asarray((pos == 0).astype(np.int32)), jnp.asarray((pos + tm == seq).astype(np.int32))


def _rope_tables(n_tokens, rot_dim, lane0, tm):
    rows = n_tokens // GRID_W
    row_pos = jnp.repeat(jnp.arange(rows, dtype=F32), GRID_W)
    col_pos = jnp.tile(jnp.arange(GRID_W, dtype=F32), rows)
    half = rot_dim // 2
    qtr = half // 2
    inv_freq = ROPE_THETA ** (-jnp.arange(0, half, 2, dtype=F32) / half)
    ang_r = row_pos[:, None] * inv_freq
    ang_c = col_pos[:, None] * inv_freq
    ang = jnp.concatenate([ang_r, ang_r, ang_c, ang_c], axis=-1)
    cos, sin = jnp.cos(ang), jnp.sin(ang)
    first = ((np.arange(rot_dim) % half) < qtr).astype(np.float32)
    sa = -sin * first
    sb = sin * (1.0 - first)

    def place(tbl, fill):
        full = jnp.full((n_tokens, LANE), fill, F32).at[:, lane0:lane0 + rot_dim].set(tbl)
        return jnp.concatenate([jnp.full((tm, LANE), fill, F32), full], axis=0)

    return place(cos, 1.0), place(sa, 0.0), place(sb, 0.0)


def _pad_heads(w, n_heads, width):
    lead = w.shape[:-1]
    w = w.reshape(lead + (n_heads, width))
    w = jnp.pad(w, [(0, 0)] * len(lead) + [(0, 0), (0, LANE - width)])
    return w.reshape(lead + (n_heads * LANE,))


def _pad_head_rows(w, n_heads, width):
    d = w.shape[-1]
    w = w.reshape(n_heads, width, d)
    return jnp.pad(w, ((0, 0), (0, LANE - width), (0, 0))).reshape(n_heads * LANE, d)


def kernel(x_prompt, x_sample, cache_attn_k, cache_attn_v, cache_mla_ckv, cache_mla_kpe, state_ssm_re, state_ssm_im, c, c_ctx, ada_w, ada_b, ln_g, ln_b, ev_w_in, ev_conv_w, ev_q_gain, ev_k_gain, ev_w_out, ffn_w_gate, ffn_w_up, ffn_w_down, od_w_in, s5_a_re, s5_a_im, s5_log_dt, s5_b_re, s5_b_im, s5_c_re, s5_c_im, s5_d, s5_w_glu, s5_b_glu, mla_q_gain, mla_w_uq, mla_kv_gain, mla_w_ukv, od_w_out, moe_router, moe_w_gate, moe_w_up, moe_w_down):
    bp, sp, d = x_prompt.shape
    bs, ss, _ = x_sample.shape
    depth = ada_w.shape[0]
    alpha = (2 * depth) ** 0.25
    p = bp * sp
    n = p + bs * ss
    past = cache_attn_k.shape[2]
    for tm in (TM_PROJ, TM_FFN, T_MOE):
        assert p % tm == 0 and ss % tm == 0 and (sp % tm == 0 or tm % sp == 0)
    assert sp % TM_PROJ == 0 and sp % S5_CHUNK == 0 and ss % S5_CHUNK == 0

    x = jnp.concatenate([x_prompt.reshape(p, d), x_sample.reshape(bs * ss, d)], axis=0)

    nrow = -(-(1 + bs) // SUBLANE) * SUBLANE
    cond = jnp.zeros((nrow, d), F32).at[0].set(c_ctx).at[1:1 + bs].set(c)
    mod = _modulation(cond, ada_w, ada_b)

    def mod_part(l, k):
        return mod[l, :, k * d:(k + 1) * d].reshape(nrow, 1, d)

    rows_proj = _tile_rows(n, p, ss, TM_PROJ)
    rows_ffn = _tile_rows(n, p, ss, TM_FFN)
    rows_moe = _tile_rows(n, p, ss, T_MOE)
    rblk = _tile_rope_blocks(n, p, ss, TM_PROJ)
    first, last = _tile_edges(n, p, sp, ss, TM_PROJ)
    tables_even = _rope_tables(ss, HEAD_DIM, 0, TM_PROJ)
    tables_mla = _rope_tables(ss, MLA_ROPE, MLA_NOPE, TM_PROJ)

    out_k, out_v, out_ckv, out_kpe, out_sre, out_sim = [], [], [], [], [], []
    cw = CONV_WIDTH
    hq = ATTN_HEADS * HEAD_DIM
    hkv = ATTN_KV_HEADS * HEAD_DIM
    for l in range(depth):
        i = l // 2
        sh1, sc1, g1, sh2, sc2, g2 = [mod_part(l, k) for k in range(6)]
        lg = ln_g[l].reshape(2, 1, d)
        lb = ln_b[l].reshape(2, 1, d)
        if l % 2 == 0:
            w_in = ev_w_in[i]
            wc = w_in[:, :3 * cw].astype(BF16)
            wq = _pad_heads(w_in[:, 3 * cw:3 * cw + hq], ATTN_HEADS, HEAD_DIM).astype(BF16)
            wkv = jnp.concatenate([_pad_heads(w_in[:, 3 * cw + hq:3 * cw + hq + hkv], ATTN_KV_HEADS, HEAD_DIM),
                                   _pad_heads(w_in[:, 3 * cw + hq + hkv:], ATTN_KV_HEADS, HEAD_DIM)],
                                  axis=1).astype(BF16)
            qg = jnp.pad(ev_q_gain[i], (0, LANE - HEAD_DIM)).reshape(1, LANE)
            kg = jnp.pad(ev_k_gain[i], (0, LANE - HEAD_DIM)).reshape(1, LANE)
            conv, q, k, v, kn, vf = _even_in(x, sc1, sh1, wc, wq, wkv, qg, kg, tables_even, rows_proj, rblk)
            kvw = ATTN_KV_HEADS * LANE
            out_k.append(kn[:p].reshape(bp, sp, ATTN_KV_HEADS, LANE)[..., :HEAD_DIM])
            out_v.append(vf[:p].reshape(bp, sp, ATTN_KV_HEADS, LANE)[..., :HEAD_DIM])
            group = ATTN_HEADS // ATTN_KV_HEADS
            qw = ATTN_HEADS * LANE
            a_p = _attention(q, [(k, v, True)], ATTN_HEADS, group, bp, sp, 0)
            kc = _pad_heads(cache_attn_k[:, i].reshape(bs, past, hkv), ATTN_KV_HEADS, HEAD_DIM).astype(BF16)
            vc = _pad_heads(cache_attn_v[:, i].reshape(bs, past, hkv), ATTN_KV_HEADS, HEAD_DIM).astype(BF16)
            a_s = _attention(q, [(kc, vc, False), (k, v, True)], ATTN_HEADS, group, bs, ss, p)
            w_out = jnp.concatenate([ev_w_out[i][:cw], _pad_head_rows(ev_w_out[i][cw:], ATTN_HEADS, HEAD_DIM)],
                                    axis=0).astype(BF16)
            conv_w = jnp.pad(ev_conv_w[i], ((0, SUBLANE - ev_conv_w.shape[1]), (0, 0)))
            x = _even_out(conv, a_p.reshape(p, qw), a_s.reshape(bs * ss, qw), x, g1, conv_w, w_out, lg[0], lb[0],
                          first, last, rows_proj, alpha)
            x = _ffn(x, sc2, sh2, g2, ffn_w_gate[i].astype(BF16), ffn_w_up[i].astype(BF16),
                     ffn_w_down[i].astype(BF16), lg[1], lb[1], rows_ffn, alpha)
        else:
            q_end = S5_WIDTH + MLA_Q_RANK
            kv_end = q_end + MLA_KV_RANK
            w_in = od_w_in[i]
            w_in = jnp.concatenate([w_in[:, :kv_end], jnp.zeros((d, MLA_NOPE), F32), w_in[:, kv_end:],
                                    jnp.zeros((d, LANE - MLA_NOPE - MLA_ROPE), F32)], axis=1).astype(BF16)
            m_ctx, m_lat = p // S5_CHUNK, bs * ss // S5_CHUNK
            ctx_chunk0, lat_chunk0 = (m_lat, 0) if m_lat >= m_ctx else (0, m_ctx)
            assert ctx_chunk0 % m_ctx == 0 and lat_chunk0 % m_lat == 0
            proj, u = _odd_in(x, sc1, sh1, w_in, rows_proj, p // TM_PROJ, ctx_chunk0, lat_chunk0)
            out_kpe.append(proj[:p, kv_end + MLA_NOPE:kv_end + MLA_NOPE + MLA_ROPE].reshape(bp, sp, MLA_ROPE))

            mats = _s5_matrices(s5_a_re[i], s5_a_im[i], s5_log_dt[i], s5_b_re[i], s5_b_im[i], s5_c_re[i], s5_c_im[i])
            h0_p = jnp.zeros((S5_GROUPS, bp, 4 * S5_STATE), F32)
            y_p, fin_p = _s5(u, mats, h0_p, bp, sp // S5_CHUNK, ctx_chunk0 // m_ctx)

            def pack_state(re, im):
                to = lambda a: a.transpose(2, 0, 1, 3).reshape(S5_GROUPS, a.shape[0], 2 * S5_STATE)
                return jnp.concatenate([to(re), to(im)], axis=-1)

            y_s, _ = _s5(u, mats, pack_state(state_ssm_re[:, i], state_ssm_im[:, i]), bs, ss // S5_CHUNK,
                         lat_chunk0 // m_lat)
            fin = fin_p.reshape(S5_GROUPS, bp, 2, 2, S5_STATE)
            out_sre.append(fin[:, :, 0].transpose(1, 2, 0, 3))
            out_sim.append(fin[:, :, 1].transpose(1, 2, 0, 3))

            qk = MLA_NOPE + MLA_ROPE
            wuq = _pad_heads(mla_w_uq[i], MLA_HEADS, qk).astype(BF16)
            wukv = mla_w_ukv[i].reshape(MLA_KV_RANK, MLA_HEADS, MLA_NOPE + MLA_V)
            wk = _pad_heads(wukv[..., :MLA_NOPE].reshape(MLA_KV_RANK, -1), MLA_HEADS, MLA_NOPE).astype(BF16)
            wv = _pad_heads(wukv[..., MLA_NOPE:].reshape(MLA_KV_RANK, -1), MLA_HEADS, MLA_V).astype(BF16)
            place = np.zeros((LANE, MLA_HEADS * LANE), np.float32)
            for hd in range(MLA_HEADS):
                for j in range(MLA_ROPE):
                    place[MLA_NOPE + j, hd * LANE + MLA_NOPE + j] = 1.0
            place = jnp.asarray(place, BF16)
            q, k, v, ckv = _mla_prep(proj, mla_q_gain[i].reshape(1, -1), mla_kv_gain[i].reshape(1, -1),
                                     wuq, wk, wv, place, tables_mla, rblk)
            out_ckv.append(ckv[:p].reshape(bp, sp, MLA_KV_RANK))
            hw = MLA_HEADS * LANE
            a_p = _attention(q, [(k, v, True)], MLA_HEADS, 1, bp, sp, 0)
            kpe_c = jnp.pad(cache_mla_kpe[:, i].reshape(bs * past, MLA_ROPE),
                            ((0, 0), (MLA_NOPE, LANE - MLA_NOPE - MLA_ROPE)))
            kc, vc = _mla_cache(cache_mla_ckv[:, i].reshape(bs * past, MLA_KV_RANK), kpe_c, wk, wv, place)
            a_s = _attention(q, [(kc.reshape(bs, past, hw), vc.reshape(bs, past, hw), False), (k, v, True)],
                             MLA_HEADS, 1, bs, ss, p)
            w_out = jnp.concatenate([od_w_out[i][:S5_WIDTH], _pad_head_rows(od_w_out[i][S5_WIDTH:], MLA_HEADS, MLA_V)],
                                    axis=0).astype(BF16)
            x = _odd_out(proj, y_p, y_s, a_p.reshape(p, hw), a_s.reshape(bs * ss, hw), x, g1, s5_d[i].reshape(1, -1),
                         s5_w_glu[i].astype(BF16), s5_b_glu[i].reshape(1, -1), w_out, lg[0], lb[0], rows_proj, alpha)

            hb, pos, gate, pt, cnt = _router(x, sc2, sh2, moe_router[i].T, rows_moe)
            cnt = cnt[:, :, 0]
            nblk = n // T_MOE
            n_tiles = -(-(2 * n + nblk * N_EXPERTS * (SEG_ALIGN - 1)) // RT_GMM) + N_EXPERTS
            base, tile_expert, tile_valid = _moe_layout(cnt, n_tiles)
            counts = cnt.reshape(-1)
            xs, gs = _moe_gather(hb, pos, gate, counts, base, n_tiles * RT_GMM)
            ys = _moe_gmm(xs, gs, tile_expert, tile_valid, moe_w_gate[i].astype(BF16), moe_w_up[i].astype(BF16),
                          moe_w_down[i].astype(BF16))
            x = _moe_combine(ys, pt, x, g2, lg[1], lb[1], counts, base, rows_moe, alpha)

    y_prompt = x[:p].reshape(bp, sp, d)
    y_sample = x[p:].reshape(bs, ss, d)
    return (y_prompt, y_sample, jnp.stack(out_k, axis=1), jnp.stack(out_v, axis=1),
            jnp.stack(out_ckv, axis=1), jnp.stack(out_kpe, axis=1),
            jnp.stack(out_sre, axis=1), jnp.stack(out_sim, axis=1))
```

```python
import functools
import math

import jax
import jax.numpy as jnp
import numpy as np
from jax import lax
from jax.experimental import pallas as pl
from jax.experimental.pallas import tpu as pltpu

F32 = jnp.float32
BF16 = jnp.bfloat16

LANE = 128
SUBLANE = 8
MIB = 1024 * 1024

GRID_W = 64
ROPE_THETA = 10000.0
LN_EPS = 1e-6
RMS_EPS = 1e-6
HEAD_DIM = 64
ATTN_HEADS = 8
ATTN_KV_HEADS = 2
CONV_WIDTH = 512
S5_WIDTH = 512
S5_GROUP = 16
S5_GROUPS = 32
S5_STATE = 64
S5_CHUNK = 16
MLA_HEADS = 8
MLA_Q_RANK = 256
MLA_KV_RANK = 128
MLA_NOPE = 64
MLA_ROPE = 32
MLA_V = 64
N_EXPERTS = 8

TM_PROJ = 256
TM_FFN = 512
FFN_CHUNK = 2816
T_MOE = 1024
ROW_MOE = 128
RT_GMM = 512
MOE_CHUNK = 3584
SEG_ALIGN = 16
SEG_SIZES = tuple(T_MOE >> k for k in range(T_MOE.bit_length()) if (T_MOE >> k) >= SEG_ALIGN)
TQ_ATTN = 256


def _params(sem, vmem_mib):
    return pltpu.CompilerParams(dimension_semantics=sem, vmem_limit_bytes=vmem_mib * MIB)


def _dot(a, b):
    return jnp.dot(a, b, preferred_element_type=F32)


def _dot_nt(a, b):
    return lax.dot_general(a, b, (((1,), (1,)), ((), ())), preferred_element_type=F32)


def _split(a):
    hi = a.astype(BF16)
    lo = (a - hi.astype(F32)).astype(BF16)
    return hi, lo


def _silu(x):
    return x * jax.nn.sigmoid(x)


def _layer_norm(r, g, b):
    mu = jnp.mean(r, axis=-1, keepdims=True)
    d = r - mu
    var = jnp.mean(d * d, axis=-1, keepdims=True)
    return d * lax.rsqrt(var + LN_EPS) * g + b


def _rms(x, g, n):
    ms = jnp.sum(x * x, axis=-1, keepdims=True) * (1.0 / n)
    return x * lax.rsqrt(ms + RMS_EPS) * g


def _rope(x, cos, sa, sb, q):
    w = x.shape[-1]
    return x * cos + pltpu.roll(x, w - q, 1) * sa + pltpu.roll(x, q, 1) * sb


def _ada_kernel(c_ref, w_ref, b_ref, o_ref):
    c = c_ref[...]
    a_hi, a_lo = _split(_silu(c))
    w_hi, w_lo = _split(w_ref[0])
    o_ref[0] = _dot(a_hi, w_hi) + _dot(a_lo, w_hi) + _dot(a_hi, w_lo) + b_ref[0]


def _modulation(cond, ada_w, ada_b):
    depth, d, d6 = ada_w.shape
    r = cond.shape[0]
    tn = 1536
    return pl.pallas_call(
        _ada_kernel,
        out_shape=jax.ShapeDtypeStruct((depth, r, d6), F32),
        grid=(depth, d6 // tn),
        in_specs=[pl.BlockSpec((r, d), lambda l, j: (0, 0)),
                  pl.BlockSpec((1, d, tn), lambda l, j: (l, 0, j)),
                  pl.BlockSpec((1, 1, tn), lambda l, j: (l, 0, j))],
        out_specs=pl.BlockSpec((1, r, tn), lambda l, j: (l, 0, j)),
        compiler_params=_params(("arbitrary", "arbitrary"), 40),
        name="ada_modulation",
    )(cond, ada_w, ada_b.reshape(depth, 1, d6))


def _even_in_kernel(rows_ref, rblk_ref, x_ref, sc_ref, sh_ref, wc_ref, wq_ref, wkv_ref, qg_ref, kg_ref,
                    cos_ref, sa_ref, sb_ref, conv_ref, q_ref, k_ref, v_ref, kn_ref, vf_ref):
    del rows_ref, rblk_ref
    h = (x_ref[...] * (1.0 + sc_ref[0]) + sh_ref[0]).astype(BF16)
    conv_ref[...] = _dot(h, wc_ref[...])
    cos, sa, sb = cos_ref[...], sa_ref[...], sb_ref[...]
    quarter = HEAD_DIM // 4
    q = _dot(h, wq_ref[...])
    for hd in range(ATTN_HEADS):
        sl = slice(hd * LANE, (hd + 1) * LANE)
        qn = _rms(q[:, sl], qg_ref[...], HEAD_DIM)
        q_ref[:, sl] = (_rope(qn, cos, sa, sb, quarter) * (HEAD_DIM ** -0.5)).astype(BF16)
    kv = _dot(h, wkv_ref[...])
    kw = ATTN_KV_HEADS * LANE
    for hd in range(ATTN_KV_HEADS):
        sl = slice(hd * LANE, (hd + 1) * LANE)
        kn = _rms(kv[:, sl], kg_ref[...], HEAD_DIM)
        kn_ref[:, sl] = kn
        k_ref[:, sl] = _rope(kn, cos, sa, sb, quarter).astype(BF16)
    v = kv[:, kw:]
    vf_ref[...] = v
    v_ref[...] = v.astype(BF16)


def _even_in(x, sc, sh, wc, wq, wkv, qg, kg, tables, rows, rblk):
    n, d = x.shape
    tm = TM_PROJ
    cw, qw, kvw = wc.shape[1], wq.shape[1], wkv.shape[1]
    kw = kvw // 2
    cos, sa, sb = tables
    row3 = lambda i, rows, rblk: (rows[i], 0, 0)
    tok = lambda i, rows, rblk: (i, 0)
    const = lambda i, rows, rblk: (0, 0)
    tab = lambda i, rows, rblk: (rblk[i], 0)
    return pl.pallas_call(
        _even_in_kernel,
        out_shape=(jax.ShapeDtypeStruct((n, cw), F32), jax.ShapeDtypeStruct((n, qw), BF16),
                   jax.ShapeDtypeStruct((n, kw), BF16), jax.ShapeDtypeStruct((n, kw), BF16),
                   jax.ShapeDtypeStruct((n, kw), F32), jax.ShapeDtypeStruct((n, kw), F32)),
        grid_spec=pltpu.PrefetchScalarGridSpec(
            num_scalar_prefetch=2, grid=(n // tm,),
            in_specs=[pl.BlockSpec((tm, d), tok),
                      pl.BlockSpec((1, 1, d), row3), pl.BlockSpec((1, 1, d), row3),
                      pl.BlockSpec((d, cw), const), pl.BlockSpec((d, qw), const), pl.BlockSpec((d, kvw), const),
                      pl.BlockSpec((1, LANE), const), pl.BlockSpec((1, LANE), const),
                      pl.BlockSpec((tm, LANE), tab), pl.BlockSpec((tm, LANE), tab), pl.BlockSpec((tm, LANE), tab)],
            out_specs=[pl.BlockSpec((tm, cw), tok), pl.BlockSpec((tm, qw), tok),
                       pl.BlockSpec((tm, kw), tok), pl.BlockSpec((tm, kw), tok),
                       pl.BlockSpec((tm, kw), tok), pl.BlockSpec((tm, kw), tok)]),
        compiler_params=_params(("arbitrary",), 40),
        name="even_in_proj",
    )(rows, rblk, x, sc, sh, wc, wq, wkv, qg, kg, cos, sa, sb)


def _attn_kernel(*refs, n_heads, group, n_seg):
    q_ref = refs[0]
    seg = refs[1:1 + 2 * n_seg]
    o_ref = refs[1 + 2 * n_seg]
    for hd in range(n_heads):
        sl = slice(hd * LANE, (hd + 1) * LANE)
        ks = slice((hd // group) * LANE, (hd // group + 1) * LANE)
        qh = q_ref[:, sl]
        scores = [_dot_nt(qh, seg[2 * s][:, ks]) for s in range(n_seg)]
        m = jnp.max(scores[0], axis=-1, keepdims=True)
        for s in range(1, n_seg):
            m = jnp.maximum(m, jnp.max(scores[s], axis=-1, keepdims=True))
        den = None
        acc = None
        for s in range(n_seg):
            p = jnp.exp(scores[s] - m)
            ps = jnp.sum(p, axis=-1, keepdims=True)
            pv = _dot(p.astype(BF16), seg[2 * s + 1][:, ks])
            den = ps if den is None else den + ps
            acc = pv if acc is None else acc + pv
        o_ref[:, sl] = (acc * (1.0 / den)).astype(BF16)


def _attention(q, segments, n_heads, group, b, s, row0):
    n, qw = q.shape
    assert n % s == 0 and row0 % s == 0
    b0 = row0 // s
    tq = min(TQ_ATTN, s)
    in_specs = [pl.BlockSpec((None, tq, qw), lambda i, j: (i + b0, j, 0))]
    args = [q.reshape(n // s, s, qw)]
    for k, v, own in segments:
        kw = k.shape[-1]
        if own:
            k, v = k.reshape(n // s, s, kw), v.reshape(n // s, s, kw)
            kmap = lambda i, j: (i + b0, 0, 0)
        else:
            kmap = lambda i, j: (i, 0, 0)
        t = k.shape[1]
        in_specs += [pl.BlockSpec((None, t, kw), kmap), pl.BlockSpec((None, t, kw), kmap)]
        args += [k, v]
    return pl.pallas_call(
        functools.partial(_attn_kernel, n_heads=n_heads, group=group, n_seg=len(segments)),
        out_shape=jax.ShapeDtypeStruct((b, s, qw), BF16),
        grid=(b, s // tq),
        in_specs=in_specs,
        out_specs=pl.BlockSpec((None, tq, qw), lambda i, j: (i, j, 0)),
        compiler_params=_params(("arbitrary", "arbitrary"), 48),
        name="attention",
    )(*args)


def _pick_pass(i, n_ctx_tiles, ctx_ref, lat_ref):
    return jnp.where(i < n_ctx_tiles, ctx_ref[...], lat_ref[...])


def _pass_specs(block, n_ctx_tiles, n_lat_tiles):
    lead = (0,) * (len(block) - 2)

    def ctx(i, *_):
        return lead + (jnp.minimum(i, n_ctx_tiles - 1), 0)

    def lat(i, *_):
        return lead + (jnp.clip(i - n_ctx_tiles, 0, n_lat_tiles - 1), 0)

    return pl.BlockSpec(block, ctx), pl.BlockSpec(block, lat)


def _even_out_kernel(first_ref, last_ref, rows_ref, c_ref, cp_ref, cn_ref, ap_ref, as_ref, x_ref, g_ref, cw_ref,
                     w_ref, lg_ref, lb_ref, o_ref, *, alpha, n_ctx_tiles):
    del rows_ref
    i = pl.program_id(0)
    attn = _pick_pass(i, n_ctx_tiles, ap_ref, as_ref)
    tm = c_ref.shape[0]
    cwid = CONV_WIDTH
    c = c_ref[...]
    gate_b, z = c[:, :cwid], c[:, cwid:2 * cwid] * c[:, 2 * cwid:]
    cp = cp_ref[...]
    cn = cn_ref[...]
    zp = cp[SUBLANE - 1:SUBLANE, cwid:2 * cwid] * cp[SUBLANE - 1:SUBLANE, 2 * cwid:]
    zn = cn[0:1, cwid:2 * cwid] * cn[0:1, 2 * cwid:]
    zp = zp * (1 - first_ref[i]).astype(F32)
    zn = zn * (1 - last_ref[i]).astype(F32)
    row = lax.broadcasted_iota(jnp.int32, (tm, cwid), 0)
    z_prev = jnp.where(row == 0, zp, pltpu.roll(z, 1, 0))
    z_next = jnp.where(row == tm - 1, zn, pltpu.roll(z, tm - 1, 0))
    cw = cw_ref[...]
    y = gate_b * (cw[0:1] * z_prev + cw[1:2] * z + cw[2:3] * z_next)
    out = _dot(y.astype(BF16), w_ref[:cwid, :]) + _dot(attn, w_ref[cwid:, :])
    o_ref[...] = _layer_norm(alpha * x_ref[...] + g_ref[0] * out, lg_ref[...], lb_ref[...])


def _even_out(conv, attn_p, attn_s, x, g1, conv_w, w_out, ln_g, ln_b, first, last, rows, alpha):
    n, d = x.shape
    tm = TM_PROJ
    cw3 = conv.shape[1]
    aw = attn_p.shape[1]
    hb = tm // SUBLANE
    nblk8 = n // SUBLANE
    n_ctx, n_lat = attn_p.shape[0] // tm, attn_s.shape[0] // tm
    ap_spec, as_spec = _pass_specs((tm, aw), n_ctx, n_lat)
    tok = lambda i, f, l, r: (i, 0)
    const = lambda i, f, l, r: (0, 0)
    return pl.pallas_call(
        functools.partial(_even_out_kernel, alpha=alpha, n_ctx_tiles=n_ctx),
        out_shape=jax.ShapeDtypeStruct((n, d), F32),
        grid_spec=pltpu.PrefetchScalarGridSpec(
            num_scalar_prefetch=3, grid=(n // tm,),
            in_specs=[pl.BlockSpec((tm, cw3), tok),
                      pl.BlockSpec((SUBLANE, cw3), lambda i, f, l, r: (jnp.maximum(i * hb - 1, 0), 0)),
                      pl.BlockSpec((SUBLANE, cw3), lambda i, f, l, r: (jnp.minimum((i + 1) * hb, nblk8 - 1), 0)),
                      ap_spec, as_spec, pl.BlockSpec((tm, d), tok),
                      pl.BlockSpec((1, 1, d), lambda i, f, l, r: (r[i], 0, 0)),
                      pl.BlockSpec((SUBLANE, CONV_WIDTH), const),
                      pl.BlockSpec((CONV_WIDTH + aw, d), const),
                      pl.BlockSpec((1, d), const), pl.BlockSpec((1, d), const)],
            out_specs=pl.BlockSpec((tm, d), tok)),
        compiler_params=_params(("arbitrary",), 40),
        name="even_out_proj",
    )(first, last, rows, conv, conv, conv, attn_p, attn_s, x, g1, conv_w, w_out, ln_g, ln_b)


def _ffn_kernel(rows_ref, x_ref, sc_ref, sh_ref, g_ref, wg_ref, wu_ref, wd_ref, lg_ref, lb_ref, o_ref,
                h_s, acc_s, *, alpha):
    del rows_ref
    f = pl.program_id(1)

    @pl.when(f == 0)
    def _():
        h_s[...] = (x_ref[...] * (1.0 + sc_ref[0]) + sh_ref[0]).astype(BF16)

    h = h_s[...]
    a = (_silu(_dot(h, wg_ref[...])) * _dot(h, wu_ref[...])).astype(BF16)
    y = _dot(a, wd_ref[...])

    @pl.when(f == 0)
    def _():
        acc_s[...] = y

    @pl.when(f > 0)
    def _():
        acc_s[...] += y

    @pl.when(f == pl.num_programs(1) - 1)
    def _():
        o_ref[...] = _layer_norm(alpha * x_ref[...] + g_ref[0] * acc_s[...], lg_ref[...], lb_ref[...])


def _ffn(x, sc, sh, g2, wg, wu, wd, layer, ln_g, ln_b, rows, alpha):
    n, d = x.shape
    tm = TM_FFN
    ff = wg.shape[-1]
    fc = FFN_CHUNK
    tok = lambda i, f, r: (i, 0)
    row3 = lambda i, f, r: (r[i], 0, 0)
    const = lambda i, f, r: (0, 0)
    wmode = dict(pipeline_mode=pl.Buffered(1)) if fc == ff else {}
    return pl.pallas_call(
        functools.partial(_ffn_kernel, alpha=alpha),
        out_shape=jax.ShapeDtypeStruct((n, d), F32),
        grid_spec=pltpu.PrefetchScalarGridSpec(
            num_scalar_prefetch=1, grid=(n // tm, ff // fc),
            in_specs=[pl.BlockSpec((tm, d), tok),
                      pl.BlockSpec((1, 1, d), row3), pl.BlockSpec((1, 1, d), row3), pl.BlockSpec((1, 1, d), row3),
                      pl.BlockSpec((None, d, fc), lambda i, f, r: (layer, 0, f), **wmode),
                      pl.BlockSpec((None, d, fc), lambda i, f, r: (layer, 0, f), **wmode),
                      pl.BlockSpec((None, fc, d), lambda i, f, r: (layer, f, 0), **wmode),
                      pl.BlockSpec((1, d), const), pl.BlockSpec((1, d), const)],
            out_specs=pl.BlockSpec((tm, d), tok),
            scratch_shapes=[pltpu.VMEM((tm, d), BF16), pltpu.VMEM((tm, d), F32)]),
        compiler_params=_params(("arbitrary", "arbitrary"), 56),
        name="ffn_swiglu",
    )(rows, x, sc, sh, g2, wg, wu, wd, ln_g, ln_b)


def _odd_in_kernel(rows_ref, x_ref, sc_ref, sh_ref, w_ref, o_ref, u_ref, u_s):
    del rows_ref
    h = (x_ref[...] * (1.0 + sc_ref[0]) + sh_ref[0]).astype(BF16)
    proj = _dot(h, w_ref[...])
    o_ref[...] = proj
    n_tile = S5_WIDTH // LANE
    gpt = LANE // S5_GROUP
    for j in range(n_tile):
        u_s[j] = proj[:, j * LANE:(j + 1) * LANE]
    n_chunk = u_s.shape[1] // S5_CHUNK
    steps = [[u_s[j, pl.ds(s, n_chunk, stride=S5_CHUNK), :] for j in range(n_tile)] for s in range(S5_CHUNK)]
    for g in range(S5_GROUPS):
        sl = slice((g % gpt) * S5_GROUP, (g % gpt + 1) * S5_GROUP)
        u_ref[g] = jnp.concatenate([st[g // gpt][:, sl] for st in steps], axis=1).astype(BF16)


def _odd_in(x, sc, sh, w, rows, n_ctx_tiles, ctx_chunk0, lat_chunk0):
    n, d = x.shape
    tm = TM_PROJ
    nw = w.shape[1]
    cpt = tm // S5_CHUNK
    u_rows = n // S5_CHUNK
    tok = lambda i, r: (i, 0)
    row3 = lambda i, r: (r[i], 0, 0)
    ublk = lambda i, r: (0, jnp.where(i < n_ctx_tiles, i + ctx_chunk0 // cpt, i - n_ctx_tiles + lat_chunk0 // cpt), 0)
    return pl.pallas_call(
        _odd_in_kernel,
        out_shape=(jax.ShapeDtypeStruct((n, nw), F32),
                   jax.ShapeDtypeStruct((S5_GROUPS, u_rows, S5_CHUNK * S5_GROUP), BF16)),
        grid_spec=pltpu.PrefetchScalarGridSpec(
            num_scalar_prefetch=1, grid=(n // tm,),
            in_specs=[pl.BlockSpec((tm, d), tok), pl.BlockSpec((1, 1, d), row3), pl.BlockSpec((1, 1, d), row3),
                      pl.BlockSpec((d, nw), lambda i, r: (0, 0))],
            out_specs=[pl.BlockSpec((tm, nw), tok), pl.BlockSpec((S5_GROUPS, cpt, S5_CHUNK * S5_GROUP), ublk)],
            scratch_shapes=[pltpu.VMEM((S5_WIDTH // LANE, tm, LANE), F32)]),
        compiler_params=_params(("arbitrary",), 32),
        name="odd_in_proj",
    )(rows, x, sc, sh, w)


def _s5_kernel(u_ref, zin_ref, t_ref, zout_ref, al_ref, h0_ref, y_ref, fin_ref, z_s, hf_s, hb_s, *, nb, nc):
    u = u_ref[0]
    z = _dot(u, zin_ref[0])
    for part in range(2):
        for b in range(nb):
            z_s[part, pl.ds(b, nc, stride=nb), :] = z[b * nc:(b + 1) * nc, part * LANE:(part + 1) * LANE]
    ar = al_ref[0, 0:1, :]
    ai = al_ref[0, 1:2, :]
    fwd = lax.broadcasted_iota(jnp.int32, (nb, LANE), 1) < S5_STATE
    aligned = (lambda r: pl.multiple_of(r, SUBLANE)) if nb % SUBLANE == 0 else (lambda r: r)

    def step(k, carry):
        re, im = carry
        rf = aligned(k * nb)
        rb = aligned((nc - 1 - k) * nb)
        hf_s[0, pl.ds(rf, nb), :] = re
        hf_s[1, pl.ds(rf, nb), :] = im
        hb_s[0, pl.ds(rb, nb), :] = re
        hb_s[1, pl.ds(rb, nb), :] = im
        zr = jnp.where(fwd, z_s[0, pl.ds(rf, nb), :], z_s[0, pl.ds(rb, nb), :])
        zi = jnp.where(fwd, z_s[1, pl.ds(rf, nb), :], z_s[1, pl.ds(rb, nb), :])
        return ar * re - ai * im + zr, ar * im + ai * re + zi

    h0 = h0_ref[0]
    re, im = lax.fori_loop(0, nc, step, (h0[:, :LANE], h0[:, LANE:]))
    fin_ref[0, :, :LANE] = re
    fin_ref[0, :, LANE:] = im
    m = u.shape[0]
    is_fwd = lax.broadcasted_iota(jnp.int32, (m, LANE), 1) < S5_STATE
    halves = []
    for part in range(2):
        hf_s[part] = jnp.where(is_fwd, hf_s[part], hb_s[part])
        halves.append(jnp.concatenate([hf_s[part, pl.ds(b, nc, stride=nb), :] for b in range(nb)], axis=0))
    h_in = jnp.concatenate(halves, axis=1).astype(BF16)
    y_ref[0] = _dot(u, t_ref[0]) + _dot(h_in, zout_ref[0])


def _s5(u, mats, h0, nb, nc, row_block):
    zin, tmat, zout, al = mats
    g, _, w = u.shape
    m = nb * nc
    blk = lambda i: (i, 0, 0)
    return pl.pallas_call(
        functools.partial(_s5_kernel, nb=nb, nc=nc),
        out_shape=(jax.ShapeDtypeStruct((g, m, w), F32), jax.ShapeDtypeStruct((g, nb, w), F32)),
        grid=(g,),
        in_specs=[pl.BlockSpec((1, m, w), lambda i: (i, row_block, 0)),
                  pl.BlockSpec((1, w, w), blk), pl.BlockSpec((1, w, w), blk),
                  pl.BlockSpec((1, w, w), blk), pl.BlockSpec((1, 2, LANE), blk), pl.BlockSpec((1, nb, w), blk)],
        out_specs=(pl.BlockSpec((1, m, w), blk), pl.BlockSpec((1, nb, w), blk)),
        scratch_shapes=[pltpu.VMEM((2, m, LANE), F32), pltpu.VMEM((2, m, LANE), F32), pltpu.VMEM((2, m, LANE), F32)],
        compiler_params=_params(("arbitrary",), 32),
        name="s5_scan",
    )(u, zin, tmat, zout, al, h0)


def _s5_matrices(a_re, a_im, log_dt, b_re, b_im, c_re, c_im):
    hp = lax.Precision.HIGHEST
    L = S5_CHUNK
    dt = jnp.exp(log_dt)[..., None]
    lam_re, lam_im = a_re * dt, a_im * dt

    def power(k):
        k = k[:, None, None, None]
        mag = jnp.exp(lam_re * k)
        return mag * jnp.cos(lam_im * k), mag * jnp.sin(lam_im * k)

    ab_re, ab_im = power(jnp.ones((1,), F32))
    ab_re, ab_im = ab_re[0], ab_im[0]
    num_re, num_im = ab_re - 1.0, ab_im
    den = a_re * a_re + a_im * a_im
    f_re = (num_re * a_re + num_im * a_im) / den
    f_im = (num_im * a_re - num_re * a_im) / den
    bb_re = f_re[..., None] * b_re - f_im[..., None] * b_im
    bb_im = f_re[..., None] * b_im + f_im[..., None] * b_re

    ks = jnp.arange(L + 1, dtype=F32)
    pw_re, pw_im = power(ks)

    def zin_dir(d, exps):
        pr, pi = pw_re[exps, d], pw_im[exps, d]
        w_re = pr[..., None] * bb_re[d][None] - pi[..., None] * bb_im[d][None]
        w_im = pr[..., None] * bb_im[d][None] + pi[..., None] * bb_re[d][None]
        to = lambda w: jnp.transpose(w, (1, 0, 3, 2)).reshape(S5_GROUPS, L * S5_GROUP, S5_STATE)
        return to(w_re), to(w_im)

    steps = np.arange(L)
    zf_re, zf_im = zin_dir(0, L - 1 - steps)
    zb_re, zb_im = zin_dir(1, steps)
    zin = jnp.concatenate([zf_re, zb_re, zf_im, zb_im], axis=-1)

    def zout_dir(d, exps):
        pr, pi = pw_re[exps, d], pw_im[exps, d]
        cr, ci = c_re[d], c_im[d]
        e_re = cr[None] * pr[:, :, None, :] - ci[None] * pi[:, :, None, :]
        e_im = cr[None] * pi[:, :, None, :] + ci[None] * pr[:, :, None, :]
        to = lambda e: jnp.transpose(e, (1, 3, 0, 2)).reshape(S5_GROUPS, S5_STATE, L * S5_GROUP)
        return to(e_re), to(-e_im)

    of_re, of_im = zout_dir(0, steps + 1)
    ob_re, ob_im = zout_dir(1, L - steps)
    zout = jnp.concatenate([of_re, ob_re, of_im, ob_im], axis=1)

    def taps(d):
        pr, pi = pw_re[:L, d], pw_im[:L, d]
        m_re = pr[..., None] * bb_re[d][None] - pi[..., None] * bb_im[d][None]
        m_im = pr[..., None] * bb_im[d][None] + pi[..., None] * bb_re[d][None]
        return (jnp.einsum('gpn,lgnq->lgpq', c_re[d], m_re, precision=hp)
                - jnp.einsum('gpn,lgnq->lgpq', c_im[d], m_im, precision=hp))

    kf, kb = taps(0), taps(1)
    pad4 = ((0, 0),) * 3
    tm = jnp.stack([jnp.pad(kf[:L - s], ((s, 0),) + pad4) + jnp.pad(kb[:s + 1][::-1], ((0, L - 1 - s),) + pad4)
                    for s in range(L)])
    tmat = jnp.transpose(tm, (2, 0, 4, 1, 3)).reshape(S5_GROUPS, L * S5_GROUP, L * S5_GROUP)

    al = jnp.stack([jnp.concatenate([pw_re[L, 0], pw_re[L, 1]], axis=-1),
                    jnp.concatenate([pw_im[L, 0], pw_im[L, 1]], axis=-1)], axis=1)
    return zin.astype(BF16), tmat.astype(BF16), zout.astype(BF16), al


def _mla_kv(ckv, kpe, wk_ref, wv_ref, pl_ref, k_ref, v_ref):
    cb = ckv.astype(BF16)
    k_ref[...] = (_dot(cb, wk_ref[...]) + _dot(kpe.astype(BF16), pl_ref[...])).astype(BF16)
    v_ref[...] = _dot(cb, wv_ref[...]).astype(BF16)


def _mla_prep_kernel(rblk_ref, p_ref, qg_ref, kvg_ref, wuq_ref, wk_ref, wv_ref, pl_ref, cos_ref, sa_ref, sb_ref,
                     q_ref, k_ref, v_ref, ckv_ref):
    del rblk_ref
    pr = p_ref[...]
    cos, sa, sb = cos_ref[...], sa_ref[...], sb_ref[...]
    quarter = MLA_ROPE // 4
    scale = (MLA_NOPE + MLA_ROPE) ** -0.5
    qn = _rms(pr[:, :MLA_Q_RANK], qg_ref[...], MLA_Q_RANK).astype(BF16)
    q = _dot(qn, wuq_ref[...])
    for hd in range(MLA_HEADS):
        sl = slice(hd * LANE, (hd + 1) * LANE)
        q_ref[:, sl] = (_rope(q[:, sl], cos, sa, sb, quarter) * scale).astype(BF16)
    ckv = _rms(pr[:, MLA_Q_RANK:MLA_Q_RANK + MLA_KV_RANK], kvg_ref[...], MLA_KV_RANK)
    ckv_ref[...] = ckv
    kpe = _rope(pr[:, MLA_Q_RANK + MLA_KV_RANK:], cos, sa, sb, quarter)
    _mla_kv(ckv, kpe, wk_ref, wv_ref, pl_ref, k_ref, v_ref)


def _mla_prep(proj, qg, kvg, wuq, wk, wv, place, tables, rblk):
    n = proj.shape[0]
    tm = TM_PROJ
    hw = MLA_HEADS * LANE
    pw = MLA_Q_RANK + MLA_KV_RANK + LANE
    cos, sa, sb = tables
    tok = lambda i, r: (i, 0)
    const = lambda i, r: (0, 0)
    tab = lambda i, r: (r[i], 0)
    return pl.pallas_call(
        _mla_prep_kernel,
        out_shape=(jax.ShapeDtypeStruct((n, hw), BF16), jax.ShapeDtypeStruct((n, hw), BF16),
                   jax.ShapeDtypeStruct((n, hw), BF16), jax.ShapeDtypeStruct((n, MLA_KV_RANK), F32)),
        grid_spec=pltpu.PrefetchScalarGridSpec(
            num_scalar_prefetch=1, grid=(n // tm,),
            in_specs=[pl.BlockSpec((tm, pw), lambda i, r: (i, 1)),
                      pl.BlockSpec((1, MLA_Q_RANK), const), pl.BlockSpec((1, MLA_KV_RANK), const),
                      pl.BlockSpec((MLA_Q_RANK, hw), const), pl.BlockSpec((MLA_KV_RANK, hw), const),
                      pl.BlockSpec((MLA_KV_RANK, hw), const), pl.BlockSpec((LANE, hw), const),
                      pl.BlockSpec((tm, LANE), tab), pl.BlockSpec((tm, LANE), tab), pl.BlockSpec((tm, LANE), tab)],
            out_specs=[pl.BlockSpec((tm, hw), tok), pl.BlockSpec((tm, hw), tok), pl.BlockSpec((tm, hw), tok),
                       pl.BlockSpec((tm, MLA_KV_RANK), tok)]),
        compiler_params=_params(("arbitrary",), 32),
        name="mla_prep",
    )(rblk, proj, qg, kvg, wuq, wk, wv, place, cos, sa, sb)


def _mla_cache_kernel(c_ref, p_ref, wk_ref, wv_ref, pl_ref, k_ref, v_ref):
    _mla_kv(c_ref[...], p_ref[...], wk_ref, wv_ref, pl_ref, k_ref, v_ref)


def _mla_cache(ckv, kpe, wk, wv, place):
    n = ckv.shape[0]
    tm = TM_PROJ
    hw = MLA_HEADS * LANE
    tok = lambda i: (i, 0)
    const = lambda i: (0, 0)
    return pl.pallas_call(
        _mla_cache_kernel,
        out_shape=(jax.ShapeDtypeStruct((n, hw), BF16), jax.ShapeDtypeStruct((n, hw), BF16)),
        grid=(n // tm,),
        in_specs=[pl.BlockSpec((tm, MLA_KV_RANK), tok), pl.BlockSpec((tm, LANE), tok),
                  pl.BlockSpec((MLA_KV_RANK, hw), const), pl.BlockSpec((MLA_KV_RANK, hw), const),
                  pl.BlockSpec((LANE, hw), const)],
        out_specs=[pl.BlockSpec((tm, hw), tok), pl.BlockSpec((tm, hw), tok)],
        compiler_params=_params(("arbitrary",), 32),
        name="mla_cache_kv",
    )(ckv, kpe, wk, wv, place)


def _gelu_tanh(x):
    return 0.5 * x * (1.0 + jnp.tanh(math.sqrt(2.0 / math.pi) * (x + 0.044715 * (x * x * x))))


def _odd_out_kernel(rows_ref, u_ref, yp_ref, ys_ref, ap_ref, as_ref, x_ref, g_ref, d_ref, wglu_ref, bglu_ref, w_ref,
                    lg_ref, lb_ref, o_ref, y_s, *, alpha, n_ctx_tiles):
    del rows_ref
    i = pl.program_id(0)
    yg = _pick_pass(i, n_ctx_tiles, yp_ref, ys_ref)
    n_chunk = yg.shape[1]
    n_tile = S5_WIDTH // LANE
    gpt = LANE // S5_GROUP
    for s in range(S5_CHUNK):
        sl = slice(s * S5_GROUP, (s + 1) * S5_GROUP)
        for j in range(n_tile):
            y_s[j, pl.ds(s, n_chunk, stride=S5_CHUNK), :] = jnp.concatenate(
                [yg[g][:, sl] for g in range(j * gpt, (j + 1) * gpt)], axis=1)
    y_ssm = jnp.concatenate([y_s[j] for j in range(n_tile)], axis=1)
    y = _gelu_tanh(u_ref[...] * d_ref[...] + y_ssm)
    y = y * jax.nn.sigmoid(_dot(y.astype(BF16), wglu_ref[...]) + bglu_ref[...])
    attn = _pick_pass(i, n_ctx_tiles, ap_ref, as_ref)
    out = _dot(y.astype(BF16), w_ref[:S5_WIDTH, :]) + _dot(attn, w_ref[S5_WIDTH:, :])
    o_ref[...] = _layer_norm(alpha * x_ref[...] + g_ref[0] * out, lg_ref[...], lb_ref[...])


def _odd_out(proj, y_p, y_s, attn_p, attn_s, x, g1, s5_d, w_glu, b_glu, w_out, ln_g, ln_b, rows, alpha):
    n, d = x.shape
    tm = TM_PROJ
    aw = attn_p.shape[1]
    cpt = tm // S5_CHUNK
    n_ctx, n_lat = attn_p.shape[0] // tm, attn_s.shape[0] // tm
    ap_spec, as_spec = _pass_specs((tm, aw), n_ctx, n_lat)
    yp_spec, ys_spec = _pass_specs((S5_GROUPS, cpt, S5_CHUNK * S5_GROUP), n_ctx, n_lat)
    tok = lambda i, r: (i, 0)
    const = lambda i, r: (0, 0)
    return pl.pallas_call(
        functools.partial(_odd_out_kernel, alpha=alpha, n_ctx_tiles=n_ctx),
        out_shape=jax.ShapeDtypeStruct((n, d), F32),
        grid_spec=pltpu.PrefetchScalarGridSpec(
            num_scalar_prefetch=1, grid=(n // tm,),
            in_specs=[pl.BlockSpec((tm, S5_WIDTH), tok), yp_spec, ys_spec,
                      ap_spec, as_spec, pl.BlockSpec((tm, d), tok),
                      pl.BlockSpec((1, 1, d), lambda i, r: (r[i], 0, 0)),
                      pl.BlockSpec((1, S5_WIDTH), const), pl.BlockSpec((S5_WIDTH, S5_WIDTH), const),
                      pl.BlockSpec((1, S5_WIDTH), const), pl.BlockSpec((S5_WIDTH + aw, d), const),
                      pl.BlockSpec((1, d), const), pl.BlockSpec((1, d), const)],
            out_specs=pl.BlockSpec((tm, d), tok),
            scratch_shapes=[pltpu.VMEM((S5_WIDTH // LANE, tm, LANE), F32)]),
        compiler_params=_params(("arbitrary",), 40),
        name="odd_out_proj",
    )(rows, proj, y_p, y_s, attn_p, attn_s, x, g1, s5_d, w_glu, b_glu, w_out, ln_g, ln_b)


def _router_kernel(rows_ref, x_ref, sc_ref, sh_ref, rt_ref, hb_ref, pos_ref, gate_ref, pt_ref, cnt_ref, tri_s):
    del rows_ref
    t = x_ref.shape[0]
    ne = N_EXPERTS

    @pl.when(pl.program_id(0) == 0)
    def _():
        before = lax.broadcasted_iota(jnp.int32, (t, t), 0) < lax.broadcasted_iota(jnp.int32, (t, t), 1)
        tri_s[...] = jnp.where(before, 1.0, 0.0).astype(BF16)

    h = x_ref[...] * (1.0 + sc_ref[0]) + sh_ref[0]
    h_hi, h_lo = _split(h)
    hb_ref[...] = h_hi
    r_hi, r_lo = _split(rt_ref[...])
    logits = _dot_nt(r_hi, h_hi) + _dot_nt(r_lo, h_hi) + _dot_nt(r_hi, h_lo)
    eid = lax.broadcasted_iota(jnp.int32, (ne, t), 0).astype(F32)
    m0 = jnp.max(logits, axis=0, keepdims=True)
    i0 = jnp.min(jnp.where(logits == m0, eid, float(ne)), axis=0, keepdims=True)
    rest = jnp.where(eid == i0, -jnp.inf, logits)
    m1 = jnp.max(rest, axis=0, keepdims=True)
    i1 = jnp.min(jnp.where(rest == m1, eid, float(ne)), axis=0, keepdims=True)
    ex = jnp.exp(m1 - m0)
    g0 = 1.0 / (1.0 + ex)
    g1 = ex / (1.0 + ex)
    sel0 = eid == i0
    sel1 = eid == i1
    member = jnp.where(sel0, 1.0, jnp.where(sel1, 1.0, 0.0))
    gate = jnp.where(sel0, g0, jnp.where(sel1, g1, 0.0))
    rank = _dot(member.astype(BF16), tri_s[...])
    pos = jnp.where(member > 0.0, rank, -1.0)
    pos_ref[0] = pos.astype(jnp.int32)
    gate_ref[0] = gate
    cnt = jnp.sum(member, axis=1, keepdims=True)
    cnt_ref[0] = jnp.broadcast_to(cnt, (ne, LANE)).astype(jnp.int32)
    packed = jnp.concatenate([pos, gate, jnp.zeros((LANE - 2 * ne, t), F32)], axis=0)
    pt_ref[...] = packed.T


def _router(x, sc, sh, router_t, rows):
    n, d = x.shape
    t = T_MOE
    nb = n // t
    ne = N_EXPERTS
    tok = lambda i, r: (i, 0)
    row3 = lambda i, r: (r[i], 0, 0)
    blk3 = lambda i, r: (i, 0, 0)
    return pl.pallas_call(
        _router_kernel,
        out_shape=(jax.ShapeDtypeStruct((n, d), BF16), jax.ShapeDtypeStruct((nb, ne, t), jnp.int32),
                   jax.ShapeDtypeStruct((nb, ne, t), F32),
                   jax.ShapeDtypeStruct((n, LANE), F32), jax.ShapeDtypeStruct((nb, ne, LANE), jnp.int32)),
        grid_spec=pltpu.PrefetchScalarGridSpec(
            num_scalar_prefetch=1, grid=(nb,),
            in_specs=[pl.BlockSpec((t, d), tok), pl.BlockSpec((1, 1, d), row3), pl.BlockSpec((1, 1, d), row3),
                      pl.BlockSpec((ne, d), lambda i, r: (0, 0))],
            out_specs=[pl.BlockSpec((t, d), tok), pl.BlockSpec((1, ne, t), blk3), pl.BlockSpec((1, ne, t), blk3),
                       pl.BlockSpec((t, LANE), tok), pl.BlockSpec((1, ne, LANE), blk3)],
            scratch_shapes=[pltpu.VMEM((t, t), BF16)]),
        compiler_params=_params(("arbitrary",), 48),
        name="moe_router",
    )(rows, x, sc, sh, router_t)


def _segment_copies(rows, src, dst, src0, dst0, sems):
    out = []
    for k, sz in enumerate(SEG_SIZES):
        off = (rows // (2 * sz)) * (2 * sz)
        s0 = pl.multiple_of(src0 + off, SEG_ALIGN)
        d0 = pl.multiple_of(dst0 + off, SEG_ALIGN)
        out.append(((rows & sz) != 0,
                    pltpu.make_async_copy(src.at[pl.ds(s0, sz)], dst.at[pl.ds(d0, sz)], sems.at[k])))
    return out


def _run_copies(copies):
    for pred, cp in copies:
        pl.when(pred)(cp.start)
    for pred, cp in copies:
        pl.when(pred)(cp.wait)


def _moe_gather_kernel(cnt_ref, base_ref, h_ref, pos_ref, gate_ref, xs_in, gs_in, xs_ref, gs_ref,
                       xe_s, ge_s, sem_x, sem_g):
    del xs_in, gs_in
    b, e = pl.program_id(0), pl.program_id(1)
    t = h_ref.shape[0]
    rt = ROW_MOE
    cnt = cnt_ref[b * N_EXPERTS + e]
    base = base_ref[b * N_EXPERTS + e]
    pos_row = pos_ref[0, pl.ds(e, 1), :]
    gate_row = gate_ref[0, pl.ds(e, 1), :]

    def gather(r, c):
        r0 = pl.multiple_of(r * rt, rt)
        hit = (lax.broadcasted_iota(jnp.int32, (rt, t), 0) + r0) == pos_row
        onehot = jnp.where(hit, 1.0, 0.0).astype(BF16)
        xe_s[pl.ds(r0, rt), :] = _dot(onehot, h_ref[...]).astype(BF16)
        g = jnp.sum(jnp.where(hit, gate_row, 0.0), axis=1, keepdims=True)
        ge_s[pl.ds(r0, rt), :] = jnp.broadcast_to(g, (rt, LANE))
        return c

    lax.fori_loop(0, (cnt + rt - 1) // rt, gather, 0)
    rows = ((cnt + SEG_ALIGN - 1) // SEG_ALIGN) * SEG_ALIGN
    _run_copies(_segment_copies(rows, xe_s, xs_ref, 0, base, sem_x)
                + _segment_copies(rows, ge_s, gs_ref, 0, base, sem_g))


def _moe_gather(h, pos, gate, counts, base, n_rows):
    n, d = h.shape
    t = T_MOE
    nb = n // t
    ne = N_EXPERTS
    tok = lambda b, e, c, o: (b, 0)
    blk3 = lambda b, e, c, o: (b, 0, 0)
    any_spec = pl.BlockSpec(memory_space=pl.ANY)
    xs0 = jnp.zeros((n_rows, d), BF16)
    gs0 = jnp.zeros((n_rows, LANE), F32)
    return pl.pallas_call(
        _moe_gather_kernel,
        out_shape=(jax.ShapeDtypeStruct((n_rows, d), BF16), jax.ShapeDtypeStruct((n_rows, LANE), F32)),
        grid_spec=pltpu.PrefetchScalarGridSpec(
            num_scalar_prefetch=2, grid=(nb, ne),
            in_specs=[pl.BlockSpec((t, d), tok), pl.BlockSpec((1, ne, t), blk3), pl.BlockSpec((1, ne, t), blk3),
                      any_spec, any_spec],
            out_specs=[any_spec, any_spec],
            scratch_shapes=[pltpu.VMEM((t, d), BF16), pltpu.VMEM((t, LANE), F32),
                            pltpu.SemaphoreType.DMA((len(SEG_SIZES),)), pltpu.SemaphoreType.DMA((len(SEG_SIZES),))]),
        input_output_aliases={5: 0, 6: 1},
        compiler_params=_params(("arbitrary", "arbitrary"), 32),
        name="moe_gather",
    )(counts, base, h, pos, gate, xs0, gs0)


def _moe_gmm_kernel(te_ref, valid_ref, x_ref, gs_ref, wg_ref, wu_ref, wd_ref, y_ref, acc_s):
    del te_ref
    i, f = pl.program_id(0), pl.program_id(1)
    last = pl.num_programs(1) - 1
    valid = valid_ref[i] != 0

    @pl.when(valid)
    def _():
        x = x_ref[...]
        a = (_silu(_dot(x, wg_ref[0])) * _dot(x, wu_ref[0])).astype(BF16)
        y = _dot(a, wd_ref[0])

        @pl.when(f == 0)
        def _():
            acc_s[...] = y

        @pl.when(f > 0)
        def _():
            acc_s[...] += y

        @pl.when(f == last)
        def _():
            y_ref[...] = (acc_s[...] * gs_ref[:, 0:1]).astype(BF16)

    @pl.when(jnp.logical_not(valid) & (f == last))
    def _():
        y_ref[...] = jnp.zeros_like(y_ref)


def _moe_gmm(xs, gs, tile_expert, tile_valid, wg, wu, wd, layer):
    n_rows, d = xs.shape
    rt = RT_GMM
    ff = wg.shape[-1]
    fc = MOE_CHUNK
    tok = lambda i, f, te, tv: (i, 0)
    wmode = dict(pipeline_mode=pl.Buffered(1)) if fc == ff else {}
    return pl.pallas_call(
        _moe_gmm_kernel,
        out_shape=jax.ShapeDtypeStruct((n_rows, d), BF16),
        grid_spec=pltpu.PrefetchScalarGridSpec(
            num_scalar_prefetch=2, grid=(n_rows // rt, ff // fc),
            in_specs=[pl.BlockSpec((rt, d), tok), pl.BlockSpec((rt, LANE), tok),
                      pl.BlockSpec((None, 1, d, fc), lambda i, f, te, tv: (layer, te[i], 0, f), **wmode),
                      pl.BlockSpec((None, 1, d, fc), lambda i, f, te, tv: (layer, te[i], 0, f), **wmode),
                      pl.BlockSpec((None, 1, fc, d), lambda i, f, te, tv: (layer, te[i], f, 0), **wmode)],
            out_specs=pl.BlockSpec((rt, d), tok),
            scratch_shapes=[pltpu.VMEM((rt, d), F32)]),
        compiler_params=_params(("arbitrary", "arbitrary"), 56),
        name="moe_gmm",
    )(tile_expert, tile_valid, xs, gs, wg, wu, wd)


def _moe_combine_kernel(cnt_ref, base_ref, rows_ref, ys_ref, pt_ref, x_ref, g_ref, lg_ref, lb_ref, o_ref,
                        ycat_s, sems, *, alpha):
    del rows_ref
    b = pl.program_id(0)
    t = x_ref.shape[0]
    cap = ycat_s.shape[0]
    ne = N_EXPERTS

    @pl.when(b == 0)
    def _():
        ycat_s[...] = jnp.zeros_like(ycat_s)

    copies = []
    offs = []
    off = 0
    for e in range(ne):
        cnt = cnt_ref[b * ne + e]
        rows = ((cnt + SEG_ALIGN - 1) // SEG_ALIGN) * SEG_ALIGN
        copies += _segment_copies(rows, ys_ref, ycat_s, base_ref[b * ne + e], off, sems.at[e])
        offs.append(off)
        off = off + rows
    _run_copies(copies)

    pt = pt_ref[...]
    lane = lax.broadcasted_iota(jnp.int32, (1, LANE), 1)
    off_lane = jnp.zeros((1, LANE), F32)
    for e in range(1, ne):
        off_lane = jnp.where(lane == e, offs[e].astype(F32), off_lane)
    chosen = (pt >= 0.0) & (lax.broadcasted_iota(jnp.int32, (t, LANE), 1) < ne)
    slot = pt + off_lane
    s_lo = jnp.min(jnp.where(chosen, slot, float(cap)), axis=1, keepdims=True)
    s_hi = jnp.max(jnp.where(chosen, slot, -1.0), axis=1, keepdims=True)
    n_piece = 3
    wp = cap // n_piece
    f = None
    for j in range(n_piece):
        col = (lax.broadcasted_iota(jnp.int32, (t, wp), 1) + j * wp).astype(F32)
        sel = jnp.where(col == s_lo, 1.0, jnp.where(col == s_hi, 1.0, 0.0)).astype(BF16)
        part = _dot(sel, ycat_s[j * wp:(j + 1) * wp, :])
        f = part if f is None else f + part
    o_ref[...] = _layer_norm(alpha * x_ref[...] + g_ref[0] * f, lg_ref[...], lb_ref[...])


def _moe_combine(ys, pt, x, g2, ln_g, ln_b, counts, base, rows, alpha):
    n, d = x.shape
    t = T_MOE
    ne = N_EXPERTS
    cap = -(-(2 * t + ne * (SEG_ALIGN - 1)) // (6 * LANE)) * (6 * LANE)
    tok = lambda b, c, o, r: (b, 0)
    const = lambda b, c, o, r: (0, 0)
    return pl.pallas_call(
        functools.partial(_moe_combine_kernel, alpha=alpha),
        out_shape=jax.ShapeDtypeStruct((n, d), F32),
        grid_spec=pltpu.PrefetchScalarGridSpec(
            num_scalar_prefetch=3, grid=(n // t,),
            in_specs=[pl.BlockSpec(memory_space=pl.ANY), pl.BlockSpec((t, LANE), tok), pl.BlockSpec((t, d), tok),
                      pl.BlockSpec((1, 1, d), lambda b, c, o, r: (r[b], 0, 0)),
                      pl.BlockSpec((1, d), const), pl.BlockSpec((1, d), const)],
            out_specs=pl.BlockSpec((t, d), tok),
            scratch_shapes=[pltpu.VMEM((cap, d), BF16), pltpu.SemaphoreType.DMA((ne, len(SEG_SIZES)))]),
        compiler_params=_params(("arbitrary",), 48),
        name="moe_combine",
    )(counts, base, rows, ys, pt, x, g2, ln_g, ln_b)


def _moe_layout(cnt, n_tiles):
    cnt16 = (cnt + SEG_ALIGN - 1) // SEG_ALIGN * SEG_ALIGN
    rows_e = jnp.sum(cnt16, axis=0)
    rows_e = (rows_e + RT_GMM - 1) // RT_GMM * RT_GMM
    ends = jnp.cumsum(rows_e)
    base = (ends - rows_e)[None, :] + jnp.cumsum(cnt16, axis=0) - cnt16
    tile_end = ends // RT_GMM
    idx = jnp.arange(n_tiles, dtype=jnp.int32)
    valid = idx < tile_end[-1]
    te = jnp.sum((idx[:, None] >= tile_end[None, :]).astype(jnp.int32), axis=1)
    last = jnp.sum(((tile_end[-1] - 1) >= tile_end).astype(jnp.int32))
    te = jnp.where(valid, te, last)
    return base.reshape(-1).astype(jnp.int32), te.astype(jnp.int32), valid.astype(jnp.int32)


def _tile_rows(n, p, ss, tm):
    starts = np.arange(n // tm) * tm
    return jnp.asarray(np.where(starts < p, 0, 1 + (starts - p) // ss).astype(np.int32))


def _tile_rope_blocks(n, p, ss, tm):
    starts = np.arange(n // tm) * tm
    return jnp.asarray(np.where(starts < p, 0, 1 + ((starts - p) % ss) // tm).astype(np.int32))


def _tile_edges(n, p, sp, ss, tm):
    starts = np.arange(n // tm) * tm
    pos = np.where(starts < p, starts % sp, (starts - p) % ss)
    seq = np.where(starts < p, sp, ss)
    return jnp.asarray((pos == 0).astype(np.int32)), jnp.asarray((pos + tm == seq).astype(np.int32))


def _rope_tables(n_tokens, rot_dim, lane0, tm):
    rows = n_tokens // GRID_W
    row_pos = jnp.repeat(jnp.arange(rows, dtype=F32), GRID_W)
    col_pos = jnp.tile(jnp.arange(GRID_W, dtype=F32), rows)
    half = rot_dim // 2
    qtr = half // 2
    inv_freq = ROPE_THETA ** (-jnp.arange(0, half, 2, dtype=F32) / half)
    ang_r = row_pos[:, None] * inv_freq
    ang_c = col_pos[:, None] * inv_freq
    ang = jnp.concatenate([ang_r, ang_r, ang_c, ang_c], axis=-1)
    cos, sin = jnp.cos(ang), jnp.sin(ang)
    first = ((np.arange(rot_dim) % half) < qtr).astype(np.float32)
    sa = -sin * first
    sb = sin * (1.0 - first)

    def place(tbl, fill):
        full = jnp.full((n_tokens, LANE), fill, F32).at[:, lane0:lane0 + rot_dim].set(tbl)
        return jnp.concatenate([jnp.full((tm, LANE), fill, F32), full], axis=0)

    return place(cos, 1.0), place(sa, 0.0), place(sb, 0.0)


def _pad_heads(w, n_heads, width):
    lead = w.shape[:-1]
    w = w.reshape(lead + (n_heads, width))
    w = jnp.pad(w, [(0, 0)] * len(lead) + [(0, 0), (0, LANE - width)])
    return w.reshape(lead + (n_heads * LANE,))


def _pad_head_rows(w, n_heads, width):
    d = w.shape[-1]
    w = w.reshape(n_heads, width, d)
    return jnp.pad(w, ((0, 0), (0, LANE - width), (0, 0))).reshape(n_heads * LANE, d)


def kernel(x_prompt, x_sample, cache_attn_k, cache_attn_v, cache_mla_ckv, cache_mla_kpe, state_ssm_re, state_ssm_im, c, c_ctx, ada_w, ada_b, ln_g, ln_b, ev_w_in, ev_conv_w, ev_q_gain, ev_k_gain, ev_w_out, ffn_w_gate, ffn_w_up, ffn_w_down, od_w_in, s5_a_re, s5_a_im, s5_log_dt, s5_b_re, s5_b_im, s5_c_re, s5_c_im, s5_d, s5_w_glu, s5_b_glu, mla_q_gain, mla_w_uq, mla_kv_gain, mla_w_ukv, od_w_out, moe_router, moe_w_gate, moe_w_up, moe_w_down):
    bp, sp, d = x_prompt.shape
    bs, ss, _ = x_sample.shape
    depth = ada_w.shape[0]
    alpha = (2 * depth) ** 0.25
    p = bp * sp
    n = p + bs * ss
    past = cache_attn_k.shape[2]
    for tm in (TM_PROJ, TM_FFN, T_MOE):
        assert p % tm == 0 and ss % tm == 0 and (sp % tm == 0 or tm % sp == 0)
    assert sp % TM_PROJ == 0 and sp % S5_CHUNK == 0 and ss % S5_CHUNK == 0

    x = jnp.concatenate([x_prompt.reshape(p, d), x_sample.reshape(bs * ss, d)], axis=0)

    nrow = -(-(1 + bs) // SUBLANE) * SUBLANE
    cond = jnp.zeros((nrow, d), F32).at[0].set(c_ctx).at[1:1 + bs].set(c)
    mod = _modulation(cond, ada_w, ada_b)

    def mod_part(l, k):
        return mod[l, :, k * d:(k + 1) * d].reshape(nrow, 1, d)

    rows_proj = _tile_rows(n, p, ss, TM_PROJ)
    rows_ffn = _tile_rows(n, p, ss, TM_FFN)
    rows_moe = _tile_rows(n, p, ss, T_MOE)
    rblk = _tile_rope_blocks(n, p, ss, TM_PROJ)
    first, last = _tile_edges(n, p, sp, ss, TM_PROJ)
    tables_even = _rope_tables(ss, HEAD_DIM, 0, TM_PROJ)
    tables_mla = _rope_tables(ss, MLA_ROPE, MLA_NOPE, TM_PROJ)

    moe_wg, moe_wu, moe_wd = moe_w_gate.astype(BF16), moe_w_up.astype(BF16), moe_w_down.astype(BF16)
    ffn_wg, ffn_wu, ffn_wd = ffn_w_gate.astype(BF16), ffn_w_up.astype(BF16), ffn_w_down.astype(BF16)
    out_k, out_v, out_ckv, out_kpe, out_sre, out_sim = [], [], [], [], [], []
    cw = CONV_WIDTH
    hq = ATTN_HEADS * HEAD_DIM
    hkv = ATTN_KV_HEADS * HEAD_DIM
    for l in range(depth):
        i = l // 2
        sh1, sc1, g1, sh2, sc2, g2 = [mod_part(l, k) for k in range(6)]
        lg = ln_g[l].reshape(2, 1, d)
        lb = ln_b[l].reshape(2, 1, d)
        if l % 2 == 0:
            w_in = ev_w_in[i]
            wc = w_in[:, :3 * cw].astype(BF16)
            wq = _pad_heads(w_in[:, 3 * cw:3 * cw + hq], ATTN_HEADS, HEAD_DIM).astype(BF16)
            wkv = jnp.concatenate([_pad_heads(w_in[:, 3 * cw + hq:3 * cw + hq + hkv], ATTN_KV_HEADS, HEAD_DIM),
                                   _pad_heads(w_in[:, 3 * cw + hq + hkv:], ATTN_KV_HEADS, HEAD_DIM)],
                                  axis=1).astype(BF16)
            qg = jnp.pad(ev_q_gain[i], (0, LANE - HEAD_DIM)).reshape(1, LANE)
            kg = jnp.pad(ev_k_gain[i], (0, LANE - HEAD_DIM)).reshape(1, LANE)
            conv, q, k, v, kn, vf = _even_in(x, sc1, sh1, wc, wq, wkv, qg, kg, tables_even, rows_proj, rblk)
            kvw = ATTN_KV_HEADS * LANE
            out_k.append(kn[:p].reshape(bp, sp, ATTN_KV_HEADS, LANE)[..., :HEAD_DIM])
            out_v.append(vf[:p].reshape(bp, sp, ATTN_KV_HEADS, LANE)[..., :HEAD_DIM])
            group = ATTN_HEADS // ATTN_KV_HEADS
            qw = ATTN_HEADS * LANE
            a_p = _attention(q, [(k, v, True)], ATTN_HEADS, group, bp, sp, 0)
            kc = _pad_heads(cache_attn_k[:, i].reshape(bs, past, hkv), ATTN_KV_HEADS, HEAD_DIM).astype(BF16)
            vc = _pad_heads(cache_attn_v[:, i].reshape(bs, past, hkv), ATTN_KV_HEADS, HEAD_DIM).astype(BF16)
            a_s = _attention(q, [(kc, vc, False), (k, v, True)], ATTN_HEADS, group, bs, ss, p)
            w_out = jnp.concatenate([ev_w_out[i][:cw], _pad_head_rows(ev_w_out[i][cw:], ATTN_HEADS, HEAD_DIM)],
                                    axis=0).astype(BF16)
            conv_w = jnp.pad(ev_conv_w[i], ((0, SUBLANE - ev_conv_w.shape[1]), (0, 0)))
            x = _even_out(conv, a_p.reshape(p, qw), a_s.reshape(bs * ss, qw), x, g1, conv_w, w_out, lg[0], lb[0],
                          first, last, rows_proj, alpha)
            x = _ffn(x, sc2, sh2, g2, ffn_wg, ffn_wu, ffn_wd, i, lg[1], lb[1], rows_ffn, alpha)
        else:
            q_end = S5_WIDTH + MLA_Q_RANK
            kv_end = q_end + MLA_KV_RANK
            w_in = od_w_in[i]
            w_in = jnp.concatenate([w_in[:, :kv_end], jnp.zeros((d, MLA_NOPE), F32), w_in[:, kv_end:],
                                    jnp.zeros((d, LANE - MLA_NOPE - MLA_ROPE), F32)], axis=1).astype(BF16)
            m_ctx, m_lat = p // S5_CHUNK, bs * ss // S5_CHUNK
            ctx_chunk0, lat_chunk0 = (m_lat, 0) if m_lat >= m_ctx else (0, m_ctx)
            assert ctx_chunk0 % m_ctx == 0 and lat_chunk0 % m_lat == 0
            proj, u = _odd_in(x, sc1, sh1, w_in, rows_proj, p // TM_PROJ, ctx_chunk0, lat_chunk0)
            out_kpe.append(proj[:p, kv_end + MLA_NOPE:kv_end + MLA_NOPE + MLA_ROPE].reshape(bp, sp, MLA_ROPE))

            mats = _s5_matrices(s5_a_re[i], s5_a_im[i], s5_log_dt[i], s5_b_re[i], s5_b_im[i], s5_c_re[i], s5_c_im[i])
            h0_p = jnp.zeros((S5_GROUPS, bp, 4 * S5_STATE), F32)
            y_p, fin_p = _s5(u, mats, h0_p, bp, sp // S5_CHUNK, ctx_chunk0 // m_ctx)

            def pack_state(re, im):
                to = lambda a: a.transpose(2, 0, 1, 3).reshape(S5_GROUPS, a.shape[0], 2 * S5_STATE)
                return jnp.concatenate([to(re), to(im)], axis=-1)

            y_s, _ = _s5(u, mats, pack_state(state_ssm_re[:, i], state_ssm_im[:, i]), bs, ss // S5_CHUNK,
                         lat_chunk0 // m_lat)
            fin = fin_p.reshape(S5_GROUPS, bp, 2, 2, S5_STATE)
            out_sre.append(fin[:, :, 0].transpose(1, 2, 0, 3))
            out_sim.append(fin[:, :, 1].transpose(1, 2, 0, 3))

            qk = MLA_NOPE + MLA_ROPE
            wuq = _pad_heads(mla_w_uq[i], MLA_HEADS, qk).astype(BF16)
            wukv = mla_w_ukv[i].reshape(MLA_KV_RANK, MLA_HEADS, MLA_NOPE + MLA_V)
            wk = _pad_heads(wukv[..., :MLA_NOPE].reshape(MLA_KV_RANK, -1), MLA_HEADS, MLA_NOPE).astype(BF16)
            wv = _pad_heads(wukv[..., MLA_NOPE:].reshape(MLA_KV_RANK, -1), MLA_HEADS, MLA_V).astype(BF16)
            place = np.zeros((LANE, MLA_HEADS * LANE), np.float32)
            for hd in range(MLA_HEADS):
                for j in range(MLA_ROPE):
                    place[MLA_NOPE + j, hd * LANE + MLA_NOPE + j] = 1.0
            place = jnp.asarray(place, BF16)
            q, k, v, ckv = _mla_prep(proj, mla_q_gain[i].reshape(1, -1), mla_kv_gain[i].reshape(1, -1),
                                     wuq, wk, wv, place, tables_mla, rblk)
            out_ckv.append(ckv[:p].reshape(bp, sp, MLA_KV_RANK))
            hw = MLA_HEADS * LANE
            a_p = _attention(q, [(k, v, True)], MLA_HEADS, 1, bp, sp, 0)
            kpe_c = jnp.pad(cache_mla_kpe[:, i].reshape(bs * past, MLA_ROPE),
                            ((0, 0), (MLA_NOPE, LANE - MLA_NOPE - MLA_ROPE)))
            kc, vc = _mla_cache(cache_mla_ckv[:, i].reshape(bs * past, MLA_KV_RANK), kpe_c, wk, wv, place)
            a_s = _attention(q, [(kc.reshape(bs, past, hw), vc.reshape(bs, past, hw), False), (k, v, True)],
                             MLA_HEADS, 1, bs, ss, p)
            w_out = jnp.concatenate([od_w_out[i][:S5_WIDTH], _pad_head_rows(od_w_out[i][S5_WIDTH:], MLA_HEADS, MLA_V)],
                                    axis=0).astype(BF16)
            x = _odd_out(proj, y_p, y_s, a_p.reshape(p, hw), a_s.reshape(bs * ss, hw), x, g1, s5_d[i].reshape(1, -1),
                         s5_w_glu[i].astype(BF16), s5_b_glu[i].reshape(1, -1), w_out, lg[0], lb[0], rows_proj, alpha)

            hb, pos, gate, pt, cnt = _router(x, sc2, sh2, moe_router[i].T, rows_moe)
            cnt = cnt[:, :, 0]
            nblk = n // T_MOE
            n_tiles = -(-(2 * n + nblk * N_EXPERTS * (SEG_ALIGN - 1)) // RT_GMM) + N_EXPERTS
            base, tile_expert, tile_valid = _moe_layout(cnt, n_tiles)
            counts = cnt.reshape(-1)
            xs, gs = _moe_gather(hb, pos, gate, counts, base, n_tiles * RT_GMM)
            ys = _moe_gmm(xs, gs, tile_expert, tile_valid, moe_wg, moe_wu, moe_wd, i)
            x = _moe_combine(ys, pt, x, g2, lg[1], lb[1], counts, base, rows_moe, alpha)

    y_prompt = x[:p].reshape(bp, sp, d)
    y_sample = x[p:].reshape(bs, ss, d)
    return (y_prompt, y_sample, jnp.stack(out_k, axis=1), jnp.stack(out_v, axis=1),
            jnp.stack(out_ckv, axis=1), jnp.stack(out_kpe, axis=1),
            jnp.stack(out_sre, axis=1), jnp.stack(out_sim, axis=1))
```

```python
import functools
import math

import jax
import jax.numpy as jnp
import numpy as np
from jax import lax
from jax.experimental import pallas as pl
from jax.experimental.pallas import tpu as pltpu

F32 = jnp.float32
BF16 = jnp.bfloat16

LANE = 128
SUBLANE = 8
MIB = 1024 * 1024

GRID_W = 64
ROPE_THETA = 10000.0
LN_EPS = 1e-6
RMS_EPS = 1e-6
HEAD_DIM = 64
ATTN_HEADS = 8
ATTN_KV_HEADS = 2
CONV_WIDTH = 512
S5_WIDTH = 512
S5_GROUP = 16
S5_GROUPS = 32
S5_STATE = 64
S5_CHUNK = 16
MLA_HEADS = 8
MLA_Q_RANK = 256
MLA_KV_RANK = 128
MLA_NOPE = 64
MLA_ROPE = 32
MLA_V = 64
N_EXPERTS = 8

TM_PROJ = 512
TM_EVEN_IN = 256
TM_FFN = 512
FFN_CHUNK = 2816
T_MOE = 1024
ROW_MOE = 128
RT_GMM = 512
MOE_CHUNK = 3584
SEG_ALIGN = 16
SEG_SIZES = tuple(T_MOE >> k for k in range(T_MOE.bit_length()) if (T_MOE >> k) >= SEG_ALIGN)
TQ_ATTN = 512


def _params(sem, vmem_mib):
    return pltpu.CompilerParams(dimension_semantics=sem, vmem_limit_bytes=vmem_mib * MIB)


def _dot(a, b):
    return jnp.dot(a, b, preferred_element_type=F32)


def _dot_nt(a, b):
    return lax.dot_general(a, b, (((1,), (1,)), ((), ())), preferred_element_type=F32)


def _split(a):
    hi = a.astype(BF16)
    lo = (a - hi.astype(F32)).astype(BF16)
    return hi, lo


def _silu(x):
    return x * jax.nn.sigmoid(x)


def _layer_norm(r, g, b):
    mu = jnp.mean(r, axis=-1, keepdims=True)
    d = r - mu
    var = jnp.mean(d * d, axis=-1, keepdims=True)
    return d * lax.rsqrt(var + LN_EPS) * g + b


def _rms(x, g, n):
    ms = jnp.sum(x * x, axis=-1, keepdims=True) * (1.0 / n)
    return x * lax.rsqrt(ms + RMS_EPS) * g


def _rope(x, cos, sa, sb, q):
    w = x.shape[-1]
    return x * cos + pltpu.roll(x, w - q, 1) * sa + pltpu.roll(x, q, 1) * sb


def _ada_kernel(c_ref, w_ref, b_ref, o_ref):
    c = c_ref[...]
    a_hi, a_lo = _split(_silu(c))
    w_hi, w_lo = _split(w_ref[0])
    o_ref[0] = _dot(a_hi, w_hi) + _dot(a_lo, w_hi) + _dot(a_hi, w_lo) + b_ref[0]


def _modulation(cond, ada_w, ada_b):
    depth, d, d6 = ada_w.shape
    r = cond.shape[0]
    tn = 1536
    return pl.pallas_call(
        _ada_kernel,
        out_shape=jax.ShapeDtypeStruct((depth, r, d6), F32),
        grid=(depth, d6 // tn),
        in_specs=[pl.BlockSpec((r, d), lambda l, j: (0, 0)),
                  pl.BlockSpec((1, d, tn), lambda l, j: (l, 0, j)),
                  pl.BlockSpec((1, 1, tn), lambda l, j: (l, 0, j))],
        out_specs=pl.BlockSpec((1, r, tn), lambda l, j: (l, 0, j)),
        compiler_params=_params(("arbitrary", "arbitrary"), 40),
        name="ada_modulation",
    )(cond, ada_w, ada_b.reshape(depth, 1, d6))


def _even_in_kernel(rows_ref, rblk_ref, x_ref, sc_ref, sh_ref, wc_ref, wq_ref, wkv_ref, qg_ref, kg_ref,
                    cos_ref, sa_ref, sb_ref, conv_ref, q_ref, k_ref, v_ref, kn_ref, vf_ref):
    del rows_ref, rblk_ref
    h = (x_ref[...] * (1.0 + sc_ref[0]) + sh_ref[0]).astype(BF16)
    conv_ref[...] = _dot(h, wc_ref[...])
    cos, sa, sb = cos_ref[...], sa_ref[...], sb_ref[...]
    quarter = HEAD_DIM // 4
    q = _dot(h, wq_ref[...])
    for hd in range(ATTN_HEADS):
        sl = slice(hd * LANE, (hd + 1) * LANE)
        qn = _rms(q[:, sl], qg_ref[...], HEAD_DIM)
        q_ref[:, sl] = (_rope(qn, cos, sa, sb, quarter) * (HEAD_DIM ** -0.5)).astype(BF16)
    kv = _dot(h, wkv_ref[...])
    kw = ATTN_KV_HEADS * LANE
    for hd in range(ATTN_KV_HEADS):
        sl = slice(hd * LANE, (hd + 1) * LANE)
        kn = _rms(kv[:, sl], kg_ref[...], HEAD_DIM)
        kn_ref[:, sl] = kn
        k_ref[:, sl] = _rope(kn, cos, sa, sb, quarter).astype(BF16)
    v = kv[:, kw:]
    vf_ref[...] = v
    v_ref[...] = v.astype(BF16)


def _even_in(x, sc, sh, wc, wq, wkv, qg, kg, tables, rows, rblk):
    n, d = x.shape
    tm = TM_EVEN_IN
    cw, qw, kvw = wc.shape[1], wq.shape[1], wkv.shape[1]
    kw = kvw // 2
    cos, sa, sb = tables
    row3 = lambda i, rows, rblk: (rows[i], 0, 0)
    tok = lambda i, rows, rblk: (i, 0)
    const = lambda i, rows, rblk: (0, 0)
    tab = lambda i, rows, rblk: (rblk[i], 0)
    return pl.pallas_call(
        _even_in_kernel,
        out_shape=(jax.ShapeDtypeStruct((n, cw), F32), jax.ShapeDtypeStruct((n, qw), BF16),
                   jax.ShapeDtypeStruct((n, kw), BF16), jax.ShapeDtypeStruct((n, kw), BF16),
                   jax.ShapeDtypeStruct((n, kw), F32), jax.ShapeDtypeStruct((n, kw), F32)),
        grid_spec=pltpu.PrefetchScalarGridSpec(
            num_scalar_prefetch=2, grid=(n // tm,),
            in_specs=[pl.BlockSpec((tm, d), tok),
                      pl.BlockSpec((1, 1, d), row3), pl.BlockSpec((1, 1, d), row3),
                      pl.BlockSpec((d, cw), const), pl.BlockSpec((d, qw), const), pl.BlockSpec((d, kvw), const),
                      pl.BlockSpec((1, LANE), const), pl.BlockSpec((1, LANE), const),
                      pl.BlockSpec((tm, LANE), tab), pl.BlockSpec((tm, LANE), tab), pl.BlockSpec((tm, LANE), tab)],
            out_specs=[pl.BlockSpec((tm, cw), tok), pl.BlockSpec((tm, qw), tok),
                       pl.BlockSpec((tm, kw), tok), pl.BlockSpec((tm, kw), tok),
                       pl.BlockSpec((tm, kw), tok), pl.BlockSpec((tm, kw), tok)]),
        compiler_params=_params(("arbitrary",), 40),
        name="even_in_proj",
    )(rows, rblk, x, sc, sh, wc, wq, wkv, qg, kg, cos, sa, sb)


def _attn_kernel(*refs, n_heads, group, n_seg):
    q_ref = refs[0]
    seg = refs[1:1 + 2 * n_seg]
    o_ref = refs[1 + 2 * n_seg]
    for hd in range(n_heads):
        sl = slice(hd * LANE, (hd + 1) * LANE)
        ks = slice((hd // group) * LANE, (hd // group + 1) * LANE)
        qh = q_ref[:, sl]
        scores = [_dot_nt(qh, seg[2 * s][:, ks]) for s in range(n_seg)]
        m = jnp.max(scores[0], axis=-1, keepdims=True)
        for s in range(1, n_seg):
            m = jnp.maximum(m, jnp.max(scores[s], axis=-1, keepdims=True))
        den = None
        acc = None
        for s in range(n_seg):
            p = jnp.exp(scores[s] - m)
            ps = jnp.sum(p, axis=-1, keepdims=True)
            pv = _dot(p.astype(BF16), seg[2 * s + 1][:, ks])
            den = ps if den is None else den + ps
            acc = pv if acc is None else acc + pv
        o_ref[:, sl] = (acc * (1.0 / den)).astype(BF16)


def _attention(q, segments, n_heads, group, b, s, row0):
    n, qw = q.shape
    assert n % s == 0 and row0 % s == 0
    b0 = row0 // s
    tq = min(TQ_ATTN, s)
    in_specs = [pl.BlockSpec((None, tq, qw), lambda i, j: (i + b0, j, 0))]
    args = [q.reshape(n // s, s, qw)]
    for k, v, own in segments:
        kw = k.shape[-1]
        if own:
            k, v = k.reshape(n // s, s, kw), v.reshape(n // s, s, kw)
            kmap = lambda i, j: (i + b0, 0, 0)
        else:
            kmap = lambda i, j: (i, 0, 0)
        t = k.shape[1]
        in_specs += [pl.BlockSpec((None, t, kw), kmap, pipeline_mode=pl.Buffered(1)),
                     pl.BlockSpec((None, t, kw), kmap, pipeline_mode=pl.Buffered(1))]
        args += [k, v]
    return pl.pallas_call(
        functools.partial(_attn_kernel, n_heads=n_heads, group=group, n_seg=len(segments)),
        out_shape=jax.ShapeDtypeStruct((b, s, qw), BF16),
        grid=(b, s // tq),
        in_specs=in_specs,
        out_specs=pl.BlockSpec((None, tq, qw), lambda i, j: (i, j, 0)),
        compiler_params=_params(("arbitrary", "arbitrary"), 56),
        name="attention",
    )(*args)


def _pick_pass(i, n_ctx_tiles, ctx_ref, lat_ref):
    return jnp.where(i < n_ctx_tiles, ctx_ref[...], lat_ref[...])


def _pass_specs(block, n_ctx_tiles, n_lat_tiles):
    lead = (0,) * (len(block) - 2)

    def ctx(i, *_):
        return lead + (jnp.minimum(i, n_ctx_tiles - 1), 0)

    def lat(i, *_):
        return lead + (jnp.clip(i - n_ctx_tiles, 0, n_lat_tiles - 1), 0)

    return pl.BlockSpec(block, ctx), pl.BlockSpec(block, lat)


def _even_out_kernel(first_ref, last_ref, rows_ref, c_ref, cp_ref, cn_ref, ap_ref, as_ref, x_ref, g_ref, cw_ref,
                     w_ref, lg_ref, lb_ref, o_ref, *, alpha, n_ctx_tiles, ctx_seq):
    del rows_ref
    i = pl.program_id(0)
    attn = _pick_pass(i, n_ctx_tiles, ap_ref, as_ref)
    tm = c_ref.shape[0]
    cwid = CONV_WIDTH
    c = c_ref[...]
    gate_b, z = c[:, :cwid], c[:, cwid:2 * cwid] * c[:, 2 * cwid:]
    cp = cp_ref[...]
    cn = cn_ref[...]
    zp = cp[SUBLANE - 1:SUBLANE, cwid:2 * cwid] * cp[SUBLANE - 1:SUBLANE, 2 * cwid:]
    zn = cn[0:1, cwid:2 * cwid] * cn[0:1, 2 * cwid:]
    zp = zp * (1 - first_ref[i]).astype(F32)
    zn = zn * (1 - last_ref[i]).astype(F32)
    row = lax.broadcasted_iota(jnp.int32, (tm, cwid), 0)
    z_prev = jnp.where(row == 0, zp, pltpu.roll(z, 1, 0))
    z_next = jnp.where(row == tm - 1, zn, pltpu.roll(z, tm - 1, 0))
    for k in range(1, tm // ctx_seq):
        edge = jnp.where(i < n_ctx_tiles, k * ctx_seq, -1)
        z_prev = jnp.where(row == edge, 0.0, z_prev)
        z_next = jnp.where(row == edge - 1, 0.0, z_next)
    cw = cw_ref[...]
    y = gate_b * (cw[0:1] * z_prev + cw[1:2] * z + cw[2:3] * z_next)
    out = _dot(y.astype(BF16), w_ref[:cwid, :]) + _dot(attn, w_ref[cwid:, :])
    o_ref[...] = _layer_norm(alpha * x_ref[...] + g_ref[0] * out, lg_ref[...], lb_ref[...])


def _even_out(conv, attn_p, attn_s, x, g1, conv_w, w_out, ln_g, ln_b, first, last, rows, alpha, ctx_seq):
    n, d = x.shape
    tm = TM_PROJ
    cw3 = conv.shape[1]
    aw = attn_p.shape[1]
    hb = tm // SUBLANE
    nblk8 = n // SUBLANE
    n_ctx, n_lat = attn_p.shape[0] // tm, attn_s.shape[0] // tm
    ap_spec, as_spec = _pass_specs((tm, aw), n_ctx, n_lat)
    tok = lambda i, f, l, r: (i, 0)
    const = lambda i, f, l, r: (0, 0)
    return pl.pallas_call(
        functools.partial(_even_out_kernel, alpha=alpha, n_ctx_tiles=n_ctx, ctx_seq=ctx_seq),
        out_shape=jax.ShapeDtypeStruct((n, d), F32),
        grid_spec=pltpu.PrefetchScalarGridSpec(
            num_scalar_prefetch=3, grid=(n // tm,),
            in_specs=[pl.BlockSpec((tm, cw3), tok),
                      pl.BlockSpec((SUBLANE, cw3), lambda i, f, l, r: (jnp.maximum(i * hb - 1, 0), 0)),
                      pl.BlockSpec((SUBLANE, cw3), lambda i, f, l, r: (jnp.minimum((i + 1) * hb, nblk8 - 1), 0)),
                      ap_spec, as_spec, pl.BlockSpec((tm, d), tok),
                      pl.BlockSpec((1, 1, d), lambda i, f, l, r: (r[i], 0, 0)),
                      pl.BlockSpec((SUBLANE, CONV_WIDTH), const),
                      pl.BlockSpec((CONV_WIDTH + aw, d), const),
                      pl.BlockSpec((1, d), const), pl.BlockSpec((1, d), const)],
            out_specs=pl.BlockSpec((tm, d), tok)),
        compiler_params=_params(("arbitrary",), 40),
        name="even_out_proj",
    )(first, last, rows, conv, conv, conv, attn_p, attn_s, x, g1, conv_w, w_out, ln_g, ln_b)


def _ffn_kernel(rows_ref, x_ref, sc_ref, sh_ref, g_ref, wg_ref, wu_ref, wd_ref, lg_ref, lb_ref, o_ref,
                h_s, acc_s, *, alpha):
    del rows_ref
    f = pl.program_id(1)

    @pl.when(f == 0)
    def _():
        h_s[...] = (x_ref[...] * (1.0 + sc_ref[0]) + sh_ref[0]).astype(BF16)

    h = h_s[...]
    a = (_silu(_dot(h, wg_ref[...])) * _dot(h, wu_ref[...])).astype(BF16)
    y = _dot(a, wd_ref[...])

    @pl.when(f == 0)
    def _():
        acc_s[...] = y

    @pl.when(f > 0)
    def _():
        acc_s[...] += y

    @pl.when(f == pl.num_programs(1) - 1)
    def _():
        o_ref[...] = _layer_norm(alpha * x_ref[...] + g_ref[0] * acc_s[...], lg_ref[...], lb_ref[...])


def _ffn(x, sc, sh, g2, wg, wu, wd, layer, ln_g, ln_b, rows, alpha):
    n, d = x.shape
    tm = TM_FFN
    ff = wg.shape[-1]
    fc = FFN_CHUNK
    tok = lambda i, f, r: (i, 0)
    row3 = lambda i, f, r: (r[i], 0, 0)
    const = lambda i, f, r: (0, 0)
    wmode = dict(pipeline_mode=pl.Buffered(1)) if fc == ff else {}
    return pl.pallas_call(
        functools.partial(_ffn_kernel, alpha=alpha),
        out_shape=jax.ShapeDtypeStruct((n, d), F32),
        grid_spec=pltpu.PrefetchScalarGridSpec(
            num_scalar_prefetch=1, grid=(n // tm, ff // fc),
            in_specs=[pl.BlockSpec((tm, d), tok),
                      pl.BlockSpec((1, 1, d), row3), pl.BlockSpec((1, 1, d), row3), pl.BlockSpec((1, 1, d), row3),
                      pl.BlockSpec((None, d, fc), lambda i, f, r: (layer, 0, f), **wmode),
                      pl.BlockSpec((None, d, fc), lambda i, f, r: (layer, 0, f), **wmode),
                      pl.BlockSpec((None, fc, d), lambda i, f, r: (layer, f, 0), **wmode),
                      pl.BlockSpec((1, d), const), pl.BlockSpec((1, d), const)],
            out_specs=pl.BlockSpec((tm, d), tok),
            scratch_shapes=[pltpu.VMEM((tm, d), BF16), pltpu.VMEM((tm, d), F32)]),
        compiler_params=_params(("arbitrary", "arbitrary"), 56),
        name="ffn_swiglu",
    )(rows, x, sc, sh, g2, wg, wu, wd, ln_g, ln_b)


def _odd_in_kernel(rows_ref, x_ref, sc_ref, sh_ref, w_ref, o_ref, u_ref, u_s):
    del rows_ref
    h = (x_ref[...] * (1.0 + sc_ref[0]) + sh_ref[0]).astype(BF16)
    proj = _dot(h, w_ref[...])
    o_ref[...] = proj
    n_tile = S5_WIDTH // LANE
    gpt = LANE // S5_GROUP
    for j in range(n_tile):
        u_s[j] = proj[:, j * LANE:(j + 1) * LANE]
    n_chunk = u_s.shape[1] // S5_CHUNK
    steps = [[u_s[j, pl.ds(s, n_chunk, stride=S5_CHUNK), :] for j in range(n_tile)] for s in range(S5_CHUNK)]
    for g in range(S5_GROUPS):
        sl = slice((g % gpt) * S5_GROUP, (g % gpt + 1) * S5_GROUP)
        u_ref[g] = jnp.concatenate([st[g // gpt][:, sl] for st in steps], axis=1).astype(BF16)


def _odd_in(x, sc, sh, w, rows, n_ctx_tiles, ctx_chunk0, lat_chunk0):
    n, d = x.shape
    tm = TM_PROJ
    nw = w.shape[1]
    cpt = tm // S5_CHUNK
    u_rows = n // S5_CHUNK
    tok = lambda i, r: (i, 0)
    row3 = lambda i, r: (r[i], 0, 0)
    ublk = lambda i, r: (0, jnp.where(i < n_ctx_tiles, i + ctx_chunk0 // cpt, i - n_ctx_tiles + lat_chunk0 // cpt), 0)
    return pl.pallas_call(
        _odd_in_kernel,
        out_shape=(jax.ShapeDtypeStruct((n, nw), F32),
                   jax.ShapeDtypeStruct((S5_GROUPS, u_rows, S5_CHUNK * S5_GROUP), BF16)),
        grid_spec=pltpu.PrefetchScalarGridSpec(
            num_scalar_prefetch=1, grid=(n // tm,),
            in_specs=[pl.BlockSpec((tm, d), tok), pl.BlockSpec((1, 1, d), row3), pl.BlockSpec((1, 1, d), row3),
                      pl.BlockSpec((d, nw), lambda i, r: (0, 0))],
            out_specs=[pl.BlockSpec((tm, nw), tok), pl.BlockSpec((S5_GROUPS, cpt, S5_CHUNK * S5_GROUP), ublk)],
            scratch_shapes=[pltpu.VMEM((S5_WIDTH // LANE, tm, LANE), F32)]),
        compiler_params=_params(("arbitrary",), 32),
        name="odd_in_proj",
    )(rows, x, sc, sh, w)


def _s5_kernel(u_ref, zin_ref, t_ref, zout_ref, al_ref, h0_ref, y_ref, fin_ref, z_s, hf_s, hb_s, *, nb, nc):
    u = u_ref[0]
    z = _dot(u, zin_ref[0])
    for part in range(2):
        for b in range(nb):
            z_s[part, pl.ds(b, nc, stride=nb), :] = z[b * nc:(b + 1) * nc, part * LANE:(part + 1) * LANE]
    ar = al_ref[0, 0:1, :]
    ai = al_ref[0, 1:2, :]
    fwd = lax.broadcasted_iota(jnp.int32, (nb, LANE), 1) < S5_STATE
    aligned = (lambda r: pl.multiple_of(r, SUBLANE)) if nb % SUBLANE == 0 else (lambda r: r)

    def step(k, carry):
        re, im = carry
        rf = aligned(k * nb)
        rb = aligned((nc - 1 - k) * nb)
        hf_s[0, pl.ds(rf, nb), :] = re
        hf_s[1, pl.ds(rf, nb), :] = im
        hb_s[0, pl.ds(rb, nb), :] = re
        hb_s[1, pl.ds(rb, nb), :] = im
        zr = jnp.where(fwd, z_s[0, pl.ds(rf, nb), :], z_s[0, pl.ds(rb, nb), :])
        zi = jnp.where(fwd, z_s[1, pl.ds(rf, nb), :], z_s[1, pl.ds(rb, nb), :])
        return ar * re - ai * im + zr, ar * im + ai * re + zi

    h0 = h0_ref[0]
    re, im = lax.fori_loop(0, nc, step, (h0[:, :LANE], h0[:, LANE:]))
    fin_ref[0, :, :LANE] = re
    fin_ref[0, :, LANE:] = im
    m = u.shape[0]
    is_fwd = lax.broadcasted_iota(jnp.int32, (m, LANE), 1) < S5_STATE
    halves = []
    for part in range(2):
        hf_s[part] = jnp.where(is_fwd, hf_s[part], hb_s[part])
        halves.append(jnp.concatenate([hf_s[part, pl.ds(b, nc, stride=nb), :] for b in range(nb)], axis=0))
    h_in = jnp.concatenate(halves, axis=1).astype(BF16)
    y_ref[0] = _dot(u, t_ref[0]) + _dot(h_in, zout_ref[0])


def _s5(u, mats, layer, h0, nb, nc, row_block):
    zin, tmat, zout, al = mats
    g, _, w = u.shape
    m = nb * nc
    blk = lambda i: (i, 0, 0)
    lblk = lambda i: (layer, i, 0, 0)
    return pl.pallas_call(
        functools.partial(_s5_kernel, nb=nb, nc=nc),
        out_shape=(jax.ShapeDtypeStruct((g, m, w), F32), jax.ShapeDtypeStruct((g, nb, w), F32)),
        grid=(g,),
        in_specs=[pl.BlockSpec((1, m, w), lambda i: (i, row_block, 0)),
                  pl.BlockSpec((None, 1, w, w), lblk), pl.BlockSpec((None, 1, w, w), lblk),
                  pl.BlockSpec((None, 1, w, w), lblk), pl.BlockSpec((None, 1, 2, LANE), lblk),
                  pl.BlockSpec((1, nb, w), blk)],
        out_specs=(pl.BlockSpec((1, m, w), blk), pl.BlockSpec((1, nb, w), blk)),
        scratch_shapes=[pltpu.VMEM((2, m, LANE), F32), pltpu.VMEM((2, m, LANE), F32), pltpu.VMEM((2, m, LANE), F32)],
        compiler_params=_params(("arbitrary",), 32),
        name="s5_scan",
    )(u, zin, tmat, zout, al, h0)


def _s5_matrices(a_re, a_im, log_dt, b_re, b_im, c_re, c_im):
    hp = lax.Precision.HIGHEST
    L = S5_CHUNK
    dt = jnp.exp(log_dt)[..., None]
    lam_re, lam_im = a_re * dt, a_im * dt

    def power(k):
        k = k[:, None, None, None]
        mag = jnp.exp(lam_re * k)
        return mag * jnp.cos(lam_im * k), mag * jnp.sin(lam_im * k)

    ab_re, ab_im = power(jnp.ones((1,), F32))
    ab_re, ab_im = ab_re[0], ab_im[0]
    num_re, num_im = ab_re - 1.0, ab_im
    den = a_re * a_re + a_im * a_im
    f_re = (num_re * a_re + num_im * a_im) / den
    f_im = (num_im * a_re - num_re * a_im) / den
    bb_re = f_re[..., None] * b_re - f_im[..., None] * b_im
    bb_im = f_re[..., None] * b_im + f_im[..., None] * b_re

    ks = jnp.arange(L + 1, dtype=F32)
    pw_re, pw_im = power(ks)

    def zin_dir(d, exps):
        pr, pi = pw_re[exps, d], pw_im[exps, d]
        w_re = pr[..., None] * bb_re[d][None] - pi[..., None] * bb_im[d][None]
        w_im = pr[..., None] * bb_im[d][None] + pi[..., None] * bb_re[d][None]
        to = lambda w: jnp.transpose(w, (1, 0, 3, 2)).reshape(S5_GROUPS, L * S5_GROUP, S5_STATE)
        return to(w_re), to(w_im)

    steps = np.arange(L)
    zf_re, zf_im = zin_dir(0, L - 1 - steps)
    zb_re, zb_im = zin_dir(1, steps)
    zin = jnp.concatenate([zf_re, zb_re, zf_im, zb_im], axis=-1)

    def zout_dir(d, exps):
        pr, pi = pw_re[exps, d], pw_im[exps, d]
        cr, ci = c_re[d], c_im[d]
        e_re = cr[None] * pr[:, :, None, :] - ci[None] * pi[:, :, None, :]
        e_im = cr[None] * pi[:, :, None, :] + ci[None] * pr[:, :, None, :]
        to = lambda e: jnp.transpose(e, (1, 3, 0, 2)).reshape(S5_GROUPS, S5_STATE, L * S5_GROUP)
        return to(e_re), to(-e_im)

    of_re, of_im = zout_dir(0, steps + 1)
    ob_re, ob_im = zout_dir(1, L - steps)
    zout = jnp.concatenate([of_re, ob_re, of_im, ob_im], axis=1)

    def taps(d):
        pr, pi = pw_re[:L, d], pw_im[:L, d]
        m_re = pr[..., None] * bb_re[d][None] - pi[..., None] * bb_im[d][None]
        m_im = pr[..., None] * bb_im[d][None] + pi[..., None] * bb_re[d][None]
        return (jnp.einsum('gpn,lgnq->lgpq', c_re[d], m_re, precision=hp)
                - jnp.einsum('gpn,lgnq->lgpq', c_im[d], m_im, precision=hp))

    kf, kb = taps(0), taps(1)
    pad4 = ((0, 0),) * 3
    tm = jnp.stack([jnp.pad(kf[:L - s], ((s, 0),) + pad4) + jnp.pad(kb[:s + 1][::-1], ((0, L - 1 - s),) + pad4)
                    for s in range(L)])
    tmat = jnp.transpose(tm, (2, 0, 4, 1, 3)).reshape(S5_GROUPS, L * S5_GROUP, L * S5_GROUP)

    al = jnp.stack([jnp.concatenate([pw_re[L, 0], pw_re[L, 1]], axis=-1),
                    jnp.concatenate([pw_im[L, 0], pw_im[L, 1]], axis=-1)], axis=1)
    return zin.astype(BF16), tmat.astype(BF16), zout.astype(BF16), al


def _mla_kv(ckv, kpe, wk_ref, wv_ref, pl_ref, k_ref, v_ref):
    cb = ckv.astype(BF16)
    k_ref[...] = (_dot(cb, wk_ref[...]) + _dot(kpe.astype(BF16), pl_ref[...])).astype(BF16)
    v_ref[...] = _dot(cb, wv_ref[...]).astype(BF16)


def _mla_prep_kernel(rblk_ref, p_ref, qg_ref, kvg_ref, wuq_ref, wk_ref, wv_ref, pl_ref, cos_ref, sa_ref, sb_ref,
                     q_ref, k_ref, v_ref, ckv_ref):
    del rblk_ref
    pr = p_ref[...]
    cos, sa, sb = cos_ref[...], sa_ref[...], sb_ref[...]
    quarter = MLA_ROPE // 4
    scale = (MLA_NOPE + MLA_ROPE) ** -0.5
    qn = _rms(pr[:, :MLA_Q_RANK], qg_ref[...], MLA_Q_RANK).astype(BF16)
    q = _dot(qn, wuq_ref[...])
    for hd in range(MLA_HEADS):
        sl = slice(hd * LANE, (hd + 1) * LANE)
        q_ref[:, sl] = (_rope(q[:, sl], cos, sa, sb, quarter) * scale).astype(BF16)
    ckv = _rms(pr[:, MLA_Q_RANK:MLA_Q_RANK + MLA_KV_RANK], kvg_ref[...], MLA_KV_RANK)
    ckv_ref[...] = ckv
    kpe = _rope(pr[:, MLA_Q_RANK + MLA_KV_RANK:], cos, sa, sb, quarter)
    _mla_kv(ckv, kpe, wk_ref, wv_ref, pl_ref, k_ref, v_ref)


def _mla_prep(proj, qg, kvg, wuq, wk, wv, place, tables, rblk):
    n = proj.shape[0]
    tm = TM_PROJ
    hw = MLA_HEADS * LANE
    pw = MLA_Q_RANK + MLA_KV_RANK + LANE
    cos, sa, sb = tables
    tok = lambda i, r: (i, 0)
    const = lambda i, r: (0, 0)
    tab = lambda i, r: (r[i], 0)
    return pl.pallas_call(
        _mla_prep_kernel,
        out_shape=(jax.ShapeDtypeStruct((n, hw), BF16), jax.ShapeDtypeStruct((n, hw), BF16),
                   jax.ShapeDtypeStruct((n, hw), BF16), jax.ShapeDtypeStruct((n, MLA_KV_RANK), F32)),
        grid_spec=pltpu.PrefetchScalarGridSpec(
            num_scalar_prefetch=1, grid=(n // tm,),
            in_specs=[pl.BlockSpec((tm, pw), lambda i, r: (i, 1)),
                      pl.BlockSpec((1, MLA_Q_RANK), const), pl.BlockSpec((1, MLA_KV_RANK), const),
                      pl.BlockSpec((MLA_Q_RANK, hw), const), pl.BlockSpec((MLA_KV_RANK, hw), const),
                      pl.BlockSpec((MLA_KV_RANK, hw), const), pl.BlockSpec((LANE, hw), const),
                      pl.BlockSpec((tm, LANE), tab), pl.BlockSpec((tm, LANE), tab), pl.BlockSpec((tm, LANE), tab)],
            out_specs=[pl.BlockSpec((tm, hw), tok), pl.BlockSpec((tm, hw), tok), pl.BlockSpec((tm, hw), tok),
                       pl.BlockSpec((tm, MLA_KV_RANK), tok)]),
        compiler_params=_params(("arbitrary",), 32),
        name="mla_prep",
    )(rblk, proj, qg, kvg, wuq, wk, wv, place, cos, sa, sb)


def _mla_cache_kernel(c_ref, p_ref, wk_ref, wv_ref, pl_ref, k_ref, v_ref):
    _mla_kv(c_ref[...], p_ref[...], wk_ref, wv_ref, pl_ref, k_ref, v_ref)


def _mla_cache(ckv, kpe, wk, wv, place):
    n = ckv.shape[0]
    tm = min(TM_PROJ, n)
    assert n % tm == 0
    hw = MLA_HEADS * LANE
    tok = lambda i: (i, 0)
    const = lambda i: (0, 0)
    return pl.pallas_call(
        _mla_cache_kernel,
        out_shape=(jax.ShapeDtypeStruct((n, hw), BF16), jax.ShapeDtypeStruct((n, hw), BF16)),
        grid=(n // tm,),
        in_specs=[pl.BlockSpec((tm, MLA_KV_RANK), tok), pl.BlockSpec((tm, LANE), tok),
                  pl.BlockSpec((MLA_KV_RANK, hw), const), pl.BlockSpec((MLA_KV_RANK, hw), const),
                  pl.BlockSpec((LANE, hw), const)],
        out_specs=[pl.BlockSpec((tm, hw), tok), pl.BlockSpec((tm, hw), tok)],
        compiler_params=_params(("arbitrary",), 32),
        name="mla_cache_kv",
    )(ckv, kpe, wk, wv, place)


def _gelu_tanh(x):
    return 0.5 * x * (1.0 + jnp.tanh(math.sqrt(2.0 / math.pi) * (x + 0.044715 * (x * x * x))))


def _odd_out_kernel(rows_ref, u_ref, yp_ref, ys_ref, ap_ref, as_ref, x_ref, g_ref, d_ref, wglu_ref, bglu_ref, w_ref,
                    lg_ref, lb_ref, o_ref, y_s, *, alpha, n_ctx_tiles):
    del rows_ref
    i = pl.program_id(0)
    yg = _pick_pass(i, n_ctx_tiles, yp_ref, ys_ref)
    n_chunk = yg.shape[1]
    n_tile = S5_WIDTH // LANE
    gpt = LANE // S5_GROUP
    for s in range(S5_CHUNK):
        sl = slice(s * S5_GROUP, (s + 1) * S5_GROUP)
        for j in range(n_tile):
            y_s[j, pl.ds(s, n_chunk, stride=S5_CHUNK), :] = jnp.concatenate(
                [yg[g][:, sl] for g in range(j * gpt, (j + 1) * gpt)], axis=1)
    y_ssm = jnp.concatenate([y_s[j] for j in range(n_tile)], axis=1)
    y = _gelu_tanh(u_ref[...] * d_ref[...] + y_ssm)
    y = y * jax.nn.sigmoid(_dot(y.astype(BF16), wglu_ref[...]) + bglu_ref[...])
    attn = _pick_pass(i, n_ctx_tiles, ap_ref, as_ref)
    out = _dot(y.astype(BF16), w_ref[:S5_WIDTH, :]) + _dot(attn, w_ref[S5_WIDTH:, :])
    o_ref[...] = _layer_norm(alpha * x_ref[...] + g_ref[0] * out, lg_ref[...], lb_ref[...])


def _odd_out(proj, y_p, y_s, attn_p, attn_s, x, g1, s5_d, w_glu, b_glu, w_out, ln_g, ln_b, rows, alpha):
    n, d = x.shape
    tm = TM_PROJ
    aw = attn_p.shape[1]
    cpt = tm // S5_CHUNK
    n_ctx, n_lat = attn_p.shape[0] // tm, attn_s.shape[0] // tm
    ap_spec, as_spec = _pass_specs((tm, aw), n_ctx, n_lat)
    yp_spec, ys_spec = _pass_specs((S5_GROUPS, cpt, S5_CHUNK * S5_GROUP), n_ctx, n_lat)
    tok = lambda i, r: (i, 0)
    const = lambda i, r: (0, 0)
    return pl.pallas_call(
        functools.partial(_odd_out_kernel, alpha=alpha, n_ctx_tiles=n_ctx),
        out_shape=jax.ShapeDtypeStruct((n, d), F32),
        grid_spec=pltpu.PrefetchScalarGridSpec(
            num_scalar_prefetch=1, grid=(n // tm,),
            in_specs=[pl.BlockSpec((tm, S5_WIDTH), tok), yp_spec, ys_spec,
                      ap_spec, as_spec, pl.BlockSpec((tm, d), tok),
                      pl.BlockSpec((1, 1, d), lambda i, r: (r[i], 0, 0)),
                      pl.BlockSpec((1, S5_WIDTH), const), pl.BlockSpec((S5_WIDTH, S5_WIDTH), const),
                      pl.BlockSpec((1, S5_WIDTH), const), pl.BlockSpec((S5_WIDTH + aw, d), const),
                      pl.BlockSpec((1, d), const), pl.BlockSpec((1, d), const)],
            out_specs=pl.BlockSpec((tm, d), tok),
            scratch_shapes=[pltpu.VMEM((S5_WIDTH // LANE, tm, LANE), F32)]),
        compiler_params=_params(("arbitrary",), 40),
        name="odd_out_proj",
    )(rows, proj, y_p, y_s, attn_p, attn_s, x, g1, s5_d, w_glu, b_glu, w_out, ln_g, ln_b)


def _router_kernel(rows_ref, x_ref, sc_ref, sh_ref, rt_ref, hb_ref, pos_ref, gate_ref, pt_ref, cnt_ref, tri_s):
    del rows_ref
    t = x_ref.shape[0]
    ne = N_EXPERTS

    @pl.when(pl.program_id(0) == 0)
    def _():
        before = lax.broadcasted_iota(jnp.int32, (t, t), 0) < lax.broadcasted_iota(jnp.int32, (t, t), 1)
        tri_s[...] = jnp.where(before, 1.0, 0.0).astype(BF16)

    h = x_ref[...] * (1.0 + sc_ref[0]) + sh_ref[0]
    h_hi, h_lo = _split(h)
    hb_ref[...] = h_hi
    r_hi, r_lo = _split(rt_ref[...])
    logits = _dot_nt(r_hi, h_hi) + _dot_nt(r_lo, h_hi) + _dot_nt(r_hi, h_lo)
    eid = lax.broadcasted_iota(jnp.int32, (ne, t), 0).astype(F32)
    m0 = jnp.max(logits, axis=0, keepdims=True)
    i0 = jnp.min(jnp.where(logits == m0, eid, float(ne)), axis=0, keepdims=True)
    rest = jnp.where(eid == i0, -jnp.inf, logits)
    m1 = jnp.max(rest, axis=0, keepdims=True)
    i1 = jnp.min(jnp.where(rest == m1, eid, float(ne)), axis=0, keepdims=True)
    ex = jnp.exp(m1 - m0)
    g0 = 1.0 / (1.0 + ex)
    g1 = ex / (1.0 + ex)
    sel0 = eid == i0
    sel1 = eid == i1
    member = jnp.where(sel0, 1.0, jnp.where(sel1, 1.0, 0.0))
    gate = jnp.where(sel0, g0, jnp.where(sel1, g1, 0.0))
    rank = _dot(member.astype(BF16), tri_s[...])
    pos = jnp.where(member > 0.0, rank, -1.0)
    pos_ref[0] = pos.astype(jnp.int32)
    gate_ref[0] = gate
    cnt = jnp.sum(member, axis=1, keepdims=True)
    cnt_ref[0] = jnp.broadcast_to(cnt, (ne, LANE)).astype(jnp.int32)
    packed = jnp.concatenate([pos, gate, jnp.zeros((LANE - 2 * ne, t), F32)], axis=0)
    pt_ref[...] = packed.T


def _router(x, sc, sh, router_t, rows):
    n, d = x.shape
    t = T_MOE
    nb = n // t
    ne = N_EXPERTS
    tok = lambda i, r: (i, 0)
    row3 = lambda i, r: (r[i], 0, 0)
    blk3 = lambda i, r: (i, 0, 0)
    return pl.pallas_call(
        _router_kernel,
        out_shape=(jax.ShapeDtypeStruct((n, d), BF16), jax.ShapeDtypeStruct((nb, ne, t), jnp.int32),
                   jax.ShapeDtypeStruct((nb, ne, t), F32),
                   jax.ShapeDtypeStruct((n, LANE), F32), jax.ShapeDtypeStruct((nb, ne, LANE), jnp.int32)),
        grid_spec=pltpu.PrefetchScalarGridSpec(
            num_scalar_prefetch=1, grid=(nb,),
            in_specs=[pl.BlockSpec((t, d), tok), pl.BlockSpec((1, 1, d), row3), pl.BlockSpec((1, 1, d), row3),
                      pl.BlockSpec((ne, d), lambda i, r: (0, 0))],
            out_specs=[pl.BlockSpec((t, d), tok), pl.BlockSpec((1, ne, t), blk3), pl.BlockSpec((1, ne, t), blk3),
                       pl.BlockSpec((t, LANE), tok), pl.BlockSpec((1, ne, LANE), blk3)],
            scratch_shapes=[pltpu.VMEM((t, t), BF16)]),
        compiler_params=_params(("arbitrary",), 48),
        name="moe_router",
    )(rows, x, sc, sh, router_t)


def _segment_copies(rows, src, dst, src0, dst0, sems):
    out = []
    for k, sz in enumerate(SEG_SIZES):
        off = (rows // (2 * sz)) * (2 * sz)
        s0 = pl.multiple_of(src0 + off, SEG_ALIGN)
        d0 = pl.multiple_of(dst0 + off, SEG_ALIGN)
        out.append(((rows & sz) != 0,
                    pltpu.make_async_copy(src.at[pl.ds(s0, sz)], dst.at[pl.ds(d0, sz)], sems.at[k])))
    return out


def _run_copies(copies):
    for pred, cp in copies:
        pl.when(pred)(cp.start)
    for pred, cp in copies:
        pl.when(pred)(cp.wait)


def _moe_gather_kernel(cnt_ref, base_ref, h_ref, pos_ref, gate_ref, xs_in, gs_in, xs_ref, gs_ref,
                       xe_s, ge_s, sem_x, sem_g):
    del xs_in, gs_in
    b = pl.program_id(0)
    t = h_ref.shape[0]
    rt = ROW_MOE
    ne = N_EXPERTS
    copies = []
    off = 0
    for e in range(ne):
        cnt = cnt_ref[b * ne + e]
        pos_row = pos_ref[0, e:e + 1, :]
        gate_row = gate_ref[0, e:e + 1, :]

        def gather(r, c, off=off, pos_row=pos_row, gate_row=gate_row):
            r0 = r * rt
            hit = (lax.broadcasted_iota(jnp.int32, (rt, t), 0) + r0) == pos_row
            onehot = jnp.where(hit, 1.0, 0.0).astype(BF16)
            dst = pl.multiple_of(off + r0, SEG_ALIGN)
            xe_s[pl.ds(dst, rt), :] = _dot(onehot, h_ref[...]).astype(BF16)
            g = jnp.sum(jnp.where(hit, gate_row, 0.0), axis=1, keepdims=True)
            ge_s[pl.ds(dst, rt), :] = jnp.broadcast_to(g, (rt, LANE))
            return c

        lax.fori_loop(0, (cnt + rt - 1) // rt, gather, 0)
        rows = ((cnt + SEG_ALIGN - 1) // SEG_ALIGN) * SEG_ALIGN
        base = base_ref[b * ne + e]
        seg = (_segment_copies(rows, xe_s, xs_ref, off, base, sem_x.at[e])
               + _segment_copies(rows, ge_s, gs_ref, off, base, sem_g.at[e]))
        for pred, cp in seg:
            pl.when(pred)(cp.start)
        copies += seg
        off = off + rows
    for pred, cp in copies:
        pl.when(pred)(cp.wait)


def _moe_gather(h, pos, gate, counts, base, n_rows):
    n, d = h.shape
    t = T_MOE
    nb = n // t
    ne = N_EXPERTS
    tok = lambda b, c, o: (b, 0)
    blk3 = lambda b, c, o: (b, 0, 0)
    any_spec = pl.BlockSpec(memory_space=pl.ANY)
    xs0 = jnp.zeros((n_rows, d), BF16)
    gs0 = jnp.zeros((n_rows, LANE), F32)
    cap = -(-(2 * t + ne * (SEG_ALIGN - 1) + ROW_MOE - 1) // ROW_MOE) * ROW_MOE
    return pl.pallas_call(
        _moe_gather_kernel,
        out_shape=(jax.ShapeDtypeStruct((n_rows, d), BF16), jax.ShapeDtypeStruct((n_rows, LANE), F32)),
        grid_spec=pltpu.PrefetchScalarGridSpec(
            num_scalar_prefetch=2, grid=(nb,),
            in_specs=[pl.BlockSpec((t, d), tok), pl.BlockSpec((1, ne, t), blk3), pl.BlockSpec((1, ne, t), blk3),
                      any_spec, any_spec],
            out_specs=[any_spec, any_spec],
            scratch_shapes=[pltpu.VMEM((cap, d), BF16), pltpu.VMEM((cap, LANE), F32),
                            pltpu.SemaphoreType.DMA((ne, len(SEG_SIZES))),
                            pltpu.SemaphoreType.DMA((ne, len(SEG_SIZES)))]),
        input_output_aliases={5: 0, 6: 1},
        compiler_params=_params(("arbitrary",), 32),
        name="moe_gather",
    )(counts, base, h, pos, gate, xs0, gs0)


def _moe_gmm_kernel(te_ref, valid_ref, x_ref, gs_ref, wg_ref, wu_ref, wd_ref, y_ref, acc_s):
    del te_ref
    i, f = pl.program_id(0), pl.program_id(1)
    last = pl.num_programs(1) - 1
    valid = valid_ref[i] != 0

    @pl.when(valid)
    def _():
        x = x_ref[...]
        a = (_silu(_dot(x, wg_ref[0])) * _dot(x, wu_ref[0])).astype(BF16)
        y = _dot(a, wd_ref[0])

        @pl.when(f == 0)
        def _():
            acc_s[...] = y

        @pl.when(f > 0)
        def _():
            acc_s[...] += y

        @pl.when(f == last)
        def _():
            y_ref[...] = (acc_s[...] * gs_ref[:, 0:1]).astype(BF16)

    @pl.when(jnp.logical_not(valid) & (f == last))
    def _():
        y_ref[...] = jnp.zeros_like(y_ref)


def _moe_gmm(xs, gs, tile_expert, tile_valid, wg, wu, wd, layer):
    n_rows, d = xs.shape
    rt = RT_GMM
    ff = wg.shape[-1]
    fc = MOE_CHUNK
    tok = lambda i, f, te, tv: (i, 0)
    wmode = dict(pipeline_mode=pl.Buffered(1)) if fc == ff else {}
    return pl.pallas_call(
        _moe_gmm_kernel,
        out_shape=jax.ShapeDtypeStruct((n_rows, d), BF16),
        grid_spec=pltpu.PrefetchScalarGridSpec(
            num_scalar_prefetch=2, grid=(n_rows // rt, ff // fc),
            in_specs=[pl.BlockSpec((rt, d), tok), pl.BlockSpec((rt, LANE), tok),
                      pl.BlockSpec((None, 1, d, fc), lambda i, f, te, tv: (layer, te[i], 0, f), **wmode),
                      pl.BlockSpec((None, 1, d, fc), lambda i, f, te, tv: (layer, te[i], 0, f), **wmode),
                      pl.BlockSpec((None, 1, fc, d), lambda i, f, te, tv: (layer, te[i], f, 0), **wmode)],
            out_specs=pl.BlockSpec((rt, d), tok),
            scratch_shapes=[pltpu.VMEM((rt, d), F32)]),
        compiler_params=_params(("arbitrary", "arbitrary"), 56),
        name="moe_gmm",
    )(tile_expert, tile_valid, xs, gs, wg, wu, wd)


def _moe_combine_kernel(cnt_ref, base_ref, rows_ref, ys_ref, pt_ref, x_ref, g_ref, lg_ref, lb_ref, o_ref,
                        ycat_s, sems, *, alpha):
    del rows_ref
    b = pl.program_id(0)
    t = x_ref.shape[0]
    cap = ycat_s.shape[0]
    ne = N_EXPERTS

    @pl.when(b == 0)
    def _():
        ycat_s[...] = jnp.zeros_like(ycat_s)

    copies = []
    offs = []
    off = 0
    for e in range(ne):
        cnt = cnt_ref[b * ne + e]
        rows = ((cnt + SEG_ALIGN - 1) // SEG_ALIGN) * SEG_ALIGN
        copies += _segment_copies(rows, ys_ref, ycat_s, base_ref[b * ne + e], off, sems.at[e])
        offs.append(off)
        off = off + rows
    _run_copies(copies)

    pt = pt_ref[...]
    lane = lax.broadcasted_iota(jnp.int32, (1, LANE), 1)
    off_lane = jnp.zeros((1, LANE), F32)
    for e in range(1, ne):
        off_lane = jnp.where(lane == e, offs[e].astype(F32), off_lane)
    chosen = (pt >= 0.0) & (lax.broadcasted_iota(jnp.int32, (t, LANE), 1) < ne)
    slot = pt + off_lane
    s_lo = jnp.min(jnp.where(chosen, slot, float(cap)), axis=1, keepdims=True)
    s_hi = jnp.max(jnp.where(chosen, slot, -1.0), axis=1, keepdims=True)
    n_piece = 3
    wp = cap // n_piece
    f = None
    for j in range(n_piece):
        col = (lax.broadcasted_iota(jnp.int32, (t, wp), 1) + j * wp).astype(F32)
        sel = jnp.where(col == s_lo, 1.0, jnp.where(col == s_hi, 1.0, 0.0)).astype(BF16)
        part = _dot(sel, ycat_s[j * wp:(j + 1) * wp, :])
        f = part if f is None else f + part
    o_ref[...] = _layer_norm(alpha * x_ref[...] + g_ref[0] * f, lg_ref[...], lb_ref[...])


def _moe_combine(ys, pt, x, g2, ln_g, ln_b, counts, base, rows, alpha):
    n, d = x.shape
    t = T_MOE
    ne = N_EXPERTS
    cap = -(-(2 * t + ne * (SEG_ALIGN - 1)) // (6 * LANE)) * (6 * LANE)
    tok = lambda b, c, o, r: (b, 0)
    const = lambda b, c, o, r: (0, 0)
    return pl.pallas_call(
        functools.partial(_moe_combine_kernel, alpha=alpha),
        out_shape=jax.ShapeDtypeStruct((n, d), F32),
        grid_spec=pltpu.PrefetchScalarGridSpec(
            num_scalar_prefetch=3, grid=(n // t,),
            in_specs=[pl.BlockSpec(memory_space=pl.ANY), pl.BlockSpec((t, LANE), tok), pl.BlockSpec((t, d), tok),
                      pl.BlockSpec((1, 1, d), lambda b, c, o, r: (r[b], 0, 0)),
                      pl.BlockSpec((1, d), const), pl.BlockSpec((1, d), const)],
            out_specs=pl.BlockSpec((t, d), tok),
            scratch_shapes=[pltpu.VMEM((cap, d), BF16), pltpu.SemaphoreType.DMA((ne, len(SEG_SIZES)))]),
        compiler_params=_params(("arbitrary",), 48),
        name="moe_combine",
    )(counts, base, rows, ys, pt, x, g2, ln_g, ln_b)


def _moe_layout(cnt, n_tiles):
    cnt16 = (cnt + SEG_ALIGN - 1) // SEG_ALIGN * SEG_ALIGN
    rows_e = jnp.sum(cnt16, axis=0)
    rows_e = (rows_e + RT_GMM - 1) // RT_GMM * RT_GMM
    ends = jnp.cumsum(rows_e)
    base = (ends - rows_e)[None, :] + jnp.cumsum(cnt16, axis=0) - cnt16
    tile_end = ends // RT_GMM
    idx = jnp.arange(n_tiles, dtype=jnp.int32)
    valid = idx < tile_end[-1]
    te = jnp.sum((idx[:, None] >= tile_end[None, :]).astype(jnp.int32), axis=1)
    last = jnp.sum(((tile_end[-1] - 1) >= tile_end).astype(jnp.int32))
    te = jnp.where(valid, te, last)
    return base.reshape(-1).astype(jnp.int32), te.astype(jnp.int32), valid.astype(jnp.int32)


def _tile_rows(n, p, ss, tm):
    starts = np.arange(n // tm) * tm
    return jnp.asarray(np.where(starts < p, 0, 1 + (starts - p) // ss).astype(np.int32))


def _tile_rope_blocks(n, p, ss, tm):
    starts = np.arange(n // tm) * tm
    return jnp.asarray(np.where(starts < p, 0, 1 + ((starts - p) % ss) // tm).astype(np.int32))


def _tile_edges(n, p, sp, ss, tm):
    starts = np.arange(n // tm) * tm
    pos = np.where(starts < p, starts % sp, (starts - p) % ss)
    seq = np.where(starts < p, sp, ss)
    return jnp.asarray((pos == 0).astype(np.int32)), jnp.asarray(((pos + tm) % seq == 0).astype(np.int32))


def _rope_tables(n_tokens, rot_dim, lane0, tm):
    rows = n_tokens // GRID_W
    row_pos = jnp.repeat(jnp.arange(rows, dtype=F32), GRID_W)
    col_pos = jnp.tile(jnp.arange(GRID_W, dtype=F32), rows)
    half = rot_dim // 2
    qtr = half // 2
    inv_freq = ROPE_THETA ** (-jnp.arange(0, half, 2, dtype=F32) / half)
    ang_r = row_pos[:, None] * inv_freq
    ang_c = col_pos[:, None] * inv_freq
    ang = jnp.concatenate([ang_r, ang_r, ang_c, ang_c], axis=-1)
    cos, sin = jnp.cos(ang), jnp.sin(ang)
    first = ((np.arange(rot_dim) % half) < qtr).astype(np.float32)
    sa = -sin * first
    sb = sin * (1.0 - first)

    def place(tbl, fill):
        full = jnp.full((n_tokens, LANE), fill, F32).at[:, lane0:lane0 + rot_dim].set(tbl)
        return jnp.concatenate([jnp.full((tm, LANE), fill, F32), full], axis=0)

    return place(cos, 1.0), place(sa, 0.0), place(sb, 0.0)


def _pad_heads(w, n_heads, width):
    lead = w.shape[:-1]
    w = w.reshape(lead + (n_heads, width))
    w = jnp.pad(w, [(0, 0)] * len(lead) + [(0, 0), (0, LANE - width)])
    return w.reshape(lead + (n_heads * LANE,))


def _pad_head_rows(w, n_heads, width):
    d = w.shape[-1]
    w = w.reshape(n_heads, width, d)
    return jnp.pad(w, ((0, 0), (0, LANE - width), (0, 0))).reshape(n_heads * LANE, d)


def kernel(x_prompt, x_sample, cache_attn_k, cache_attn_v, cache_mla_ckv, cache_mla_kpe, state_ssm_re, state_ssm_im, c, c_ctx, ada_w, ada_b, ln_g, ln_b, ev_w_in, ev_conv_w, ev_q_gain, ev_k_gain, ev_w_out, ffn_w_gate, ffn_w_up, ffn_w_down, od_w_in, s5_a_re, s5_a_im, s5_log_dt, s5_b_re, s5_b_im, s5_c_re, s5_c_im, s5_d, s5_w_glu, s5_b_glu, mla_q_gain, mla_w_uq, mla_kv_gain, mla_w_ukv, od_w_out, moe_router, moe_w_gate, moe_w_up, moe_w_down):
    bp, sp, d = x_prompt.shape
    bs, ss, _ = x_sample.shape
    depth = ada_w.shape[0]
    alpha = (2 * depth) ** 0.25
    p = bp * sp
    n = p + bs * ss
    past = cache_attn_k.shape[2]
    for tm in (TM_PROJ, TM_EVEN_IN, TM_FFN, T_MOE):
        assert p % tm == 0 and ss % tm == 0 and (sp % tm == 0 or tm % sp == 0)
    assert sp % S5_CHUNK == 0 and ss % S5_CHUNK == 0

    x = jnp.concatenate([x_prompt.reshape(p, d), x_sample.reshape(bs * ss, d)], axis=0)

    nrow = -(-(1 + bs) // SUBLANE) * SUBLANE
    cond = jnp.zeros((nrow, d), F32).at[0].set(c_ctx).at[1:1 + bs].set(c)
    mod = _modulation(cond, ada_w, ada_b)

    def mod_part(l, k):
        return mod[l, :, k * d:(k + 1) * d].reshape(nrow, 1, d)

    rows_proj = _tile_rows(n, p, ss, TM_PROJ)
    rows_ffn = _tile_rows(n, p, ss, TM_FFN)
    rows_moe = _tile_rows(n, p, ss, T_MOE)
    rblk = _tile_rope_blocks(n, p, ss, TM_PROJ)
    first, last = _tile_edges(n, p, sp, ss, TM_PROJ)
    rows_even = _tile_rows(n, p, ss, TM_EVEN_IN)
    rblk_even = _tile_rope_blocks(n, p, ss, TM_EVEN_IN)
    tables_even = _rope_tables(ss, HEAD_DIM, 0, TM_EVEN_IN)
    tables_mla = _rope_tables(ss, MLA_ROPE, MLA_NOPE, TM_PROJ)

    s5_mats = jax.vmap(_s5_matrices)(s5_a_re, s5_a_im, s5_log_dt, s5_b_re, s5_b_im, s5_c_re, s5_c_im)
    moe_wg, moe_wu, moe_wd = moe_w_gate.astype(BF16), moe_w_up.astype(BF16), moe_w_down.astype(BF16)
    ffn_wg, ffn_wu, ffn_wd = ffn_w_gate.astype(BF16), ffn_w_up.astype(BF16), ffn_w_down.astype(BF16)
    out_k, out_v, out_ckv, out_kpe, out_sre, out_sim = [], [], [], [], [], []
    cw = CONV_WIDTH
    hq = ATTN_HEADS * HEAD_DIM
    hkv = ATTN_KV_HEADS * HEAD_DIM
    for l in range(depth):
        i = l // 2
        sh1, sc1, g1, sh2, sc2, g2 = [mod_part(l, k) for k in range(6)]
        lg = ln_g[l].reshape(2, 1, d)
        lb = ln_b[l].reshape(2, 1, d)
        if l % 2 == 0:
            w_in = ev_w_in[i]
            wc = w_in[:, :3 * cw].astype(BF16)
            wq = _pad_heads(w_in[:, 3 * cw:3 * cw + hq], ATTN_HEADS, HEAD_DIM).astype(BF16)
            wkv = jnp.concatenate([_pad_heads(w_in[:, 3 * cw + hq:3 * cw + hq + hkv], ATTN_KV_HEADS, HEAD_DIM),
                                   _pad_heads(w_in[:, 3 * cw + hq + hkv:], ATTN_KV_HEADS, HEAD_DIM)],
                                  axis=1).astype(BF16)
            qg = jnp.pad(ev_q_gain[i], (0, LANE - HEAD_DIM)).reshape(1, LANE)
            kg = jnp.pad(ev_k_gain[i], (0, LANE - HEAD_DIM)).reshape(1, LANE)
            conv, q, k, v, kn, vf = _even_in(x, sc1, sh1, wc, wq, wkv, qg, kg, tables_even, rows_even, rblk_even)
            kvw = ATTN_KV_HEADS * LANE
            out_k.append(kn[:p].reshape(bp, sp, ATTN_KV_HEADS, LANE)[..., :HEAD_DIM])
            out_v.append(vf[:p].reshape(bp, sp, ATTN_KV_HEADS, LANE)[..., :HEAD_DIM])
            group = ATTN_HEADS // ATTN_KV_HEADS
            qw = ATTN_HEADS * LANE
            a_p = _attention(q, [(k, v, True)], ATTN_HEADS, group, bp, sp, 0)
            kc = _pad_heads(cache_attn_k[:, i].reshape(bs, past, hkv), ATTN_KV_HEADS, HEAD_DIM).astype(BF16)
            vc = _pad_heads(cache_attn_v[:, i].reshape(bs, past, hkv), ATTN_KV_HEADS, HEAD_DIM).astype(BF16)
            a_s = _attention(q, [(kc, vc, False), (k, v, True)], ATTN_HEADS, group, bs, ss, p)
            w_out = jnp.concatenate([ev_w_out[i][:cw], _pad_head_rows(ev_w_out[i][cw:], ATTN_HEADS, HEAD_DIM)],
                                    axis=0).astype(BF16)
            conv_w = jnp.pad(ev_conv_w[i], ((0, SUBLANE - ev_conv_w.shape[1]), (0, 0)))
            x = _even_out(conv, a_p.reshape(p, qw), a_s.reshape(bs * ss, qw), x, g1, conv_w, w_out, lg[0], lb[0],
                          first, last, rows_proj, alpha, sp)
            x = _ffn(x, sc2, sh2, g2, ffn_wg, ffn_wu, ffn_wd, i, lg[1], lb[1], rows_ffn, alpha)
        else:
            q_end = S5_WIDTH + MLA_Q_RANK
            kv_end = q_end + MLA_KV_RANK
            w_in = od_w_in[i]
            w_in = jnp.concatenate([w_in[:, :kv_end], jnp.zeros((d, MLA_NOPE), F32), w_in[:, kv_end:],
                                    jnp.zeros((d, LANE - MLA_NOPE - MLA_ROPE), F32)], axis=1).astype(BF16)
            m_ctx, m_lat = p // S5_CHUNK, bs * ss // S5_CHUNK
            ctx_chunk0, lat_chunk0 = (m_lat, 0) if m_lat >= m_ctx else (0, m_ctx)
            assert ctx_chunk0 % m_ctx == 0 and lat_chunk0 % m_lat == 0
            proj, u = _odd_in(x, sc1, sh1, w_in, rows_proj, p // TM_PROJ, ctx_chunk0, lat_chunk0)
            out_kpe.append(proj[:p, kv_end + MLA_NOPE:kv_end + MLA_NOPE + MLA_ROPE].reshape(bp, sp, MLA_ROPE))

            h0_p = jnp.zeros((S5_GROUPS, bp, 4 * S5_STATE), F32)
            y_p, fin_p = _s5(u, s5_mats, i, h0_p, bp, sp // S5_CHUNK, ctx_chunk0 // m_ctx)

            def pack_state(re, im):
                to = lambda a: a.transpose(2, 0, 1, 3).reshape(S5_GROUPS, a.shape[0], 2 * S5_STATE)
                return jnp.concatenate([to(re), to(im)], axis=-1)

            y_s, _ = _s5(u, s5_mats, i, pack_state(state_ssm_re[:, i], state_ssm_im[:, i]), bs, ss // S5_CHUNK,
                         lat_chunk0 // m_lat)
            fin = fin_p.reshape(S5_GROUPS, bp, 2, 2, S5_STATE)
            out_sre.append(fin[:, :, 0].transpose(1, 2, 0, 3))
            out_sim.append(fin[:, :, 1].transpose(1, 2, 0, 3))

            qk = MLA_NOPE + MLA_ROPE
            wuq = _pad_heads(mla_w_uq[i], MLA_HEADS, qk).astype(BF16)
            wukv = mla_w_ukv[i].reshape(MLA_KV_RANK, MLA_HEADS, MLA_NOPE + MLA_V)
            wk = _pad_heads(wukv[..., :MLA_NOPE].reshape(MLA_KV_RANK, -1), MLA_HEADS, MLA_NOPE).astype(BF16)
            wv = _pad_heads(wukv[..., MLA_NOPE:].reshape(MLA_KV_RANK, -1), MLA_HEADS, MLA_V).astype(BF16)
            place = np.zeros((LANE, MLA_HEADS * LANE), np.float32)
            for hd in range(MLA_HEADS):
                for j in range(MLA_ROPE):
                    place[MLA_NOPE + j, hd * LANE + MLA_NOPE + j] = 1.0
            place = jnp.asarray(place, BF16)
            q, k, v, ckv = _mla_prep(proj, mla_q_gain[i].reshape(1, -1), mla_kv_gain[i].reshape(1, -1),
                                     wuq, wk, wv, place, tables_mla, rblk)
            out_ckv.append(ckv[:p].reshape(bp, sp, MLA_KV_RANK))
            hw = MLA_HEADS * LANE
            a_p = _attention(q, [(k, v, True)], MLA_HEADS, 1, bp, sp, 0)
            kpe_c = jnp.pad(cache_mla_kpe[:, i].reshape(bs * past, MLA_ROPE),
                            ((0, 0), (MLA_NOPE, LANE - MLA_NOPE - MLA_ROPE)))
            kc, vc = _mla_cache(cache_mla_ckv[:, i].reshape(bs * past, MLA_KV_RANK), kpe_c, wk, wv, place)
            a_s = _attention(q, [(kc.reshape(bs, past, hw), vc.reshape(bs, past, hw), False), (k, v, True)],
                             MLA_HEADS, 1, bs, ss, p)
            w_out = jnp.concatenate([od_w_out[i][:S5_WIDTH], _pad_head_rows(od_w_out[i][S5_WIDTH:], MLA_HEADS, MLA_V)],
                                    axis=0).astype(BF16)
            x = _odd_out(proj, y_p, y_s, a_p.reshape(p, hw), a_s.reshape(bs * ss, hw), x, g1, s5_d[i].reshape(1, -1),
                         s5_w_glu[i].astype(BF16), s5_b_glu[i].reshape(1, -1), w_out, lg[0], lb[0], rows_proj, alpha)

            hb, pos, gate, pt, cnt = _router(x, sc2, sh2, moe_router[i].T, rows_moe)
            cnt = cnt[:, :, 0]
            nblk = n // T_MOE
            n_tiles = -(-(2 * n + nblk * N_EXPERTS * (SEG_ALIGN - 1)) // RT_GMM) + N_EXPERTS
            base, tile_expert, tile_valid = _moe_layout(cnt, n_tiles)
            counts = cnt.reshape(-1)
            xs, gs = _moe_gather(hb, pos, gate, counts, base, n_tiles * RT_GMM)
            ys = _moe_gmm(xs, gs, tile_expert, tile_valid, moe_wg, moe_wu, moe_wd, i)
            x = _moe_combine(ys, pt, x, g2, lg[1], lb[1], counts, base, rows_moe, alpha)

    y_prompt = x[:p].reshape(bp, sp, d)
    y_sample = x[p:].reshape(bs, ss, d)
    return (y_prompt, y_sample, jnp.stack(out_k, axis=1), jnp.stack(out_v, axis=1),
            jnp.stack(out_ckv, axis=1), jnp.stack(out_kpe, axis=1),
            jnp.stack(out_sre, axis=1), jnp.stack(out_sim, axis=1))
```

```python
import functools
import math

import jax
import jax.numpy as jnp
import numpy as np
from jax import lax
from jax.experimental import pallas as pl
from jax.experimental.pallas import tpu as pltpu

F32 = jnp.float32
BF16 = jnp.bfloat16

LANE = 128
SUBLANE = 8
MIB = 1024 * 1024

GRID_W = 64
ROPE_THETA = 10000.0
LN_EPS = 1e-6
RMS_EPS = 1e-6
HEAD_DIM = 64
ATTN_HEADS = 8
ATTN_KV_HEADS = 2
CONV_WIDTH = 512
S5_WIDTH = 512
S5_GROUP = 16
S5_GROUPS = 32
S5_STATE = 64
S5_CHUNK = 16
MLA_HEADS = 8
MLA_Q_RANK = 256
MLA_KV_RANK = 128
MLA_NOPE = 64
MLA_ROPE = 32
MLA_V = 64
N_EXPERTS = 8

TM_PROJ = 512
TM_EVEN_IN = 256
TM_FFN = 512
FFN_CHUNK = 2816
T_MOE = 1024
ROW_MOE = 128
RT_GMM = 512
GMM_STAGE_PIECES = 16
SEG_ALIGN = 16
SEG_SIZES = tuple(T_MOE >> k for k in range(T_MOE.bit_length()) if (T_MOE >> k) >= SEG_ALIGN)
TQ_ATTN = 512


def _params(sem, vmem_mib):
    return pltpu.CompilerParams(dimension_semantics=sem, vmem_limit_bytes=vmem_mib * MIB)


def _dot(a, b):
    return jnp.dot(a, b, preferred_element_type=F32)


def _dot_nt(a, b):
    return lax.dot_general(a, b, (((1,), (1,)), ((), ())), preferred_element_type=F32)


def _split(a):
    hi = a.astype(BF16)
    lo = (a - hi.astype(F32)).astype(BF16)
    return hi, lo


def _silu(x):
    return x * jax.nn.sigmoid(x)


def _layer_norm(r, g, b):
    mu = jnp.mean(r, axis=-1, keepdims=True)
    d = r - mu
    var = jnp.mean(d * d, axis=-1, keepdims=True)
    return d * lax.rsqrt(var + LN_EPS) * g + b


def _rms(x, g, n):
    ms = jnp.sum(x * x, axis=-1, keepdims=True) * (1.0 / n)
    return x * lax.rsqrt(ms + RMS_EPS) * g


def _rope(x, cos, sa, sb, q):
    w = x.shape[-1]
    return x * cos + pltpu.roll(x, w - q, 1) * sa + pltpu.roll(x, q, 1) * sb


def _ada_kernel(c_ref, w_ref, b_ref, o_ref):
    c = c_ref[...]
    a_hi, a_lo = _split(_silu(c))
    w_hi, w_lo = _split(w_ref[0])
    o_ref[0] = _dot(a_hi, w_hi) + _dot(a_lo, w_hi) + _dot(a_hi, w_lo) + b_ref[0]


def _modulation(cond, ada_w, ada_b):
    depth, d, d6 = ada_w.shape
    r = cond.shape[0]
    tn = 1536
    return pl.pallas_call(
        _ada_kernel,
        out_shape=jax.ShapeDtypeStruct((depth, r, d6), F32),
        grid=(depth, d6 // tn),
        in_specs=[pl.BlockSpec((r, d), lambda l, j: (0, 0)),
                  pl.BlockSpec((1, d, tn), lambda l, j: (l, 0, j)),
                  pl.BlockSpec((1, 1, tn), lambda l, j: (l, 0, j))],
        out_specs=pl.BlockSpec((1, r, tn), lambda l, j: (l, 0, j)),
        compiler_params=_params(("arbitrary", "arbitrary"), 40),
        name="ada_modulation",
    )(cond, ada_w, ada_b.reshape(depth, 1, d6))


def _load_x(x_refs, n_ctx_tiles):
    if len(x_refs) == 1:
        return x_refs[0][...]
    return _pick_pass(pl.program_id(0), n_ctx_tiles, *x_refs)


def _x_specs(x, tm, tok):
    if not isinstance(x, tuple):
        return (x,), [pl.BlockSpec((tm, x.shape[1]), tok)], 0
    n_ctx, n_lat = x[0].shape[0] // tm, x[1].shape[0] // tm
    return x, list(_pass_specs((tm, x[0].shape[1]), n_ctx, n_lat)), n_ctx


def _even_in_kernel(rows_ref, rblk_ref, *refs, n_x, n_ctx_tiles):
    (sc_ref, sh_ref, wc_ref, wq_ref, wkv_ref, qg_ref, kg_ref, cos_ref, sa_ref, sb_ref,
     conv_ref, q_ref, k_ref, v_ref, kn_ref, vf_ref) = refs[n_x:]
    del rows_ref, rblk_ref
    h = (_load_x(refs[:n_x], n_ctx_tiles) * (1.0 + sc_ref[0]) + sh_ref[0]).astype(BF16)
    conv_ref[...] = _dot(h, wc_ref[...])
    cos, sa, sb = cos_ref[...], sa_ref[...], sb_ref[...]
    quarter = HEAD_DIM // 4
    q = _dot(h, wq_ref[...])
    for hd in range(ATTN_HEADS):
        sl = slice(hd * LANE, (hd + 1) * LANE)
        qn = _rms(q[:, sl], qg_ref[...], HEAD_DIM)
        q_ref[:, sl] = (_rope(qn, cos, sa, sb, quarter) * (HEAD_DIM ** -0.5)).astype(BF16)
    kv = _dot(h, wkv_ref[...])
    kw = ATTN_KV_HEADS * LANE
    for hd in range(ATTN_KV_HEADS):
        sl = slice(hd * LANE, (hd + 1) * LANE)
        kn = _rms(kv[:, sl], kg_ref[...], HEAD_DIM)
        kn_ref[:, sl] = kn
        k_ref[:, sl] = _rope(kn, cos, sa, sb, quarter).astype(BF16)
    v = kv[:, kw:]
    vf_ref[...] = v
    v_ref[...] = v.astype(BF16)


def _even_in(x, sc, sh, wc, wq, wkv, qg, kg, tables, rows, rblk):
    tm = TM_EVEN_IN
    d = wc.shape[0]
    n = rows.shape[0] * tm
    cw, qw, kvw = wc.shape[1], wq.shape[1], wkv.shape[1]
    kw = kvw // 2
    cos, sa, sb = tables
    row3 = lambda i, rows, rblk: (rows[i], 0, 0)
    tok = lambda i, rows, rblk: (i, 0)
    const = lambda i, rows, rblk: (0, 0)
    tab = lambda i, rows, rblk: (rblk[i], 0)
    xs, x_specs, n_ctx = _x_specs(x, tm, tok)
    return pl.pallas_call(
        functools.partial(_even_in_kernel, n_x=len(xs), n_ctx_tiles=n_ctx),
        out_shape=(jax.ShapeDtypeStruct((n, cw), F32), jax.ShapeDtypeStruct((n, qw), BF16),
                   jax.ShapeDtypeStruct((n, kw), BF16), jax.ShapeDtypeStruct((n, kw), BF16),
                   jax.ShapeDtypeStruct((n, kw), F32), jax.ShapeDtypeStruct((n, kw), F32)),
        grid_spec=pltpu.PrefetchScalarGridSpec(
            num_scalar_prefetch=2, grid=(n // tm,),
            in_specs=x_specs + [
                      pl.BlockSpec((1, 1, d), row3), pl.BlockSpec((1, 1, d), row3),
                      pl.BlockSpec((d, cw), const), pl.BlockSpec((d, qw), const), pl.BlockSpec((d, kvw), const),
                      pl.BlockSpec((1, LANE), const), pl.BlockSpec((1, LANE), const),
                      pl.BlockSpec((tm, LANE), tab), pl.BlockSpec((tm, LANE), tab), pl.BlockSpec((tm, LANE), tab)],
            out_specs=[pl.BlockSpec((tm, cw), tok), pl.BlockSpec((tm, qw), tok),
                       pl.BlockSpec((tm, kw), tok), pl.BlockSpec((tm, kw), tok),
                       pl.BlockSpec((tm, kw), tok), pl.BlockSpec((tm, kw), tok)]),
        compiler_params=_params(("arbitrary",), 40),
        name="even_in_proj",
    )(rows, rblk, *xs, sc, sh, wc, wq, wkv, qg, kg, cos, sa, sb)


def _attn_kernel(*refs, n_heads, group, n_seg):
    q_ref = refs[0]
    seg = refs[1:1 + 2 * n_seg]
    o_ref = refs[1 + 2 * n_seg]
    for hd in range(n_heads):
        sl = slice(hd * LANE, (hd + 1) * LANE)
        ks = slice((hd // group) * LANE, (hd // group + 1) * LANE)
        qh = q_ref[:, sl]
        scores = [_dot_nt(qh, seg[2 * s][:, ks]) for s in range(n_seg)]
        m = jnp.max(scores[0], axis=-1, keepdims=True)
        for s in range(1, n_seg):
            m = jnp.maximum(m, jnp.max(scores[s], axis=-1, keepdims=True))
        den = None
        acc = None
        for s in range(n_seg):
            p = jnp.exp(scores[s] - m)
            ps = jnp.sum(p, axis=-1, keepdims=True)
            pv = _dot(p.astype(BF16), seg[2 * s + 1][:, ks])
            den = ps if den is None else den + ps
            acc = pv if acc is None else acc + pv
        o_ref[:, sl] = (acc * (1.0 / den)).astype(BF16)


def _attention(q, segments, n_heads, group, b, s, row0):
    n, qw = q.shape
    assert n % s == 0 and row0 % s == 0
    b0 = row0 // s
    tq = min(TQ_ATTN, s)
    in_specs = [pl.BlockSpec((None, tq, qw), lambda i, j: (i + b0, j, 0))]
    args = [q.reshape(n // s, s, qw)]
    for k, v, own in segments:
        kw = k.shape[-1]
        if own:
            k, v = k.reshape(n // s, s, kw), v.reshape(n // s, s, kw)
            kmap = lambda i, j: (i + b0, 0, 0)
        else:
            kmap = lambda i, j: (i, 0, 0)
        t = k.shape[1]
        in_specs += [pl.BlockSpec((None, t, kw), kmap, pipeline_mode=pl.Buffered(1)),
                     pl.BlockSpec((None, t, kw), kmap, pipeline_mode=pl.Buffered(1))]
        args += [k, v]
    return pl.pallas_call(
        functools.partial(_attn_kernel, n_heads=n_heads, group=group, n_seg=len(segments)),
        out_shape=jax.ShapeDtypeStruct((b, s, qw), BF16),
        grid=(b, s // tq),
        in_specs=in_specs,
        out_specs=pl.BlockSpec((None, tq, qw), lambda i, j: (i, j, 0)),
        compiler_params=_params(("arbitrary", "arbitrary"), 56),
        name="attention",
    )(*args)


def _pick_pass(i, n_ctx_tiles, ctx_ref, lat_ref):
    return jnp.where(i < n_ctx_tiles, ctx_ref[...], lat_ref[...])


def _pass_specs(block, n_ctx_tiles, n_lat_tiles):
    lead = (0,) * (len(block) - 2)

    def ctx(i, *_):
        return lead + (jnp.minimum(i, n_ctx_tiles - 1), 0)

    def lat(i, *_):
        return lead + (jnp.clip(i - n_ctx_tiles, 0, n_lat_tiles - 1), 0)

    return pl.BlockSpec(block, ctx), pl.BlockSpec(block, lat)


def _even_out_kernel(first_ref, last_ref, rows_ref, c_ref, cp_ref, cn_ref, ap_ref, as_ref, *refs,
                     alpha, n_ctx_tiles, ctx_seq, n_x):
    g_ref, cw_ref, w_ref, lg_ref, lb_ref, o_ref = refs[n_x:]
    del rows_ref
    i = pl.program_id(0)
    attn = _pick_pass(i, n_ctx_tiles, ap_ref, as_ref)
    tm = c_ref.shape[0]
    cwid = CONV_WIDTH
    c = c_ref[...]
    gate_b, z = c[:, :cwid], c[:, cwid:2 * cwid] * c[:, 2 * cwid:]
    cp = cp_ref[...]
    cn = cn_ref[...]
    zp = cp[SUBLANE - 1:SUBLANE, cwid:2 * cwid] * cp[SUBLANE - 1:SUBLANE, 2 * cwid:]
    zn = cn[0:1, cwid:2 * cwid] * cn[0:1, 2 * cwid:]
    zp = zp * (1 - first_ref[i]).astype(F32)
    zn = zn * (1 - last_ref[i]).astype(F32)
    row = lax.broadcasted_iota(jnp.int32, (tm, cwid), 0)
    z_prev = jnp.where(row == 0, zp, pltpu.roll(z, 1, 0))
    z_next = jnp.where(row == tm - 1, zn, pltpu.roll(z, tm - 1, 0))
    for k in range(1, tm // ctx_seq):
        edge = jnp.where(i < n_ctx_tiles, k * ctx_seq, -1)
        z_prev = jnp.where(row == edge, 0.0, z_prev)
        z_next = jnp.where(row == edge - 1, 0.0, z_next)
    cw = cw_ref[...]
    y = gate_b * (cw[0:1] * z_prev + cw[1:2] * z + cw[2:3] * z_next)
    out = _dot(y.astype(BF16), w_ref[:cwid, :]) + _dot(attn, w_ref[cwid:, :])
    x = _load_x(refs[:n_x], n_ctx_tiles)
    o_ref[...] = _layer_norm(alpha * x + g_ref[0] * out, lg_ref[...], lb_ref[...])


def _even_out(conv, attn_p, attn_s, x, g1, conv_w, w_out, ln_g, ln_b, first, last, rows, alpha, ctx_seq):
    tm = TM_PROJ
    n, d = conv.shape[0], w_out.shape[1]
    cw3 = conv.shape[1]
    aw = attn_p.shape[1]
    hb = tm // SUBLANE
    nblk8 = n // SUBLANE
    n_ctx, n_lat = attn_p.shape[0] // tm, attn_s.shape[0] // tm
    ap_spec, as_spec = _pass_specs((tm, aw), n_ctx, n_lat)
    tok = lambda i, f, l, r: (i, 0)
    const = lambda i, f, l, r: (0, 0)
    xs, x_specs, _ = _x_specs(x, tm, tok)
    return pl.pallas_call(
        functools.partial(_even_out_kernel, alpha=alpha, n_ctx_tiles=n_ctx, ctx_seq=ctx_seq, n_x=len(xs)),
        out_shape=jax.ShapeDtypeStruct((n, d), F32),
        grid_spec=pltpu.PrefetchScalarGridSpec(
            num_scalar_prefetch=3, grid=(n // tm,),
            in_specs=[pl.BlockSpec((tm, cw3), tok),
                      pl.BlockSpec((SUBLANE, cw3), lambda i, f, l, r: (jnp.maximum(i * hb - 1, 0), 0)),
                      pl.BlockSpec((SUBLANE, cw3), lambda i, f, l, r: (jnp.minimum((i + 1) * hb, nblk8 - 1), 0)),
                      ap_spec, as_spec] + x_specs + [
                      pl.BlockSpec((1, 1, d), lambda i, f, l, r: (r[i], 0, 0)),
                      pl.BlockSpec((SUBLANE, CONV_WIDTH), const),
                      pl.BlockSpec((CONV_WIDTH + aw, d), const),
                      pl.BlockSpec((1, d), const), pl.BlockSpec((1, d), const)],
            out_specs=pl.BlockSpec((tm, d), tok)),
        compiler_params=_params(("arbitrary",), 40),
        name="even_out_proj",
    )(first, last, rows, conv, conv, conv, attn_p, attn_s, *xs, g1, conv_w, w_out, ln_g, ln_b)


def _ffn_kernel(rows_ref, x_ref, sc_ref, sh_ref, g_ref, wg_ref, wu_ref, wd_ref, lg_ref, lb_ref, o_ref,
                h_s, acc_s, *, alpha):
    del rows_ref
    f = pl.program_id(1)

    @pl.when(f == 0)
    def _():
        h_s[...] = (x_ref[...] * (1.0 + sc_ref[0]) + sh_ref[0]).astype(BF16)

    h = h_s[...]
    a = (_silu(_dot(h, wg_ref[...])) * _dot(h, wu_ref[...])).astype(BF16)
    y = _dot(a, wd_ref[...])

    @pl.when(f == 0)
    def _():
        acc_s[...] = y

    @pl.when(f > 0)
    def _():
        acc_s[...] += y

    @pl.when(f == pl.num_programs(1) - 1)
    def _():
        o_ref[...] = _layer_norm(alpha * x_ref[...] + g_ref[0] * acc_s[...], lg_ref[...], lb_ref[...])


def _ffn(x, sc, sh, g2, wg, wu, wd, layer, ln_g, ln_b, rows, alpha):
    n, d = x.shape
    tm = TM_FFN
    ff = wg.shape[-1]
    fc = FFN_CHUNK
    tok = lambda i, f, r: (i, 0)
    row3 = lambda i, f, r: (r[i], 0, 0)
    const = lambda i, f, r: (0, 0)
    wmode = dict(pipeline_mode=pl.Buffered(1)) if fc == ff else {}
    return pl.pallas_call(
        functools.partial(_ffn_kernel, alpha=alpha),
        out_shape=jax.ShapeDtypeStruct((n, d), F32),
        grid_spec=pltpu.PrefetchScalarGridSpec(
            num_scalar_prefetch=1, grid=(n // tm, ff // fc),
            in_specs=[pl.BlockSpec((tm, d), tok),
                      pl.BlockSpec((1, 1, d), row3), pl.BlockSpec((1, 1, d), row3), pl.BlockSpec((1, 1, d), row3),
                      pl.BlockSpec((None, d, fc), lambda i, f, r: (layer, 0, f), **wmode),
                      pl.BlockSpec((None, d, fc), lambda i, f, r: (layer, 0, f), **wmode),
                      pl.BlockSpec((None, fc, d), lambda i, f, r: (layer, f, 0), **wmode),
                      pl.BlockSpec((1, d), const), pl.BlockSpec((1, d), const)],
            out_specs=pl.BlockSpec((tm, d), tok),
            scratch_shapes=[pltpu.VMEM((tm, d), BF16), pltpu.VMEM((tm, d), F32)]),
        compiler_params=_params(("arbitrary", "arbitrary"), 56),
        name="ffn_swiglu",
    )(rows, x, sc, sh, g2, wg, wu, wd, ln_g, ln_b)


def _odd_in_kernel(rows_ref, x_ref, sc_ref, sh_ref, w_ref, o_ref, u_ref, u_s):
    del rows_ref
    h = (x_ref[...] * (1.0 + sc_ref[0]) + sh_ref[0]).astype(BF16)
    proj = _dot(h, w_ref[...])
    o_ref[...] = proj
    n_tile = S5_WIDTH // LANE
    gpt = LANE // S5_GROUP
    for j in range(n_tile):
        u_s[j] = proj[:, j * LANE:(j + 1) * LANE]
    n_chunk = u_s.shape[1] // S5_CHUNK
    steps = [[u_s[j, pl.ds(s, n_chunk, stride=S5_CHUNK), :] for j in range(n_tile)] for s in range(S5_CHUNK)]
    for g in range(S5_GROUPS):
        sl = slice((g % gpt) * S5_GROUP, (g % gpt + 1) * S5_GROUP)
        u_ref[g] = jnp.concatenate([st[g // gpt][:, sl] for st in steps], axis=1).astype(BF16)


def _odd_in(x, sc, sh, w, rows, n_ctx_tiles, ctx_chunk0, lat_chunk0):
    n, d = x.shape
    tm = TM_PROJ
    nw = w.shape[1]
    cpt = tm // S5_CHUNK
    u_rows = n // S5_CHUNK
    tok = lambda i, r: (i, 0)
    row3 = lambda i, r: (r[i], 0, 0)
    ublk = lambda i, r: (0, jnp.where(i < n_ctx_tiles, i + ctx_chunk0 // cpt, i - n_ctx_tiles + lat_chunk0 // cpt), 0)
    return pl.pallas_call(
        _odd_in_kernel,
        out_shape=(jax.ShapeDtypeStruct((n, nw), F32),
                   jax.ShapeDtypeStruct((S5_GROUPS, u_rows, S5_CHUNK * S5_GROUP), BF16)),
        grid_spec=pltpu.PrefetchScalarGridSpec(
            num_scalar_prefetch=1, grid=(n // tm,),
            in_specs=[pl.BlockSpec((tm, d), tok), pl.BlockSpec((1, 1, d), row3), pl.BlockSpec((1, 1, d), row3),
                      pl.BlockSpec((d, nw), lambda i, r: (0, 0))],
            out_specs=[pl.BlockSpec((tm, nw), tok), pl.BlockSpec((S5_GROUPS, cpt, S5_CHUNK * S5_GROUP), ublk)],
            scratch_shapes=[pltpu.VMEM((S5_WIDTH // LANE, tm, LANE), F32)]),
        compiler_params=_params(("arbitrary",), 32),
        name="odd_in_proj",
    )(rows, x, sc, sh, w)


def _s5_kernel(u_ref, zin_ref, t_ref, zout_ref, al_ref, h0_ref, y_ref, fin_ref, z_s, hf_s, hb_s, *, nb, nc):
    u = u_ref[0]
    z = _dot(u, zin_ref[0])
    for part in range(2):
        for b in range(nb):
            z_s[part, pl.ds(b, nc, stride=nb), :] = z[b * nc:(b + 1) * nc, part * LANE:(part + 1) * LANE]
    ar = al_ref[0, 0:1, :]
    ai = al_ref[0, 1:2, :]
    fwd = lax.broadcasted_iota(jnp.int32, (nb, LANE), 1) < S5_STATE
    aligned = (lambda r: pl.multiple_of(r, SUBLANE)) if nb % SUBLANE == 0 else (lambda r: r)

    def step(k, carry):
        re, im = carry
        rf = aligned(k * nb)
        rb = aligned((nc - 1 - k) * nb)
        hf_s[0, pl.ds(rf, nb), :] = re
        hf_s[1, pl.ds(rf, nb), :] = im
        hb_s[0, pl.ds(rb, nb), :] = re
        hb_s[1, pl.ds(rb, nb), :] = im
        zr = jnp.where(fwd, z_s[0, pl.ds(rf, nb), :], z_s[0, pl.ds(rb, nb), :])
        zi = jnp.where(fwd, z_s[1, pl.ds(rf, nb), :], z_s[1, pl.ds(rb, nb), :])
        return ar * re - ai * im + zr, ar * im + ai * re + zi

    h0 = h0_ref[0]
    re, im = lax.fori_loop(0, nc, step, (h0[:, :LANE], h0[:, LANE:]))
    fin_ref[0, :, :LANE] = re
    fin_ref[0, :, LANE:] = im
    m = u.shape[0]
    is_fwd = lax.broadcasted_iota(jnp.int32, (m, LANE), 1) < S5_STATE
    halves = []
    for part in range(2):
        hf_s[part] = jnp.where(is_fwd, hf_s[part], hb_s[part])
        halves.append(jnp.concatenate([hf_s[part, pl.ds(b, nc, stride=nb), :] for b in range(nb)], axis=0))
    h_in = jnp.concatenate(halves, axis=1).astype(BF16)
    y_ref[0] = _dot(u, t_ref[0]) + _dot(h_in, zout_ref[0])


def _s5(u, mats, layer, h0, nb, nc, row_block):
    zin, tmat, zout, al = mats
    g, _, w = u.shape
    m = nb * nc
    blk = lambda i: (i, 0, 0)
    lblk = lambda i: (layer, i, 0, 0)
    return pl.pallas_call(
        functools.partial(_s5_kernel, nb=nb, nc=nc),
        out_shape=(jax.ShapeDtypeStruct((g, m, w), F32), jax.ShapeDtypeStruct((g, nb, w), F32)),
        grid=(g,),
        in_specs=[pl.BlockSpec((1, m, w), lambda i: (i, row_block, 0)),
                  pl.BlockSpec((None, 1, w, w), lblk), pl.BlockSpec((None, 1, w, w), lblk),
                  pl.BlockSpec((None, 1, w, w), lblk), pl.BlockSpec((None, 1, 2, LANE), lblk),
                  pl.BlockSpec((1, nb, w), blk)],
        out_specs=(pl.BlockSpec((1, m, w), blk), pl.BlockSpec((1, nb, w), blk)),
        scratch_shapes=[pltpu.VMEM((2, m, LANE), F32), pltpu.VMEM((2, m, LANE), F32), pltpu.VMEM((2, m, LANE), F32)],
        compiler_params=_params(("arbitrary",), 32),
        name="s5_scan",
    )(u, zin, tmat, zout, al, h0)


def _s5_matrices(a_re, a_im, log_dt, b_re, b_im, c_re, c_im):
    hp = lax.Precision.HIGHEST
    L = S5_CHUNK
    dt = jnp.exp(log_dt)[..., None]
    lam_re, lam_im = a_re * dt, a_im * dt

    def power(k):
        k = k[:, None, None, None]
        mag = jnp.exp(lam_re * k)
        return mag * jnp.cos(lam_im * k), mag * jnp.sin(lam_im * k)

    ab_re, ab_im = power(jnp.ones((1,), F32))
    ab_re, ab_im = ab_re[0], ab_im[0]
    num_re, num_im = ab_re - 1.0, ab_im
    den = a_re * a_re + a_im * a_im
    f_re = (num_re * a_re + num_im * a_im) / den
    f_im = (num_im * a_re - num_re * a_im) / den
    bb_re = f_re[..., None] * b_re - f_im[..., None] * b_im
    bb_im = f_re[..., None] * b_im + f_im[..., None] * b_re

    ks = jnp.arange(L + 1, dtype=F32)
    pw_re, pw_im = power(ks)

    def zin_dir(d, exps):
        pr, pi = pw_re[exps, d], pw_im[exps, d]
        w_re = pr[..., None] * bb_re[d][None] - pi[..., None] * bb_im[d][None]
        w_im = pr[..., None] * bb_im[d][None] + pi[..., None] * bb_re[d][None]
        to = lambda w: jnp.transpose(w, (1, 0, 3, 2)).reshape(S5_GROUPS, L * S5_GROUP, S5_STATE)
        return to(w_re), to(w_im)

    steps = np.arange(L)
    zf_re, zf_im = zin_dir(0, L - 1 - steps)
    zb_re, zb_im = zin_dir(1, steps)
    zin = jnp.concatenate([zf_re, zb_re, zf_im, zb_im], axis=-1)

    def zout_dir(d, exps):
        pr, pi = pw_re[exps, d], pw_im[exps, d]
        cr, ci = c_re[d], c_im[d]
        e_re = cr[None] * pr[:, :, None, :] - ci[None] * pi[:, :, None, :]
        e_im = cr[None] * pi[:, :, None, :] + ci[None] * pr[:, :, None, :]
        to = lambda e: jnp.transpose(e, (1, 3, 0, 2)).reshape(S5_GROUPS, S5_STATE, L * S5_GROUP)
        return to(e_re), to(-e_im)

    of_re, of_im = zout_dir(0, steps + 1)
    ob_re, ob_im = zout_dir(1, L - steps)
    zout = jnp.concatenate([of_re, ob_re, of_im, ob_im], axis=1)

    def taps(d):
        pr, pi = pw_re[:L, d], pw_im[:L, d]
        m_re = pr[..., None] * bb_re[d][None] - pi[..., None] * bb_im[d][None]
        m_im = pr[..., None] * bb_im[d][None] + pi[..., None] * bb_re[d][None]
        return (jnp.einsum('gpn,lgnq->lgpq', c_re[d], m_re, precision=hp)
                - jnp.einsum('gpn,lgnq->lgpq', c_im[d], m_im, precision=hp))

    kf, kb = taps(0), taps(1)
    pad4 = ((0, 0),) * 3
    tm = jnp.stack([jnp.pad(kf[:L - s], ((s, 0),) + pad4) + jnp.pad(kb[:s + 1][::-1], ((0, L - 1 - s),) + pad4)
                    for s in range(L)])
    tmat = jnp.transpose(tm, (2, 0, 4, 1, 3)).reshape(S5_GROUPS, L * S5_GROUP, L * S5_GROUP)

    al = jnp.stack([jnp.concatenate([pw_re[L, 0], pw_re[L, 1]], axis=-1),
                    jnp.concatenate([pw_im[L, 0], pw_im[L, 1]], axis=-1)], axis=1)
    return zin.astype(BF16), tmat.astype(BF16), zout.astype(BF16), al


def _mla_kv(ckv, kpe, wk_ref, wv_ref, pl_ref, k_ref, v_ref):
    cb = ckv.astype(BF16)
    k_ref[...] = (_dot(cb, wk_ref[...]) + _dot(kpe.astype(BF16), pl_ref[...])).astype(BF16)
    v_ref[...] = _dot(cb, wv_ref[...]).astype(BF16)


def _mla_prep_kernel(rblk_ref, p_ref, qg_ref, kvg_ref, wuq_ref, wk_ref, wv_ref, pl_ref, cos_ref, sa_ref, sb_ref,
                     q_ref, k_ref, v_ref, ckv_ref):
    del rblk_ref
    pr = p_ref[...]
    cos, sa, sb = cos_ref[...], sa_ref[...], sb_ref[...]
    quarter = MLA_ROPE // 4
    scale = (MLA_NOPE + MLA_ROPE) ** -0.5
    qn = _rms(pr[:, :MLA_Q_RANK], qg_ref[...], MLA_Q_RANK).astype(BF16)
    q = _dot(qn, wuq_ref[...])
    for hd in range(MLA_HEADS):
        sl = slice(hd * LANE, (hd + 1) * LANE)
        q_ref[:, sl] = (_rope(q[:, sl], cos, sa, sb, quarter) * scale).astype(BF16)
    ckv = _rms(pr[:, MLA_Q_RANK:MLA_Q_RANK + MLA_KV_RANK], kvg_ref[...], MLA_KV_RANK)
    ckv_ref[...] = ckv
    kpe = _rope(pr[:, MLA_Q_RANK + MLA_KV_RANK:], cos, sa, sb, quarter)
    _mla_kv(ckv, kpe, wk_ref, wv_ref, pl_ref, k_ref, v_ref)


def _mla_prep(proj, qg, kvg, wuq, wk, wv, place, tables, rblk):
    n = proj.shape[0]
    tm = TM_PROJ
    hw = MLA_HEADS * LANE
    pw = MLA_Q_RANK + MLA_KV_RANK + LANE
    cos, sa, sb = tables
    tok = lambda i, r: (i, 0)
    const = lambda i, r: (0, 0)
    tab = lambda i, r: (r[i], 0)
    return pl.pallas_call(
        _mla_prep_kernel,
        out_shape=(jax.ShapeDtypeStruct((n, hw), BF16), jax.ShapeDtypeStruct((n, hw), BF16),
                   jax.ShapeDtypeStruct((n, hw), BF16), jax.ShapeDtypeStruct((n, MLA_KV_RANK), F32)),
        grid_spec=pltpu.PrefetchScalarGridSpec(
            num_scalar_prefetch=1, grid=(n // tm,),
            in_specs=[pl.BlockSpec((tm, pw), lambda i, r: (i, 1)),
                      pl.BlockSpec((1, MLA_Q_RANK), const), pl.BlockSpec((1, MLA_KV_RANK), const),
                      pl.BlockSpec((MLA_Q_RANK, hw), const), pl.BlockSpec((MLA_KV_RANK, hw), const),
                      pl.BlockSpec((MLA_KV_RANK, hw), const), pl.BlockSpec((LANE, hw), const),
                      pl.BlockSpec((tm, LANE), tab), pl.BlockSpec((tm, LANE), tab), pl.BlockSpec((tm, LANE), tab)],
            out_specs=[pl.BlockSpec((tm, hw), tok), pl.BlockSpec((tm, hw), tok), pl.BlockSpec((tm, hw), tok),
                       pl.BlockSpec((tm, MLA_KV_RANK), tok)]),
        compiler_params=_params(("arbitrary",), 32),
        name="mla_prep",
    )(rblk, proj, qg, kvg, wuq, wk, wv, place, cos, sa, sb)


def _mla_cache_kernel(c_ref, p_ref, wk_ref, wv_ref, pl_ref, k_ref, v_ref):
    _mla_kv(c_ref[...], p_ref[...], wk_ref, wv_ref, pl_ref, k_ref, v_ref)


def _mla_cache(ckv, kpe, wk, wv, place):
    n = ckv.shape[0]
    tm = min(TM_PROJ, n)
    assert n % tm == 0
    hw = MLA_HEADS * LANE
    tok = lambda i: (i, 0)
    const = lambda i: (0, 0)
    return pl.pallas_call(
        _mla_cache_kernel,
        out_shape=(jax.ShapeDtypeStruct((n, hw), BF16), jax.ShapeDtypeStruct((n, hw), BF16)),
        grid=(n // tm,),
        in_specs=[pl.BlockSpec((tm, MLA_KV_RANK), tok), pl.BlockSpec((tm, LANE), tok),
                  pl.BlockSpec((MLA_KV_RANK, hw), const), pl.BlockSpec((MLA_KV_RANK, hw), const),
                  pl.BlockSpec((LANE, hw), const)],
        out_specs=[pl.BlockSpec((tm, hw), tok), pl.BlockSpec((tm, hw), tok)],
        compiler_params=_params(("arbitrary",), 32),
        name="mla_cache_kv",
    )(ckv, kpe, wk, wv, place)


def _gelu_tanh(x):
    return 0.5 * x * (1.0 + jnp.tanh(math.sqrt(2.0 / math.pi) * (x + 0.044715 * (x * x * x))))


def _odd_out_kernel(rows_ref, u_ref, yp_ref, ys_ref, ap_ref, as_ref, x_ref, g_ref, d_ref, wglu_ref, bglu_ref, w_ref,
                    lg_ref, lb_ref, o_ref, y_s, *, alpha, n_ctx_tiles):
    del rows_ref
    i = pl.program_id(0)
    yg = _pick_pass(i, n_ctx_tiles, yp_ref, ys_ref)
    n_chunk = yg.shape[1]
    n_tile = S5_WIDTH // LANE
    gpt = LANE // S5_GROUP
    for s in range(S5_CHUNK):
        sl = slice(s * S5_GROUP, (s + 1) * S5_GROUP)
        for j in range(n_tile):
            y_s[j, pl.ds(s, n_chunk, stride=S5_CHUNK), :] = jnp.concatenate(
                [yg[g][:, sl] for g in range(j * gpt, (j + 1) * gpt)], axis=1)
    y_ssm = jnp.concatenate([y_s[j] for j in range(n_tile)], axis=1)
    y = _gelu_tanh(u_ref[...] * d_ref[...] + y_ssm)
    y = y * jax.nn.sigmoid(_dot(y.astype(BF16), wglu_ref[...]) + bglu_ref[...])
    attn = _pick_pass(i, n_ctx_tiles, ap_ref, as_ref)
    out = _dot(y.astype(BF16), w_ref[:S5_WIDTH, :]) + _dot(attn, w_ref[S5_WIDTH:, :])
    o_ref[...] = _layer_norm(alpha * x_ref[...] + g_ref[0] * out, lg_ref[...], lb_ref[...])


def _odd_out(proj, y_p, y_s, attn_p, attn_s, x, g1, s5_d, w_glu, b_glu, w_out, ln_g, ln_b, rows, alpha):
    n, d = x.shape
    tm = TM_PROJ
    aw = attn_p.shape[1]
    cpt = tm // S5_CHUNK
    n_ctx, n_lat = attn_p.shape[0] // tm, attn_s.shape[0] // tm
    ap_spec, as_spec = _pass_specs((tm, aw), n_ctx, n_lat)
    yp_spec, ys_spec = _pass_specs((S5_GROUPS, cpt, S5_CHUNK * S5_GROUP), n_ctx, n_lat)
    tok = lambda i, r: (i, 0)
    const = lambda i, r: (0, 0)
    return pl.pallas_call(
        functools.partial(_odd_out_kernel, alpha=alpha, n_ctx_tiles=n_ctx),
        out_shape=jax.ShapeDtypeStruct((n, d), F32),
        grid_spec=pltpu.PrefetchScalarGridSpec(
            num_scalar_prefetch=1, grid=(n // tm,),
            in_specs=[pl.BlockSpec((tm, S5_WIDTH), tok), yp_spec, ys_spec,
                      ap_spec, as_spec, pl.BlockSpec((tm, d), tok),
                      pl.BlockSpec((1, 1, d), lambda i, r: (r[i], 0, 0)),
                      pl.BlockSpec((1, S5_WIDTH), const), pl.BlockSpec((S5_WIDTH, S5_WIDTH), const),
                      pl.BlockSpec((1, S5_WIDTH), const), pl.BlockSpec((S5_WIDTH + aw, d), const),
                      pl.BlockSpec((1, d), const), pl.BlockSpec((1, d), const)],
            out_specs=pl.BlockSpec((tm, d), tok),
            scratch_shapes=[pltpu.VMEM((S5_WIDTH // LANE, tm, LANE), F32)]),
        compiler_params=_params(("arbitrary",), 40),
        name="odd_out_proj",
    )(rows, proj, y_p, y_s, attn_p, attn_s, x, g1, s5_d, w_glu, b_glu, w_out, ln_g, ln_b)


def _router_kernel(rows_ref, x_ref, sc_ref, sh_ref, rt_ref, hb_ref, pos_ref, gate_ref, pt_ref, cnt_ref, tri_s):
    del rows_ref
    t = x_ref.shape[0]
    ne = N_EXPERTS

    @pl.when(pl.program_id(0) == 0)
    def _():
        before = lax.broadcasted_iota(jnp.int32, (t, t), 0) < lax.broadcasted_iota(jnp.int32, (t, t), 1)
        tri_s[...] = jnp.where(before, 1.0, 0.0).astype(BF16)

    h = x_ref[...] * (1.0 + sc_ref[0]) + sh_ref[0]
    h_hi, h_lo = _split(h)
    hb_ref[...] = h_hi
    r_hi, r_lo = _split(rt_ref[...])
    logits = _dot_nt(r_hi, h_hi) + _dot_nt(r_lo, h_hi) + _dot_nt(r_hi, h_lo)
    eid = lax.broadcasted_iota(jnp.int32, (ne, t), 0).astype(F32)
    m0 = jnp.max(logits, axis=0, keepdims=True)
    i0 = jnp.min(jnp.where(logits == m0, eid, float(ne)), axis=0, keepdims=True)
    rest = jnp.where(eid == i0, -jnp.inf, logits)
    m1 = jnp.max(rest, axis=0, keepdims=True)
    i1 = jnp.min(jnp.where(rest == m1, eid, float(ne)), axis=0, keepdims=True)
    ex = jnp.exp(m1 - m0)
    g0 = 1.0 / (1.0 + ex)
    g1 = ex / (1.0 + ex)
    sel0 = eid == i0
    sel1 = eid == i1
    member = jnp.where(sel0, 1.0, jnp.where(sel1, 1.0, 0.0))
    gate = jnp.where(sel0, g0, jnp.where(sel1, g1, 0.0))
    rank = _dot(member.astype(BF16), tri_s[...])
    pos = jnp.where(member > 0.0, rank, -1.0)
    pos_ref[0] = pos.astype(jnp.int32)
    gate_ref[0] = gate
    cnt = jnp.sum(member, axis=1, keepdims=True)
    cnt_ref[0] = jnp.broadcast_to(cnt, (ne, LANE)).astype(jnp.int32)
    packed = jnp.concatenate([pos, gate, jnp.zeros((LANE - 2 * ne, t), F32)], axis=0)
    pt_ref[...] = packed.T


def _router(x, sc, sh, router_t, rows):
    n, d = x.shape
    t = T_MOE
    nb = n // t
    ne = N_EXPERTS
    tok = lambda i, r: (i, 0)
    row3 = lambda i, r: (r[i], 0, 0)
    blk3 = lambda i, r: (i, 0, 0)
    return pl.pallas_call(
        _router_kernel,
        out_shape=(jax.ShapeDtypeStruct((n, d), BF16), jax.ShapeDtypeStruct((nb, ne, t), jnp.int32),
                   jax.ShapeDtypeStruct((nb, ne, t), F32),
                   jax.ShapeDtypeStruct((n, LANE), F32), jax.ShapeDtypeStruct((nb, ne, LANE), jnp.int32)),
        grid_spec=pltpu.PrefetchScalarGridSpec(
            num_scalar_prefetch=1, grid=(nb,),
            in_specs=[pl.BlockSpec((t, d), tok), pl.BlockSpec((1, 1, d), row3), pl.BlockSpec((1, 1, d), row3),
                      pl.BlockSpec((ne, d), lambda i, r: (0, 0))],
            out_specs=[pl.BlockSpec((t, d), tok), pl.BlockSpec((1, ne, t), blk3), pl.BlockSpec((1, ne, t), blk3),
                       pl.BlockSpec((t, LANE), tok), pl.BlockSpec((1, ne, LANE), blk3)],
            scratch_shapes=[pltpu.VMEM((t, t), BF16)]),
        compiler_params=_params(("arbitrary",), 48),
        name="moe_router",
    )(rows, x, sc, sh, router_t)


def _segment_copies(rows, src, dst, src0, dst0, sems, slot):
    out = []
    for k, sz in enumerate(SEG_SIZES):
        off = (rows // (2 * sz)) * (2 * sz)
        s0 = pl.multiple_of(src0 + off, SEG_ALIGN)
        d0 = pl.multiple_of(dst0 + off, SEG_ALIGN)
        sem = sems.at[slot * len(SEG_SIZES) + k]
        out.append(((rows & sz) != 0, pltpu.make_async_copy(src.at[pl.ds(s0, sz)], dst.at[pl.ds(d0, sz)], sem)))
    return out


def _run_copies(copies):
    for pred, cp in copies:
        pl.when(pred)(cp.start)
    for pred, cp in copies:
        pl.when(pred)(cp.wait)


def _moe_gather_kernel(cnt_ref, base_ref, h_ref, pos_ref, gate_ref, xs_in, gs_in, xs_ref, gs_ref,
                       xe_s, ge_s, sem_x, sem_g):
    del xs_in, gs_in
    b = pl.program_id(0)
    t = h_ref.shape[0]
    rt = ROW_MOE
    ne = N_EXPERTS
    copies = []
    off = 0
    for e in range(ne):
        cnt = cnt_ref[b * ne + e]
        pos_row = pos_ref[0, e:e + 1, :]
        gate_row = gate_ref[0, e:e + 1, :]

        def gather(r, c, off=off, pos_row=pos_row, gate_row=gate_row):
            r0 = r * rt
            hit = (lax.broadcasted_iota(jnp.int32, (rt, t), 0) + r0) == pos_row
            onehot = jnp.where(hit, 1.0, 0.0).astype(BF16)
            dst = pl.multiple_of(off + r0, SEG_ALIGN)
            xe_s[pl.ds(dst, rt), :] = _dot(onehot, h_ref[...]).astype(BF16)
            g = jnp.sum(jnp.where(hit, gate_row, 0.0), axis=1, keepdims=True)
            ge_s[pl.ds(dst, rt), :] = jnp.broadcast_to(g, (rt, LANE))
            return c

        lax.fori_loop(0, (cnt + rt - 1) // rt, gather, 0)
        rows = ((cnt + SEG_ALIGN - 1) // SEG_ALIGN) * SEG_ALIGN
        base = base_ref[b * ne + e]
        seg = (_segment_copies(rows, xe_s, xs_ref, off, base, sem_x, e)
               + _segment_copies(rows, ge_s, gs_ref, off, base, sem_g, e))
        for pred, cp in seg:
            pl.when(pred)(cp.start)
        copies += seg
        off = off + rows
    for pred, cp in copies:
        pl.when(pred)(cp.wait)


def _moe_gather(h, pos, gate, counts, base, n_rows):
    n, d = h.shape
    t = T_MOE
    nb = n // t
    ne = N_EXPERTS
    tok = lambda b, c, o: (b, 0)
    blk3 = lambda b, c, o: (b, 0, 0)
    any_spec = pl.BlockSpec(memory_space=pl.ANY)
    xs0 = jnp.zeros((n_rows, d), BF16)
    gs0 = jnp.zeros((n_rows, LANE), F32)
    cap = -(-(2 * t + ne * (SEG_ALIGN - 1) + ROW_MOE - 1) // ROW_MOE) * ROW_MOE
    return pl.pallas_call(
        _moe_gather_kernel,
        out_shape=(jax.ShapeDtypeStruct((n_rows, d), BF16), jax.ShapeDtypeStruct((n_rows, LANE), F32)),
        grid_spec=pltpu.PrefetchScalarGridSpec(
            num_scalar_prefetch=2, grid=(nb,),
            in_specs=[pl.BlockSpec((t, d), tok), pl.BlockSpec((1, ne, t), blk3), pl.BlockSpec((1, ne, t), blk3),
                      any_spec, any_spec],
            out_specs=[any_spec, any_spec],
            scratch_shapes=[pltpu.VMEM((cap, d), BF16), pltpu.VMEM((cap, LANE), F32),
                            pltpu.SemaphoreType.DMA((ne * len(SEG_SIZES),)),
                            pltpu.SemaphoreType.DMA((ne * len(SEG_SIZES),))]),
        input_output_aliases={5: 0, 6: 1},
        compiler_params=_params(("arbitrary",), 32),
        name="moe_gather",
    )(counts, base, h, pos, gate, xs0, gs0)


def _stream_cast(src, dst, stage, sems):
    rows = stage.shape[1]
    n = src.shape[0] // rows

    def piece(c):
        return pltpu.make_async_copy(src.at[pl.ds(c * rows, rows)], stage.at[c % 2], sems.at[c % 2])

    piece(0).start()
    for c in range(n):
        if c + 1 < n:
            piece(c + 1).start()
        piece(c).wait()
        dst[c * rows:(c + 1) * rows, :] = stage[c % 2].astype(BF16)


def _moe_gmm_kernel(te_ref, valid_ref, x_ref, gs_ref, wg_hbm, wu_hbm, wd_hbm, y_ref,
                    wg_s, wu_s, wd_s, stage_a, stage_b, sem_a, sem_b, *, layer):
    i = pl.program_id(0)
    valid = valid_ref[i] != 0
    e = te_ref[i]
    switch = valid & ((i == 0) | (e != te_ref[jnp.maximum(i - 1, 0)]))

    @pl.when(switch)
    def _():
        _stream_cast(wg_hbm.at[layer, e], wg_s, stage_a, sem_a)
        _stream_cast(wu_hbm.at[layer, e], wu_s, stage_a, sem_a)
        _stream_cast(wd_hbm.at[layer, e], wd_s, stage_b, sem_b)

    @pl.when(valid)
    def _():
        x = x_ref[...]
        a = (_silu(_dot(x, wg_s[...])) * _dot(x, wu_s[...])).astype(BF16)
        y_ref[...] = (_dot(a, wd_s[...]) * gs_ref[:, 0:1]).astype(BF16)

    @pl.when(jnp.logical_not(valid))
    def _():
        y_ref[...] = jnp.zeros_like(y_ref)


def _moe_gmm(xs, gs, tile_expert, tile_valid, wg, wu, wd, layer):
    n_rows, d = xs.shape
    rt = RT_GMM
    ff = wg.shape[-1]
    tok = lambda i, te, tv: (i, 0)
    any_spec = pl.BlockSpec(memory_space=pl.ANY)
    return pl.pallas_call(
        functools.partial(_moe_gmm_kernel, layer=layer),
        out_shape=jax.ShapeDtypeStruct((n_rows, d), BF16),
        grid_spec=pltpu.PrefetchScalarGridSpec(
            num_scalar_prefetch=2, grid=(n_rows // rt,),
            in_specs=[pl.BlockSpec((rt, d), tok), pl.BlockSpec((rt, LANE), tok), any_spec, any_spec, any_spec],
            out_specs=pl.BlockSpec((rt, d), tok),
            scratch_shapes=[pltpu.VMEM((d, ff), BF16), pltpu.VMEM((d, ff), BF16), pltpu.VMEM((ff, d), BF16),
                            pltpu.VMEM((2, d // GMM_STAGE_PIECES, ff), F32),
                            pltpu.VMEM((2, ff // GMM_STAGE_PIECES, d), F32),
                            pltpu.SemaphoreType.DMA((2,)), pltpu.SemaphoreType.DMA((2,))]),
        compiler_params=_params(("arbitrary",), 56),
        name="moe_gmm",
    )(tile_expert, tile_valid, xs, gs, wg, wu, wd)


def _moe_combine_kernel(cnt_ref, base_ref, rows_ref, ys_ref, pt_ref, x_ref, g_ref, lg_ref, lb_ref, o_ref,
                        ycat_s, sems, *, alpha, block0):
    del rows_ref
    t = x_ref.shape[0]
    cap = ycat_s.shape[0]
    ne = N_EXPERTS

    @pl.when(pl.program_id(0) == 0)
    def _():
        ycat_s[...] = jnp.zeros_like(ycat_s)

    b = pl.program_id(0) + block0

    copies = []
    offs = []
    off = 0
    for e in range(ne):
        cnt = cnt_ref[b * ne + e]
        rows = ((cnt + SEG_ALIGN - 1) // SEG_ALIGN) * SEG_ALIGN
        copies += _segment_copies(rows, ys_ref, ycat_s, base_ref[b * ne + e], off, sems, e)
        offs.append(off)
        off = off + rows
    _run_copies(copies)

    pt = pt_ref[...]
    lane = lax.broadcasted_iota(jnp.int32, (1, LANE), 1)
    off_lane = jnp.zeros((1, LANE), F32)
    for e in range(1, ne):
        off_lane = jnp.where(lane == e, offs[e].astype(F32), off_lane)
    chosen = (pt >= 0.0) & (lax.broadcasted_iota(jnp.int32, (t, LANE), 1) < ne)
    slot = pt + off_lane
    s_lo = jnp.min(jnp.where(chosen, slot, float(cap)), axis=1, keepdims=True)
    s_hi = jnp.max(jnp.where(chosen, slot, -1.0), axis=1, keepdims=True)
    n_piece = 3
    wp = cap // n_piece
    f = None
    for j in range(n_piece):
        col = (lax.broadcasted_iota(jnp.int32, (t, wp), 1) + j * wp).astype(F32)
        sel = jnp.where(col == s_lo, 1.0, jnp.where(col == s_hi, 1.0, 0.0)).astype(BF16)
        part = _dot(sel, ycat_s[j * wp:(j + 1) * wp, :])
        f = part if f is None else f + part
    o_ref[...] = _layer_norm(alpha * x_ref[...] + g_ref[0] * f, lg_ref[...], lb_ref[...])


def _moe_combine(ys, pt, x, g2, ln_g, ln_b, counts, base, rows, alpha, block0=0, n_blocks=None):
    n, d = x.shape
    t = T_MOE
    ne = N_EXPERTS
    n_blocks = n // t if n_blocks is None else n_blocks
    cap = -(-(2 * t + ne * (SEG_ALIGN - 1)) // (6 * LANE)) * (6 * LANE)
    tok = lambda b, c, o, r: (b + block0, 0)
    const = lambda b, c, o, r: (0, 0)
    return pl.pallas_call(
        functools.partial(_moe_combine_kernel, alpha=alpha, block0=block0),
        out_shape=jax.ShapeDtypeStruct((n_blocks * t, d), F32),
        grid_spec=pltpu.PrefetchScalarGridSpec(
            num_scalar_prefetch=3, grid=(n_blocks,),
            in_specs=[pl.BlockSpec(memory_space=pl.ANY), pl.BlockSpec((t, LANE), tok), pl.BlockSpec((t, d), tok),
                      pl.BlockSpec((1, 1, d), lambda b, c, o, r: (r[b + block0], 0, 0)),
                      pl.BlockSpec((1, d), const), pl.BlockSpec((1, d), const)],
            out_specs=pl.BlockSpec((t, d), lambda b, c, o, r: (b, 0)),
            scratch_shapes=[pltpu.VMEM((cap, d), BF16), pltpu.SemaphoreType.DMA((ne * len(SEG_SIZES),))]),
        compiler_params=_params(("arbitrary",), 48),
        name="moe_combine",
    )(counts, base, rows, ys, pt, x, g2, ln_g, ln_b)


def _moe_layout(cnt, n_tiles):
    cnt16 = (cnt + SEG_ALIGN - 1) // SEG_ALIGN * SEG_ALIGN
    rows_e = jnp.sum(cnt16, axis=0)
    rows_e = (rows_e + RT_GMM - 1) // RT_GMM * RT_GMM
    ends = jnp.cumsum(rows_e)
    base = (ends - rows_e)[None, :] + jnp.cumsum(cnt16, axis=0) - cnt16
    tile_end = ends // RT_GMM
    idx = jnp.arange(n_tiles, dtype=jnp.int32)
    valid = idx < tile_end[-1]
    te = jnp.sum((idx[:, None] >= tile_end[None, :]).astype(jnp.int32), axis=1)
    last = jnp.sum(((tile_end[-1] - 1) >= tile_end).astype(jnp.int32))
    te = jnp.where(valid, te, last)
    return base.reshape(-1).astype(jnp.int32), te.astype(jnp.int32), valid.astype(jnp.int32)


def _tile_rows(n, p, ss, tm):
    starts = np.arange(n // tm) * tm
    return jnp.asarray(np.where(starts < p, 0, 1 + (starts - p) // ss).astype(np.int32))


def _tile_rope_blocks(n, p, ss, tm):
    starts = np.arange(n // tm) * tm
    return jnp.asarray(np.where(starts < p, 0, 1 + ((starts - p) % ss) // tm).astype(np.int32))


def _tile_edges(n, p, sp, ss, tm):
    starts = np.arange(n // tm) * tm
    pos = np.where(starts < p, starts % sp, (starts - p) % ss)
    seq = np.where(starts < p, sp, ss)
    return jnp.asarray((pos == 0).astype(np.int32)), jnp.asarray(((pos + tm) % seq == 0).astype(np.int32))


def _rope_tables(n_tokens, rot_dim, lane0, tm):
    rows = n_tokens // GRID_W
    row_pos = jnp.repeat(jnp.arange(rows, dtype=F32), GRID_W)
    col_pos = jnp.tile(jnp.arange(GRID_W, dtype=F32), rows)
    half = rot_dim // 2
    qtr = half // 2
    inv_freq = ROPE_THETA ** (-jnp.arange(0, half, 2, dtype=F32) / half)
    ang_r = row_pos[:, None] * inv_freq
    ang_c = col_pos[:, None] * inv_freq
    ang = jnp.concatenate([ang_r, ang_r, ang_c, ang_c], axis=-1)
    cos, sin = jnp.cos(ang), jnp.sin(ang)
    first = ((np.arange(rot_dim) % half) < qtr).astype(np.float32)
    sa = -sin * first
    sb = sin * (1.0 - first)

    def place(tbl, fill):
        full = jnp.full((n_tokens, LANE), fill, F32).at[:, lane0:lane0 + rot_dim].set(tbl)
        return jnp.concatenate([jnp.full((tm, LANE), fill, F32), full], axis=0)

    return place(cos, 1.0), place(sa, 0.0), place(sb, 0.0)


def _pad_heads(w, n_heads, width):
    lead = w.shape[:-1]
    w = w.reshape(lead + (n_heads, width))
    w = jnp.pad(w, [(0, 0)] * len(lead) + [(0, 0), (0, LANE - width)])
    return w.reshape(lead + (n_heads * LANE,))


def _pad_head_rows(w, n_heads, width):
    d = w.shape[-1]
    w = w.reshape(n_heads, width, d)
    return jnp.pad(w, ((0, 0), (0, LANE - width), (0, 0))).reshape(n_heads * LANE, d)


def kernel(x_prompt, x_sample, cache_attn_k, cache_attn_v, cache_mla_ckv, cache_mla_kpe, state_ssm_re, state_ssm_im, c, c_ctx, ada_w, ada_b, ln_g, ln_b, ev_w_in, ev_conv_w, ev_q_gain, ev_k_gain, ev_w_out, ffn_w_gate, ffn_w_up, ffn_w_down, od_w_in, s5_a_re, s5_a_im, s5_log_dt, s5_b_re, s5_b_im, s5_c_re, s5_c_im, s5_d, s5_w_glu, s5_b_glu, mla_q_gain, mla_w_uq, mla_kv_gain, mla_w_ukv, od_w_out, moe_router, moe_w_gate, moe_w_up, moe_w_down):
    bp, sp, d = x_prompt.shape
    bs, ss, _ = x_sample.shape
    depth = ada_w.shape[0]
    alpha = (2 * depth) ** 0.25
    p = bp * sp
    n = p + bs * ss
    past = cache_attn_k.shape[2]
    for tm in (TM_PROJ, TM_EVEN_IN, TM_FFN, T_MOE):
        assert p % tm == 0 and ss % tm == 0 and (sp % tm == 0 or tm % sp == 0)
    assert sp % S5_CHUNK == 0 and ss % S5_CHUNK == 0

    x = (x_prompt.reshape(p, d), x_sample.reshape(bs * ss, d))

    nrow = -(-(1 + bs) // SUBLANE) * SUBLANE
    cond = jnp.zeros((nrow, d), F32).at[0].set(c_ctx).at[1:1 + bs].set(c)
    mod = _modulation(cond, ada_w, ada_b)

    def mod_part(l, k):
        return mod[l, :, k * d:(k + 1) * d].reshape(nrow, 1, d)

    rows_proj = _tile_rows(n, p, ss, TM_PROJ)
    rows_ffn = _tile_rows(n, p, ss, TM_FFN)
    rows_moe = _tile_rows(n, p, ss, T_MOE)
    rblk = _tile_rope_blocks(n, p, ss, TM_PROJ)
    first, last = _tile_edges(n, p, sp, ss, TM_PROJ)
    rows_even = _tile_rows(n, p, ss, TM_EVEN_IN)
    rblk_even = _tile_rope_blocks(n, p, ss, TM_EVEN_IN)
    tables_even = _rope_tables(ss, HEAD_DIM, 0, TM_EVEN_IN)
    tables_mla = _rope_tables(ss, MLA_ROPE, MLA_NOPE, TM_PROJ)

    s5_mats = jax.vmap(_s5_matrices)(s5_a_re, s5_a_im, s5_log_dt, s5_b_re, s5_b_im, s5_c_re, s5_c_im)
    ffn_wg, ffn_wu, ffn_wd = ffn_w_gate.astype(BF16), ffn_w_up.astype(BF16), ffn_w_down.astype(BF16)
    out_k, out_v, out_ckv, out_kpe, out_sre, out_sim = [], [], [], [], [], []
    cw = CONV_WIDTH
    hq = ATTN_HEADS * HEAD_DIM
    hkv = ATTN_KV_HEADS * HEAD_DIM
    for l in range(depth):
        i = l // 2
        sh1, sc1, g1, sh2, sc2, g2 = [mod_part(l, k) for k in range(6)]
        lg = ln_g[l].reshape(2, 1, d)
        lb = ln_b[l].reshape(2, 1, d)
        if l % 2 == 0:
            w_in = ev_w_in[i]
            wc = w_in[:, :3 * cw].astype(BF16)
            wq = _pad_heads(w_in[:, 3 * cw:3 * cw + hq], ATTN_HEADS, HEAD_DIM).astype(BF16)
            wkv = jnp.concatenate([_pad_heads(w_in[:, 3 * cw + hq:3 * cw + hq + hkv], ATTN_KV_HEADS, HEAD_DIM),
                                   _pad_heads(w_in[:, 3 * cw + hq + hkv:], ATTN_KV_HEADS, HEAD_DIM)],
                                  axis=1).astype(BF16)
            qg = jnp.pad(ev_q_gain[i], (0, LANE - HEAD_DIM)).reshape(1, LANE)
            kg = jnp.pad(ev_k_gain[i], (0, LANE - HEAD_DIM)).reshape(1, LANE)
            conv, q, k, v, kn, vf = _even_in(x, sc1, sh1, wc, wq, wkv, qg, kg, tables_even, rows_even, rblk_even)
            kvw = ATTN_KV_HEADS * LANE
            out_k.append(kn[:p].reshape(bp, sp, ATTN_KV_HEADS, LANE)[..., :HEAD_DIM])
            out_v.append(vf[:p].reshape(bp, sp, ATTN_KV_HEADS, LANE)[..., :HEAD_DIM])
            group = ATTN_HEADS // ATTN_KV_HEADS
            qw = ATTN_HEADS * LANE
            a_p = _attention(q, [(k, v, True)], ATTN_HEADS, group, bp, sp, 0)
            kc = _pad_heads(cache_attn_k[:, i].reshape(bs, past, hkv), ATTN_KV_HEADS, HEAD_DIM).astype(BF16)
            vc = _pad_heads(cache_attn_v[:, i].reshape(bs, past, hkv), ATTN_KV_HEADS, HEAD_DIM).astype(BF16)
            a_s = _attention(q, [(kc, vc, False), (k, v, True)], ATTN_HEADS, group, bs, ss, p)
            w_out = jnp.concatenate([ev_w_out[i][:cw], _pad_head_rows(ev_w_out[i][cw:], ATTN_HEADS, HEAD_DIM)],
                                    axis=0).astype(BF16)
            conv_w = jnp.pad(ev_conv_w[i], ((0, SUBLANE - ev_conv_w.shape[1]), (0, 0)))
            x = _even_out(conv, a_p.reshape(p, qw), a_s.reshape(bs * ss, qw), x, g1, conv_w, w_out, lg[0], lb[0],
                          first, last, rows_proj, alpha, sp)
            x = _ffn(x, sc2, sh2, g2, ffn_wg, ffn_wu, ffn_wd, i, lg[1], lb[1], rows_ffn, alpha)
        else:
            q_end = S5_WIDTH + MLA_Q_RANK
            kv_end = q_end + MLA_KV_RANK
            w_in = od_w_in[i]
            w_in = jnp.concatenate([w_in[:, :kv_end], jnp.zeros((d, MLA_NOPE), F32), w_in[:, kv_end:],
                                    jnp.zeros((d, LANE - MLA_NOPE - MLA_ROPE), F32)], axis=1).astype(BF16)
            m_ctx, m_lat = p // S5_CHUNK, bs * ss // S5_CHUNK
            ctx_chunk0, lat_chunk0 = (m_lat, 0) if m_lat >= m_ctx else (0, m_ctx)
            assert ctx_chunk0 % m_ctx == 0 and lat_chunk0 % m_lat == 0
            proj, u = _odd_in(x, sc1, sh1, w_in, rows_proj, p // TM_PROJ, ctx_chunk0, lat_chunk0)
            out_kpe.append(proj[:p, kv_end + MLA_NOPE:kv_end + MLA_NOPE + MLA_ROPE].reshape(bp, sp, MLA_ROPE))

            h0_p = jnp.zeros((S5_GROUPS, bp, 4 * S5_STATE), F32)
            y_p, fin_p = _s5(u, s5_mats, i, h0_p, bp, sp // S5_CHUNK, ctx_chunk0 // m_ctx)

            def pack_state(re, im):
                to = lambda a: a.transpose(2, 0, 1, 3).reshape(S5_GROUPS, a.shape[0], 2 * S5_STATE)
                return jnp.concatenate([to(re), to(im)], axis=-1)

            y_s, _ = _s5(u, s5_mats, i, pack_state(state_ssm_re[:, i], state_ssm_im[:, i]), bs, ss // S5_CHUNK,
                         lat_chunk0 // m_lat)
            fin = fin_p.reshape(S5_GROUPS, bp, 2, 2, S5_STATE)
            out_sre.append(fin[:, :, 0].transpose(1, 2, 0, 3))
            out_sim.append(fin[:, :, 1].transpose(1, 2, 0, 3))

            qk = MLA_NOPE + MLA_ROPE
            wuq = _pad_heads(mla_w_uq[i], MLA_HEADS, qk).astype(BF16)
            wukv = mla_w_ukv[i].reshape(MLA_KV_RANK, MLA_HEADS, MLA_NOPE + MLA_V)
            wk = _pad_heads(wukv[..., :MLA_NOPE].reshape(MLA_KV_RANK, -1), MLA_HEADS, MLA_NOPE).astype(BF16)
            wv = _pad_heads(wukv[..., MLA_NOPE:].reshape(MLA_KV_RANK, -1), MLA_HEADS, MLA_V).astype(BF16)
            place = np.zeros((LANE, MLA_HEADS * LANE), np.float32)
            for hd in range(MLA_HEADS):
                for j in range(MLA_ROPE):
                    place[MLA_NOPE + j, hd * LANE + MLA_NOPE + j] = 1.0
            place = jnp.asarray(place, BF16)
            q, k, v, ckv = _mla_prep(proj, mla_q_gain[i].reshape(1, -1), mla_kv_gain[i].reshape(1, -1),
                                     wuq, wk, wv, place, tables_mla, rblk)
            out_ckv.append(ckv[:p].reshape(bp, sp, MLA_KV_RANK))
            hw = MLA_HEADS * LANE
            a_p = _attention(q, [(k, v, True)], MLA_HEADS, 1, bp, sp, 0)
            kpe_c = jnp.pad(cache_mla_kpe[:, i].reshape(bs * past, MLA_ROPE),
                            ((0, 0), (MLA_NOPE, LANE - MLA_NOPE - MLA_ROPE)))
            kc, vc = _mla_cache(cache_mla_ckv[:, i].reshape(bs * past, MLA_KV_RANK), kpe_c, wk, wv, place)
            a_s = _attention(q, [(kc.reshape(bs, past, hw), vc.reshape(bs, past, hw), False), (k, v, True)],
                             MLA_HEADS, 1, bs, ss, p)
            w_out = jnp.concatenate([od_w_out[i][:S5_WIDTH], _pad_head_rows(od_w_out[i][S5_WIDTH:], MLA_HEADS, MLA_V)],
                                    axis=0).astype(BF16)
            x = _odd_out(proj, y_p, y_s, a_p.reshape(p, hw), a_s.reshape(bs * ss, hw), x, g1, s5_d[i].reshape(1, -1),
                         s5_w_glu[i].astype(BF16), s5_b_glu[i].reshape(1, -1), w_out, lg[0], lb[0], rows_proj, alpha)

            hb, pos, gate, pt, cnt = _router(x, sc2, sh2, moe_router[i].T, rows_moe)
            cnt = cnt[:, :, 0]
            nblk = n // T_MOE
            n_tiles = -(-(2 * n + nblk * N_EXPERTS * (SEG_ALIGN - 1)) // RT_GMM) + N_EXPERTS
            base, tile_expert, tile_valid = _moe_layout(cnt, n_tiles)
            counts = cnt.reshape(-1)
            xs, gs = _moe_gather(hb, pos, gate, counts, base, n_tiles * RT_GMM)
            ys = _moe_gmm(xs, gs, tile_expert, tile_valid, moe_w_gate, moe_w_up, moe_w_down, i)
            if l == depth - 1:
                nb_ctx = p // T_MOE
                x = tuple(_moe_combine(ys, pt, x, g2, lg[1], lb[1], counts, base, rows_moe, alpha, b0, nbk)
                          for b0, nbk in ((0, nb_ctx), (nb_ctx, nblk - nb_ctx)))
            else:
                x = _moe_combine(ys, pt, x, g2, lg[1], lb[1], counts, base, rows_moe, alpha)

    if not isinstance(x, tuple):
        x = (x[:p], x[p:])
    y_prompt = x[0].reshape(bp, sp, d)
    y_sample = x[1].reshape(bs, ss, d)
    return (y_prompt, y_sample, jnp.stack(out_k, axis=1), jnp.stack(out_v, axis=1),
            jnp.stack(out_ckv, axis=1), jnp.stack(out_kpe, axis=1),
            jnp.stack(out_sre, axis=1), jnp.stack(out_sim, axis=1))
```

```python
import functools
import math

import jax
import jax.numpy as jnp
import numpy as np
from jax import lax
from jax.experimental import pallas as pl
from jax.experimental.pallas import tpu as pltpu

F32 = jnp.float32
BF16 = jnp.bfloat16

LANE = 128
SUBLANE = 8
MIB = 1024 * 1024

GRID_W = 64
ROPE_THETA = 10000.0
LN_EPS = 1e-6
RMS_EPS = 1e-6
HEAD_DIM = 64
ATTN_HEADS = 8
ATTN_KV_HEADS = 2
CONV_WIDTH = 512
S5_WIDTH = 512
S5_GROUP = 16
S5_GROUPS = 32
S5_STATE = 64
S5_CHUNK = 16
MLA_HEADS = 8
MLA_Q_RANK = 256
MLA_KV_RANK = 128
MLA_NOPE = 64
MLA_ROPE = 32
MLA_V = 64
N_EXPERTS = 8

TM_PROJ = 512
TM_EVEN_IN = 256
TM_FFN = 512
FFN_CHUNK = 2816
T_MOE = 1024
ROW_MOE = 128
RT_GMM = 512
GMM_STAGE_PIECES = 16
GMM_STAGE_SLOTS = 6
SEG_ALIGN = 16
SEG_SIZES = tuple(T_MOE >> k for k in range(T_MOE.bit_length()) if (T_MOE >> k) >= SEG_ALIGN)
TQ_ATTN = 512


def _params(sem, vmem_mib):
    return pltpu.CompilerParams(dimension_semantics=sem, vmem_limit_bytes=vmem_mib * MIB)


def _dot(a, b):
    return jnp.dot(a, b, preferred_element_type=F32)


def _dot_nt(a, b):
    return lax.dot_general(a, b, (((1,), (1,)), ((), ())), preferred_element_type=F32)


def _split(a):
    hi = a.astype(BF16)
    lo = (a - hi.astype(F32)).astype(BF16)
    return hi, lo


def _silu(x):
    return x * jax.nn.sigmoid(x)


def _layer_norm(r, g, b):
    mu = jnp.mean(r, axis=-1, keepdims=True)
    d = r - mu
    var = jnp.mean(d * d, axis=-1, keepdims=True)
    return d * lax.rsqrt(var + LN_EPS) * g + b


def _rms(x, g, n):
    ms = jnp.sum(x * x, axis=-1, keepdims=True) * (1.0 / n)
    return x * lax.rsqrt(ms + RMS_EPS) * g


def _rope(x, cos, sa, sb, q):
    w = x.shape[-1]
    return x * cos + pltpu.roll(x, w - q, 1) * sa + pltpu.roll(x, q, 1) * sb


def _ada_kernel(c_ref, w_ref, b_ref, o_ref):
    c = c_ref[...]
    a_hi, a_lo = _split(_silu(c))
    w_hi, w_lo = _split(w_ref[0])
    o_ref[0] = _dot(a_hi, w_hi) + _dot(a_lo, w_hi) + _dot(a_hi, w_lo) + b_ref[0]


def _modulation(cond, ada_w, ada_b):
    depth, d, d6 = ada_w.shape
    r = cond.shape[0]
    tn = 1536
    return pl.pallas_call(
        _ada_kernel,
        out_shape=jax.ShapeDtypeStruct((depth, r, d6), F32),
        grid=(depth, d6 // tn),
        in_specs=[pl.BlockSpec((r, d), lambda l, j: (0, 0)),
                  pl.BlockSpec((1, d, tn), lambda l, j: (l, 0, j)),
                  pl.BlockSpec((1, 1, tn), lambda l, j: (l, 0, j))],
        out_specs=pl.BlockSpec((1, r, tn), lambda l, j: (l, 0, j)),
        compiler_params=_params(("arbitrary", "arbitrary"), 40),
        name="ada_modulation",
    )(cond, ada_w, ada_b.reshape(depth, 1, d6))


def _load_x(x_refs, n_ctx_tiles):
    if len(x_refs) == 1:
        return x_refs[0][...]
    return _pick_pass(pl.program_id(0), n_ctx_tiles, *x_refs)


def _x_specs(x, tm, tok):
    if not isinstance(x, tuple):
        return (x,), [pl.BlockSpec((tm, x.shape[1]), tok)], 0
    n_ctx, n_lat = x[0].shape[0] // tm, x[1].shape[0] // tm
    return x, list(_pass_specs((tm, x[0].shape[1]), n_ctx, n_lat)), n_ctx


def _even_in_kernel(rows_ref, rblk_ref, *refs, n_x, n_ctx_tiles):
    (sc_ref, sh_ref, wc_ref, wq_ref, wkv_ref, qg_ref, kg_ref, cos_ref, sa_ref, sb_ref,
     conv_ref, q_ref, k_ref, v_ref, kn_ref, vf_ref) = refs[n_x:]
    del rows_ref, rblk_ref
    h = (_load_x(refs[:n_x], n_ctx_tiles) * (1.0 + sc_ref[0]) + sh_ref[0]).astype(BF16)
    conv_ref[...] = _dot(h, wc_ref[...])
    cos, sa, sb = cos_ref[...], sa_ref[...], sb_ref[...]
    quarter = HEAD_DIM // 4
    q = _dot(h, wq_ref[...])
    for hd in range(ATTN_HEADS):
        sl = slice(hd * LANE, (hd + 1) * LANE)
        qn = _rms(q[:, sl], qg_ref[...], HEAD_DIM)
        q_ref[:, sl] = (_rope(qn, cos, sa, sb, quarter) * (HEAD_DIM ** -0.5)).astype(BF16)
    kv = _dot(h, wkv_ref[...])
    kw = ATTN_KV_HEADS * LANE
    for hd in range(ATTN_KV_HEADS):
        sl = slice(hd * LANE, (hd + 1) * LANE)
        kn = _rms(kv[:, sl], kg_ref[...], HEAD_DIM)
        kn_ref[:, sl] = kn
        k_ref[:, sl] = _rope(kn, cos, sa, sb, quarter).astype(BF16)
    v = kv[:, kw:]
    vf_ref[...] = v
    v_ref[...] = v.astype(BF16)


def _even_in(x, sc, sh, wc, wq, wkv, qg, kg, tables, rows, rblk):
    tm = TM_EVEN_IN
    d = wc.shape[0]
    n = rows.shape[0] * tm
    cw, qw, kvw = wc.shape[1], wq.shape[1], wkv.shape[1]
    kw = kvw // 2
    cos, sa, sb = tables
    row3 = lambda i, rows, rblk: (rows[i], 0, 0)
    tok = lambda i, rows, rblk: (i, 0)
    const = lambda i, rows, rblk: (0, 0)
    tab = lambda i, rows, rblk: (rblk[i], 0)
    xs, x_specs, n_ctx = _x_specs(x, tm, tok)
    return pl.pallas_call(
        functools.partial(_even_in_kernel, n_x=len(xs), n_ctx_tiles=n_ctx),
        out_shape=(jax.ShapeDtypeStruct((n, cw), F32), jax.ShapeDtypeStruct((n, qw), BF16),
                   jax.ShapeDtypeStruct((n, kw), BF16), jax.ShapeDtypeStruct((n, kw), BF16),
                   jax.ShapeDtypeStruct((n, kw), F32), jax.ShapeDtypeStruct((n, kw), F32)),
        grid_spec=pltpu.PrefetchScalarGridSpec(
            num_scalar_prefetch=2, grid=(n // tm,),
            in_specs=x_specs + [
                      pl.BlockSpec((1, 1, d), row3), pl.BlockSpec((1, 1, d), row3),
                      pl.BlockSpec((d, cw), const), pl.BlockSpec((d, qw), const), pl.BlockSpec((d, kvw), const),
                      pl.BlockSpec((1, LANE), const), pl.BlockSpec((1, LANE), const),
                      pl.BlockSpec((tm, LANE), tab), pl.BlockSpec((tm, LANE), tab), pl.BlockSpec((tm, LANE), tab)],
            out_specs=[pl.BlockSpec((tm, cw), tok), pl.BlockSpec((tm, qw), tok),
                       pl.BlockSpec((tm, kw), tok), pl.BlockSpec((tm, kw), tok),
                       pl.BlockSpec((tm, kw), tok), pl.BlockSpec((tm, kw), tok)]),
        compiler_params=_params(("arbitrary",), 40),
        name="even_in_proj",
    )(rows, rblk, *xs, sc, sh, wc, wq, wkv, qg, kg, cos, sa, sb)


def _attn_kernel(*refs, n_heads, group, n_seg):
    q_ref = refs[0]
    seg = refs[1:1 + 2 * n_seg]
    o_ref = refs[1 + 2 * n_seg]
    for hd in range(n_heads):
        sl = slice(hd * LANE, (hd + 1) * LANE)
        ks = slice((hd // group) * LANE, (hd // group + 1) * LANE)
        qh = q_ref[:, sl]
        scores = [_dot_nt(qh, seg[2 * s][:, ks]) for s in range(n_seg)]
        m = jnp.max(scores[0], axis=-1, keepdims=True)
        for s in range(1, n_seg):
            m = jnp.maximum(m, jnp.max(scores[s], axis=-1, keepdims=True))
        den = None
        acc = None
        for s in range(n_seg):
            p = jnp.exp(scores[s] - m)
            ps = jnp.sum(p, axis=-1, keepdims=True)
            pv = _dot(p.astype(BF16), seg[2 * s + 1][:, ks])
            den = ps if den is None else den + ps
            acc = pv if acc is None else acc + pv
        o_ref[:, sl] = (acc * (1.0 / den)).astype(BF16)


def _attention(q, segments, n_heads, group, b, s, row0):
    n, qw = q.shape
    assert n % s == 0 and row0 % s == 0
    b0 = row0 // s
    tq = min(TQ_ATTN, s)
    in_specs = [pl.BlockSpec((None, tq, qw), lambda i, j: (i + b0, j, 0))]
    args = [q.reshape(n // s, s, qw)]
    for k, v, own in segments:
        kw = k.shape[-1]
        if own:
            k, v = k.reshape(n // s, s, kw), v.reshape(n // s, s, kw)
            kmap = lambda i, j: (i + b0, 0, 0)
        else:
            kmap = lambda i, j: (i, 0, 0)
        t = k.shape[1]
        in_specs += [pl.BlockSpec((None, t, kw), kmap, pipeline_mode=pl.Buffered(1)),
                     pl.BlockSpec((None, t, kw), kmap, pipeline_mode=pl.Buffered(1))]
        args += [k, v]
    return pl.pallas_call(
        functools.partial(_attn_kernel, n_heads=n_heads, group=group, n_seg=len(segments)),
        out_shape=jax.ShapeDtypeStruct((b, s, qw), BF16),
        grid=(b, s // tq),
        in_specs=in_specs,
        out_specs=pl.BlockSpec((None, tq, qw), lambda i, j: (i, j, 0)),
        compiler_params=_params(("arbitrary", "arbitrary"), 56),
        name="attention",
    )(*args)


def _pick_pass(i, n_ctx_tiles, ctx_ref, lat_ref):
    return jnp.where(i < n_ctx_tiles, ctx_ref[...], lat_ref[...])


def _pass_specs(block, n_ctx_tiles, n_lat_tiles):
    lead = (0,) * (len(block) - 2)

    def ctx(i, *_):
        return lead + (jnp.minimum(i, n_ctx_tiles - 1), 0)

    def lat(i, *_):
        return lead + (jnp.clip(i - n_ctx_tiles, 0, n_lat_tiles - 1), 0)

    return pl.BlockSpec(block, ctx), pl.BlockSpec(block, lat)


def _even_out_kernel(first_ref, last_ref, rows_ref, c_ref, cp_ref, cn_ref, ap_ref, as_ref, *refs,
                     alpha, n_ctx_tiles, ctx_seq, n_x):
    g_ref, cw_ref, w_ref, lg_ref, lb_ref, o_ref = refs[n_x:]
    del rows_ref
    i = pl.program_id(0)
    attn = _pick_pass(i, n_ctx_tiles, ap_ref, as_ref)
    tm = c_ref.shape[0]
    cwid = CONV_WIDTH
    c = c_ref[...]
    gate_b, z = c[:, :cwid], c[:, cwid:2 * cwid] * c[:, 2 * cwid:]
    cp = cp_ref[...]
    cn = cn_ref[...]
    zp = cp[SUBLANE - 1:SUBLANE, cwid:2 * cwid] * cp[SUBLANE - 1:SUBLANE, 2 * cwid:]
    zn = cn[0:1, cwid:2 * cwid] * cn[0:1, 2 * cwid:]
    zp = zp * (1 - first_ref[i]).astype(F32)
    zn = zn * (1 - last_ref[i]).astype(F32)
    row = lax.broadcasted_iota(jnp.int32, (tm, cwid), 0)
    z_prev = jnp.where(row == 0, zp, pltpu.roll(z, 1, 0))
    z_next = jnp.where(row == tm - 1, zn, pltpu.roll(z, tm - 1, 0))
    for k in range(1, tm // ctx_seq):
        edge = jnp.where(i < n_ctx_tiles, k * ctx_seq, -1)
        z_prev = jnp.where(row == edge, 0.0, z_prev)
        z_next = jnp.where(row == edge - 1, 0.0, z_next)
    cw = cw_ref[...]
    y = gate_b * (cw[0:1] * z_prev + cw[1:2] * z + cw[2:3] * z_next)
    out = _dot(y.astype(BF16), w_ref[:cwid, :]) + _dot(attn, w_ref[cwid:, :])
    x = _load_x(refs[:n_x], n_ctx_tiles)
    o_ref[...] = _layer_norm(alpha * x + g_ref[0] * out, lg_ref[...], lb_ref[...])


def _even_out(conv, attn_p, attn_s, x, g1, conv_w, w_out, ln_g, ln_b, first, last, rows, alpha, ctx_seq):
    tm = TM_PROJ
    n, d = conv.shape[0], w_out.shape[1]
    cw3 = conv.shape[1]
    aw = attn_p.shape[1]
    hb = tm // SUBLANE
    nblk8 = n // SUBLANE
    n_ctx, n_lat = attn_p.shape[0] // tm, attn_s.shape[0] // tm
    ap_spec, as_spec = _pass_specs((tm, aw), n_ctx, n_lat)
    tok = lambda i, f, l, r: (i, 0)
    const = lambda i, f, l, r: (0, 0)
    xs, x_specs, _ = _x_specs(x, tm, tok)
    return pl.pallas_call(
        functools.partial(_even_out_kernel, alpha=alpha, n_ctx_tiles=n_ctx, ctx_seq=ctx_seq, n_x=len(xs)),
        out_shape=jax.ShapeDtypeStruct((n, d), F32),
        grid_spec=pltpu.PrefetchScalarGridSpec(
            num_scalar_prefetch=3, grid=(n // tm,),
            in_specs=[pl.BlockSpec((tm, cw3), tok),
                      pl.BlockSpec((SUBLANE, cw3), lambda i, f, l, r: (jnp.maximum(i * hb - 1, 0), 0)),
                      pl.BlockSpec((SUBLANE, cw3), lambda i, f, l, r: (jnp.minimum((i + 1) * hb, nblk8 - 1), 0)),
                      ap_spec, as_spec] + x_specs + [
                      pl.BlockSpec((1, 1, d), lambda i, f, l, r: (r[i], 0, 0)),
                      pl.BlockSpec((SUBLANE, CONV_WIDTH), const),
                      pl.BlockSpec((CONV_WIDTH + aw, d), const),
                      pl.BlockSpec((1, d), const), pl.BlockSpec((1, d), const)],
            out_specs=pl.BlockSpec((tm, d), tok)),
        compiler_params=_params(("arbitrary",), 40),
        name="even_out_proj",
    )(first, last, rows, conv, conv, conv, attn_p, attn_s, *xs, g1, conv_w, w_out, ln_g, ln_b)


def _ffn_kernel(rows_ref, x_ref, sc_ref, sh_ref, g_ref, wg_ref, wu_ref, wd_ref, lg_ref, lb_ref, o_ref,
                h_s, acc_s, *, alpha):
    del rows_ref
    f = pl.program_id(1)

    @pl.when(f == 0)
    def _():
        h_s[...] = (x_ref[...] * (1.0 + sc_ref[0]) + sh_ref[0]).astype(BF16)

    h = h_s[...]
    a = (_silu(_dot(h, wg_ref[...])) * _dot(h, wu_ref[...])).astype(BF16)
    y = _dot(a, wd_ref[...])

    @pl.when(f == 0)
    def _():
        acc_s[...] = y

    @pl.when(f > 0)
    def _():
        acc_s[...] += y

    @pl.when(f == pl.num_programs(1) - 1)
    def _():
        o_ref[...] = _layer_norm(alpha * x_ref[...] + g_ref[0] * acc_s[...], lg_ref[...], lb_ref[...])


def _ffn(x, sc, sh, g2, wg, wu, wd, layer, ln_g, ln_b, rows, alpha):
    n, d = x.shape
    tm = TM_FFN
    ff = wg.shape[-1]
    fc = FFN_CHUNK
    tok = lambda i, f, r: (i, 0)
    row3 = lambda i, f, r: (r[i], 0, 0)
    const = lambda i, f, r: (0, 0)
    wmode = dict(pipeline_mode=pl.Buffered(1)) if fc == ff else {}
    return pl.pallas_call(
        functools.partial(_ffn_kernel, alpha=alpha),
        out_shape=jax.ShapeDtypeStruct((n, d), F32),
        grid_spec=pltpu.PrefetchScalarGridSpec(
            num_scalar_prefetch=1, grid=(n // tm, ff // fc),
            in_specs=[pl.BlockSpec((tm, d), tok),
                      pl.BlockSpec((1, 1, d), row3), pl.BlockSpec((1, 1, d), row3), pl.BlockSpec((1, 1, d), row3),
                      pl.BlockSpec((None, d, fc), lambda i, f, r: (layer, 0, f), **wmode),
                      pl.BlockSpec((None, d, fc), lambda i, f, r: (layer, 0, f), **wmode),
                      pl.BlockSpec((None, fc, d), lambda i, f, r: (layer, f, 0), **wmode),
                      pl.BlockSpec((1, d), const), pl.BlockSpec((1, d), const)],
            out_specs=pl.BlockSpec((tm, d), tok),
            scratch_shapes=[pltpu.VMEM((tm, d), BF16), pltpu.VMEM((tm, d), F32)]),
        compiler_params=_params(("arbitrary", "arbitrary"), 56),
        name="ffn_swiglu",
    )(rows, x, sc, sh, g2, wg, wu, wd, ln_g, ln_b)


def _odd_in_kernel(rows_ref, x_ref, sc_ref, sh_ref, w_ref, o_ref, u_ref, u_s):
    del rows_ref
    h = (x_ref[...] * (1.0 + sc_ref[0]) + sh_ref[0]).astype(BF16)
    proj = _dot(h, w_ref[...])
    o_ref[...] = proj
    n_tile = S5_WIDTH // LANE
    gpt = LANE // S5_GROUP
    for j in range(n_tile):
        u_s[j] = proj[:, j * LANE:(j + 1) * LANE]
    n_chunk = u_s.shape[1] // S5_CHUNK
    steps = [[u_s[j, pl.ds(s, n_chunk, stride=S5_CHUNK), :] for j in range(n_tile)] for s in range(S5_CHUNK)]
    for g in range(S5_GROUPS):
        sl = slice((g % gpt) * S5_GROUP, (g % gpt + 1) * S5_GROUP)
        u_ref[g] = jnp.concatenate([st[g // gpt][:, sl] for st in steps], axis=1).astype(BF16)


def _odd_in(x, sc, sh, w, rows, n_ctx_tiles, ctx_chunk0, lat_chunk0):
    n, d = x.shape
    tm = TM_PROJ
    nw = w.shape[1]
    cpt = tm // S5_CHUNK
    u_rows = n // S5_CHUNK
    tok = lambda i, r: (i, 0)
    row3 = lambda i, r: (r[i], 0, 0)
    ublk = lambda i, r: (0, jnp.where(i < n_ctx_tiles, i + ctx_chunk0 // cpt, i - n_ctx_tiles + lat_chunk0 // cpt), 0)
    return pl.pallas_call(
        _odd_in_kernel,
        out_shape=(jax.ShapeDtypeStruct((n, nw), F32),
                   jax.ShapeDtypeStruct((S5_GROUPS, u_rows, S5_CHUNK * S5_GROUP), BF16)),
        grid_spec=pltpu.PrefetchScalarGridSpec(
            num_scalar_prefetch=1, grid=(n // tm,),
            in_specs=[pl.BlockSpec((tm, d), tok), pl.BlockSpec((1, 1, d), row3), pl.BlockSpec((1, 1, d), row3),
                      pl.BlockSpec((d, nw), lambda i, r: (0, 0))],
            out_specs=[pl.BlockSpec((tm, nw), tok), pl.BlockSpec((S5_GROUPS, cpt, S5_CHUNK * S5_GROUP), ublk)],
            scratch_shapes=[pltpu.VMEM((S5_WIDTH // LANE, tm, LANE), F32)]),
        compiler_params=_params(("arbitrary",), 32),
        name="odd_in_proj",
    )(rows, x, sc, sh, w)


def _s5_kernel(u_ref, zin_ref, t_ref, zout_ref, al_ref, h0_ref, y_ref, fin_ref, z_s, hf_s, hb_s, *, nb, nc):
    u = u_ref[0]
    z = _dot(u, zin_ref[0])
    for part in range(2):
        for b in range(nb):
            z_s[part, pl.ds(b, nc, stride=nb), :] = z[b * nc:(b + 1) * nc, part * LANE:(part + 1) * LANE]
    ar = al_ref[0, 0:1, :]
    ai = al_ref[0, 1:2, :]
    fwd = lax.broadcasted_iota(jnp.int32, (nb, LANE), 1) < S5_STATE
    aligned = (lambda r: pl.multiple_of(r, SUBLANE)) if nb % SUBLANE == 0 else (lambda r: r)

    def step(k, carry):
        re, im = carry
        rf = aligned(k * nb)
        rb = aligned((nc - 1 - k) * nb)
        hf_s[0, pl.ds(rf, nb), :] = re
        hf_s[1, pl.ds(rf, nb), :] = im
        hb_s[0, pl.ds(rb, nb), :] = re
        hb_s[1, pl.ds(rb, nb), :] = im
        zr = jnp.where(fwd, z_s[0, pl.ds(rf, nb), :], z_s[0, pl.ds(rb, nb), :])
        zi = jnp.where(fwd, z_s[1, pl.ds(rf, nb), :], z_s[1, pl.ds(rb, nb), :])
        return ar * re - ai * im + zr, ar * im + ai * re + zi

    h0 = h0_ref[0]
    re, im = lax.fori_loop(0, nc, step, (h0[:, :LANE], h0[:, LANE:]))
    fin_ref[0, :, :LANE] = re
    fin_ref[0, :, LANE:] = im
    m = u.shape[0]
    is_fwd = lax.broadcasted_iota(jnp.int32, (m, LANE), 1) < S5_STATE
    halves = []
    for part in range(2):
        hf_s[part] = jnp.where(is_fwd, hf_s[part], hb_s[part])
        halves.append(jnp.concatenate([hf_s[part, pl.ds(b, nc, stride=nb), :] for b in range(nb)], axis=0))
    h_in = jnp.concatenate(halves, axis=1).astype(BF16)
    y_ref[0] = _dot(u, t_ref[0]) + _dot(h_in, zout_ref[0])


def _s5(u, mats, layer, h0, nb, nc, row_block):
    zin, tmat, zout, al = mats
    g, _, w = u.shape
    m = nb * nc
    blk = lambda i: (i, 0, 0)
    lblk = lambda i: (layer, i, 0, 0)
    return pl.pallas_call(
        functools.partial(_s5_kernel, nb=nb, nc=nc),
        out_shape=(jax.ShapeDtypeStruct((g, m, w), F32), jax.ShapeDtypeStruct((g, nb, w), F32)),
        grid=(g,),
        in_specs=[pl.BlockSpec((1, m, w), lambda i: (i, row_block, 0)),
                  pl.BlockSpec((None, 1, w, w), lblk), pl.BlockSpec((None, 1, w, w), lblk),
                  pl.BlockSpec((None, 1, w, w), lblk), pl.BlockSpec((None, 1, 2, LANE), lblk),
                  pl.BlockSpec((1, nb, w), blk)],
        out_specs=(pl.BlockSpec((1, m, w), blk), pl.BlockSpec((1, nb, w), blk)),
        scratch_shapes=[pltpu.VMEM((2, m, LANE), F32), pltpu.VMEM((2, m, LANE), F32), pltpu.VMEM((2, m, LANE), F32)],
        compiler_params=_params(("arbitrary",), 32),
        name="s5_scan",
    )(u, zin, tmat, zout, al, h0)


def _s5_matrices(a_re, a_im, log_dt, b_re, b_im, c_re, c_im):
    hp = lax.Precision.HIGHEST
    L = S5_CHUNK
    dt = jnp.exp(log_dt)[..., None]
    lam_re, lam_im = a_re * dt, a_im * dt

    def power(k):
        k = k[:, None, None, None]
        mag = jnp.exp(lam_re * k)
        return mag * jnp.cos(lam_im * k), mag * jnp.sin(lam_im * k)

    ab_re, ab_im = power(jnp.ones((1,), F32))
    ab_re, ab_im = ab_re[0], ab_im[0]
    num_re, num_im = ab_re - 1.0, ab_im
    den = a_re * a_re + a_im * a_im
    f_re = (num_re * a_re + num_im * a_im) / den
    f_im = (num_im * a_re - num_re * a_im) / den
    bb_re = f_re[..., None] * b_re - f_im[..., None] * b_im
    bb_im = f_re[..., None] * b_im + f_im[..., None] * b_re

    ks = jnp.arange(L + 1, dtype=F32)
    pw_re, pw_im = power(ks)

    def zin_dir(d, exps):
        pr, pi = pw_re[exps, d], pw_im[exps, d]
        w_re = pr[..., None] * bb_re[d][None] - pi[..., None] * bb_im[d][None]
        w_im = pr[..., None] * bb_im[d][None] + pi[..., None] * bb_re[d][None]
        to = lambda w: jnp.transpose(w, (1, 0, 3, 2)).reshape(S5_GROUPS, L * S5_GROUP, S5_STATE)
        return to(w_re), to(w_im)

    steps = np.arange(L)
    zf_re, zf_im = zin_dir(0, L - 1 - steps)
    zb_re, zb_im = zin_dir(1, steps)
    zin = jnp.concatenate([zf_re, zb_re, zf_im, zb_im], axis=-1)

    def zout_dir(d, exps):
        pr, pi = pw_re[exps, d], pw_im[exps, d]
        cr, ci = c_re[d], c_im[d]
        e_re = cr[None] * pr[:, :, None, :] - ci[None] * pi[:, :, None, :]
        e_im = cr[None] * pi[:, :, None, :] + ci[None] * pr[:, :, None, :]
        to = lambda e: jnp.transpose(e, (1, 3, 0, 2)).reshape(S5_GROUPS, S5_STATE, L * S5_GROUP)
        return to(e_re), to(-e_im)

    of_re, of_im = zout_dir(0, steps + 1)
    ob_re, ob_im = zout_dir(1, L - steps)
    zout = jnp.concatenate([of_re, ob_re, of_im, ob_im], axis=1)

    def taps(d):
        pr, pi = pw_re[:L, d], pw_im[:L, d]
        m_re = pr[..., None] * bb_re[d][None] - pi[..., None] * bb_im[d][None]
        m_im = pr[..., None] * bb_im[d][None] + pi[..., None] * bb_re[d][None]
        return (jnp.einsum('gpn,lgnq->lgpq', c_re[d], m_re, precision=hp)
                - jnp.einsum('gpn,lgnq->lgpq', c_im[d], m_im, precision=hp))

    kf, kb = taps(0), taps(1)
    pad4 = ((0, 0),) * 3
    tm = jnp.stack([jnp.pad(kf[:L - s], ((s, 0),) + pad4) + jnp.pad(kb[:s + 1][::-1], ((0, L - 1 - s),) + pad4)
                    for s in range(L)])
    tmat = jnp.transpose(tm, (2, 0, 4, 1, 3)).reshape(S5_GROUPS, L * S5_GROUP, L * S5_GROUP)

    al = jnp.stack([jnp.concatenate([pw_re[L, 0], pw_re[L, 1]], axis=-1),
                    jnp.concatenate([pw_im[L, 0], pw_im[L, 1]], axis=-1)], axis=1)
    return zin.astype(BF16), tmat.astype(BF16), zout.astype(BF16), al


def _mla_kv(ckv, kpe, wk_ref, wv_ref, pl_ref, k_ref, v_ref):
    cb = ckv.astype(BF16)
    k_ref[...] = (_dot(cb, wk_ref[...]) + _dot(kpe.astype(BF16), pl_ref[...])).astype(BF16)
    v_ref[...] = _dot(cb, wv_ref[...]).astype(BF16)


def _mla_prep_kernel(rblk_ref, p_ref, qg_ref, kvg_ref, wuq_ref, wk_ref, wv_ref, pl_ref, cos_ref, sa_ref, sb_ref,
                     q_ref, k_ref, v_ref, ckv_ref):
    del rblk_ref
    pr = p_ref[...]
    cos, sa, sb = cos_ref[...], sa_ref[...], sb_ref[...]
    quarter = MLA_ROPE // 4
    scale = (MLA_NOPE + MLA_ROPE) ** -0.5
    qn = _rms(pr[:, :MLA_Q_RANK], qg_ref[...], MLA_Q_RANK).astype(BF16)
    q = _dot(qn, wuq_ref[...])
    for hd in range(MLA_HEADS):
        sl = slice(hd * LANE, (hd + 1) * LANE)
        q_ref[:, sl] = (_rope(q[:, sl], cos, sa, sb, quarter) * scale).astype(BF16)
    ckv = _rms(pr[:, MLA_Q_RANK:MLA_Q_RANK + MLA_KV_RANK], kvg_ref[...], MLA_KV_RANK)
    ckv_ref[...] = ckv
    kpe = _rope(pr[:, MLA_Q_RANK + MLA_KV_RANK:], cos, sa, sb, quarter)
    _mla_kv(ckv, kpe, wk_ref, wv_ref, pl_ref, k_ref, v_ref)


def _mla_prep(proj, qg, kvg, wuq, wk, wv, place, tables, rblk):
    n = proj.shape[0]
    tm = TM_PROJ
    hw = MLA_HEADS * LANE
    pw = MLA_Q_RANK + MLA_KV_RANK + LANE
    cos, sa, sb = tables
    tok = lambda i, r: (i, 0)
    const = lambda i, r: (0, 0)
    tab = lambda i, r: (r[i], 0)
    return pl.pallas_call(
        _mla_prep_kernel,
        out_shape=(jax.ShapeDtypeStruct((n, hw), BF16), jax.ShapeDtypeStruct((n, hw), BF16),
                   jax.ShapeDtypeStruct((n, hw), BF16), jax.ShapeDtypeStruct((n, MLA_KV_RANK), F32)),
        grid_spec=pltpu.PrefetchScalarGridSpec(
            num_scalar_prefetch=1, grid=(n // tm,),
            in_specs=[pl.BlockSpec((tm, pw), lambda i, r: (i, 1)),
                      pl.BlockSpec((1, MLA_Q_RANK), const), pl.BlockSpec((1, MLA_KV_RANK), const),
                      pl.BlockSpec((MLA_Q_RANK, hw), const), pl.BlockSpec((MLA_KV_RANK, hw), const),
                      pl.BlockSpec((MLA_KV_RANK, hw), const), pl.BlockSpec((LANE, hw), const),
                      pl.BlockSpec((tm, LANE), tab), pl.BlockSpec((tm, LANE), tab), pl.BlockSpec((tm, LANE), tab)],
            out_specs=[pl.BlockSpec((tm, hw), tok), pl.BlockSpec((tm, hw), tok), pl.BlockSpec((tm, hw), tok),
                       pl.BlockSpec((tm, MLA_KV_RANK), tok)]),
        compiler_params=_params(("arbitrary",), 32),
        name="mla_prep",
    )(rblk, proj, qg, kvg, wuq, wk, wv, place, cos, sa, sb)


def _mla_cache_kernel(c_ref, p_ref, wk_ref, wv_ref, pl_ref, k_ref, v_ref):
    _mla_kv(c_ref[...], p_ref[...], wk_ref, wv_ref, pl_ref, k_ref, v_ref)


def _mla_cache(ckv, kpe, wk, wv, place):
    n = ckv.shape[0]
    tm = min(TM_PROJ, n)
    assert n % tm == 0
    hw = MLA_HEADS * LANE
    tok = lambda i: (i, 0)
    const = lambda i: (0, 0)
    return pl.pallas_call(
        _mla_cache_kernel,
        out_shape=(jax.ShapeDtypeStruct((n, hw), BF16), jax.ShapeDtypeStruct((n, hw), BF16)),
        grid=(n // tm,),
        in_specs=[pl.BlockSpec((tm, MLA_KV_RANK), tok), pl.BlockSpec((tm, LANE), tok),
                  pl.BlockSpec((MLA_KV_RANK, hw), const), pl.BlockSpec((MLA_KV_RANK, hw), const),
                  pl.BlockSpec((LANE, hw), const)],
        out_specs=[pl.BlockSpec((tm, hw), tok), pl.BlockSpec((tm, hw), tok)],
        compiler_params=_params(("arbitrary",), 32),
        name="mla_cache_kv",
    )(ckv, kpe, wk, wv, place)


def _gelu_tanh(x):
    return 0.5 * x * (1.0 + jnp.tanh(math.sqrt(2.0 / math.pi) * (x + 0.044715 * (x * x * x))))


def _odd_out_kernel(rows_ref, u_ref, yp_ref, ys_ref, ap_ref, as_ref, x_ref, g_ref, d_ref, wglu_ref, bglu_ref, w_ref,
                    lg_ref, lb_ref, o_ref, y_s, *, alpha, n_ctx_tiles):
    del rows_ref
    i = pl.program_id(0)
    yg = _pick_pass(i, n_ctx_tiles, yp_ref, ys_ref)
    n_chunk = yg.shape[1]
    n_tile = S5_WIDTH // LANE
    gpt = LANE // S5_GROUP
    for s in range(S5_CHUNK):
        sl = slice(s * S5_GROUP, (s + 1) * S5_GROUP)
        for j in range(n_tile):
            y_s[j, pl.ds(s, n_chunk, stride=S5_CHUNK), :] = jnp.concatenate(
                [yg[g][:, sl] for g in range(j * gpt, (j + 1) * gpt)], axis=1)
    y_ssm = jnp.concatenate([y_s[j] for j in range(n_tile)], axis=1)
    y = _gelu_tanh(u_ref[...] * d_ref[...] + y_ssm)
    y = y * jax.nn.sigmoid(_dot(y.astype(BF16), wglu_ref[...]) + bglu_ref[...])
    attn = _pick_pass(i, n_ctx_tiles, ap_ref, as_ref)
    out = _dot(y.astype(BF16), w_ref[:S5_WIDTH, :]) + _dot(attn, w_ref[S5_WIDTH:, :])
    o_ref[...] = _layer_norm(alpha * x_ref[...] + g_ref[0] * out, lg_ref[...], lb_ref[...])


def _odd_out(proj, y_p, y_s, attn_p, attn_s, x, g1, s5_d, w_glu, b_glu, w_out, ln_g, ln_b, rows, alpha):
    n, d = x.shape
    tm = TM_PROJ
    aw = attn_p.shape[1]
    cpt = tm // S5_CHUNK
    n_ctx, n_lat = attn_p.shape[0] // tm, attn_s.shape[0] // tm
    ap_spec, as_spec = _pass_specs((tm, aw), n_ctx, n_lat)
    yp_spec, ys_spec = _pass_specs((S5_GROUPS, cpt, S5_CHUNK * S5_GROUP), n_ctx, n_lat)
    tok = lambda i, r: (i, 0)
    const = lambda i, r: (0, 0)
    return pl.pallas_call(
        functools.partial(_odd_out_kernel, alpha=alpha, n_ctx_tiles=n_ctx),
        out_shape=jax.ShapeDtypeStruct((n, d), F32),
        grid_spec=pltpu.PrefetchScalarGridSpec(
            num_scalar_prefetch=1, grid=(n // tm,),
            in_specs=[pl.BlockSpec((tm, S5_WIDTH), tok), yp_spec, ys_spec,
                      ap_spec, as_spec, pl.BlockSpec((tm, d), tok),
                      pl.BlockSpec((1, 1, d), lambda i, r: (r[i], 0, 0)),
                      pl.BlockSpec((1, S5_WIDTH), const), pl.BlockSpec((S5_WIDTH, S5_WIDTH), const),
                      pl.BlockSpec((1, S5_WIDTH), const), pl.BlockSpec((S5_WIDTH + aw, d), const),
                      pl.BlockSpec((1, d), const), pl.BlockSpec((1, d), const)],
            out_specs=pl.BlockSpec((tm, d), tok),
            scratch_shapes=[pltpu.VMEM((S5_WIDTH // LANE, tm, LANE), F32)]),
        compiler_params=_params(("arbitrary",), 40),
        name="odd_out_proj",
    )(rows, proj, y_p, y_s, attn_p, attn_s, x, g1, s5_d, w_glu, b_glu, w_out, ln_g, ln_b)


def _router_kernel(rows_ref, x_ref, sc_ref, sh_ref, rt_ref, hb_ref, pos_ref, gate_ref, pt_ref, cnt_ref, tri_s):
    del rows_ref
    t = x_ref.shape[0]
    ne = N_EXPERTS

    @pl.when(pl.program_id(0) == 0)
    def _():
        before = lax.broadcasted_iota(jnp.int32, (t, t), 0) < lax.broadcasted_iota(jnp.int32, (t, t), 1)
        tri_s[...] = jnp.where(before, 1.0, 0.0).astype(BF16)

    h = x_ref[...] * (1.0 + sc_ref[0]) + sh_ref[0]
    h_hi, h_lo = _split(h)
    hb_ref[...] = h_hi
    r_hi, r_lo = _split(rt_ref[...])
    logits = _dot_nt(r_hi, h_hi) + _dot_nt(r_lo, h_hi) + _dot_nt(r_hi, h_lo)
    eid = lax.broadcasted_iota(jnp.int32, (ne, t), 0).astype(F32)
    m0 = jnp.max(logits, axis=0, keepdims=True)
    i0 = jnp.min(jnp.where(logits == m0, eid, float(ne)), axis=0, keepdims=True)
    rest = jnp.where(eid == i0, -jnp.inf, logits)
    m1 = jnp.max(rest, axis=0, keepdims=True)
    i1 = jnp.min(jnp.where(rest == m1, eid, float(ne)), axis=0, keepdims=True)
    ex = jnp.exp(m1 - m0)
    g0 = 1.0 / (1.0 + ex)
    g1 = ex / (1.0 + ex)
    sel0 = eid == i0
    sel1 = eid == i1
    member = jnp.where(sel0, 1.0, jnp.where(sel1, 1.0, 0.0))
    gate = jnp.where(sel0, g0, jnp.where(sel1, g1, 0.0))
    rank = _dot(member.astype(BF16), tri_s[...])
    pos = jnp.where(member > 0.0, rank, -1.0)
    pos_ref[0] = pos.astype(jnp.int32)
    gate_ref[0] = gate
    cnt = jnp.sum(member, axis=1, keepdims=True)
    cnt_ref[0] = jnp.broadcast_to(cnt, (ne, LANE)).astype(jnp.int32)
    packed = jnp.concatenate([pos, gate, jnp.zeros((LANE - 2 * ne, t), F32)], axis=0)
    pt_ref[...] = packed.T


def _router(x, sc, sh, router_t, rows):
    n, d = x.shape
    t = T_MOE
    nb = n // t
    ne = N_EXPERTS
    tok = lambda i, r: (i, 0)
    row3 = lambda i, r: (r[i], 0, 0)
    blk3 = lambda i, r: (i, 0, 0)
    return pl.pallas_call(
        _router_kernel,
        out_shape=(jax.ShapeDtypeStruct((n, d), BF16), jax.ShapeDtypeStruct((nb, ne, t), jnp.int32),
                   jax.ShapeDtypeStruct((nb, ne, t), F32),
                   jax.ShapeDtypeStruct((n, LANE), F32), jax.ShapeDtypeStruct((nb, ne, LANE), jnp.int32)),
        grid_spec=pltpu.PrefetchScalarGridSpec(
            num_scalar_prefetch=1, grid=(nb,),
            in_specs=[pl.BlockSpec((t, d), tok), pl.BlockSpec((1, 1, d), row3), pl.BlockSpec((1, 1, d), row3),
                      pl.BlockSpec((ne, d), lambda i, r: (0, 0))],
            out_specs=[pl.BlockSpec((t, d), tok), pl.BlockSpec((1, ne, t), blk3), pl.BlockSpec((1, ne, t), blk3),
                       pl.BlockSpec((t, LANE), tok), pl.BlockSpec((1, ne, LANE), blk3)],
            scratch_shapes=[pltpu.VMEM((t, t), BF16)]),
        compiler_params=_params(("arbitrary",), 48),
        name="moe_router",
    )(rows, x, sc, sh, router_t)


def _segment_copies(rows, src, dst, src0, dst0, sems, slot):
    out = []
    for k, sz in enumerate(SEG_SIZES):
        off = (rows // (2 * sz)) * (2 * sz)
        s0 = pl.multiple_of(src0 + off, SEG_ALIGN)
        d0 = pl.multiple_of(dst0 + off, SEG_ALIGN)
        sem = sems.at[slot * len(SEG_SIZES) + k]
        out.append(((rows & sz) != 0, pltpu.make_async_copy(src.at[pl.ds(s0, sz)], dst.at[pl.ds(d0, sz)], sem)))
    return out


def _run_copies(copies):
    for pred, cp in copies:
        pl.when(pred)(cp.start)
    for pred, cp in copies:
        pl.when(pred)(cp.wait)


def _moe_gather_kernel(cnt_ref, base_ref, h_ref, pos_ref, gate_ref, xs_in, gs_in, xs_ref, gs_ref,
                       xe_s, ge_s, sem_x, sem_g):
    del xs_in, gs_in
    b = pl.program_id(0)
    t = h_ref.shape[0]
    rt = ROW_MOE
    ne = N_EXPERTS
    copies = []
    off = 0
    for e in range(ne):
        cnt = cnt_ref[b * ne + e]
        pos_row = pos_ref[0, e:e + 1, :]
        gate_row = gate_ref[0, e:e + 1, :]

        def gather(r, c, off=off, pos_row=pos_row, gate_row=gate_row):
            r0 = r * rt
            hit = (lax.broadcasted_iota(jnp.int32, (rt, t), 0) + r0) == pos_row
            onehot = jnp.where(hit, 1.0, 0.0).astype(BF16)
            dst = pl.multiple_of(off + r0, SEG_ALIGN)
            xe_s[pl.ds(dst, rt), :] = _dot(onehot, h_ref[...]).astype(BF16)
            g = jnp.sum(jnp.where(hit, gate_row, 0.0), axis=1, keepdims=True)
            ge_s[pl.ds(dst, rt), :] = jnp.broadcast_to(g, (rt, LANE))
            return c

        lax.fori_loop(0, (cnt + rt - 1) // rt, gather, 0)
        rows = ((cnt + SEG_ALIGN - 1) // SEG_ALIGN) * SEG_ALIGN
        base = base_ref[b * ne + e]
        seg = (_segment_copies(rows, xe_s, xs_ref, off, base, sem_x, e)
               + _segment_copies(rows, ge_s, gs_ref, off, base, sem_g, e))
        for pred, cp in seg:
            pl.when(pred)(cp.start)
        copies += seg
        off = off + rows
    for pred, cp in copies:
        pl.when(pred)(cp.wait)


def _moe_gather(h, pos, gate, counts, base, n_rows):
    n, d = h.shape
    t = T_MOE
    nb = n // t
    ne = N_EXPERTS
    tok = lambda b, c, o: (b, 0)
    blk3 = lambda b, c, o: (b, 0, 0)
    any_spec = pl.BlockSpec(memory_space=pl.ANY)
    xs0 = jnp.zeros((n_rows, d), BF16)
    gs0 = jnp.zeros((n_rows, LANE), F32)
    cap = -(-(2 * t + ne * (SEG_ALIGN - 1) + ROW_MOE - 1) // ROW_MOE) * ROW_MOE
    return pl.pallas_call(
        _moe_gather_kernel,
        out_shape=(jax.ShapeDtypeStruct((n_rows, d), BF16), jax.ShapeDtypeStruct((n_rows, LANE), F32)),
        grid_spec=pltpu.PrefetchScalarGridSpec(
            num_scalar_prefetch=2, grid=(nb,),
            in_specs=[pl.BlockSpec((t, d), tok), pl.BlockSpec((1, ne, t), blk3), pl.BlockSpec((1, ne, t), blk3),
                      any_spec, any_spec],
            out_specs=[any_spec, any_spec],
            scratch_shapes=[pltpu.VMEM((cap, d), BF16), pltpu.VMEM((cap, LANE), F32),
                            pltpu.SemaphoreType.DMA((ne * len(SEG_SIZES),)),
                            pltpu.SemaphoreType.DMA((ne * len(SEG_SIZES),))]),
        input_output_aliases={5: 0, 6: 1},
        compiler_params=_params(("arbitrary",), 32),
        name="moe_gather",
    )(counts, base, h, pos, gate, xs0, gs0)


def _stream_cast(src, dst):
    r, c = dst.shape
    rows = r // GMM_STAGE_PIECES
    slots = GMM_STAGE_SLOTS

    def run(stage, sems):
        def piece(k):
            return pltpu.make_async_copy(src.at[pl.ds(k * rows, rows)], stage.at[k % slots], sems.at[k % slots])

        for k in range(slots - 1):
            piece(k).start()
        for k in range(GMM_STAGE_PIECES):
            if k + slots - 1 < GMM_STAGE_PIECES:
                piece(k + slots - 1).start()
            piece(k).wait()
            dst[k * rows:(k + 1) * rows, :] = stage[k % slots].astype(BF16)

    pl.run_scoped(run, pltpu.VMEM((slots, rows, c), F32), pltpu.SemaphoreType.DMA((slots,)))


def _moe_gmm_kernel(te_ref, valid_ref, x_ref, gs_ref, wg_hbm, wu_hbm, wd_hbm, y_ref, wg_s, wu_s, wd_s, *, layer):
    i = pl.program_id(0)
    valid = valid_ref[i] != 0
    e = te_ref[i]
    switch = valid & ((i == 0) | (e != te_ref[jnp.maximum(i - 1, 0)]))

    @pl.when(switch)
    def _():
        _stream_cast(wg_hbm.at[layer, e], wg_s)
        _stream_cast(wu_hbm.at[layer, e], wu_s)
        _stream_cast(wd_hbm.at[layer, e], wd_s)

    @pl.when(valid)
    def _():
        x = x_ref[...]
        a = (_silu(_dot(x, wg_s[...])) * _dot(x, wu_s[...])).astype(BF16)
        y_ref[...] = (_dot(a, wd_s[...]) * gs_ref[:, 0:1]).astype(BF16)

    @pl.when(jnp.logical_not(valid))
    def _():
        y_ref[...] = jnp.zeros_like(y_ref)


def _moe_gmm(xs, gs, tile_expert, tile_valid, wg, wu, wd, layer):
    n_rows, d = xs.shape
    rt = RT_GMM
    ff = wg.shape[-1]
    tok = lambda i, te, tv: (i, 0)
    any_spec = pl.BlockSpec(memory_space=pl.ANY)
    return pl.pallas_call(
        functools.partial(_moe_gmm_kernel, layer=layer),
        out_shape=jax.ShapeDtypeStruct((n_rows, d), BF16),
        grid_spec=pltpu.PrefetchScalarGridSpec(
            num_scalar_prefetch=2, grid=(n_rows // rt,),
            in_specs=[pl.BlockSpec((rt, d), tok), pl.BlockSpec((rt, LANE), tok), any_spec, any_spec, any_spec],
            out_specs=pl.BlockSpec((rt, d), tok),
            scratch_shapes=[pltpu.VMEM((d, ff), BF16), pltpu.VMEM((d, ff), BF16), pltpu.VMEM((ff, d), BF16)]),
        compiler_params=_params(("arbitrary",), 56),
        name="moe_gmm",
    )(tile_expert, tile_valid, xs, gs, wg, wu, wd)


def _moe_combine_kernel(cnt_ref, base_ref, rows_ref, ys_ref, pt_ref, x_ref, g_ref, lg_ref, lb_ref, o_ref,
                        ycat_s, sems, *, alpha, block0):
    del rows_ref
    t = x_ref.shape[0]
    cap = ycat_s.shape[0]
    ne = N_EXPERTS

    @pl.when(pl.program_id(0) == 0)
    def _():
        ycat_s[...] = jnp.zeros_like(ycat_s)

    b = pl.program_id(0) + block0

    copies = []
    offs = []
    off = 0
    for e in range(ne):
        cnt = cnt_ref[b * ne + e]
        rows = ((cnt + SEG_ALIGN - 1) // SEG_ALIGN) * SEG_ALIGN
        copies += _segment_copies(rows, ys_ref, ycat_s, base_ref[b * ne + e], off, sems, e)
        offs.append(off)
        off = off + rows
    _run_copies(copies)

    pt = pt_ref[...]
    lane = lax.broadcasted_iota(jnp.int32, (1, LANE), 1)
    off_lane = jnp.zeros((1, LANE), F32)
    for e in range(1, ne):
        off_lane = jnp.where(lane == e, offs[e].astype(F32), off_lane)
    chosen = (pt >= 0.0) & (lax.broadcasted_iota(jnp.int32, (t, LANE), 1) < ne)
    slot = pt + off_lane
    s_lo = jnp.min(jnp.where(chosen, slot, float(cap)), axis=1, keepdims=True)
    s_hi = jnp.max(jnp.where(chosen, slot, -1.0), axis=1, keepdims=True)
    n_piece = 3
    wp = cap // n_piece
    f = None
    for j in range(n_piece):
        col = (lax.broadcasted_iota(jnp.int32, (t, wp), 1) + j * wp).astype(F32)
        sel = jnp.where(col == s_lo, 1.0, jnp.where(col == s_hi, 1.0, 0.0)).astype(BF16)
        part = _dot(sel, ycat_s[j * wp:(j + 1) * wp, :])
        f = part if f is None else f + part
    o_ref[...] = _layer_norm(alpha * x_ref[...] + g_ref[0] * f, lg_ref[...], lb_ref[...])


def _moe_combine(ys, pt, x, g2, ln_g, ln_b, counts, base, rows, alpha, block0=0, n_blocks=None):
    n, d = x.shape
    t = T_MOE
    ne = N_EXPERTS
    n_blocks = n // t if n_blocks is None else n_blocks
    cap = -(-(2 * t + ne * (SEG_ALIGN - 1)) // (6 * LANE)) * (6 * LANE)
    tok = lambda b, c, o, r: (b + block0, 0)
    const = lambda b, c, o, r: (0, 0)
    return pl.pallas_call(
        functools.partial(_moe_combine_kernel, alpha=alpha, block0=block0),
        out_shape=jax.ShapeDtypeStruct((n_blocks * t, d), F32),
        grid_spec=pltpu.PrefetchScalarGridSpec(
            num_scalar_prefetch=3, grid=(n_blocks,),
            in_specs=[pl.BlockSpec(memory_space=pl.ANY), pl.BlockSpec((t, LANE), tok), pl.BlockSpec((t, d), tok),
                      pl.BlockSpec((1, 1, d), lambda b, c, o, r: (r[b + block0], 0, 0)),
                      pl.BlockSpec((1, d), const), pl.BlockSpec((1, d), const)],
            out_specs=pl.BlockSpec((t, d), lambda b, c, o, r: (b, 0)),
            scratch_shapes=[pltpu.VMEM((cap, d), BF16), pltpu.SemaphoreType.DMA((ne * len(SEG_SIZES),))]),
        compiler_params=_params(("arbitrary",), 48),
        name="moe_combine",
    )(counts, base, rows, ys, pt, x, g2, ln_g, ln_b)


def _moe_layout(cnt, n_tiles):
    cnt16 = (cnt + SEG_ALIGN - 1) // SEG_ALIGN * SEG_ALIGN
    rows_e = jnp.sum(cnt16, axis=0)
    rows_e = (rows_e + RT_GMM - 1) // RT_GMM * RT_GMM
    ends = jnp.cumsum(rows_e)
    base = (ends - rows_e)[None, :] + jnp.cumsum(cnt16, axis=0) - cnt16
    tile_end = ends // RT_GMM
    idx = jnp.arange(n_tiles, dtype=jnp.int32)
    valid = idx < tile_end[-1]
    te = jnp.sum((idx[:, None] >= tile_end[None, :]).astype(jnp.int32), axis=1)
    last = jnp.sum(((tile_end[-1] - 1) >= tile_end).astype(jnp.int32))
    te = jnp.where(valid, te, last)
    return base.reshape(-1).astype(jnp.int32), te.astype(jnp.int32), valid.astype(jnp.int32)


def _tile_rows(n, p, ss, tm):
    starts = np.arange(n // tm) * tm
    return jnp.asarray(np.where(starts < p, 0, 1 + (starts - p) // ss).astype(np.int32))


def _tile_rope_blocks(n, p, ss, tm):
    starts = np.arange(n // tm) * tm
    return jnp.asarray(np.where(starts < p, 0, 1 + ((starts - p) % ss) // tm).astype(np.int32))


def _tile_edges(n, p, sp, ss, tm):
    starts = np.arange(n // tm) * tm
    pos = np.where(starts < p, starts % sp, (starts - p) % ss)
    seq = np.where(starts < p, sp, ss)
    return jnp.asarray((pos == 0).astype(np.int32)), jnp.asarray(((pos + tm) % seq == 0).astype(np.int32))


def _rope_tables(n_tokens, rot_dim, lane0, tm):
    rows = n_tokens // GRID_W
    row_pos = jnp.repeat(jnp.arange(rows, dtype=F32), GRID_W)
    col_pos = jnp.tile(jnp.arange(GRID_W, dtype=F32), rows)
    half = rot_dim // 2
    qtr = half // 2
    inv_freq = ROPE_THETA ** (-jnp.arange(0, half, 2, dtype=F32) / half)
    ang_r = row_pos[:, None] * inv_freq
    ang_c = col_pos[:, None] * inv_freq
    ang = jnp.concatenate([ang_r, ang_r, ang_c, ang_c], axis=-1)
    cos, sin = jnp.cos(ang), jnp.sin(ang)
    first = ((np.arange(rot_dim) % half) < qtr).astype(np.float32)
    sa = -sin * first
    sb = sin * (1.0 - first)

    def place(tbl, fill):
        full = jnp.full((n_tokens, LANE), fill, F32).at[:, lane0:lane0 + rot_dim].set(tbl)
        return jnp.concatenate([jnp.full((tm, LANE), fill, F32), full], axis=0)

    return place(cos, 1.0), place(sa, 0.0), place(sb, 0.0)


def _pad_heads(w, n_heads, width):
    lead = w.shape[:-1]
    w = w.reshape(lead + (n_heads, width))
    w = jnp.pad(w, [(0, 0)] * len(lead) + [(0, 0), (0, LANE - width)])
    return w.reshape(lead + (n_heads * LANE,))


def _pad_head_rows(w, n_heads, width):
    d = w.shape[-1]
    w = w.reshape(n_heads, width, d)
    return jnp.pad(w, ((0, 0), (0, LANE - width), (0, 0))).reshape(n_heads * LANE, d)


def kernel(x_prompt, x_sample, cache_attn_k, cache_attn_v, cache_mla_ckv, cache_mla_kpe, state_ssm_re, state_ssm_im, c, c_ctx, ada_w, ada_b, ln_g, ln_b, ev_w_in, ev_conv_w, ev_q_gain, ev_k_gain, ev_w_out, ffn_w_gate, ffn_w_up, ffn_w_down, od_w_in, s5_a_re, s5_a_im, s5_log_dt, s5_b_re, s5_b_im, s5_c_re, s5_c_im, s5_d, s5_w_glu, s5_b_glu, mla_q_gain, mla_w_uq, mla_kv_gain, mla_w_ukv, od_w_out, moe_router, moe_w_gate, moe_w_up, moe_w_down):
    bp, sp, d = x_prompt.shape
    bs, ss, _ = x_sample.shape
    depth = ada_w.shape[0]
    alpha = (2 * depth) ** 0.25
    p = bp * sp
    n = p + bs * ss
    past = cache_attn_k.shape[2]
    for tm in (TM_PROJ, TM_EVEN_IN, TM_FFN, T_MOE):
        assert p % tm == 0 and ss % tm == 0 and (sp % tm == 0 or tm % sp == 0)
    assert sp % S5_CHUNK == 0 and ss % S5_CHUNK == 0

    x = (x_prompt.reshape(p, d), x_sample.reshape(bs * ss, d))

    nrow = -(-(1 + bs) // SUBLANE) * SUBLANE
    cond = jnp.zeros((nrow, d), F32).at[0].set(c_ctx).at[1:1 + bs].set(c)
    mod = _modulation(cond, ada_w, ada_b)

    def mod_part(l, k):
        return mod[l, :, k * d:(k + 1) * d].reshape(nrow, 1, d)

    rows_proj = _tile_rows(n, p, ss, TM_PROJ)
    rows_ffn = _tile_rows(n, p, ss, TM_FFN)
    rows_moe = _tile_rows(n, p, ss, T_MOE)
    rblk = _tile_rope_blocks(n, p, ss, TM_PROJ)
    first, last = _tile_edges(n, p, sp, ss, TM_PROJ)
    rows_even = _tile_rows(n, p, ss, TM_EVEN_IN)
    rblk_even = _tile_rope_blocks(n, p, ss, TM_EVEN_IN)
    tables_even = _rope_tables(ss, HEAD_DIM, 0, TM_EVEN_IN)
    tables_mla = _rope_tables(ss, MLA_ROPE, MLA_NOPE, TM_PROJ)

    s5_mats = jax.vmap(_s5_matrices)(s5_a_re, s5_a_im, s5_log_dt, s5_b_re, s5_b_im, s5_c_re, s5_c_im)
    ffn_wg, ffn_wu, ffn_wd = ffn_w_gate.astype(BF16), ffn_w_up.astype(BF16), ffn_w_down.astype(BF16)
    out_k, out_v, out_ckv, out_kpe, out_sre, out_sim = [], [], [], [], [], []
    cw = CONV_WIDTH
    hq = ATTN_HEADS * HEAD_DIM
    hkv = ATTN_KV_HEADS * HEAD_DIM
    for l in range(depth):
        i = l // 2
        sh1, sc1, g1, sh2, sc2, g2 = [mod_part(l, k) for k in range(6)]
        lg = ln_g[l].reshape(2, 1, d)
        lb = ln_b[l].reshape(2, 1, d)
        if l % 2 == 0:
            w_in = ev_w_in[i]
            wc = w_in[:, :3 * cw].astype(BF16)
            wq = _pad_heads(w_in[:, 3 * cw:3 * cw + hq], ATTN_HEADS, HEAD_DIM).astype(BF16)
            wkv = jnp.concatenate([_pad_heads(w_in[:, 3 * cw + hq:3 * cw + hq + hkv], ATTN_KV_HEADS, HEAD_DIM),
                                   _pad_heads(w_in[:, 3 * cw + hq + hkv:], ATTN_KV_HEADS, HEAD_DIM)],
                                  axis=1).astype(BF16)
            qg = jnp.pad(ev_q_gain[i], (0, LANE - HEAD_DIM)).reshape(1, LANE)
            kg = jnp.pad(ev_k_gain[i], (0, LANE - HEAD_DIM)).reshape(1, LANE)
            conv, q, k, v, kn, vf = _even_in(x, sc1, sh1, wc, wq, wkv, qg, kg, tables_even, rows_even, rblk_even)
            kvw = ATTN_KV_HEADS * LANE
            out_k.append(kn[:p].reshape(bp, sp, ATTN_KV_HEADS, LANE)[..., :HEAD_DIM])
            out_v.append(vf[:p].reshape(bp, sp, ATTN_KV_HEADS, LANE)[..., :HEAD_DIM])
            group = ATTN_HEADS // ATTN_KV_HEADS
            qw = ATTN_HEADS * LANE
            a_p = _attention(q, [(k, v, True)], ATTN_HEADS, group, bp, sp, 0)
            kc = _pad_heads(cache_attn_k[:, i].reshape(bs, past, hkv), ATTN_KV_HEADS, HEAD_DIM).astype(BF16)
            vc = _pad_heads(cache_attn_v[:, i].reshape(bs, past, hkv), ATTN_KV_HEADS, HEAD_DIM).astype(BF16)
            a_s = _attention(q, [(kc, vc, False), (k, v, True)], ATTN_HEADS, group, bs, ss, p)
            w_out = jnp.concatenate([ev_w_out[i][:cw], _pad_head_rows(ev_w_out[i][cw:], ATTN_HEADS, HEAD_DIM)],
                                    axis=0).astype(BF16)
            conv_w = jnp.pad(ev_conv_w[i], ((0, SUBLANE - ev_conv_w.shape[1]), (0, 0)))
            x = _even_out(conv, a_p.reshape(p, qw), a_s.reshape(bs * ss, qw), x, g1, conv_w, w_out, lg[0], lb[0],
                          first, last, rows_proj, alpha, sp)
            x = _ffn(x, sc2, sh2, g2, ffn_wg, ffn_wu, ffn_wd, i, lg[1], lb[1], rows_ffn, alpha)
        else:
            q_end = S5_WIDTH + MLA_Q_RANK
            kv_end = q_end + MLA_KV_RANK
            w_in = od_w_in[i]
            w_in = jnp.concatenate([w_in[:, :kv_end], jnp.zeros((d, MLA_NOPE), F32), w_in[:, kv_end:],
                                    jnp.zeros((d, LANE - MLA_NOPE - MLA_ROPE), F32)], axis=1).astype(BF16)
            m_ctx, m_lat = p // S5_CHUNK, bs * ss // S5_CHUNK
            ctx_chunk0, lat_chunk0 = (m_lat, 0) if m_lat >= m_ctx else (0, m_ctx)
            assert ctx_chunk0 % m_ctx == 0 and lat_chunk0 % m_lat == 0
            proj, u = _odd_in(x, sc1, sh1, w_in, rows_proj, p // TM_PROJ, ctx_chunk0, lat_chunk0)
            out_kpe.append(proj[:p, kv_end + MLA_NOPE:kv_end + MLA_NOPE + MLA_ROPE].reshape(bp, sp, MLA_ROPE))

            h0_p = jnp.zeros((S5_GROUPS, bp, 4 * S5_STATE), F32)
            y_p, fin_p = _s5(u, s5_mats, i, h0_p, bp, sp // S5_CHUNK, ctx_chunk0 // m_ctx)

            def pack_state(re, im):
                to = lambda a: a.transpose(2, 0, 1, 3).reshape(S5_GROUPS, a.shape[0], 2 * S5_STATE)
                return jnp.concatenate([to(re), to(im)], axis=-1)

            y_s, _ = _s5(u, s5_mats, i, pack_state(state_ssm_re[:, i], state_ssm_im[:, i]), bs, ss // S5_CHUNK,
                         lat_chunk0 // m_lat)
            fin = fin_p.reshape(S5_GROUPS, bp, 2, 2, S5_STATE)
            out_sre.append(fin[:, :, 0].transpose(1, 2, 0, 3))
            out_sim.append(fin[:, :, 1].transpose(1, 2, 0, 3))

            qk = MLA_NOPE + MLA_ROPE
            wuq = _pad_heads(mla_w_uq[i], MLA_HEADS, qk).astype(BF16)
            wukv = mla_w_ukv[i].reshape(MLA_KV_RANK, MLA_HEADS, MLA_NOPE + MLA_V)
            wk = _pad_heads(wukv[..., :MLA_NOPE].reshape(MLA_KV_RANK, -1), MLA_HEADS, MLA_NOPE).astype(BF16)
            wv = _pad_heads(wukv[..., MLA_NOPE:].reshape(MLA_KV_RANK, -1), MLA_HEADS, MLA_V).astype(BF16)
            place = np.zeros((LANE, MLA_HEADS * LANE), np.float32)
            for hd in range(MLA_HEADS):
                for j in range(MLA_ROPE):
                    place[MLA_NOPE + j, hd * LANE + MLA_NOPE + j] = 1.0
            place = jnp.asarray(place, BF16)
            q, k, v, ckv = _mla_prep(proj, mla_q_gain[i].reshape(1, -1), mla_kv_gain[i].reshape(1, -1),
                                     wuq, wk, wv, place, tables_mla, rblk)
            out_ckv.append(ckv[:p].reshape(bp, sp, MLA_KV_RANK))
            hw = MLA_HEADS * LANE
            a_p = _attention(q, [(k, v, True)], MLA_HEADS, 1, bp, sp, 0)
            kpe_c = jnp.pad(cache_mla_kpe[:, i].reshape(bs * past, MLA_ROPE),
                            ((0, 0), (MLA_NOPE, LANE - MLA_NOPE - MLA_ROPE)))
            kc, vc = _mla_cache(cache_mla_ckv[:, i].reshape(bs * past, MLA_KV_RANK), kpe_c, wk, wv, place)
            a_s = _attention(q, [(kc.reshape(bs, past, hw), vc.reshape(bs, past, hw), False), (k, v, True)],
                             MLA_HEADS, 1, bs, ss, p)
            w_out = jnp.concatenate([od_w_out[i][:S5_WIDTH], _pad_head_rows(od_w_out[i][S5_WIDTH:], MLA_HEADS, MLA_V)],
                                    axis=0).astype(BF16)
            x = _odd_out(proj, y_p, y_s, a_p.reshape(p, hw), a_s.reshape(bs * ss, hw), x, g1, s5_d[i].reshape(1, -1),
                         s5_w_glu[i].astype(BF16), s5_b_glu[i].reshape(1, -1), w_out, lg[0], lb[0], rows_proj, alpha)

            hb, pos, gate, pt, cnt = _router(x, sc2, sh2, moe_router[i].T, rows_moe)
            cnt = cnt[:, :, 0]
            nblk = n // T_MOE
            n_tiles = -(-(2 * n + nblk * N_EXPERTS * (SEG_ALIGN - 1)) // RT_GMM) + N_EXPERTS
            base, tile_expert, tile_valid = _moe_layout(cnt, n_tiles)
            counts = cnt.reshape(-1)
            xs, gs = _moe_gather(hb, pos, gate, counts, base, n_tiles * RT_GMM)
            ys = _moe_gmm(xs, gs, tile_expert, tile_valid, moe_w_gate, moe_w_up, moe_w_down, i)
            if l == depth - 1:
                nb_ctx = p // T_MOE
                x = tuple(_moe_combine(ys, pt, x, g2, lg[1], lb[1], counts, base, rows_moe, alpha, b0, nbk)
                          for b0, nbk in ((0, nb_ctx), (nb_ctx, nblk - nb_ctx)))
            else:
                x = _moe_combine(ys, pt, x, g2, lg[1], lb[1], counts, base, rows_moe, alpha)

    if not isinstance(x, tuple):
        x = (x[:p], x[p:])
    y_prompt = x[0].reshape(bp, sp, d)
    y_sample = x[1].reshape(bs, ss, d)
    return (y_prompt, y_sample, jnp.stack(out_k, axis=1), jnp.stack(out_v, axis=1),
            jnp.stack(out_ckv, axis=1), jnp.stack(out_kpe, axis=1),
            jnp.stack(out_sre, axis=1), jnp.stack(out_sim, axis=1))
```

```python
import functools
import math

import jax
import jax.numpy as jnp
import numpy as np
from jax import lax
from jax.experimental import pallas as pl
from jax.experimental.pallas import tpu as pltpu

F32 = jnp.float32
BF16 = jnp.bfloat16

LANE = 128
SUBLANE = 8
MIB = 1024 * 1024

GRID_W = 64
ROPE_THETA = 10000.0
LN_EPS = 1e-6
RMS_EPS = 1e-6
HEAD_DIM = 64
ATTN_HEADS = 8
ATTN_KV_HEADS = 2
CONV_WIDTH = 512
S5_WIDTH = 512
S5_GROUP = 16
S5_GROUPS = 32
S5_STATE = 64
S5_CHUNK = 16
MLA_HEADS = 8
MLA_Q_RANK = 256
MLA_KV_RANK = 128
MLA_NOPE = 64
MLA_ROPE = 32
MLA_V = 64
N_EXPERTS = 8

TM_PROJ = 512
TM_EVEN_IN = 256
TM_FFN = 512
FFN_CHUNK = 2816
T_MOE = 1024
ROW_MOE = 128
RT_GMM = 512
GMM_STAGE_PIECES = 16
GMM_STAGE_SLOTS = 6
SEG_ALIGN = 16
SEG_SIZES = tuple(T_MOE >> k for k in range(T_MOE.bit_length()) if (T_MOE >> k) >= SEG_ALIGN)
TQ_ATTN = 512


def _params(sem, vmem_mib):
    return pltpu.CompilerParams(dimension_semantics=sem, vmem_limit_bytes=vmem_mib * MIB)


def _dot(a, b):
    return jnp.dot(a, b, preferred_element_type=F32)


def _dot_nt(a, b):
    return lax.dot_general(a, b, (((1,), (1,)), ((), ())), preferred_element_type=F32)


def _split(a):
    hi = a.astype(BF16)
    lo = (a - hi.astype(F32)).astype(BF16)
    return hi, lo


def _silu(x):
    return x * jax.nn.sigmoid(x)


def _layer_norm(r, g, b):
    mu = jnp.mean(r, axis=-1, keepdims=True)
    d = r - mu
    var = jnp.mean(d * d, axis=-1, keepdims=True)
    return d * lax.rsqrt(var + LN_EPS) * g + b


def _rms(x, g, n):
    ms = jnp.sum(x * x, axis=-1, keepdims=True) * (1.0 / n)
    return x * lax.rsqrt(ms + RMS_EPS) * g


def _rope(x, cos, sa, sb, q):
    w = x.shape[-1]
    return x * cos + pltpu.roll(x, w - q, 1) * sa + pltpu.roll(x, q, 1) * sb


def _ada_kernel(c_ref, w_ref, b_ref, o_ref):
    c = c_ref[...]
    a_hi, a_lo = _split(_silu(c))
    w_hi, w_lo = _split(w_ref[0])
    o_ref[0] = _dot(a_hi, w_hi) + _dot(a_lo, w_hi) + _dot(a_hi, w_lo) + b_ref[0]


def _modulation(cond, ada_w, ada_b):
    depth, d, d6 = ada_w.shape
    r = cond.shape[0]
    out = pl.pallas_call(
        _ada_kernel,
        out_shape=jax.ShapeDtypeStruct((depth, d6 // d, r, d), F32),
        grid=(depth, d6 // d),
        in_specs=[pl.BlockSpec((r, d), lambda l, j: (0, 0)),
                  pl.BlockSpec((1, d, d), lambda l, j: (l, 0, j)),
                  pl.BlockSpec((1, 1, d), lambda l, j: (l, 0, j))],
        out_specs=pl.BlockSpec((None, 1, r, d), lambda l, j: (l, j, 0, 0)),
        compiler_params=_params(("arbitrary", "arbitrary"), 40),
        name="ada_modulation",
    )(cond, ada_w, ada_b.reshape(depth, 1, d6))
    return out.reshape(depth, d6 // d, r, 1, d)


def _mod_operand(m, row_fn):
    arr, layer, part = m
    return arr, pl.BlockSpec((None, None, 1, 1, arr.shape[-1]), lambda *a: (layer, part, row_fn(*a), 0, 0))


def _load_x(x_refs, n_ctx_tiles):
    if len(x_refs) == 1:
        return x_refs[0][...]
    return _pick_pass(pl.program_id(0), n_ctx_tiles, *x_refs)


def _x_specs(x, tm, tok):
    if not isinstance(x, tuple):
        return (x,), [pl.BlockSpec((tm, x.shape[1]), tok)], 0
    n_ctx, n_lat = x[0].shape[0] // tm, x[1].shape[0] // tm
    return x, list(_pass_specs((tm, x[0].shape[1]), n_ctx, n_lat)), n_ctx


def _even_in_kernel(rows_ref, rblk_ref, *refs, n_x, n_ctx_tiles):
    (sc_ref, sh_ref, wc_ref, wq_ref, wkv_ref, qg_ref, kg_ref, cos_ref, sa_ref, sb_ref,
     conv_ref, q_ref, k_ref, v_ref, kn_ref, vf_ref) = refs[n_x:]
    del rows_ref, rblk_ref
    h = (_load_x(refs[:n_x], n_ctx_tiles) * (1.0 + sc_ref[0]) + sh_ref[0]).astype(BF16)
    conv_ref[...] = _dot(h, wc_ref[...])
    cos, sa, sb = cos_ref[...], sa_ref[...], sb_ref[...]
    quarter = HEAD_DIM // 4
    q = _dot(h, wq_ref[...])
    for hd in range(ATTN_HEADS):
        sl = slice(hd * LANE, (hd + 1) * LANE)
        qn = _rms(q[:, sl], qg_ref[...], HEAD_DIM)
        q_ref[:, sl] = (_rope(qn, cos, sa, sb, quarter) * (HEAD_DIM ** -0.5)).astype(BF16)
    kv = _dot(h, wkv_ref[...])
    kw = ATTN_KV_HEADS * LANE
    for hd in range(ATTN_KV_HEADS):
        sl = slice(hd * LANE, (hd + 1) * LANE)
        kn = _rms(kv[:, sl], kg_ref[...], HEAD_DIM)
        kn_ref[:, sl] = kn
        k_ref[:, sl] = _rope(kn, cos, sa, sb, quarter).astype(BF16)
    v = kv[:, kw:]
    vf_ref[...] = v
    v_ref[...] = v.astype(BF16)


def _even_in(x, sc, sh, wc, wq, wkv, qg, kg, tables, rows, rblk):
    tm = TM_EVEN_IN
    d = wc.shape[0]
    n = rows.shape[0] * tm
    cw, qw, kvw = wc.shape[1], wq.shape[1], wkv.shape[1]
    kw = kvw // 2
    cos, sa, sb = tables
    tok = lambda i, rows, rblk: (i, 0)
    const = lambda i, rows, rblk: (0, 0)
    tab = lambda i, rows, rblk: (rblk[i], 0)
    xs, x_specs, n_ctx = _x_specs(x, tm, tok)
    row = lambda i, rows, rblk: rows[i]
    (sc, sc_spec), (sh, sh_spec) = _mod_operand(sc, row), _mod_operand(sh, row)
    return pl.pallas_call(
        functools.partial(_even_in_kernel, n_x=len(xs), n_ctx_tiles=n_ctx),
        out_shape=(jax.ShapeDtypeStruct((n, cw), F32), jax.ShapeDtypeStruct((n, qw), BF16),
                   jax.ShapeDtypeStruct((n, kw), BF16), jax.ShapeDtypeStruct((n, kw), BF16),
                   jax.ShapeDtypeStruct((n, kw), F32), jax.ShapeDtypeStruct((n, kw), F32)),
        grid_spec=pltpu.PrefetchScalarGridSpec(
            num_scalar_prefetch=2, grid=(n // tm,),
            in_specs=x_specs + [
                      sc_spec, sh_spec,
                      pl.BlockSpec((d, cw), const), pl.BlockSpec((d, qw), const), pl.BlockSpec((d, kvw), const),
                      pl.BlockSpec((1, LANE), const), pl.BlockSpec((1, LANE), const),
                      pl.BlockSpec((tm, LANE), tab), pl.BlockSpec((tm, LANE), tab), pl.BlockSpec((tm, LANE), tab)],
            out_specs=[pl.BlockSpec((tm, cw), tok), pl.BlockSpec((tm, qw), tok),
                       pl.BlockSpec((tm, kw), tok), pl.BlockSpec((tm, kw), tok),
                       pl.BlockSpec((tm, kw), tok), pl.BlockSpec((tm, kw), tok)]),
        compiler_params=_params(("arbitrary",), 40),
        name="even_in_proj",
    )(rows, rblk, *xs, sc, sh, wc, wq, wkv, qg, kg, cos, sa, sb)


def _attn_kernel(*refs, n_heads, group, n_seg):
    q_ref = refs[0]
    seg = refs[1:1 + 2 * n_seg]
    o_ref = refs[1 + 2 * n_seg]
    for hd in range(n_heads):
        sl = slice(hd * LANE, (hd + 1) * LANE)
        ks = slice((hd // group) * LANE, (hd // group + 1) * LANE)
        qh = q_ref[:, sl]
        scores = [_dot_nt(qh, seg[2 * s][:, ks]) for s in range(n_seg)]
        m = jnp.max(scores[0], axis=-1, keepdims=True)
        for s in range(1, n_seg):
            m = jnp.maximum(m, jnp.max(scores[s], axis=-1, keepdims=True))
        den = None
        acc = None
        for s in range(n_seg):
            p = jnp.exp(scores[s] - m)
            ps = jnp.sum(p, axis=-1, keepdims=True)
            pv = _dot(p.astype(BF16), seg[2 * s + 1][:, ks])
            den = ps if den is None else den + ps
            acc = pv if acc is None else acc + pv
        o_ref[:, sl] = (acc * (1.0 / den)).astype(BF16)


def _attention(q, segments, n_heads, group, b, s, row0):
    n, qw = q.shape
    assert n % s == 0 and row0 % s == 0
    b0 = row0 // s
    tq = min(TQ_ATTN, s)
    in_specs = [pl.BlockSpec((None, tq, qw), lambda i, j: (i + b0, j, 0))]
    args = [q.reshape(n // s, s, qw)]
    for k, v, own in segments:
        kw = k.shape[-1]
        if own:
            k, v = k.reshape(n // s, s, kw), v.reshape(n // s, s, kw)
            kmap = lambda i, j: (i + b0, 0, 0)
        else:
            kmap = lambda i, j: (i, 0, 0)
        t = k.shape[1]
        in_specs += [pl.BlockSpec((None, t, kw), kmap, pipeline_mode=pl.Buffered(1)),
                     pl.BlockSpec((None, t, kw), kmap, pipeline_mode=pl.Buffered(1))]
        args += [k, v]
    return pl.pallas_call(
        functools.partial(_attn_kernel, n_heads=n_heads, group=group, n_seg=len(segments)),
        out_shape=jax.ShapeDtypeStruct((b, s, qw), BF16),
        grid=(b, s // tq),
        in_specs=in_specs,
        out_specs=pl.BlockSpec((None, tq, qw), lambda i, j: (i, j, 0)),
        compiler_params=_params(("arbitrary", "arbitrary"), 56),
        name="attention",
    )(*args)


def _pick_pass(i, n_ctx_tiles, ctx_ref, lat_ref):
    return jnp.where(i < n_ctx_tiles, ctx_ref[...], lat_ref[...])


def _pass_specs(block, n_ctx_tiles, n_lat_tiles):
    lead = (0,) * (len(block) - 2)

    def ctx(i, *_):
        return lead + (jnp.minimum(i, n_ctx_tiles - 1), 0)

    def lat(i, *_):
        return lead + (jnp.clip(i - n_ctx_tiles, 0, n_lat_tiles - 1), 0)

    return pl.BlockSpec(block, ctx), pl.BlockSpec(block, lat)


def _even_out_kernel(first_ref, last_ref, rows_ref, c_ref, cp_ref, cn_ref, ap_ref, as_ref, *refs,
                     alpha, n_ctx_tiles, ctx_seq, n_x):
    g_ref, cw_ref, w_ref, lg_ref, lb_ref, o_ref = refs[n_x:]
    del rows_ref
    i = pl.program_id(0)
    attn = _pick_pass(i, n_ctx_tiles, ap_ref, as_ref)
    tm = c_ref.shape[0]
    cwid = CONV_WIDTH
    c = c_ref[...]
    gate_b, z = c[:, :cwid], c[:, cwid:2 * cwid] * c[:, 2 * cwid:]
    cp = cp_ref[...]
    cn = cn_ref[...]
    zp = cp[SUBLANE - 1:SUBLANE, cwid:2 * cwid] * cp[SUBLANE - 1:SUBLANE, 2 * cwid:]
    zn = cn[0:1, cwid:2 * cwid] * cn[0:1, 2 * cwid:]
    zp = zp * (1 - first_ref[i]).astype(F32)
    zn = zn * (1 - last_ref[i]).astype(F32)
    row = lax.broadcasted_iota(jnp.int32, (tm, cwid), 0)
    z_prev = jnp.where(row == 0, zp, pltpu.roll(z, 1, 0))
    z_next = jnp.where(row == tm - 1, zn, pltpu.roll(z, tm - 1, 0))
    for k in range(1, tm // ctx_seq):
        edge = jnp.where(i < n_ctx_tiles, k * ctx_seq, -1)
        z_prev = jnp.where(row == edge, 0.0, z_prev)
        z_next = jnp.where(row == edge - 1, 0.0, z_next)
    cw = cw_ref[...]
    y = gate_b * (cw[0:1] * z_prev + cw[1:2] * z + cw[2:3] * z_next)
    out = _dot(y.astype(BF16), w_ref[:cwid, :]) + _dot(attn, w_ref[cwid:, :])
    x = _load_x(refs[:n_x], n_ctx_tiles)
    o_ref[...] = _layer_norm(alpha * x + g_ref[0] * out, lg_ref[...], lb_ref[...])


def _even_out(conv, attn_p, attn_s, x, g1, conv_w, w_out, ln_g, ln_b, first, last, rows, alpha, ctx_seq):
    tm = TM_PROJ
    n, d = conv.shape[0], w_out.shape[1]
    cw3 = conv.shape[1]
    aw = attn_p.shape[1]
    hb = tm // SUBLANE
    nblk8 = n // SUBLANE
    n_ctx, n_lat = attn_p.shape[0] // tm, attn_s.shape[0] // tm
    ap_spec, as_spec = _pass_specs((tm, aw), n_ctx, n_lat)
    tok = lambda i, f, l, r: (i, 0)
    const = lambda i, f, l, r: (0, 0)
    xs, x_specs, _ = _x_specs(x, tm, tok)
    g1, g1_spec = _mod_operand(g1, lambda i, f, l, r: r[i])
    return pl.pallas_call(
        functools.partial(_even_out_kernel, alpha=alpha, n_ctx_tiles=n_ctx, ctx_seq=ctx_seq, n_x=len(xs)),
        out_shape=jax.ShapeDtypeStruct((n, d), F32),
        grid_spec=pltpu.PrefetchScalarGridSpec(
            num_scalar_prefetch=3, grid=(n // tm,),
            in_specs=[pl.BlockSpec((tm, cw3), tok),
                      pl.BlockSpec((SUBLANE, cw3), lambda i, f, l, r: (jnp.maximum(i * hb - 1, 0), 0)),
                      pl.BlockSpec((SUBLANE, cw3), lambda i, f, l, r: (jnp.minimum((i + 1) * hb, nblk8 - 1), 0)),
                      ap_spec, as_spec] + x_specs + [
                      g1_spec,
                      pl.BlockSpec((SUBLANE, CONV_WIDTH), const),
                      pl.BlockSpec((CONV_WIDTH + aw, d), const),
                      pl.BlockSpec((1, d), const), pl.BlockSpec((1, d), const)],
            out_specs=pl.BlockSpec((tm, d), tok)),
        compiler_params=_params(("arbitrary",), 40),
        name="even_out_proj",
    )(first, last, rows, conv, conv, conv, attn_p, attn_s, *xs, g1, conv_w, w_out, ln_g, ln_b)


def _ffn_kernel(rows_ref, x_ref, sc_ref, sh_ref, g_ref, wg_ref, wu_ref, wd_ref, lg_ref, lb_ref, o_ref,
                h_s, acc_s, *, alpha):
    del rows_ref
    f = pl.program_id(1)

    @pl.when(f == 0)
    def _():
        h_s[...] = (x_ref[...] * (1.0 + sc_ref[0]) + sh_ref[0]).astype(BF16)

    h = h_s[...]
    a = (_silu(_dot(h, wg_ref[...])) * _dot(h, wu_ref[...])).astype(BF16)
    y = _dot(a, wd_ref[...])

    @pl.when(f == 0)
    def _():
        acc_s[...] = y

    @pl.when(f > 0)
    def _():
        acc_s[...] += y

    @pl.when(f == pl.num_programs(1) - 1)
    def _():
        o_ref[...] = _layer_norm(alpha * x_ref[...] + g_ref[0] * acc_s[...], lg_ref[...], lb_ref[...])


def _ffn(x, sc, sh, g2, wg, wu, wd, layer, ln_g, ln_b, rows, alpha):
    n, d = x.shape
    tm = TM_FFN
    ff = wg.shape[-1]
    fc = FFN_CHUNK
    tok = lambda i, f, r: (i, 0)
    const = lambda i, f, r: (0, 0)
    wmode = dict(pipeline_mode=pl.Buffered(1)) if fc == ff else {}
    row = lambda i, f, r: r[i]
    (sc, sc_spec), (sh, sh_spec), (g2, g2_spec) = (_mod_operand(m, row) for m in (sc, sh, g2))
    return pl.pallas_call(
        functools.partial(_ffn_kernel, alpha=alpha),
        out_shape=jax.ShapeDtypeStruct((n, d), F32),
        grid_spec=pltpu.PrefetchScalarGridSpec(
            num_scalar_prefetch=1, grid=(n // tm, ff // fc),
            in_specs=[pl.BlockSpec((tm, d), tok),
                      sc_spec, sh_spec, g2_spec,
                      pl.BlockSpec((None, d, fc), lambda i, f, r: (layer, 0, f), **wmode),
                      pl.BlockSpec((None, d, fc), lambda i, f, r: (layer, 0, f), **wmode),
                      pl.BlockSpec((None, fc, d), lambda i, f, r: (layer, f, 0), **wmode),
                      pl.BlockSpec((1, d), const), pl.BlockSpec((1, d), const)],
            out_specs=pl.BlockSpec((tm, d), tok),
            scratch_shapes=[pltpu.VMEM((tm, d), BF16), pltpu.VMEM((tm, d), F32)]),
        compiler_params=_params(("arbitrary", "arbitrary"), 56),
        name="ffn_swiglu",
    )(rows, x, sc, sh, g2, wg, wu, wd, ln_g, ln_b)


def _odd_in_kernel(rows_ref, x_ref, sc_ref, sh_ref, w_ref, o_ref, u_ref, u_s):
    del rows_ref
    h = (x_ref[...] * (1.0 + sc_ref[0]) + sh_ref[0]).astype(BF16)
    proj = _dot(h, w_ref[...])
    o_ref[...] = proj
    n_tile = S5_WIDTH // LANE
    gpt = LANE // S5_GROUP
    for j in range(n_tile):
        u_s[j] = proj[:, j * LANE:(j + 1) * LANE]
    n_chunk = u_s.shape[1] // S5_CHUNK
    steps = [[u_s[j, pl.ds(s, n_chunk, stride=S5_CHUNK), :] for j in range(n_tile)] for s in range(S5_CHUNK)]
    for g in range(S5_GROUPS):
        sl = slice((g % gpt) * S5_GROUP, (g % gpt + 1) * S5_GROUP)
        u_ref[g] = jnp.concatenate([st[g // gpt][:, sl] for st in steps], axis=1).astype(BF16)


def _odd_in(x, sc, sh, w, rows, n_ctx_tiles, ctx_chunk0, lat_chunk0):
    n, d = x.shape
    tm = TM_PROJ
    nw = w.shape[1]
    cpt = tm // S5_CHUNK
    u_rows = n // S5_CHUNK
    tok = lambda i, r: (i, 0)
    (sc, sc_spec), (sh, sh_spec) = (_mod_operand(m, lambda i, r: r[i]) for m in (sc, sh))
    ublk = lambda i, r: (0, jnp.where(i < n_ctx_tiles, i + ctx_chunk0 // cpt, i - n_ctx_tiles + lat_chunk0 // cpt), 0)
    return pl.pallas_call(
        _odd_in_kernel,
        out_shape=(jax.ShapeDtypeStruct((n, nw), F32),
                   jax.ShapeDtypeStruct((S5_GROUPS, u_rows, S5_CHUNK * S5_GROUP), BF16)),
        grid_spec=pltpu.PrefetchScalarGridSpec(
            num_scalar_prefetch=1, grid=(n // tm,),
            in_specs=[pl.BlockSpec((tm, d), tok), sc_spec, sh_spec,
                      pl.BlockSpec((d, nw), lambda i, r: (0, 0))],
            out_specs=[pl.BlockSpec((tm, nw), tok), pl.BlockSpec((S5_GROUPS, cpt, S5_CHUNK * S5_GROUP), ublk)],
            scratch_shapes=[pltpu.VMEM((S5_WIDTH // LANE, tm, LANE), F32)]),
        compiler_params=_params(("arbitrary",), 32),
        name="odd_in_proj",
    )(rows, x, sc, sh, w)


def _s5_kernel(u_ref, zin_ref, t_ref, zout_ref, al_ref, h0_ref, y_ref, fin_ref, z_s, hf_s, hb_s, *, nb, nc):
    u = u_ref[0]
    z = _dot(u, zin_ref[0])
    for part in range(2):
        for b in range(nb):
            z_s[part, pl.ds(b, nc, stride=nb), :] = z[b * nc:(b + 1) * nc, part * LANE:(part + 1) * LANE]
    ar = al_ref[0, 0:1, :]
    ai = al_ref[0, 1:2, :]
    fwd = lax.broadcasted_iota(jnp.int32, (nb, LANE), 1) < S5_STATE
    aligned = (lambda r: pl.multiple_of(r, SUBLANE)) if nb % SUBLANE == 0 else (lambda r: r)

    def step(k, carry):
        re, im = carry
        rf = aligned(k * nb)
        rb = aligned((nc - 1 - k) * nb)
        hf_s[0, pl.ds(rf, nb), :] = re
        hf_s[1, pl.ds(rf, nb), :] = im
        hb_s[0, pl.ds(rb, nb), :] = re
        hb_s[1, pl.ds(rb, nb), :] = im
        zr = jnp.where(fwd, z_s[0, pl.ds(rf, nb), :], z_s[0, pl.ds(rb, nb), :])
        zi = jnp.where(fwd, z_s[1, pl.ds(rf, nb), :], z_s[1, pl.ds(rb, nb), :])
        return ar * re - ai * im + zr, ar * im + ai * re + zi

    h0 = h0_ref[0]
    re, im = lax.fori_loop(0, nc, step, (h0[:, :LANE], h0[:, LANE:]))
    fin_ref[0, :, :LANE] = re
    fin_ref[0, :, LANE:] = im
    m = u.shape[0]
    is_fwd = lax.broadcasted_iota(jnp.int32, (m, LANE), 1) < S5_STATE
    halves = []
    for part in range(2):
        hf_s[part] = jnp.where(is_fwd, hf_s[part], hb_s[part])
        halves.append(jnp.concatenate([hf_s[part, pl.ds(b, nc, stride=nb), :] for b in range(nb)], axis=0))
    h_in = jnp.concatenate(halves, axis=1).astype(BF16)
    y_ref[0] = _dot(u, t_ref[0]) + _dot(h_in, zout_ref[0])


def _s5(u, mats, layer, h0, nb, nc, row_block):
    zin, tmat, zout, al = mats
    g, _, w = u.shape
    m = nb * nc
    blk = lambda i: (i, 0, 0)
    lblk = lambda i: (layer, i, 0, 0)
    return pl.pallas_call(
        functools.partial(_s5_kernel, nb=nb, nc=nc),
        out_shape=(jax.ShapeDtypeStruct((g, m, w), F32), jax.ShapeDtypeStruct((g, nb, w), F32)),
        grid=(g,),
        in_specs=[pl.BlockSpec((1, m, w), lambda i: (i, row_block, 0)),
                  pl.BlockSpec((None, 1, w, w), lblk), pl.BlockSpec((None, 1, w, w), lblk),
                  pl.BlockSpec((None, 1, w, w), lblk), pl.BlockSpec((None, 1, 2, LANE), lblk),
                  pl.BlockSpec((1, nb, w), blk)],
        out_specs=(pl.BlockSpec((1, m, w), blk), pl.BlockSpec((1, nb, w), blk)),
        scratch_shapes=[pltpu.VMEM((2, m, LANE), F32), pltpu.VMEM((2, m, LANE), F32), pltpu.VMEM((2, m, LANE), F32)],
        compiler_params=_params(("arbitrary",), 32),
        name="s5_scan",
    )(u, zin, tmat, zout, al, h0)


def _s5_matrices(a_re, a_im, log_dt, b_re, b_im, c_re, c_im):
    hp = lax.Precision.HIGHEST
    L = S5_CHUNK
    dt = jnp.exp(log_dt)[..., None]
    lam_re, lam_im = a_re * dt, a_im * dt

    def power(k):
        k = k[:, None, None, None]
        mag = jnp.exp(lam_re * k)
        return mag * jnp.cos(lam_im * k), mag * jnp.sin(lam_im * k)

    ab_re, ab_im = power(jnp.ones((1,), F32))
    ab_re, ab_im = ab_re[0], ab_im[0]
    num_re, num_im = ab_re - 1.0, ab_im
    den = a_re * a_re + a_im * a_im
    f_re = (num_re * a_re + num_im * a_im) / den
    f_im = (num_im * a_re - num_re * a_im) / den
    bb_re = f_re[..., None] * b_re - f_im[..., None] * b_im
    bb_im = f_re[..., None] * b_im + f_im[..., None] * b_re

    ks = jnp.arange(L + 1, dtype=F32)
    pw_re, pw_im = power(ks)

    def zin_dir(d, exps):
        pr, pi = pw_re[exps, d], pw_im[exps, d]
        w_re = pr[..., None] * bb_re[d][None] - pi[..., None] * bb_im[d][None]
        w_im = pr[..., None] * bb_im[d][None] + pi[..., None] * bb_re[d][None]
        to = lambda w: jnp.transpose(w, (1, 0, 3, 2)).reshape(S5_GROUPS, L * S5_GROUP, S5_STATE)
        return to(w_re), to(w_im)

    steps = np.arange(L)
    zf_re, zf_im = zin_dir(0, L - 1 - steps)
    zb_re, zb_im = zin_dir(1, steps)
    zin = jnp.concatenate([zf_re, zb_re, zf_im, zb_im], axis=-1)

    def zout_dir(d, exps):
        pr, pi = pw_re[exps, d], pw_im[exps, d]
        cr, ci = c_re[d], c_im[d]
        e_re = cr[None] * pr[:, :, None, :] - ci[None] * pi[:, :, None, :]
        e_im = cr[None] * pi[:, :, None, :] + ci[None] * pr[:, :, None, :]
        to = lambda e: jnp.transpose(e, (1, 3, 0, 2)).reshape(S5_GROUPS, S5_STATE, L * S5_GROUP)
        return to(e_re), to(-e_im)

    of_re, of_im = zout_dir(0, steps + 1)
    ob_re, ob_im = zout_dir(1, L - steps)
    zout = jnp.concatenate([of_re, ob_re, of_im, ob_im], axis=1)

    def taps(d):
        pr, pi = pw_re[:L, d], pw_im[:L, d]
        m_re = pr[..., None] * bb_re[d][None] - pi[..., None] * bb_im[d][None]
        m_im = pr[..., None] * bb_im[d][None] + pi[..., None] * bb_re[d][None]
        return (jnp.einsum('gpn,lgnq->lgpq', c_re[d], m_re, precision=hp)
                - jnp.einsum('gpn,lgnq->lgpq', c_im[d], m_im, precision=hp))

    kf, kb = taps(0), taps(1)
    pad4 = ((0, 0),) * 3
    tm = jnp.stack([jnp.pad(kf[:L - s], ((s, 0),) + pad4) + jnp.pad(kb[:s + 1][::-1], ((0, L - 1 - s),) + pad4)
                    for s in range(L)])
    tmat = jnp.transpose(tm, (2, 0, 4, 1, 3)).reshape(S5_GROUPS, L * S5_GROUP, L * S5_GROUP)

    al = jnp.stack([jnp.concatenate([pw_re[L, 0], pw_re[L, 1]], axis=-1),
                    jnp.concatenate([pw_im[L, 0], pw_im[L, 1]], axis=-1)], axis=1)
    return zin.astype(BF16), tmat.astype(BF16), zout.astype(BF16), al


def _mla_kv(ckv, kpe, wk_ref, wv_ref, pl_ref, k_ref, v_ref):
    cb = ckv.astype(BF16)
    k_ref[...] = (_dot(cb, wk_ref[...]) + _dot(kpe.astype(BF16), pl_ref[...])).astype(BF16)
    v_ref[...] = _dot(cb, wv_ref[...]).astype(BF16)


def _mla_prep_kernel(rblk_ref, p_ref, qg_ref, kvg_ref, wuq_ref, wk_ref, wv_ref, pl_ref, cos_ref, sa_ref, sb_ref,
                     q_ref, k_ref, v_ref, ckv_ref):
    del rblk_ref
    pr = p_ref[...]
    cos, sa, sb = cos_ref[...], sa_ref[...], sb_ref[...]
    quarter = MLA_ROPE // 4
    scale = (MLA_NOPE + MLA_ROPE) ** -0.5
    qn = _rms(pr[:, :MLA_Q_RANK], qg_ref[...], MLA_Q_RANK).astype(BF16)
    q = _dot(qn, wuq_ref[...])
    for hd in range(MLA_HEADS):
        sl = slice(hd * LANE, (hd + 1) * LANE)
        q_ref[:, sl] = (_rope(q[:, sl], cos, sa, sb, quarter) * scale).astype(BF16)
    ckv = _rms(pr[:, MLA_Q_RANK:MLA_Q_RANK + MLA_KV_RANK], kvg_ref[...], MLA_KV_RANK)
    ckv_ref[...] = ckv
    kpe = _rope(pr[:, MLA_Q_RANK + MLA_KV_RANK:], cos, sa, sb, quarter)
    _mla_kv(ckv, kpe, wk_ref, wv_ref, pl_ref, k_ref, v_ref)


def _mla_prep(proj, qg, kvg, wuq, wk, wv, place, tables, rblk):
    n = proj.shape[0]
    tm = TM_PROJ
    hw = MLA_HEADS * LANE
    pw = MLA_Q_RANK + MLA_KV_RANK + LANE
    cos, sa, sb = tables
    tok = lambda i, r: (i, 0)
    const = lambda i, r: (0, 0)
    tab = lambda i, r: (r[i], 0)
    return pl.pallas_call(
        _mla_prep_kernel,
        out_shape=(jax.ShapeDtypeStruct((n, hw), BF16), jax.ShapeDtypeStruct((n, hw), BF16),
                   jax.ShapeDtypeStruct((n, hw), BF16), jax.ShapeDtypeStruct((n, MLA_KV_RANK), F32)),
        grid_spec=pltpu.PrefetchScalarGridSpec(
            num_scalar_prefetch=1, grid=(n // tm,),
            in_specs=[pl.BlockSpec((tm, pw), lambda i, r: (i, 1)),
                      pl.BlockSpec((1, MLA_Q_RANK), const), pl.BlockSpec((1, MLA_KV_RANK), const),
                      pl.BlockSpec((MLA_Q_RANK, hw), const), pl.BlockSpec((MLA_KV_RANK, hw), const),
                      pl.BlockSpec((MLA_KV_RANK, hw), const), pl.BlockSpec((LANE, hw), const),
                      pl.BlockSpec((tm, LANE), tab), pl.BlockSpec((tm, LANE), tab), pl.BlockSpec((tm, LANE), tab)],
            out_specs=[pl.BlockSpec((tm, hw), tok), pl.BlockSpec((tm, hw), tok), pl.BlockSpec((tm, hw), tok),
                       pl.BlockSpec((tm, MLA_KV_RANK), tok)]),
        compiler_params=_params(("arbitrary",), 32),
        name="mla_prep",
    )(rblk, proj, qg, kvg, wuq, wk, wv, place, cos, sa, sb)


def _mla_cache_kernel(c_ref, p_ref, wk_ref, wv_ref, pl_ref, k_ref, v_ref):
    _mla_kv(c_ref[...], p_ref[...], wk_ref, wv_ref, pl_ref, k_ref, v_ref)


def _mla_cache(ckv, kpe, wk, wv, place):
    n = ckv.shape[0]
    tm = min(TM_PROJ, n)
    assert n % tm == 0
    hw = MLA_HEADS * LANE
    tok = lambda i: (i, 0)
    const = lambda i: (0, 0)
    return pl.pallas_call(
        _mla_cache_kernel,
        out_shape=(jax.ShapeDtypeStruct((n, hw), BF16), jax.ShapeDtypeStruct((n, hw), BF16)),
        grid=(n // tm,),
        in_specs=[pl.BlockSpec((tm, MLA_KV_RANK), tok), pl.BlockSpec((tm, LANE), tok),
                  pl.BlockSpec((MLA_KV_RANK, hw), const), pl.BlockSpec((MLA_KV_RANK, hw), const),
                  pl.BlockSpec((LANE, hw), const)],
        out_specs=[pl.BlockSpec((tm, hw), tok), pl.BlockSpec((tm, hw), tok)],
        compiler_params=_params(("arbitrary",), 32),
        name="mla_cache_kv",
    )(ckv, kpe, wk, wv, place)


def _gelu_tanh(x):
    return 0.5 * x * (1.0 + jnp.tanh(math.sqrt(2.0 / math.pi) * (x + 0.044715 * (x * x * x))))


def _odd_out_kernel(rows_ref, u_ref, yp_ref, ys_ref, ap_ref, as_ref, x_ref, g_ref, d_ref, wglu_ref, bglu_ref, w_ref,
                    lg_ref, lb_ref, o_ref, y_s, *, alpha, n_ctx_tiles):
    del rows_ref
    i = pl.program_id(0)
    yg = _pick_pass(i, n_ctx_tiles, yp_ref, ys_ref)
    n_chunk = yg.shape[1]
    n_tile = S5_WIDTH // LANE
    gpt = LANE // S5_GROUP
    for s in range(S5_CHUNK):
        sl = slice(s * S5_GROUP, (s + 1) * S5_GROUP)
        for j in range(n_tile):
            y_s[j, pl.ds(s, n_chunk, stride=S5_CHUNK), :] = jnp.concatenate(
                [yg[g][:, sl] for g in range(j * gpt, (j + 1) * gpt)], axis=1)
    y_ssm = jnp.concatenate([y_s[j] for j in range(n_tile)], axis=1)
    y = _gelu_tanh(u_ref[...] * d_ref[...] + y_ssm)
    y = y * jax.nn.sigmoid(_dot(y.astype(BF16), wglu_ref[...]) + bglu_ref[...])
    attn = _pick_pass(i, n_ctx_tiles, ap_ref, as_ref)
    out = _dot(y.astype(BF16), w_ref[:S5_WIDTH, :]) + _dot(attn, w_ref[S5_WIDTH:, :])
    o_ref[...] = _layer_norm(alpha * x_ref[...] + g_ref[0] * out, lg_ref[...], lb_ref[...])


def _odd_out(proj, y_p, y_s, attn_p, attn_s, x, g1, s5_d, w_glu, b_glu, w_out, ln_g, ln_b, rows, alpha):
    n, d = x.shape
    tm = TM_PROJ
    aw = attn_p.shape[1]
    cpt = tm // S5_CHUNK
    n_ctx, n_lat = attn_p.shape[0] // tm, attn_s.shape[0] // tm
    ap_spec, as_spec = _pass_specs((tm, aw), n_ctx, n_lat)
    yp_spec, ys_spec = _pass_specs((S5_GROUPS, cpt, S5_CHUNK * S5_GROUP), n_ctx, n_lat)
    tok = lambda i, r: (i, 0)
    const = lambda i, r: (0, 0)
    g1, g1_spec = _mod_operand(g1, lambda i, r: r[i])
    return pl.pallas_call(
        functools.partial(_odd_out_kernel, alpha=alpha, n_ctx_tiles=n_ctx),
        out_shape=jax.ShapeDtypeStruct((n, d), F32),
        grid_spec=pltpu.PrefetchScalarGridSpec(
            num_scalar_prefetch=1, grid=(n // tm,),
            in_specs=[pl.BlockSpec((tm, S5_WIDTH), tok), yp_spec, ys_spec,
                      ap_spec, as_spec, pl.BlockSpec((tm, d), tok),
                      g1_spec,
                      pl.BlockSpec((1, S5_WIDTH), const), pl.BlockSpec((S5_WIDTH, S5_WIDTH), const),
                      pl.BlockSpec((1, S5_WIDTH), const), pl.BlockSpec((S5_WIDTH + aw, d), const),
                      pl.BlockSpec((1, d), const), pl.BlockSpec((1, d), const)],
            out_specs=pl.BlockSpec((tm, d), tok),
            scratch_shapes=[pltpu.VMEM((S5_WIDTH // LANE, tm, LANE), F32)]),
        compiler_params=_params(("arbitrary",), 40),
        name="odd_out_proj",
    )(rows, proj, y_p, y_s, attn_p, attn_s, x, g1, s5_d, w_glu, b_glu, w_out, ln_g, ln_b)


def _router_kernel(rows_ref, x_ref, sc_ref, sh_ref, rt_ref, hb_ref, pos_ref, gate_ref, pt_ref, cnt_ref, tri_s):
    del rows_ref
    t = x_ref.shape[0]
    ne = N_EXPERTS

    @pl.when(pl.program_id(0) == 0)
    def _():
        before = lax.broadcasted_iota(jnp.int32, (t, t), 0) < lax.broadcasted_iota(jnp.int32, (t, t), 1)
        tri_s[...] = jnp.where(before, 1.0, 0.0).astype(BF16)

    h = x_ref[...] * (1.0 + sc_ref[0]) + sh_ref[0]
    h_hi, h_lo = _split(h)
    hb_ref[...] = h_hi
    r_hi, r_lo = _split(rt_ref[...])
    logits = _dot_nt(r_hi, h_hi) + _dot_nt(r_lo, h_hi) + _dot_nt(r_hi, h_lo)
    eid = lax.broadcasted_iota(jnp.int32, (ne, t), 0).astype(F32)
    m0 = jnp.max(logits, axis=0, keepdims=True)
    i0 = jnp.min(jnp.where(logits == m0, eid, float(ne)), axis=0, keepdims=True)
    rest = jnp.where(eid == i0, -jnp.inf, logits)
    m1 = jnp.max(rest, axis=0, keepdims=True)
    i1 = jnp.min(jnp.where(rest == m1, eid, float(ne)), axis=0, keepdims=True)
    ex = jnp.exp(m1 - m0)
    g0 = 1.0 / (1.0 + ex)
    g1 = ex / (1.0 + ex)
    sel0 = eid == i0
    sel1 = eid == i1
    member = jnp.where(sel0, 1.0, jnp.where(sel1, 1.0, 0.0))
    gate = jnp.where(sel0, g0, jnp.where(sel1, g1, 0.0))
    rank = _dot(member.astype(BF16), tri_s[...])
    pos = jnp.where(member > 0.0, rank, -1.0)
    pos_ref[0] = pos.astype(jnp.int32)
    gate_ref[0] = gate
    cnt = jnp.sum(member, axis=1, keepdims=True)
    cnt_ref[0] = jnp.broadcast_to(cnt, (ne, LANE)).astype(jnp.int32)
    packed = jnp.concatenate([pos, gate, jnp.zeros((LANE - 2 * ne, t), F32)], axis=0)
    pt_ref[...] = packed.T


def _router(x, sc, sh, router_t, rows):
    n, d = x.shape
    t = T_MOE
    nb = n // t
    ne = N_EXPERTS
    tok = lambda i, r: (i, 0)
    (sc, sc_spec), (sh, sh_spec) = (_mod_operand(m, lambda i, r: r[i]) for m in (sc, sh))
    blk3 = lambda i, r: (i, 0, 0)
    return pl.pallas_call(
        _router_kernel,
        out_shape=(jax.ShapeDtypeStruct((n, d), BF16), jax.ShapeDtypeStruct((nb, ne, t), jnp.int32),
                   jax.ShapeDtypeStruct((nb, ne, t), F32),
                   jax.ShapeDtypeStruct((n, LANE), F32), jax.ShapeDtypeStruct((nb, ne, LANE), jnp.int32)),
        grid_spec=pltpu.PrefetchScalarGridSpec(
            num_scalar_prefetch=1, grid=(nb,),
            in_specs=[pl.BlockSpec((t, d), tok), sc_spec, sh_spec,
                      pl.BlockSpec((ne, d), lambda i, r: (0, 0))],
            out_specs=[pl.BlockSpec((t, d), tok), pl.BlockSpec((1, ne, t), blk3), pl.BlockSpec((1, ne, t), blk3),
                       pl.BlockSpec((t, LANE), tok), pl.BlockSpec((1, ne, LANE), blk3)],
            scratch_shapes=[pltpu.VMEM((t, t), BF16)]),
        compiler_params=_params(("arbitrary",), 48),
        name="moe_router",
    )(rows, x, sc, sh, router_t)


def _segment_copies(rows, src, dst, src0, dst0, sems, slot):
    out = []
    for k, sz in enumerate(SEG_SIZES):
        off = (rows // (2 * sz)) * (2 * sz)
        s0 = pl.multiple_of(src0 + off, SEG_ALIGN)
        d0 = pl.multiple_of(dst0 + off, SEG_ALIGN)
        sem = sems.at[slot * len(SEG_SIZES) + k]
        out.append(((rows & sz) != 0, pltpu.make_async_copy(src.at[pl.ds(s0, sz)], dst.at[pl.ds(d0, sz)], sem)))
    return out


def _run_copies(copies):
    for pred, cp in copies:
        pl.when(pred)(cp.start)
    for pred, cp in copies:
        pl.when(pred)(cp.wait)


def _moe_gather_kernel(cnt_ref, base_ref, h_ref, pos_ref, gate_ref, xs_in, gs_in, xs_ref, gs_ref,
                       xe_s, ge_s, sem_x, sem_g):
    del xs_in, gs_in
    b = pl.program_id(0)
    t = h_ref.shape[0]
    rt = ROW_MOE
    ne = N_EXPERTS
    copies = []
    off = 0
    for e in range(ne):
        cnt = cnt_ref[b * ne + e]
        pos_row = pos_ref[0, e:e + 1, :]
        gate_row = gate_ref[0, e:e + 1, :]

        def gather(r, c, off=off, pos_row=pos_row, gate_row=gate_row):
            r0 = r * rt
            hit = (lax.broadcasted_iota(jnp.int32, (rt, t), 0) + r0) == pos_row
            onehot = jnp.where(hit, 1.0, 0.0).astype(BF16)
            dst = pl.multiple_of(off + r0, SEG_ALIGN)
            xe_s[pl.ds(dst, rt), :] = _dot(onehot, h_ref[...]).astype(BF16)
            g = jnp.sum(jnp.where(hit, gate_row, 0.0), axis=1, keepdims=True)
            ge_s[pl.ds(dst, rt), :] = jnp.broadcast_to(g, (rt, LANE))
            return c

        lax.fori_loop(0, (cnt + rt - 1) // rt, gather, 0)
        rows = ((cnt + SEG_ALIGN - 1) // SEG_ALIGN) * SEG_ALIGN
        base = base_ref[b * ne + e]
        seg = (_segment_copies(rows, xe_s, xs_ref, off, base, sem_x, e)
               + _segment_copies(rows, ge_s, gs_ref, off, base, sem_g, e))
        for pred, cp in seg:
            pl.when(pred)(cp.start)
        copies += seg
        off = off + rows
    for pred, cp in copies:
        pl.when(pred)(cp.wait)


def _moe_gather(h, pos, gate, counts, base, n_rows):
    n, d = h.shape
    t = T_MOE
    nb = n // t
    ne = N_EXPERTS
    tok = lambda b, c, o: (b, 0)
    blk3 = lambda b, c, o: (b, 0, 0)
    any_spec = pl.BlockSpec(memory_space=pl.ANY)
    xs0 = jnp.zeros((n_rows, d), BF16)
    gs0 = jnp.zeros((n_rows, LANE), F32)
    cap = -(-(2 * t + ne * (SEG_ALIGN - 1) + ROW_MOE - 1) // ROW_MOE) * ROW_MOE
    return pl.pallas_call(
        _moe_gather_kernel,
        out_shape=(jax.ShapeDtypeStruct((n_rows, d), BF16), jax.ShapeDtypeStruct((n_rows, LANE), F32)),
        grid_spec=pltpu.PrefetchScalarGridSpec(
            num_scalar_prefetch=2, grid=(nb,),
            in_specs=[pl.BlockSpec((t, d), tok), pl.BlockSpec((1, ne, t), blk3), pl.BlockSpec((1, ne, t), blk3),
                      any_spec, any_spec],
            out_specs=[any_spec, any_spec],
            scratch_shapes=[pltpu.VMEM((cap, d), BF16), pltpu.VMEM((cap, LANE), F32),
                            pltpu.SemaphoreType.DMA((ne * len(SEG_SIZES),)),
                            pltpu.SemaphoreType.DMA((ne * len(SEG_SIZES),))]),
        input_output_aliases={5: 0, 6: 1},
        compiler_params=_params(("arbitrary",), 32),
        name="moe_gather",
    )(counts, base, h, pos, gate, xs0, gs0)


def _stream_cast(src, dst):
    r, c = dst.shape
    rows = r // GMM_STAGE_PIECES
    slots = GMM_STAGE_SLOTS

    def run(stage, sems):
        def piece(k):
            return pltpu.make_async_copy(src.at[pl.ds(k * rows, rows)], stage.at[k % slots], sems.at[k % slots])

        for k in range(slots - 1):
            piece(k).start()
        for k in range(GMM_STAGE_PIECES):
            if k + slots - 1 < GMM_STAGE_PIECES:
                piece(k + slots - 1).start()
            piece(k).wait()
            dst[k * rows:(k + 1) * rows, :] = stage[k % slots].astype(BF16)

    pl.run_scoped(run, pltpu.VMEM((slots, rows, c), F32), pltpu.SemaphoreType.DMA((slots,)))


def _moe_gmm_kernel(te_ref, valid_ref, x_ref, gs_ref, wg_hbm, wu_hbm, wd_hbm, y_ref, wg_s, wu_s, wd_s, *, layer):
    i = pl.program_id(0)
    valid = valid_ref[i] != 0
    e = te_ref[i]
    switch = valid & ((i == 0) | (e != te_ref[jnp.maximum(i - 1, 0)]))

    @pl.when(switch)
    def _():
        _stream_cast(wg_hbm.at[layer, e], wg_s)
        _stream_cast(wu_hbm.at[layer, e], wu_s)
        _stream_cast(wd_hbm.at[layer, e], wd_s)

    @pl.when(valid)
    def _():
        x = x_ref[...]
        a = (_silu(_dot(x, wg_s[...])) * _dot(x, wu_s[...])).astype(BF16)
        y_ref[...] = (_dot(a, wd_s[...]) * gs_ref[:, 0:1]).astype(BF16)

    @pl.when(jnp.logical_not(valid))
    def _():
        y_ref[...] = jnp.zeros_like(y_ref)


def _moe_gmm(xs, gs, tile_expert, tile_valid, wg, wu, wd, layer):
    n_rows, d = xs.shape
    rt = RT_GMM
    ff = wg.shape[-1]
    tok = lambda i, te, tv: (i, 0)
    any_spec = pl.BlockSpec(memory_space=pl.ANY)
    return pl.pallas_call(
        functools.partial(_moe_gmm_kernel, layer=layer),
        out_shape=jax.ShapeDtypeStruct((n_rows, d), BF16),
        grid_spec=pltpu.PrefetchScalarGridSpec(
            num_scalar_prefetch=2, grid=(n_rows // rt,),
            in_specs=[pl.BlockSpec((rt, d), tok), pl.BlockSpec((rt, LANE), tok), any_spec, any_spec, any_spec],
            out_specs=pl.BlockSpec((rt, d), tok),
            scratch_shapes=[pltpu.VMEM((d, ff), BF16), pltpu.VMEM((d, ff), BF16), pltpu.VMEM((ff, d), BF16)]),
        compiler_params=_params(("arbitrary",), 56),
        name="moe_gmm",
    )(tile_expert, tile_valid, xs, gs, wg, wu, wd)


def _moe_combine_kernel(cnt_ref, base_ref, rows_ref, ys_ref, pt_ref, x_ref, g_ref, lg_ref, lb_ref, o_ref,
                        ycat_s, sems, *, alpha, block0):
    del rows_ref
    t = x_ref.shape[0]
    cap = ycat_s.shape[0]
    ne = N_EXPERTS

    @pl.when(pl.program_id(0) == 0)
    def _():
        ycat_s[...] = jnp.zeros_like(ycat_s)

    b = pl.program_id(0) + block0

    copies = []
    offs = []
    off = 0
    for e in range(ne):
        cnt = cnt_ref[b * ne + e]
        rows = ((cnt + SEG_ALIGN - 1) // SEG_ALIGN) * SEG_ALIGN
        copies += _segment_copies(rows, ys_ref, ycat_s, base_ref[b * ne + e], off, sems, e)
        offs.append(off)
        off = off + rows
    _run_copies(copies)

    pt = pt_ref[...]
    lane = lax.broadcasted_iota(jnp.int32, (1, LANE), 1)
    off_lane = jnp.zeros((1, LANE), F32)
    for e in range(1, ne):
        off_lane = jnp.where(lane == e, offs[e].astype(F32), off_lane)
    chosen = (pt >= 0.0) & (lax.broadcasted_iota(jnp.int32, (t, LANE), 1) < ne)
    slot = pt + off_lane
    s_lo = jnp.min(jnp.where(chosen, slot, float(cap)), axis=1, keepdims=True)
    s_hi = jnp.max(jnp.where(chosen, slot, -1.0), axis=1, keepdims=True)
    n_piece = 3
    wp = cap // n_piece
    f = None
    for j in range(n_piece):
        col = (lax.broadcasted_iota(jnp.int32, (t, wp), 1) + j * wp).astype(F32)
        sel = jnp.where(col == s_lo, 1.0, jnp.where(col == s_hi, 1.0, 0.0)).astype(BF16)
        part = _dot(sel, ycat_s[j * wp:(j + 1) * wp, :])
        f = part if f is None else f + part
    o_ref[...] = _layer_norm(alpha * x_ref[...] + g_ref[0] * f, lg_ref[...], lb_ref[...])


def _moe_combine(ys, pt, x, g2, ln_g, ln_b, counts, base, rows, alpha, block0=0, n_blocks=None):
    n, d = x.shape
    t = T_MOE
    ne = N_EXPERTS
    n_blocks = n // t if n_blocks is None else n_blocks
    cap = -(-(2 * t + ne * (SEG_ALIGN - 1)) // (6 * LANE)) * (6 * LANE)
    tok = lambda b, c, o, r: (b + block0, 0)
    const = lambda b, c, o, r: (0, 0)
    g2, g2_spec = _mod_operand(g2, lambda b, c, o, r: r[b + block0])
    return pl.pallas_call(
        functools.partial(_moe_combine_kernel, alpha=alpha, block0=block0),
        out_shape=jax.ShapeDtypeStruct((n_blocks * t, d), F32),
        grid_spec=pltpu.PrefetchScalarGridSpec(
            num_scalar_prefetch=3, grid=(n_blocks,),
            in_specs=[pl.BlockSpec(memory_space=pl.ANY), pl.BlockSpec((t, LANE), tok), pl.BlockSpec((t, d), tok),
                      g2_spec,
                      pl.BlockSpec((1, d), const), pl.BlockSpec((1, d), const)],
            out_specs=pl.BlockSpec((t, d), lambda b, c, o, r: (b, 0)),
            scratch_shapes=[pltpu.VMEM((cap, d), BF16), pltpu.SemaphoreType.DMA((ne * len(SEG_SIZES),))]),
        compiler_params=_params(("arbitrary",), 48),
        name="moe_combine",
    )(counts, base, rows, ys, pt, x, g2, ln_g, ln_b)


def _moe_layout(cnt, n_tiles):
    cnt16 = (cnt + SEG_ALIGN - 1) // SEG_ALIGN * SEG_ALIGN
    rows_e = jnp.sum(cnt16, axis=0)
    rows_e = (rows_e + RT_GMM - 1) // RT_GMM * RT_GMM
    ends = jnp.cumsum(rows_e)
    base = (ends - rows_e)[None, :] + jnp.cumsum(cnt16, axis=0) - cnt16
    tile_end = ends // RT_GMM
    idx = jnp.arange(n_tiles, dtype=jnp.int32)
    valid = idx < tile_end[-1]
    te = jnp.sum((idx[:, None] >= tile_end[None, :]).astype(jnp.int32), axis=1)
    last = jnp.sum(((tile_end[-1] - 1) >= tile_end).astype(jnp.int32))
    te = jnp.where(valid, te, last)
    return base.reshape(-1).astype(jnp.int32), te.astype(jnp.int32), valid.astype(jnp.int32)


def _tile_rows(n, p, ss, tm):
    starts = np.arange(n // tm) * tm
    return jnp.asarray(np.where(starts < p, 0, 1 + (starts - p) // ss).astype(np.int32))


def _tile_rope_blocks(n, p, ss, tm):
    starts = np.arange(n // tm) * tm
    return jnp.asarray(np.where(starts < p, 0, 1 + ((starts - p) % ss) // tm).astype(np.int32))


def _tile_edges(n, p, sp, ss, tm):
    starts = np.arange(n // tm) * tm
    pos = np.where(starts < p, starts % sp, (starts - p) % ss)
    seq = np.where(starts < p, sp, ss)
    return jnp.asarray((pos == 0).astype(np.int32)), jnp.asarray(((pos + tm) % seq == 0).astype(np.int32))


def _rope_tables(n_tokens, rot_dim, lane0, tm):
    rows = n_tokens // GRID_W
    row_pos = jnp.repeat(jnp.arange(rows, dtype=F32), GRID_W)
    col_pos = jnp.tile(jnp.arange(GRID_W, dtype=F32), rows)
    half = rot_dim // 2
    qtr = half // 2
    inv_freq = ROPE_THETA ** (-jnp.arange(0, half, 2, dtype=F32) / half)
    ang_r = row_pos[:, None] * inv_freq
    ang_c = col_pos[:, None] * inv_freq
    ang = jnp.concatenate([ang_r, ang_r, ang_c, ang_c], axis=-1)
    cos, sin = jnp.cos(ang), jnp.sin(ang)
    first = ((np.arange(rot_dim) % half) < qtr).astype(np.float32)
    sa = -sin * first
    sb = sin * (1.0 - first)

    def place(tbl, fill):
        full = jnp.full((n_tokens, LANE), fill, F32).at[:, lane0:lane0 + rot_dim].set(tbl)
        return jnp.concatenate([jnp.full((tm, LANE), fill, F32), full], axis=0)

    return place(cos, 1.0), place(sa, 0.0), place(sb, 0.0)


def _pad_heads(w, n_heads, width):
    lead = w.shape[:-1]
    w = w.reshape(lead + (n_heads, width))
    w = jnp.pad(w, [(0, 0)] * len(lead) + [(0, 0), (0, LANE - width)])
    return w.reshape(lead + (n_heads * LANE,))


def _pad_head_rows(w, n_heads, width):
    d = w.shape[-1]
    w = w.reshape(n_heads, width, d)
    return jnp.pad(w, ((0, 0), (0, LANE - width), (0, 0))).reshape(n_heads * LANE, d)


def kernel(x_prompt, x_sample, cache_attn_k, cache_attn_v, cache_mla_ckv, cache_mla_kpe, state_ssm_re, state_ssm_im, c, c_ctx, ada_w, ada_b, ln_g, ln_b, ev_w_in, ev_conv_w, ev_q_gain, ev_k_gain, ev_w_out, ffn_w_gate, ffn_w_up, ffn_w_down, od_w_in, s5_a_re, s5_a_im, s5_log_dt, s5_b_re, s5_b_im, s5_c_re, s5_c_im, s5_d, s5_w_glu, s5_b_glu, mla_q_gain, mla_w_uq, mla_kv_gain, mla_w_ukv, od_w_out, moe_router, moe_w_gate, moe_w_up, moe_w_down):
    bp, sp, d = x_prompt.shape
    bs, ss, _ = x_sample.shape
    depth = ada_w.shape[0]
    alpha = (2 * depth) ** 0.25
    p = bp * sp
    n = p + bs * ss
    past = cache_attn_k.shape[2]
    for tm in (TM_PROJ, TM_EVEN_IN, TM_FFN, T_MOE):
        assert p % tm == 0 and ss % tm == 0 and (sp % tm == 0 or tm % sp == 0)
    assert sp % S5_CHUNK == 0 and ss % S5_CHUNK == 0

    x = (x_prompt.reshape(p, d), x_sample.reshape(bs * ss, d))

    nrow = -(-(1 + bs) // SUBLANE) * SUBLANE
    cond = jnp.zeros((nrow, d), F32).at[0].set(c_ctx).at[1:1 + bs].set(c)
    mod = _modulation(cond, ada_w, ada_b)

    def mod_part(l, k):
        return (mod, l, k)

    rows_proj = _tile_rows(n, p, ss, TM_PROJ)
    rows_ffn = _tile_rows(n, p, ss, TM_FFN)
    rows_moe = _tile_rows(n, p, ss, T_MOE)
    rblk = _tile_rope_blocks(n, p, ss, TM_PROJ)
    first, last = _tile_edges(n, p, sp, ss, TM_PROJ)
    rows_even = _tile_rows(n, p, ss, TM_EVEN_IN)
    rblk_even = _tile_rope_blocks(n, p, ss, TM_EVEN_IN)
    tables_even = _rope_tables(ss, HEAD_DIM, 0, TM_EVEN_IN)
    tables_mla = _rope_tables(ss, MLA_ROPE, MLA_NOPE, TM_PROJ)

    s5_mats = jax.vmap(_s5_matrices)(s5_a_re, s5_a_im, s5_log_dt, s5_b_re, s5_b_im, s5_c_re, s5_c_im)
    ffn_wg, ffn_wu, ffn_wd = ffn_w_gate.astype(BF16), ffn_w_up.astype(BF16), ffn_w_down.astype(BF16)
    out_k, out_v, out_ckv, out_kpe, out_sre, out_sim = [], [], [], [], [], []
    cw = CONV_WIDTH
    hq = ATTN_HEADS * HEAD_DIM
    hkv = ATTN_KV_HEADS * HEAD_DIM
    for l in range(depth):
        i = l // 2
        sh1, sc1, g1, sh2, sc2, g2 = [mod_part(l, k) for k in range(6)]
        lg = ln_g[l].reshape(2, 1, d)
        lb = ln_b[l].reshape(2, 1, d)
        if l % 2 == 0:
            w_in = ev_w_in[i]
            wc = w_in[:, :3 * cw].astype(BF16)
            wq = _pad_heads(w_in[:, 3 * cw:3 * cw + hq], ATTN_HEADS, HEAD_DIM).astype(BF16)
            wkv = jnp.concatenate([_pad_heads(w_in[:, 3 * cw + hq:3 * cw + hq + hkv], ATTN_KV_HEADS, HEAD_DIM),
                                   _pad_heads(w_in[:, 3 * cw + hq + hkv:], ATTN_KV_HEADS, HEAD_DIM)],
                                  axis=1).astype(BF16)
            qg = jnp.pad(ev_q_gain[i], (0, LANE - HEAD_DIM)).reshape(1, LANE)
            kg = jnp.pad(ev_k_gain[i], (0, LANE - HEAD_DIM)).reshape(1, LANE)
            conv, q, k, v, kn, vf = _even_in(x, sc1, sh1, wc, wq, wkv, qg, kg, tables_even, rows_even, rblk_even)
            kvw = ATTN_KV_HEADS * LANE
            out_k.append(kn[:p].reshape(bp, sp, ATTN_KV_HEADS, LANE)[..., :HEAD_DIM])
            out_v.append(vf[:p].reshape(bp, sp, ATTN_KV_HEADS, LANE)[..., :HEAD_DIM])
            group = ATTN_HEADS // ATTN_KV_HEADS
            qw = ATTN_HEADS * LANE
            a_p = _attention(q, [(k, v, True)], ATTN_HEADS, group, bp, sp, 0)
            kc = _pad_heads(cache_attn_k[:, i].reshape(bs, past, hkv), ATTN_KV_HEADS, HEAD_DIM).astype(BF16)
            vc = _pad_heads(cache_attn_v[:, i].reshape(bs, past, hkv), ATTN_KV_HEADS, HEAD_DIM).astype(BF16)
            a_s = _attention(q, [(kc, vc, False), (k, v, True)], ATTN_HEADS, group, bs, ss, p)
            w_out = jnp.concatenate([ev_w_out[i][:cw], _pad_head_rows(ev_w_out[i][cw:], ATTN_HEADS, HEAD_DIM)],
                                    axis=0).astype(BF16)
            conv_w = jnp.pad(ev_conv_w[i], ((0, SUBLANE - ev_conv_w.shape[1]), (0, 0)))
            x = _even_out(conv, a_p.reshape(p, qw), a_s.reshape(bs * ss, qw), x, g1, conv_w, w_out, lg[0], lb[0],
                          first, last, rows_proj, alpha, sp)
            x = _ffn(x, sc2, sh2, g2, ffn_wg, ffn_wu, ffn_wd, i, lg[1], lb[1], rows_ffn, alpha)
        else:
            q_end = S5_WIDTH + MLA_Q_RANK
            kv_end = q_end + MLA_KV_RANK
            w_in = od_w_in[i]
            w_in = jnp.concatenate([w_in[:, :kv_end], jnp.zeros((d, MLA_NOPE), F32), w_in[:, kv_end:],
                                    jnp.zeros((d, LANE - MLA_NOPE - MLA_ROPE), F32)], axis=1).astype(BF16)
            m_ctx, m_lat = p // S5_CHUNK, bs * ss // S5_CHUNK
            ctx_chunk0, lat_chunk0 = (m_lat, 0) if m_lat >= m_ctx else (0, m_ctx)
            assert ctx_chunk0 % m_ctx == 0 and lat_chunk0 % m_lat == 0
            proj, u = _odd_in(x, sc1, sh1, w_in, rows_proj, p // TM_PROJ, ctx_chunk0, lat_chunk0)
            out_kpe.append(proj[:p, kv_end + MLA_NOPE:kv_end + MLA_NOPE + MLA_ROPE].reshape(bp, sp, MLA_ROPE))

            h0_p = jnp.zeros((S5_GROUPS, bp, 4 * S5_STATE), F32)
            y_p, fin_p = _s5(u, s5_mats, i, h0_p, bp, sp // S5_CHUNK, ctx_chunk0 // m_ctx)

            def pack_state(re, im):
                to = lambda a: a.transpose(2, 0, 1, 3).reshape(S5_GROUPS, a.shape[0], 2 * S5_STATE)
                return jnp.concatenate([to(re), to(im)], axis=-1)

            y_s, _ = _s5(u, s5_mats, i, pack_state(state_ssm_re[:, i], state_ssm_im[:, i]), bs, ss // S5_CHUNK,
                         lat_chunk0 // m_lat)
            fin = fin_p.reshape(S5_GROUPS, bp, 2, 2, S5_STATE)
            out_sre.append(fin[:, :, 0].transpose(1, 2, 0, 3))
            out_sim.append(fin[:, :, 1].transpose(1, 2, 0, 3))

            qk = MLA_NOPE + MLA_ROPE
            wuq = _pad_heads(mla_w_uq[i], MLA_HEADS, qk).astype(BF16)
            wukv = mla_w_ukv[i].reshape(MLA_KV_RANK, MLA_HEADS, MLA_NOPE + MLA_V)
            wk = _pad_heads(wukv[..., :MLA_NOPE].reshape(MLA_KV_RANK, -1), MLA_HEADS, MLA_NOPE).astype(BF16)
            wv = _pad_heads(wukv[..., MLA_NOPE:].reshape(MLA_KV_RANK, -1), MLA_HEADS, MLA_V).astype(BF16)
            place = np.zeros((LANE, MLA_HEADS * LANE), np.float32)
            for hd in range(MLA_HEADS):
                for j in range(MLA_ROPE):
                    place[MLA_NOPE + j, hd * LANE + MLA_NOPE + j] = 1.0
            place = jnp.asarray(place, BF16)
            q, k, v, ckv = _mla_prep(proj, mla_q_gain[i].reshape(1, -1), mla_kv_gain[i].reshape(1, -1),
                                     wuq, wk, wv, place, tables_mla, rblk)
            out_ckv.append(ckv[:p].reshape(bp, sp, MLA_KV_RANK))
            hw = MLA_HEADS * LANE
            a_p = _attention(q, [(k, v, True)], MLA_HEADS, 1, bp, sp, 0)
            kpe_c = jnp.pad(cache_mla_kpe[:, i].reshape(bs * past, MLA_ROPE),
                            ((0, 0), (MLA_NOPE, LANE - MLA_NOPE - MLA_ROPE)))
            kc, vc = _mla_cache(cache_mla_ckv[:, i].reshape(bs * past, MLA_KV_RANK), kpe_c, wk, wv, place)
            a_s = _attention(q, [(kc.reshape(bs, past, hw), vc.reshape(bs, past, hw), False), (k, v, True)],
                             MLA_HEADS, 1, bs, ss, p)
            w_out = jnp.concatenate([od_w_out[i][:S5_WIDTH], _pad_head_rows(od_w_out[i][S5_WIDTH:], MLA_HEADS, MLA_V)],
                                    axis=0).astype(BF16)
            x = _odd_out(proj, y_p, y_s, a_p.reshape(p, hw), a_s.reshape(bs * ss, hw), x, g1, s5_d[i].reshape(1, -1),
                         s5_w_glu[i].astype(BF16), s5_b_glu[i].reshape(1, -1), w_out, lg[0], lb[0], rows_proj, alpha)

            hb, pos, gate, pt, cnt = _router(x, sc2, sh2, moe_router[i].T, rows_moe)
            cnt = cnt[:, :, 0]
            nblk = n // T_MOE
            n_tiles = -(-(2 * n + nblk * N_EXPERTS * (SEG_ALIGN - 1)) // RT_GMM) + N_EXPERTS
            base, tile_expert, tile_valid = _moe_layout(cnt, n_tiles)
            counts = cnt.reshape(-1)
            xs, gs = _moe_gather(hb, pos, gate, counts, base, n_tiles * RT_GMM)
            ys = _moe_gmm(xs, gs, tile_expert, tile_valid, moe_w_gate, moe_w_up, moe_w_down, i)
            if l == depth - 1:
                nb_ctx = p // T_MOE
                x = tuple(_moe_combine(ys, pt, x, g2, lg[1], lb[1], counts, base, rows_moe, alpha, b0, nbk)
                          for b0, nbk in ((0, nb_ctx), (nb_ctx, nblk - nb_ctx)))
            else:
                x = _moe_combine(ys, pt, x, g2, lg[1], lb[1], counts, base, rows_moe, alpha)

    if not isinstance(x, tuple):
        x = (x[:p], x[p:])
    y_prompt = x[0].reshape(bp, sp, d)
    y_sample = x[1].reshape(bs, ss, d)
    return (y_prompt, y_sample, jnp.stack(out_k, axis=1), jnp.stack(out_v, axis=1),
            jnp.stack(out_ckv, axis=1), jnp.stack(out_kpe, axis=1),
            jnp.stack(out_sre, axis=1), jnp.stack(out_sim, axis=1))
```

```python
import functools
import math

import jax
import jax.numpy as jnp
import numpy as np
from jax import lax
from jax.experimental import pallas as pl
from jax.experimental.pallas import tpu as pltpu

F32 = jnp.float32
BF16 = jnp.bfloat16

LANE = 128
SUBLANE = 8
MIB = 1024 * 1024

GRID_W = 64
ROPE_THETA = 10000.0
LN_EPS = 1e-6
RMS_EPS = 1e-6
HEAD_DIM = 64
ATTN_HEADS = 8
ATTN_KV_HEADS = 2
CONV_WIDTH = 512
S5_WIDTH = 512
S5_GROUP = 16
S5_GROUPS = 32
S5_STATE = 64
S5_CHUNK = 16
MLA_HEADS = 8
MLA_Q_RANK = 256
MLA_KV_RANK = 128
MLA_NOPE = 64
MLA_ROPE = 32
MLA_V = 64
N_EXPERTS = 8

TM_PROJ = 512
TM_EVEN_IN = 256
TM_FFN = 512
FFN_CHUNK = 2816
T_MOE = 1024
ROW_MOE = 128
RT_GMM = 512
GMM_STAGE_PIECES = 16
GMM_STAGE_SLOTS = 6
SEG_ALIGN = 16
SEG_SIZES = tuple(T_MOE >> k for k in range(T_MOE.bit_length()) if (T_MOE >> k) >= SEG_ALIGN)
TQ_ATTN = 512


def _params(sem, vmem_mib):
    return pltpu.CompilerParams(dimension_semantics=sem, vmem_limit_bytes=vmem_mib * MIB)


def _dot(a, b):
    return jnp.dot(a, b, preferred_element_type=F32)


def _dot_nt(a, b):
    return lax.dot_general(a, b, (((1,), (1,)), ((), ())), preferred_element_type=F32)


def _split(a):
    hi = a.astype(BF16)
    lo = (a - hi.astype(F32)).astype(BF16)
    return hi, lo


def _silu(x):
    return x * jax.nn.sigmoid(x)


def _layer_norm(r, g, b):
    mu = jnp.mean(r, axis=-1, keepdims=True)
    d = r - mu
    var = jnp.mean(d * d, axis=-1, keepdims=True)
    return d * lax.rsqrt(var + LN_EPS) * g + b


def _rms(x, g, n):
    ms = jnp.sum(x * x, axis=-1, keepdims=True) * (1.0 / n)
    return x * lax.rsqrt(ms + RMS_EPS) * g


def _rope(x, cos, sa, sb, q):
    w = x.shape[-1]
    return x * cos + pltpu.roll(x, w - q, 1) * sa + pltpu.roll(x, q, 1) * sb


def _ada_kernel(c_ref, w_ref, b_ref, o_ref):
    c = c_ref[...]
    a_hi, a_lo = _split(_silu(c))
    w_hi, w_lo = _split(w_ref[0])
    o_ref[0] = _dot(a_hi, w_hi) + _dot(a_lo, w_hi) + _dot(a_hi, w_lo) + b_ref[0]


def _modulation(cond, ada_w, ada_b):
    depth, d, d6 = ada_w.shape
    r = cond.shape[0]
    out = pl.pallas_call(
        _ada_kernel,
        out_shape=jax.ShapeDtypeStruct((depth, d6 // d, r, d), F32),
        grid=(depth, d6 // d),
        in_specs=[pl.BlockSpec((r, d), lambda l, j: (0, 0)),
                  pl.BlockSpec((1, d, d), lambda l, j: (l, 0, j)),
                  pl.BlockSpec((1, 1, d), lambda l, j: (l, 0, j))],
        out_specs=pl.BlockSpec((None, 1, r, d), lambda l, j: (l, j, 0, 0)),
        compiler_params=_params(("arbitrary", "arbitrary"), 40),
        name="ada_modulation",
    )(cond, ada_w, ada_b.reshape(depth, 1, d6))
    return out.reshape(depth, d6 // d, r, 1, d)


def _mod_operand(m, row_fn):
    arr, layer, part = m
    return arr, pl.BlockSpec((None, None, 1, 1, arr.shape[-1]), lambda *a: (layer, part, row_fn(*a), 0, 0))


def _load_x(x_refs, n_ctx_tiles):
    if len(x_refs) == 1:
        return x_refs[0][...]
    return _pick_pass(pl.program_id(0), n_ctx_tiles, *x_refs)


def _x_specs(x, tm, tok):
    if not isinstance(x, tuple):
        return (x,), [pl.BlockSpec((tm, x.shape[1]), tok)], 0
    n_ctx, n_lat = x[0].shape[0] // tm, x[1].shape[0] // tm
    return x, list(_pass_specs((tm, x[0].shape[1]), n_ctx, n_lat)), n_ctx


def _even_in_kernel(rows_ref, rblk_ref, *refs, n_x, n_ctx_tiles):
    (sc_ref, sh_ref, wc_ref, wq_ref, wkv_ref, qg_ref, kg_ref, cos_ref, sa_ref, sb_ref,
     conv_ref, q_ref, k_ref, v_ref, kn_ref, vf_ref) = refs[n_x:]
    del rows_ref, rblk_ref
    h = (_load_x(refs[:n_x], n_ctx_tiles) * (1.0 + sc_ref[0]) + sh_ref[0]).astype(BF16)
    conv_ref[...] = _dot(h, wc_ref[...])
    cos, sa, sb = cos_ref[...], sa_ref[...], sb_ref[...]
    quarter = HEAD_DIM // 4
    q = _dot(h, wq_ref[...])
    for hd in range(ATTN_HEADS):
        sl = slice(hd * LANE, (hd + 1) * LANE)
        qn = _rms(q[:, sl], qg_ref[...], HEAD_DIM)
        q_ref[:, sl] = (_rope(qn, cos, sa, sb, quarter) * (HEAD_DIM ** -0.5)).astype(BF16)
    kv = _dot(h, wkv_ref[...])
    kw = ATTN_KV_HEADS * LANE
    for hd in range(ATTN_KV_HEADS):
        sl = slice(hd * LANE, (hd + 1) * LANE)
        kn = _rms(kv[:, sl], kg_ref[...], HEAD_DIM)
        kn_ref[:, sl] = kn
        k_ref[:, sl] = _rope(kn, cos, sa, sb, quarter).astype(BF16)
    v = kv[:, kw:]
    vf_ref[...] = v
    v_ref[...] = v.astype(BF16)


def _even_in(x, sc, sh, wc, wq, wkv, qg, kg, tables, rows, rblk):
    tm = TM_EVEN_IN
    d = wc.shape[0]
    n = rows.shape[0] * tm
    cw, qw, kvw = wc.shape[1], wq.shape[1], wkv.shape[1]
    kw = kvw // 2
    cos, sa, sb = tables
    tok = lambda i, rows, rblk: (i, 0)
    const = lambda i, rows, rblk: (0, 0)
    tab = lambda i, rows, rblk: (rblk[i], 0)
    xs, x_specs, n_ctx = _x_specs(x, tm, tok)
    row = lambda i, rows, rblk: rows[i]
    (sc, sc_spec), (sh, sh_spec) = _mod_operand(sc, row), _mod_operand(sh, row)
    return pl.pallas_call(
        functools.partial(_even_in_kernel, n_x=len(xs), n_ctx_tiles=n_ctx),
        out_shape=(jax.ShapeDtypeStruct((n, cw), F32), jax.ShapeDtypeStruct((n, qw), BF16),
                   jax.ShapeDtypeStruct((n, kw), BF16), jax.ShapeDtypeStruct((n, kw), BF16),
                   jax.ShapeDtypeStruct((n, kw), F32), jax.ShapeDtypeStruct((n, kw), F32)),
        grid_spec=pltpu.PrefetchScalarGridSpec(
            num_scalar_prefetch=2, grid=(n // tm,),
            in_specs=x_specs + [
                      sc_spec, sh_spec,
                      pl.BlockSpec((d, cw), const), pl.BlockSpec((d, qw), const), pl.BlockSpec((d, kvw), const),
                      pl.BlockSpec((1, LANE), const), pl.BlockSpec((1, LANE), const),
                      pl.BlockSpec((tm, LANE), tab), pl.BlockSpec((tm, LANE), tab), pl.BlockSpec((tm, LANE), tab)],
            out_specs=[pl.BlockSpec((tm, cw), tok), pl.BlockSpec((tm, qw), tok),
                       pl.BlockSpec((tm, kw), tok), pl.BlockSpec((tm, kw), tok),
                       pl.BlockSpec((tm, kw), tok), pl.BlockSpec((tm, kw), tok)]),
        compiler_params=_params(("arbitrary",), 40),
        name="even_in_proj",
    )(rows, rblk, *xs, sc, sh, wc, wq, wkv, qg, kg, cos, sa, sb)


def _attn_kernel(*refs, n_heads, group, n_seg):
    q_ref = refs[0]
    seg = refs[1:1 + 2 * n_seg]
    o_ref = refs[1 + 2 * n_seg]
    for hd in range(n_heads):
        sl = slice(hd * LANE, (hd + 1) * LANE)
        ks = slice((hd // group) * LANE, (hd // group + 1) * LANE)
        qh = q_ref[:, sl]
        scores = [_dot_nt(qh, seg[2 * s][:, ks]) for s in range(n_seg)]
        m = jnp.max(scores[0], axis=-1, keepdims=True)
        for s in range(1, n_seg):
            m = jnp.maximum(m, jnp.max(scores[s], axis=-1, keepdims=True))
        den = None
        acc = None
        for s in range(n_seg):
            p = jnp.exp(scores[s] - m)
            ps = jnp.sum(p, axis=-1, keepdims=True)
            pv = _dot(p.astype(BF16), seg[2 * s + 1][:, ks])
            den = ps if den is None else den + ps
            acc = pv if acc is None else acc + pv
        o_ref[:, sl] = (acc * (1.0 / den)).astype(BF16)


def _attention(q, segments, n_heads, group, b, s, row0):
    n, qw = q.shape
    assert n % s == 0 and row0 % s == 0
    b0 = row0 // s
    tq = min(TQ_ATTN, s)
    in_specs = [pl.BlockSpec((None, tq, qw), lambda i, j: (i + b0, j, 0))]
    args = [q.reshape(n // s, s, qw)]
    for k, v, own in segments:
        kw = k.shape[-1]
        if own:
            k, v = k.reshape(n // s, s, kw), v.reshape(n // s, s, kw)
            kmap = lambda i, j: (i + b0, 0, 0)
        else:
            kmap = lambda i, j: (i, 0, 0)
        t = k.shape[1]
        in_specs += [pl.BlockSpec((None, t, kw), kmap, pipeline_mode=pl.Buffered(1)),
                     pl.BlockSpec((None, t, kw), kmap, pipeline_mode=pl.Buffered(1))]
        args += [k, v]
    return pl.pallas_call(
        functools.partial(_attn_kernel, n_heads=n_heads, group=group, n_seg=len(segments)),
        out_shape=jax.ShapeDtypeStruct((b, s, qw), BF16),
        grid=(b, s // tq),
        in_specs=in_specs,
        out_specs=pl.BlockSpec((None, tq, qw), lambda i, j: (i, j, 0)),
        compiler_params=_params(("arbitrary", "arbitrary"), 56),
        name="attention",
    )(*args)


def _pick_pass(i, n_ctx_tiles, ctx_ref, lat_ref):
    return jnp.where(i < n_ctx_tiles, ctx_ref[...], lat_ref[...])


def _pass_specs(block, n_ctx_tiles, n_lat_tiles):
    lead = (0,) * (len(block) - 2)

    def ctx(i, *_):
        return lead + (jnp.minimum(i, n_ctx_tiles - 1), 0)

    def lat(i, *_):
        return lead + (jnp.clip(i - n_ctx_tiles, 0, n_lat_tiles - 1), 0)

    return pl.BlockSpec(block, ctx), pl.BlockSpec(block, lat)


def _even_out_kernel(first_ref, last_ref, rows_ref, c_ref, cp_ref, cn_ref, ap_ref, as_ref, *refs,
                     alpha, n_ctx_tiles, ctx_seq, n_x):
    g_ref, cw_ref, w_ref, lg_ref, lb_ref, o_ref = refs[n_x:]
    del rows_ref
    i = pl.program_id(0)
    attn = _pick_pass(i, n_ctx_tiles, ap_ref, as_ref)
    tm = c_ref.shape[0]
    cwid = CONV_WIDTH
    c = c_ref[...]
    gate_b, z = c[:, :cwid], c[:, cwid:2 * cwid] * c[:, 2 * cwid:]
    cp = cp_ref[...]
    cn = cn_ref[...]
    zp = cp[SUBLANE - 1:SUBLANE, cwid:2 * cwid] * cp[SUBLANE - 1:SUBLANE, 2 * cwid:]
    zn = cn[0:1, cwid:2 * cwid] * cn[0:1, 2 * cwid:]
    zp = zp * (1 - first_ref[i]).astype(F32)
    zn = zn * (1 - last_ref[i]).astype(F32)
    row = lax.broadcasted_iota(jnp.int32, (tm, cwid), 0)
    z_prev = jnp.where(row == 0, zp, pltpu.roll(z, 1, 0))
    z_next = jnp.where(row == tm - 1, zn, pltpu.roll(z, tm - 1, 0))
    for k in range(1, tm // ctx_seq):
        edge = jnp.where(i < n_ctx_tiles, k * ctx_seq, -1)
        z_prev = jnp.where(row == edge, 0.0, z_prev)
        z_next = jnp.where(row == edge - 1, 0.0, z_next)
    cw = cw_ref[...]
    y = gate_b * (cw[0:1] * z_prev + cw[1:2] * z + cw[2:3] * z_next)
    out = _dot(y.astype(BF16), w_ref[:cwid, :]) + _dot(attn, w_ref[cwid:, :])
    x = _load_x(refs[:n_x], n_ctx_tiles)
    o_ref[...] = _layer_norm(alpha * x + g_ref[0] * out, lg_ref[...], lb_ref[...])


def _even_out(conv, attn_p, attn_s, x, g1, conv_w, w_out, ln_g, ln_b, first, last, rows, alpha, ctx_seq):
    tm = TM_PROJ
    n, d = conv.shape[0], w_out.shape[1]
    cw3 = conv.shape[1]
    aw = attn_p.shape[1]
    hb = tm // SUBLANE
    nblk8 = n // SUBLANE
    n_ctx, n_lat = attn_p.shape[0] // tm, attn_s.shape[0] // tm
    ap_spec, as_spec = _pass_specs((tm, aw), n_ctx, n_lat)
    tok = lambda i, f, l, r: (i, 0)
    const = lambda i, f, l, r: (0, 0)
    xs, x_specs, _ = _x_specs(x, tm, tok)
    g1, g1_spec = _mod_operand(g1, lambda i, f, l, r: r[i])
    return pl.pallas_call(
        functools.partial(_even_out_kernel, alpha=alpha, n_ctx_tiles=n_ctx, ctx_seq=ctx_seq, n_x=len(xs)),
        out_shape=jax.ShapeDtypeStruct((n, d), F32),
        grid_spec=pltpu.PrefetchScalarGridSpec(
            num_scalar_prefetch=3, grid=(n // tm,),
            in_specs=[pl.BlockSpec((tm, cw3), tok),
                      pl.BlockSpec((SUBLANE, cw3), lambda i, f, l, r: (jnp.maximum(i * hb - 1, 0), 0)),
                      pl.BlockSpec((SUBLANE, cw3), lambda i, f, l, r: (jnp.minimum((i + 1) * hb, nblk8 - 1), 0)),
                      ap_spec, as_spec] + x_specs + [
                      g1_spec,
                      pl.BlockSpec((SUBLANE, CONV_WIDTH), const),
                      pl.BlockSpec((CONV_WIDTH + aw, d), const),
                      pl.BlockSpec((1, d), const), pl.BlockSpec((1, d), const)],
            out_specs=pl.BlockSpec((tm, d), tok)),
        compiler_params=_params(("arbitrary",), 40),
        name="even_out_proj",
    )(first, last, rows, conv, conv, conv, attn_p, attn_s, *xs, g1, conv_w, w_out, ln_g, ln_b)


def _ffn_kernel(rows_ref, x_ref, sc_ref, sh_ref, g_ref, wg_ref, wu_ref, wd_ref, lg_ref, lb_ref, o_ref,
                h_s, acc_s, *, alpha):
    del rows_ref
    f = pl.program_id(1)

    @pl.when(f == 0)
    def _():
        h_s[...] = (x_ref[...] * (1.0 + sc_ref[0]) + sh_ref[0]).astype(BF16)

    h = h_s[...]
    a = (_silu(_dot(h, wg_ref[...])) * _dot(h, wu_ref[...])).astype(BF16)
    y = _dot(a, wd_ref[...])

    @pl.when(f == 0)
    def _():
        acc_s[...] = y

    @pl.when(f > 0)
    def _():
        acc_s[...] += y

    @pl.when(f == pl.num_programs(1) - 1)
    def _():
        o_ref[...] = _layer_norm(alpha * x_ref[...] + g_ref[0] * acc_s[...], lg_ref[...], lb_ref[...])


def _ffn(x, sc, sh, g2, wg, wu, wd, layer, ln_g, ln_b, rows, alpha):
    n, d = x.shape
    tm = TM_FFN
    ff = wg.shape[-1]
    fc = FFN_CHUNK
    tok = lambda i, f, r: (i, 0)
    const = lambda i, f, r: (0, 0)
    wmode = dict(pipeline_mode=pl.Buffered(1)) if fc == ff else {}
    row = lambda i, f, r: r[i]
    (sc, sc_spec), (sh, sh_spec), (g2, g2_spec) = (_mod_operand(m, row) for m in (sc, sh, g2))
    return pl.pallas_call(
        functools.partial(_ffn_kernel, alpha=alpha),
        out_shape=jax.ShapeDtypeStruct((n, d), F32),
        grid_spec=pltpu.PrefetchScalarGridSpec(
            num_scalar_prefetch=1, grid=(n // tm, ff // fc),
            in_specs=[pl.BlockSpec((tm, d), tok),
                      sc_spec, sh_spec, g2_spec,
                      pl.BlockSpec((None, d, fc), lambda i, f, r: (layer, 0, f), **wmode),
                      pl.BlockSpec((None, d, fc), lambda i, f, r: (layer, 0, f), **wmode),
                      pl.BlockSpec((None, fc, d), lambda i, f, r: (layer, f, 0), **wmode),
                      pl.BlockSpec((1, d), const), pl.BlockSpec((1, d), const)],
            out_specs=pl.BlockSpec((tm, d), tok),
            scratch_shapes=[pltpu.VMEM((tm, d), BF16), pltpu.VMEM((tm, d), F32)]),
        compiler_params=_params(("arbitrary", "arbitrary"), 56),
        name="ffn_swiglu",
    )(rows, x, sc, sh, g2, wg, wu, wd, ln_g, ln_b)


def _odd_in_kernel(rows_ref, x_ref, sc_ref, sh_ref, w_ref, o_ref, u_ref, u_s):
    del rows_ref
    h = (x_ref[...] * (1.0 + sc_ref[0]) + sh_ref[0]).astype(BF16)
    proj = _dot(h, w_ref[...])
    o_ref[...] = proj
    n_tile = S5_WIDTH // LANE
    gpt = LANE // S5_GROUP
    for j in range(n_tile):
        u_s[j] = proj[:, j * LANE:(j + 1) * LANE]
    n_chunk = u_s.shape[1] // S5_CHUNK
    steps = [[u_s[j, pl.ds(s, n_chunk, stride=S5_CHUNK), :] for j in range(n_tile)] for s in range(S5_CHUNK)]
    for g in range(S5_GROUPS):
        sl = slice((g % gpt) * S5_GROUP, (g % gpt + 1) * S5_GROUP)
        u_ref[g] = jnp.concatenate([st[g // gpt][:, sl] for st in steps], axis=1).astype(BF16)


def _odd_in(x, sc, sh, w, rows, n_ctx_tiles, ctx_chunk0, lat_chunk0):
    n, d = x.shape
    tm = TM_PROJ
    nw = w.shape[1]
    cpt = tm // S5_CHUNK
    u_rows = n // S5_CHUNK
    tok = lambda i, r: (i, 0)
    (sc, sc_spec), (sh, sh_spec) = (_mod_operand(m, lambda i, r: r[i]) for m in (sc, sh))
    ublk = lambda i, r: (0, jnp.where(i < n_ctx_tiles, i + ctx_chunk0 // cpt, i - n_ctx_tiles + lat_chunk0 // cpt), 0)
    return pl.pallas_call(
        _odd_in_kernel,
        out_shape=(jax.ShapeDtypeStruct((n, nw), F32),
                   jax.ShapeDtypeStruct((S5_GROUPS, u_rows, S5_CHUNK * S5_GROUP), BF16)),
        grid_spec=pltpu.PrefetchScalarGridSpec(
            num_scalar_prefetch=1, grid=(n // tm,),
            in_specs=[pl.BlockSpec((tm, d), tok), sc_spec, sh_spec,
                      pl.BlockSpec((d, nw), lambda i, r: (0, 0))],
            out_specs=[pl.BlockSpec((tm, nw), tok), pl.BlockSpec((S5_GROUPS, cpt, S5_CHUNK * S5_GROUP), ublk)],
            scratch_shapes=[pltpu.VMEM((S5_WIDTH // LANE, tm, LANE), F32)]),
        compiler_params=_params(("arbitrary",), 32),
        name="odd_in_proj",
    )(rows, x, sc, sh, w)


def _s5_kernel(u_ref, zin_ref, t_ref, zout_ref, al_ref, h0_ref, y_ref, fin_ref, z_s, hf_s, hb_s, *, nb, nc):
    u = u_ref[0]
    z = _dot(u, zin_ref[0])
    for part in range(2):
        for b in range(nb):
            z_s[part, pl.ds(b, nc, stride=nb), :] = z[b * nc:(b + 1) * nc, part * LANE:(part + 1) * LANE]
    ar = al_ref[0, 0:1, :]
    ai = al_ref[0, 1:2, :]
    fwd = lax.broadcasted_iota(jnp.int32, (nb, LANE), 1) < S5_STATE
    aligned = (lambda r: pl.multiple_of(r, SUBLANE)) if nb % SUBLANE == 0 else (lambda r: r)

    def step(k, carry):
        re, im = carry
        rf = aligned(k * nb)
        rb = aligned((nc - 1 - k) * nb)
        hf_s[0, pl.ds(rf, nb), :] = re
        hf_s[1, pl.ds(rf, nb), :] = im
        hb_s[0, pl.ds(rb, nb), :] = re
        hb_s[1, pl.ds(rb, nb), :] = im
        zr = jnp.where(fwd, z_s[0, pl.ds(rf, nb), :], z_s[0, pl.ds(rb, nb), :])
        zi = jnp.where(fwd, z_s[1, pl.ds(rf, nb), :], z_s[1, pl.ds(rb, nb), :])
        return ar * re - ai * im + zr, ar * im + ai * re + zi

    h0 = h0_ref[0]
    re, im = lax.fori_loop(0, nc, step, (h0[:, :LANE], h0[:, LANE:]))
    fin_ref[0, :, :LANE] = re
    fin_ref[0, :, LANE:] = im
    m = u.shape[0]
    is_fwd = lax.broadcasted_iota(jnp.int32, (m, LANE), 1) < S5_STATE
    halves = []
    for part in range(2):
        hf_s[part] = jnp.where(is_fwd, hf_s[part], hb_s[part])
        halves.append(jnp.concatenate([hf_s[part, pl.ds(b, nc, stride=nb), :] for b in range(nb)], axis=0))
    h_in = jnp.concatenate(halves, axis=1).astype(BF16)
    y_ref[0] = _dot(u, t_ref[0]) + _dot(h_in, zout_ref[0])


def _s5(u, mats, layer, h0, nb, nc, row_block):
    zin, tmat, zout, al = mats
    g, _, w = u.shape
    m = nb * nc
    blk = lambda i: (i, 0, 0)
    lblk = lambda i: (layer, i, 0, 0)
    return pl.pallas_call(
        functools.partial(_s5_kernel, nb=nb, nc=nc),
        out_shape=(jax.ShapeDtypeStruct((g, m, w), F32), jax.ShapeDtypeStruct((g, nb, w), F32)),
        grid=(g,),
        in_specs=[pl.BlockSpec((1, m, w), lambda i: (i, row_block, 0)),
                  pl.BlockSpec((None, 1, w, w), lblk), pl.BlockSpec((None, 1, w, w), lblk),
                  pl.BlockSpec((None, 1, w, w), lblk), pl.BlockSpec((None, 1, 2, LANE), lblk),
                  pl.BlockSpec((1, nb, w), blk)],
        out_specs=(pl.BlockSpec((1, m, w), blk), pl.BlockSpec((1, nb, w), blk)),
        scratch_shapes=[pltpu.VMEM((2, m, LANE), F32), pltpu.VMEM((2, m, LANE), F32), pltpu.VMEM((2, m, LANE), F32)],
        compiler_params=_params(("arbitrary",), 32),
        name="s5_scan",
    )(u, zin, tmat, zout, al, h0)


def _s5_matrices(a_re, a_im, log_dt, b_re, b_im, c_re, c_im):
    hp = lax.Precision.HIGHEST
    L = S5_CHUNK
    dt = jnp.exp(log_dt)[..., None]
    lam_re, lam_im = a_re * dt, a_im * dt

    def power(k):
        k = k[:, None, None, None]
        mag = jnp.exp(lam_re * k)
        return mag * jnp.cos(lam_im * k), mag * jnp.sin(lam_im * k)

    ab_re, ab_im = power(jnp.ones((1,), F32))
    ab_re, ab_im = ab_re[0], ab_im[0]
    num_re, num_im = ab_re - 1.0, ab_im
    den = a_re * a_re + a_im * a_im
    f_re = (num_re * a_re + num_im * a_im) / den
    f_im = (num_im * a_re - num_re * a_im) / den
    bb_re = f_re[..., None] * b_re - f_im[..., None] * b_im
    bb_im = f_re[..., None] * b_im + f_im[..., None] * b_re

    ks = jnp.arange(L + 1, dtype=F32)
    pw_re, pw_im = power(ks)

    def zin_dir(d, exps):
        pr, pi = pw_re[exps, d], pw_im[exps, d]
        w_re = pr[..., None] * bb_re[d][None] - pi[..., None] * bb_im[d][None]
        w_im = pr[..., None] * bb_im[d][None] + pi[..., None] * bb_re[d][None]
        to = lambda w: jnp.transpose(w, (1, 0, 3, 2)).reshape(S5_GROUPS, L * S5_GROUP, S5_STATE)
        return to(w_re), to(w_im)

    steps = np.arange(L)
    zf_re, zf_im = zin_dir(0, L - 1 - steps)
    zb_re, zb_im = zin_dir(1, steps)
    zin = jnp.concatenate([zf_re, zb_re, zf_im, zb_im], axis=-1)

    def zout_dir(d, exps):
        pr, pi = pw_re[exps, d], pw_im[exps, d]
        cr, ci = c_re[d], c_im[d]
        e_re = cr[None] * pr[:, :, None, :] - ci[None] * pi[:, :, None, :]
        e_im = cr[None] * pi[:, :, None, :] + ci[None] * pr[:, :, None, :]
        to = lambda e: jnp.transpose(e, (1, 3, 0, 2)).reshape(S5_GROUPS, S5_STATE, L * S5_GROUP)
        return to(e_re), to(-e_im)

    of_re, of_im = zout_dir(0, steps + 1)
    ob_re, ob_im = zout_dir(1, L - steps)
    zout = jnp.concatenate([of_re, ob_re, of_im, ob_im], axis=1)

    def taps(d):
        pr, pi = pw_re[:L, d], pw_im[:L, d]
        m_re = pr[..., None] * bb_re[d][None] - pi[..., None] * bb_im[d][None]
        m_im = pr[..., None] * bb_im[d][None] + pi[..., None] * bb_re[d][None]
        return (jnp.einsum('gpn,lgnq->lgpq', c_re[d], m_re, precision=hp)
                - jnp.einsum('gpn,lgnq->lgpq', c_im[d], m_im, precision=hp))

    kf, kb = taps(0), taps(1)
    pad4 = ((0, 0),) * 3
    tm = jnp.stack([jnp.pad(kf[:L - s], ((s, 0),) + pad4) + jnp.pad(kb[:s + 1][::-1], ((0, L - 1 - s),) + pad4)
                    for s in range(L)])
    tmat = jnp.transpose(tm, (2, 0, 4, 1, 3)).reshape(S5_GROUPS, L * S5_GROUP, L * S5_GROUP)

    al = jnp.stack([jnp.concatenate([pw_re[L, 0], pw_re[L, 1]], axis=-1),
                    jnp.concatenate([pw_im[L, 0], pw_im[L, 1]], axis=-1)], axis=1)
    return zin.astype(BF16), tmat.astype(BF16), zout.astype(BF16), al


def _mla_kv(ckv, kpe, wk_ref, wv_ref, pl_ref, k_ref, v_ref):
    cb = ckv.astype(BF16)
    k_ref[...] = (_dot(cb, wk_ref[...]) + _dot(kpe.astype(BF16), pl_ref[...])).astype(BF16)
    v_ref[...] = _dot(cb, wv_ref[...]).astype(BF16)


def _mla_prep_kernel(rblk_ref, p_ref, qg_ref, kvg_ref, wuq_ref, wk_ref, wv_ref, pl_ref, cos_ref, sa_ref, sb_ref,
                     q_ref, k_ref, v_ref, ckv_ref):
    del rblk_ref
    pr = p_ref[...]
    cos, sa, sb = cos_ref[...], sa_ref[...], sb_ref[...]
    quarter = MLA_ROPE // 4
    scale = (MLA_NOPE + MLA_ROPE) ** -0.5
    qn = _rms(pr[:, :MLA_Q_RANK], qg_ref[...], MLA_Q_RANK).astype(BF16)
    q = _dot(qn, wuq_ref[...])
    for hd in range(MLA_HEADS):
        sl = slice(hd * LANE, (hd + 1) * LANE)
        q_ref[:, sl] = (_rope(q[:, sl], cos, sa, sb, quarter) * scale).astype(BF16)
    ckv = _rms(pr[:, MLA_Q_RANK:MLA_Q_RANK + MLA_KV_RANK], kvg_ref[...], MLA_KV_RANK)
    ckv_ref[...] = ckv
    kpe = _rope(pr[:, MLA_Q_RANK + MLA_KV_RANK:], cos, sa, sb, quarter)
    _mla_kv(ckv, kpe, wk_ref, wv_ref, pl_ref, k_ref, v_ref)


def _mla_prep(proj, qg, kvg, wuq, wk, wv, place, tables, rblk):
    n = proj.shape[0]
    tm = TM_PROJ
    hw = MLA_HEADS * LANE
    pw = MLA_Q_RANK + MLA_KV_RANK + LANE
    cos, sa, sb = tables
    tok = lambda i, r: (i, 0)
    const = lambda i, r: (0, 0)
    tab = lambda i, r: (r[i], 0)
    return pl.pallas_call(
        _mla_prep_kernel,
        out_shape=(jax.ShapeDtypeStruct((n, hw), BF16), jax.ShapeDtypeStruct((n, hw), BF16),
                   jax.ShapeDtypeStruct((n, hw), BF16), jax.ShapeDtypeStruct((n, MLA_KV_RANK), F32)),
        grid_spec=pltpu.PrefetchScalarGridSpec(
            num_scalar_prefetch=1, grid=(n // tm,),
            in_specs=[pl.BlockSpec((tm, pw), lambda i, r: (i, 1)),
                      pl.BlockSpec((1, MLA_Q_RANK), const), pl.BlockSpec((1, MLA_KV_RANK), const),
                      pl.BlockSpec((MLA_Q_RANK, hw), const), pl.BlockSpec((MLA_KV_RANK, hw), const),
                      pl.BlockSpec((MLA_KV_RANK, hw), const), pl.BlockSpec((LANE, hw), const),
                      pl.BlockSpec((tm, LANE), tab), pl.BlockSpec((tm, LANE), tab), pl.BlockSpec((tm, LANE), tab)],
            out_specs=[pl.BlockSpec((tm, hw), tok), pl.BlockSpec((tm, hw), tok), pl.BlockSpec((tm, hw), tok),
                       pl.BlockSpec((tm, MLA_KV_RANK), tok)]),
        compiler_params=_params(("arbitrary",), 32),
        name="mla_prep",
    )(rblk, proj, qg, kvg, wuq, wk, wv, place, cos, sa, sb)


def _mla_cache_kernel(c_ref, p_ref, wk_ref, wv_ref, pl_ref, k_ref, v_ref):
    _mla_kv(c_ref[...], p_ref[...], wk_ref, wv_ref, pl_ref, k_ref, v_ref)


def _mla_cache(ckv, kpe, wk, wv, place):
    n = ckv.shape[0]
    tm = min(TM_PROJ, n)
    assert n % tm == 0
    hw = MLA_HEADS * LANE
    tok = lambda i: (i, 0)
    const = lambda i: (0, 0)
    return pl.pallas_call(
        _mla_cache_kernel,
        out_shape=(jax.ShapeDtypeStruct((n, hw), BF16), jax.ShapeDtypeStruct((n, hw), BF16)),
        grid=(n // tm,),
        in_specs=[pl.BlockSpec((tm, MLA_KV_RANK), tok), pl.BlockSpec((tm, LANE), tok),
                  pl.BlockSpec((MLA_KV_RANK, hw), const), pl.BlockSpec((MLA_KV_RANK, hw), const),
                  pl.BlockSpec((LANE, hw), const)],
        out_specs=[pl.BlockSpec((tm, hw), tok), pl.BlockSpec((tm, hw), tok)],
        compiler_params=_params(("arbitrary",), 32),
        name="mla_cache_kv",
    )(ckv, kpe, wk, wv, place)


def _gelu_tanh(x):
    return 0.5 * x * (1.0 + jnp.tanh(math.sqrt(2.0 / math.pi) * (x + 0.044715 * (x * x * x))))


def _odd_out_kernel(rows_ref, u_ref, yp_ref, ys_ref, ap_ref, as_ref, x_ref, g_ref, d_ref, wglu_ref, bglu_ref, w_ref,
                    lg_ref, lb_ref, o_ref, y_s, *, alpha, n_ctx_tiles):
    del rows_ref
    i = pl.program_id(0)
    yg = _pick_pass(i, n_ctx_tiles, yp_ref, ys_ref)
    n_chunk = yg.shape[1]
    n_tile = S5_WIDTH // LANE
    gpt = LANE // S5_GROUP
    for s in range(S5_CHUNK):
        sl = slice(s * S5_GROUP, (s + 1) * S5_GROUP)
        for j in range(n_tile):
            y_s[j, pl.ds(s, n_chunk, stride=S5_CHUNK), :] = jnp.concatenate(
                [yg[g][:, sl] for g in range(j * gpt, (j + 1) * gpt)], axis=1)
    y_ssm = jnp.concatenate([y_s[j] for j in range(n_tile)], axis=1)
    y = _gelu_tanh(u_ref[...] * d_ref[...] + y_ssm)
    y = y * jax.nn.sigmoid(_dot(y.astype(BF16), wglu_ref[...]) + bglu_ref[...])
    attn = _pick_pass(i, n_ctx_tiles, ap_ref, as_ref)
    out = _dot(y.astype(BF16), w_ref[:S5_WIDTH, :]) + _dot(attn, w_ref[S5_WIDTH:, :])
    o_ref[...] = _layer_norm(alpha * x_ref[...] + g_ref[0] * out, lg_ref[...], lb_ref[...])


def _odd_out(proj, y_p, y_s, attn_p, attn_s, x, g1, s5_d, w_glu, b_glu, w_out, ln_g, ln_b, rows, alpha):
    n, d = x.shape
    tm = TM_PROJ
    aw = attn_p.shape[1]
    cpt = tm // S5_CHUNK
    n_ctx, n_lat = attn_p.shape[0] // tm, attn_s.shape[0] // tm
    ap_spec, as_spec = _pass_specs((tm, aw), n_ctx, n_lat)
    yp_spec, ys_spec = _pass_specs((S5_GROUPS, cpt, S5_CHUNK * S5_GROUP), n_ctx, n_lat)
    tok = lambda i, r: (i, 0)
    const = lambda i, r: (0, 0)
    g1, g1_spec = _mod_operand(g1, lambda i, r: r[i])
    return pl.pallas_call(
        functools.partial(_odd_out_kernel, alpha=alpha, n_ctx_tiles=n_ctx),
        out_shape=jax.ShapeDtypeStruct((n, d), F32),
        grid_spec=pltpu.PrefetchScalarGridSpec(
            num_scalar_prefetch=1, grid=(n // tm,),
            in_specs=[pl.BlockSpec((tm, S5_WIDTH), tok), yp_spec, ys_spec,
                      ap_spec, as_spec, pl.BlockSpec((tm, d), tok),
                      g1_spec,
                      pl.BlockSpec((1, S5_WIDTH), const), pl.BlockSpec((S5_WIDTH, S5_WIDTH), const),
                      pl.BlockSpec((1, S5_WIDTH), const), pl.BlockSpec((S5_WIDTH + aw, d), const),
                      pl.BlockSpec((1, d), const), pl.BlockSpec((1, d), const)],
            out_specs=pl.BlockSpec((tm, d), tok),
            scratch_shapes=[pltpu.VMEM((S5_WIDTH // LANE, tm, LANE), F32)]),
        compiler_params=_params(("arbitrary",), 40),
        name="odd_out_proj",
    )(rows, proj, y_p, y_s, attn_p, attn_s, x, g1, s5_d, w_glu, b_glu, w_out, ln_g, ln_b)


def _router_kernel(rows_ref, x_ref, sc_ref, sh_ref, rt_ref, hb_ref, pos_ref, gate_ref, pt_ref, cnt_ref, tri_s):
    del rows_ref
    t = x_ref.shape[0]
    ne = N_EXPERTS

    @pl.when(pl.program_id(0) == 0)
    def _():
        before = lax.broadcasted_iota(jnp.int32, (t, t), 0) < lax.broadcasted_iota(jnp.int32, (t, t), 1)
        tri_s[...] = jnp.where(before, 1.0, 0.0).astype(BF16)

    h = x_ref[...] * (1.0 + sc_ref[0]) + sh_ref[0]
    h_hi, h_lo = _split(h)
    hb_ref[...] = h_hi
    r_hi, r_lo = _split(rt_ref[...])
    logits = _dot_nt(r_hi, h_hi) + _dot_nt(r_lo, h_hi) + _dot_nt(r_hi, h_lo)
    eid = lax.broadcasted_iota(jnp.int32, (ne, t), 0).astype(F32)
    m0 = jnp.max(logits, axis=0, keepdims=True)
    i0 = jnp.min(jnp.where(logits == m0, eid, float(ne)), axis=0, keepdims=True)
    rest = jnp.where(eid == i0, -jnp.inf, logits)
    m1 = jnp.max(rest, axis=0, keepdims=True)
    i1 = jnp.min(jnp.where(rest == m1, eid, float(ne)), axis=0, keepdims=True)
    ex = jnp.exp(m1 - m0)
    g0 = 1.0 / (1.0 + ex)
    g1 = ex / (1.0 + ex)
    sel0 = eid == i0
    sel1 = eid == i1
    member = jnp.where(sel0, 1.0, jnp.where(sel1, 1.0, 0.0))
    gate = jnp.where(sel0, g0, jnp.where(sel1, g1, 0.0))
    rank = _dot(member.astype(BF16), tri_s[...])
    pos = jnp.where(member > 0.0, rank, -1.0)
    pos_ref[0] = pos.astype(jnp.int32)
    gate_ref[0] = gate
    cnt = jnp.sum(member, axis=1, keepdims=True)
    cnt_ref[0] = jnp.broadcast_to(cnt, (ne, LANE)).astype(jnp.int32)
    packed = jnp.concatenate([pos, gate, jnp.zeros((LANE - 2 * ne, t), F32)], axis=0)
    pt_ref[...] = packed.T


def _router(x, sc, sh, router_t, rows):
    n, d = x.shape
    t = T_MOE
    nb = n // t
    ne = N_EXPERTS
    tok = lambda i, r: (i, 0)
    (sc, sc_spec), (sh, sh_spec) = (_mod_operand(m, lambda i, r: r[i]) for m in (sc, sh))
    blk3 = lambda i, r: (i, 0, 0)
    return pl.pallas_call(
        _router_kernel,
        out_shape=(jax.ShapeDtypeStruct((n, d), BF16), jax.ShapeDtypeStruct((nb, ne, t), jnp.int32),
                   jax.ShapeDtypeStruct((nb, ne, t), F32),
                   jax.ShapeDtypeStruct((n, LANE), F32), jax.ShapeDtypeStruct((nb, ne, LANE), jnp.int32)),
        grid_spec=pltpu.PrefetchScalarGridSpec(
            num_scalar_prefetch=1, grid=(nb,),
            in_specs=[pl.BlockSpec((t, d), tok), sc_spec, sh_spec,
                      pl.BlockSpec((ne, d), lambda i, r: (0, 0))],
            out_specs=[pl.BlockSpec((t, d), tok), pl.BlockSpec((1, ne, t), blk3), pl.BlockSpec((1, ne, t), blk3),
                       pl.BlockSpec((t, LANE), tok), pl.BlockSpec((1, ne, LANE), blk3)],
            scratch_shapes=[pltpu.VMEM((t, t), BF16)]),
        compiler_params=_params(("arbitrary",), 48),
        name="moe_router",
    )(rows, x, sc, sh, router_t)


def _segment_copies(rows, src, dst, src0, dst0, sems, slot):
    out = []
    for k, sz in enumerate(SEG_SIZES):
        off = (rows // (2 * sz)) * (2 * sz)
        s0 = pl.multiple_of(src0 + off, SEG_ALIGN)
        d0 = pl.multiple_of(dst0 + off, SEG_ALIGN)
        sem = sems.at[slot * len(SEG_SIZES) + k]
        out.append(((rows & sz) != 0, pltpu.make_async_copy(src.at[pl.ds(s0, sz)], dst.at[pl.ds(d0, sz)], sem)))
    return out


def _run_copies(copies):
    for pred, cp in copies:
        pl.when(pred)(cp.start)
    for pred, cp in copies:
        pl.when(pred)(cp.wait)


def _moe_gather_kernel(cnt_ref, base_ref, h_ref, pos_ref, gate_ref, xs_in, gs_in, xs_ref, gs_ref,
                       xe_s, ge_s, sem_x, sem_g):
    del xs_in, gs_in
    b = pl.program_id(0)
    t = h_ref.shape[0]
    rt = ROW_MOE
    ne = N_EXPERTS
    copies = []
    off = 0
    for e in range(ne):
        cnt = cnt_ref[b * ne + e]
        pos_row = pos_ref[0, e:e + 1, :]
        gate_row = gate_ref[0, e:e + 1, :]

        def gather(r, c, off=off, pos_row=pos_row, gate_row=gate_row):
            r0 = r * rt
            hit = (lax.broadcasted_iota(jnp.int32, (rt, t), 0) + r0) == pos_row
            onehot = jnp.where(hit, 1.0, 0.0).astype(BF16)
            dst = pl.multiple_of(off + r0, SEG_ALIGN)
            xe_s[pl.ds(dst, rt), :] = _dot(onehot, h_ref[...]).astype(BF16)
            g = jnp.sum(jnp.where(hit, gate_row, 0.0), axis=1, keepdims=True)
            ge_s[pl.ds(dst, rt), :] = jnp.broadcast_to(g, (rt, LANE))
            return c

        lax.fori_loop(0, (cnt + rt - 1) // rt, gather, 0)
        rows = ((cnt + SEG_ALIGN - 1) // SEG_ALIGN) * SEG_ALIGN
        base = base_ref[b * ne + e]
        seg = (_segment_copies(rows, xe_s, xs_ref, off, base, sem_x, e)
               + _segment_copies(rows, ge_s, gs_ref, off, base, sem_g, e))
        for pred, cp in seg:
            pl.when(pred)(cp.start)
        copies += seg
        off = off + rows
    for pred, cp in copies:
        pl.when(pred)(cp.wait)


def _moe_gather(h, pos, gate, counts, base, n_rows):
    n, d = h.shape
    t = T_MOE
    nb = n // t
    ne = N_EXPERTS
    tok = lambda b, c, o: (b, 0)
    blk3 = lambda b, c, o: (b, 0, 0)
    any_spec = pl.BlockSpec(memory_space=pl.ANY)
    xs0 = jnp.zeros((n_rows, d), BF16)
    gs0 = jnp.zeros((n_rows, LANE), F32)
    cap = -(-(2 * t + ne * (SEG_ALIGN - 1) + ROW_MOE - 1) // ROW_MOE) * ROW_MOE
    return pl.pallas_call(
        _moe_gather_kernel,
        out_shape=(jax.ShapeDtypeStruct((n_rows, d), BF16), jax.ShapeDtypeStruct((n_rows, LANE), F32)),
        grid_spec=pltpu.PrefetchScalarGridSpec(
            num_scalar_prefetch=2, grid=(nb,),
            in_specs=[pl.BlockSpec((t, d), tok), pl.BlockSpec((1, ne, t), blk3), pl.BlockSpec((1, ne, t), blk3),
                      any_spec, any_spec],
            out_specs=[any_spec, any_spec],
            scratch_shapes=[pltpu.VMEM((cap, d), BF16), pltpu.VMEM((cap, LANE), F32),
                            pltpu.SemaphoreType.DMA((ne * len(SEG_SIZES),)),
                            pltpu.SemaphoreType.DMA((ne * len(SEG_SIZES),))]),
        input_output_aliases={5: 0, 6: 1},
        compiler_params=_params(("arbitrary",), 32),
        name="moe_gather",
    )(counts, base, h, pos, gate, xs0, gs0)


def _stream_cast(src, dst):
    r, c = dst.shape
    rows = r // GMM_STAGE_PIECES
    slots = GMM_STAGE_SLOTS

    def run(stage, sems):
        def piece(k):
            return pltpu.make_async_copy(src.at[pl.ds(k * rows, rows)], stage.at[k % slots], sems.at[k % slots])

        for k in range(slots - 1):
            piece(k).start()
        for k in range(GMM_STAGE_PIECES):
            if k + slots - 1 < GMM_STAGE_PIECES:
                piece(k + slots - 1).start()
            piece(k).wait()
            dst[k * rows:(k + 1) * rows, :] = stage[k % slots].astype(BF16)

    pl.run_scoped(run, pltpu.VMEM((slots, rows, c), F32), pltpu.SemaphoreType.DMA((slots,)))


def _moe_gmm_kernel(te_ref, valid_ref, x_ref, gs_ref, wg_hbm, wu_hbm, wd_hbm, y_ref, wg_s, wu_s, wd_s, *, layer):
    i = pl.program_id(0)
    valid = valid_ref[i] != 0
    e = te_ref[i]
    switch = valid & ((i == 0) | (e != te_ref[jnp.maximum(i - 1, 0)]))

    @pl.when(switch)
    def _():
        _stream_cast(wg_hbm.at[layer, e], wg_s)
        _stream_cast(wu_hbm.at[layer, e], wu_s)
        _stream_cast(wd_hbm.at[layer, e], wd_s)

    @pl.when(valid)
    def _():
        x = x_ref[...]
        a = (_silu(_dot(x, wg_s[...])) * _dot(x, wu_s[...])).astype(BF16)
        y_ref[...] = (_dot(a, wd_s[...]) * gs_ref[:, 0:1]).astype(BF16)

    @pl.when(jnp.logical_not(valid))
    def _():
        y_ref[...] = jnp.zeros_like(y_ref)


def _moe_gmm(xs, gs, tile_expert, tile_valid, wg, wu, wd, layer):
    n_rows, d = xs.shape
    rt = RT_GMM
    ff = wg.shape[-1]
    tok = lambda i, te, tv: (i, 0)
    any_spec = pl.BlockSpec(memory_space=pl.ANY)
    return pl.pallas_call(
        functools.partial(_moe_gmm_kernel, layer=layer),
        out_shape=jax.ShapeDtypeStruct((n_rows, d), BF16),
        grid_spec=pltpu.PrefetchScalarGridSpec(
            num_scalar_prefetch=2, grid=(n_rows // rt,),
            in_specs=[pl.BlockSpec((rt, d), tok), pl.BlockSpec((rt, LANE), tok), any_spec, any_spec, any_spec],
            out_specs=pl.BlockSpec((rt, d), tok),
            scratch_shapes=[pltpu.VMEM((d, ff), BF16), pltpu.VMEM((d, ff), BF16), pltpu.VMEM((ff, d), BF16)]),
        compiler_params=_params(("arbitrary",), 56),
        name="moe_gmm",
    )(tile_expert, tile_valid, xs, gs, wg, wu, wd)


def _moe_combine_kernel(cnt_ref, base_ref, rows_ref, ys_ref, pt_ref, x_ref, g_ref, lg_ref, lb_ref, o_ref,
                        ycat_s, sems, *, alpha, block0):
    del rows_ref
    t = x_ref.shape[0]
    cap = ycat_s.shape[1]
    ne = N_EXPERTS
    i = pl.program_id(0)
    slot = i % 2

    def block_copies(blk, slot_):
        copies, offs, off = [], [], 0
        for e in range(ne):
            cnt = cnt_ref[blk * ne + e]
            rows = ((cnt + SEG_ALIGN - 1) // SEG_ALIGN) * SEG_ALIGN
            copies += _segment_copies(rows, ys_ref, ycat_s.at[slot_], base_ref[blk * ne + e], off, sems,
                                      slot_ * ne + e)
            offs.append(off)
            off = off + rows
        return copies, offs

    def start(copies):
        for pred, cp in copies:
            pl.when(pred)(cp.start)

    @pl.when(i == 0)
    def _():
        ycat_s[...] = jnp.zeros_like(ycat_s)
        start(block_copies(block0, 0)[0])

    @pl.when(i + 1 < pl.num_programs(0))
    def _():
        start(block_copies(i + 1 + block0, 1 - slot)[0])

    copies, offs = block_copies(i + block0, slot)
    for pred, cp in copies:
        pl.when(pred)(cp.wait)

    pt = pt_ref[...]
    lane = lax.broadcasted_iota(jnp.int32, (1, LANE), 1)
    off_lane = jnp.zeros((1, LANE), F32)
    for e in range(1, ne):
        off_lane = jnp.where(lane == e, offs[e].astype(F32), off_lane)
    chosen = (pt >= 0.0) & (lax.broadcasted_iota(jnp.int32, (t, LANE), 1) < ne)
    row_of = pt + off_lane
    s_lo = jnp.min(jnp.where(chosen, row_of, float(cap)), axis=1, keepdims=True)
    s_hi = jnp.max(jnp.where(chosen, row_of, -1.0), axis=1, keepdims=True)
    n_piece = 3
    wp = cap // n_piece
    f = None
    for j in range(n_piece):
        col = (lax.broadcasted_iota(jnp.int32, (t, wp), 1) + j * wp).astype(F32)
        sel = jnp.where(col == s_lo, 1.0, jnp.where(col == s_hi, 1.0, 0.0)).astype(BF16)
        part = _dot(sel, ycat_s.at[slot].at[pl.ds(j * wp, wp)][...])
        f = part if f is None else f + part
    o_ref[...] = _layer_norm(alpha * x_ref[...] + g_ref[0] * f, lg_ref[...], lb_ref[...])


def _moe_combine(ys, pt, x, g2, ln_g, ln_b, counts, base, rows, alpha, block0=0, n_blocks=None):
    n, d = x.shape
    t = T_MOE
    ne = N_EXPERTS
    n_blocks = n // t if n_blocks is None else n_blocks
    cap = -(-(2 * t + ne * (SEG_ALIGN - 1)) // (6 * LANE)) * (6 * LANE)
    tok = lambda b, c, o, r: (b + block0, 0)
    const = lambda b, c, o, r: (0, 0)
    g2, g2_spec = _mod_operand(g2, lambda b, c, o, r: r[b + block0])
    return pl.pallas_call(
        functools.partial(_moe_combine_kernel, alpha=alpha, block0=block0),
        out_shape=jax.ShapeDtypeStruct((n_blocks * t, d), F32),
        grid_spec=pltpu.PrefetchScalarGridSpec(
            num_scalar_prefetch=3, grid=(n_blocks,),
            in_specs=[pl.BlockSpec(memory_space=pl.ANY), pl.BlockSpec((t, LANE), tok), pl.BlockSpec((t, d), tok),
                      g2_spec,
                      pl.BlockSpec((1, d), const), pl.BlockSpec((1, d), const)],
            out_specs=pl.BlockSpec((t, d), lambda b, c, o, r: (b, 0)),
            scratch_shapes=[pltpu.VMEM((2, cap, d), BF16), pltpu.SemaphoreType.DMA((2 * ne * len(SEG_SIZES),))]),
        compiler_params=_params(("arbitrary",), 48),
        name="moe_combine",
    )(counts, base, rows, ys, pt, x, g2, ln_g, ln_b)


def _moe_layout(cnt, n_tiles):
    cnt16 = (cnt + SEG_ALIGN - 1) // SEG_ALIGN * SEG_ALIGN
    rows_e = jnp.sum(cnt16, axis=0)
    rows_e = (rows_e + RT_GMM - 1) // RT_GMM * RT_GMM
    ends = jnp.cumsum(rows_e)
    base = (ends - rows_e)[None, :] + jnp.cumsum(cnt16, axis=0) - cnt16
    tile_end = ends // RT_GMM
    idx = jnp.arange(n_tiles, dtype=jnp.int32)
    valid = idx < tile_end[-1]
    te = jnp.sum((idx[:, None] >= tile_end[None, :]).astype(jnp.int32), axis=1)
    last = jnp.sum(((tile_end[-1] - 1) >= tile_end).astype(jnp.int32))
    te = jnp.where(valid, te, last)
    return base.reshape(-1).astype(jnp.int32), te.astype(jnp.int32), valid.astype(jnp.int32)


def _tile_rows(n, p, ss, tm):
    starts = np.arange(n // tm) * tm
    return jnp.asarray(np.where(starts < p, 0, 1 + (starts - p) // ss).astype(np.int32))


def _tile_rope_blocks(n, p, ss, tm):
    starts = np.arange(n // tm) * tm
    return jnp.asarray(np.where(starts < p, 0, 1 + ((starts - p) % ss) // tm).astype(np.int32))


def _tile_edges(n, p, sp, ss, tm):
    starts = np.arange(n // tm) * tm
    pos = np.where(starts < p, starts % sp, (starts - p) % ss)
    seq = np.where(starts < p, sp, ss)
    return jnp.asarray((pos == 0).astype(np.int32)), jnp.asarray(((pos + tm) % seq == 0).astype(np.int32))


def _rope_tables(n_tokens, rot_dim, lane0, tm):
    rows = n_tokens // GRID_W
    row_pos = jnp.repeat(jnp.arange(rows, dtype=F32), GRID_W)
    col_pos = jnp.tile(jnp.arange(GRID_W, dtype=F32), rows)
    half = rot_dim // 2
    qtr = half // 2
    inv_freq = ROPE_THETA ** (-jnp.arange(0, half, 2, dtype=F32) / half)
    ang_r = row_pos[:, None] * inv_freq
    ang_c = col_pos[:, None] * inv_freq
    ang = jnp.concatenate([ang_r, ang_r, ang_c, ang_c], axis=-1)
    cos, sin = jnp.cos(ang), jnp.sin(ang)
    first = ((np.arange(rot_dim) % half) < qtr).astype(np.float32)
    sa = -sin * first
    sb = sin * (1.0 - first)

    def place(tbl, fill):
        full = jnp.full((n_tokens, LANE), fill, F32).at[:, lane0:lane0 + rot_dim].set(tbl)
        return jnp.concatenate([jnp.full((tm, LANE), fill, F32), full], axis=0)

    return place(cos, 1.0), place(sa, 0.0), place(sb, 0.0)


def _pad_heads(w, n_heads, width):
    lead = w.shape[:-1]
    w = w.reshape(lead + (n_heads, width))
    w = jnp.pad(w, [(0, 0)] * len(lead) + [(0, 0), (0, LANE - width)])
    return w.reshape(lead + (n_heads * LANE,))


def _pad_head_rows(w, n_heads, width):
    d = w.shape[-1]
    w = w.reshape(n_heads, width, d)
    return jnp.pad(w, ((0, 0), (0, LANE - width), (0, 0))).reshape(n_heads * LANE, d)


def kernel(x_prompt, x_sample, cache_attn_k, cache_attn_v, cache_mla_ckv, cache_mla_kpe, state_ssm_re, state_ssm_im, c, c_ctx, ada_w, ada_b, ln_g, ln_b, ev_w_in, ev_conv_w, ev_q_gain, ev_k_gain, ev_w_out, ffn_w_gate, ffn_w_up, ffn_w_down, od_w_in, s5_a_re, s5_a_im, s5_log_dt, s5_b_re, s5_b_im, s5_c_re, s5_c_im, s5_d, s5_w_glu, s5_b_glu, mla_q_gain, mla_w_uq, mla_kv_gain, mla_w_ukv, od_w_out, moe_router, moe_w_gate, moe_w_up, moe_w_down):
    bp, sp, d = x_prompt.shape
    bs, ss, _ = x_sample.shape
    depth = ada_w.shape[0]
    alpha = (2 * depth) ** 0.25
    p = bp * sp
    n = p + bs * ss
    past = cache_attn_k.shape[2]
    for tm in (TM_PROJ, TM_EVEN_IN, TM_FFN, T_MOE):
        assert p % tm == 0 and ss % tm == 0 and (sp % tm == 0 or tm % sp == 0)
    assert sp % S5_CHUNK == 0 and ss % S5_CHUNK == 0

    x = (x_prompt.reshape(p, d), x_sample.reshape(bs * ss, d))

    nrow = -(-(1 + bs) // SUBLANE) * SUBLANE
    cond = jnp.zeros((nrow, d), F32).at[0].set(c_ctx).at[1:1 + bs].set(c)
    mod = _modulation(cond, ada_w, ada_b)

    def mod_part(l, k):
        return (mod, l, k)

    rows_proj = _tile_rows(n, p, ss, TM_PROJ)
    rows_ffn = _tile_rows(n, p, ss, TM_FFN)
    rows_moe = _tile_rows(n, p, ss, T_MOE)
    rblk = _tile_rope_blocks(n, p, ss, TM_PROJ)
    first, last = _tile_edges(n, p, sp, ss, TM_PROJ)
    rows_even = _tile_rows(n, p, ss, TM_EVEN_IN)
    rblk_even = _tile_rope_blocks(n, p, ss, TM_EVEN_IN)
    tables_even = _rope_tables(ss, HEAD_DIM, 0, TM_EVEN_IN)
    tables_mla = _rope_tables(ss, MLA_ROPE, MLA_NOPE, TM_PROJ)

    s5_mats = jax.vmap(_s5_matrices)(s5_a_re, s5_a_im, s5_log_dt, s5_b_re, s5_b_im, s5_c_re, s5_c_im)
    ffn_wg, ffn_wu, ffn_wd = ffn_w_gate.astype(BF16), ffn_w_up.astype(BF16), ffn_w_down.astype(BF16)
    out_k, out_v, out_ckv, out_kpe, out_sre, out_sim = [], [], [], [], [], []
    cw = CONV_WIDTH
    hq = ATTN_HEADS * HEAD_DIM
    hkv = ATTN_KV_HEADS * HEAD_DIM
    for l in range(depth):
        i = l // 2
        sh1, sc1, g1, sh2, sc2, g2 = [mod_part(l, k) for k in range(6)]
        lg = ln_g[l].reshape(2, 1, d)
        lb = ln_b[l].reshape(2, 1, d)
        if l % 2 == 0:
            w_in = ev_w_in[i]
            wc = w_in[:, :3 * cw].astype(BF16)
            wq = _pad_heads(w_in[:, 3 * cw:3 * cw + hq], ATTN_HEADS, HEAD_DIM).astype(BF16)
            wkv = jnp.concatenate([_pad_heads(w_in[:, 3 * cw + hq:3 * cw + hq + hkv], ATTN_KV_HEADS, HEAD_DIM),
                                   _pad_heads(w_in[:, 3 * cw + hq + hkv:], ATTN_KV_HEADS, HEAD_DIM)],
                                  axis=1).astype(BF16)
            qg = jnp.pad(ev_q_gain[i], (0, LANE - HEAD_DIM)).reshape(1, LANE)
            kg = jnp.pad(ev_k_gain[i], (0, LANE - HEAD_DIM)).reshape(1, LANE)
            conv, q, k, v, kn, vf = _even_in(x, sc1, sh1, wc, wq, wkv, qg, kg, tables_even, rows_even, rblk_even)
            kvw = ATTN_KV_HEADS * LANE
            out_k.append(kn[:p].reshape(bp, sp, ATTN_KV_HEADS, LANE)[..., :HEAD_DIM])
            out_v.append(vf[:p].reshape(bp, sp, ATTN_KV_HEADS, LANE)[..., :HEAD_DIM])
            group = ATTN_HEADS // ATTN_KV_HEADS
            qw = ATTN_HEADS * LANE
            a_p = _attention(q, [(k, v, True)], ATTN_HEADS, group, bp, sp, 0)
            kc = _pad_heads(cache_attn_k[:, i].reshape(bs, past, hkv), ATTN_KV_HEADS, HEAD_DIM).astype(BF16)
            vc = _pad_heads(cache_attn_v[:, i].reshape(bs, past, hkv), ATTN_KV_HEADS, HEAD_DIM).astype(BF16)
            a_s = _attention(q, [(kc, vc, False), (k, v, True)], ATTN_HEADS, group, bs, ss, p)
            w_out = jnp.concatenate([ev_w_out[i][:cw], _pad_head_rows(ev_w_out[i][cw:], ATTN_HEADS, HEAD_DIM)],
                                    axis=0).astype(BF16)
            conv_w = jnp.pad(ev_conv_w[i], ((0, SUBLANE - ev_conv_w.shape[1]), (0, 0)))
            x = _even_out(conv, a_p.reshape(p, qw), a_s.reshape(bs * ss, qw), x, g1, conv_w, w_out, lg[0], lb[0],
                          first, last, rows_proj, alpha, sp)
            x = _ffn(x, sc2, sh2, g2, ffn_wg, ffn_wu, ffn_wd, i, lg[1], lb[1], rows_ffn, alpha)
        else:
            q_end = S5_WIDTH + MLA_Q_RANK
            kv_end = q_end + MLA_KV_RANK
            w_in = od_w_in[i]
            w_in = jnp.concatenate([w_in[:, :kv_end], jnp.zeros((d, MLA_NOPE), F32), w_in[:, kv_end:],
                                    jnp.zeros((d, LANE - MLA_NOPE - MLA_ROPE), F32)], axis=1).astype(BF16)
            m_ctx, m_lat = p // S5_CHUNK, bs * ss // S5_CHUNK
            ctx_chunk0, lat_chunk0 = (m_lat, 0) if m_lat >= m_ctx else (0, m_ctx)
            assert ctx_chunk0 % m_ctx == 0 and lat_chunk0 % m_lat == 0
            proj, u = _odd_in(x, sc1, sh1, w_in, rows_proj, p // TM_PROJ, ctx_chunk0, lat_chunk0)
            out_kpe.append(proj[:p, kv_end + MLA_NOPE:kv_end + MLA_NOPE + MLA_ROPE].reshape(bp, sp, MLA_ROPE))

            h0_p = jnp.zeros((S5_GROUPS, bp, 4 * S5_STATE), F32)
            y_p, fin_p = _s5(u, s5_mats, i, h0_p, bp, sp // S5_CHUNK, ctx_chunk0 // m_ctx)

            def pack_state(re, im):
                to = lambda a: a.transpose(2, 0, 1, 3).reshape(S5_GROUPS, a.shape[0], 2 * S5_STATE)
                return jnp.concatenate([to(re), to(im)], axis=-1)

            y_s, _ = _s5(u, s5_mats, i, pack_state(state_ssm_re[:, i], state_ssm_im[:, i]), bs, ss // S5_CHUNK,
                         lat_chunk0 // m_lat)
            fin = fin_p.reshape(S5_GROUPS, bp, 2, 2, S5_STATE)
            out_sre.append(fin[:, :, 0].transpose(1, 2, 0, 3))
            out_sim.append(fin[:, :, 1].transpose(1, 2, 0, 3))

            qk = MLA_NOPE + MLA_ROPE
            wuq = _pad_heads(mla_w_uq[i], MLA_HEADS, qk).astype(BF16)
            wukv = mla_w_ukv[i].reshape(MLA_KV_RANK, MLA_HEADS, MLA_NOPE + MLA_V)
            wk = _pad_heads(wukv[..., :MLA_NOPE].reshape(MLA_KV_RANK, -1), MLA_HEADS, MLA_NOPE).astype(BF16)
            wv = _pad_heads(wukv[..., MLA_NOPE:].reshape(MLA_KV_RANK, -1), MLA_HEADS, MLA_V).astype(BF16)
            place = np.zeros((LANE, MLA_HEADS * LANE), np.float32)
            for hd in range(MLA_HEADS):
                for j in range(MLA_ROPE):
                    place[MLA_NOPE + j, hd * LANE + MLA_NOPE + j] = 1.0
            place = jnp.asarray(place, BF16)
            q, k, v, ckv = _mla_prep(proj, mla_q_gain[i].reshape(1, -1), mla_kv_gain[i].reshape(1, -1),
                                     wuq, wk, wv, place, tables_mla, rblk)
            out_ckv.append(ckv[:p].reshape(bp, sp, MLA_KV_RANK))
            hw = MLA_HEADS * LANE
            a_p = _attention(q, [(k, v, True)], MLA_HEADS, 1, bp, sp, 0)
            kpe_c = jnp.pad(cache_mla_kpe[:, i].reshape(bs * past, MLA_ROPE),
                            ((0, 0), (MLA_NOPE, LANE - MLA_NOPE - MLA_ROPE)))
            kc, vc = _mla_cache(cache_mla_ckv[:, i].reshape(bs * past, MLA_KV_RANK), kpe_c, wk, wv, place)
            a_s = _attention(q, [(kc.reshape(bs, past, hw), vc.reshape(bs, past, hw), False), (k, v, True)],
                             MLA_HEADS, 1, bs, ss, p)
            w_out = jnp.concatenate([od_w_out[i][:S5_WIDTH], _pad_head_rows(od_w_out[i][S5_WIDTH:], MLA_HEADS, MLA_V)],
                                    axis=0).astype(BF16)
            x = _odd_out(proj, y_p, y_s, a_p.reshape(p, hw), a_s.reshape(bs * ss, hw), x, g1, s5_d[i].reshape(1, -1),
                         s5_w_glu[i].astype(BF16), s5_b_glu[i].reshape(1, -1), w_out, lg[0], lb[0], rows_proj, alpha)

            hb, pos, gate, pt, cnt = _router(x, sc2, sh2, moe_router[i].T, rows_moe)
            cnt = cnt[:, :, 0]
            nblk = n // T_MOE
            n_tiles = -(-(2 * n + nblk * N_EXPERTS * (SEG_ALIGN - 1)) // RT_GMM) + N_EXPERTS
            base, tile_expert, tile_valid = _moe_layout(cnt, n_tiles)
            counts = cnt.reshape(-1)
            xs, gs = _moe_gather(hb, pos, gate, counts, base, n_tiles * RT_GMM)
            ys = _moe_gmm(xs, gs, tile_expert, tile_valid, moe_w_gate, moe_w_up, moe_w_down, i)
            if l == depth - 1:
                nb_ctx = p // T_MOE
                x = tuple(_moe_combine(ys, pt, x, g2, lg[1], lb[1], counts, base, rows_moe, alpha, b0, nbk)
                          for b0, nbk in ((0, nb_ctx), (nb_ctx, nblk - nb_ctx)))
            else:
                x = _moe_combine(ys, pt, x, g2, lg[1], lb[1], counts, base, rows_moe, alpha)

    if not isinstance(x, tuple):
        x = (x[:p], x[p:])
    y_prompt = x[0].reshape(bp, sp, d)
    y_sample = x[1].reshape(bs, ss, d)
    return (y_prompt, y_sample, jnp.stack(out_k, axis=1), jnp.stack(out_v, axis=1),
            jnp.stack(out_ckv, axis=1), jnp.stack(out_kpe, axis=1),
            jnp.stack(out_sre, axis=1), jnp.stack(out_sim, axis=1))
```

```python
import functools
import math

import jax
import jax.numpy as jnp
import numpy as np
from jax import lax
from jax.experimental import pallas as pl
from jax.experimental.pallas import tpu as pltpu

F32 = jnp.float32
BF16 = jnp.bfloat16

LANE = 128
SUBLANE = 8
MIB = 1024 * 1024

GRID_W = 64
ROPE_THETA = 10000.0
LN_EPS = 1e-6
RMS_EPS = 1e-6
HEAD_DIM = 64
ATTN_HEADS = 8
ATTN_KV_HEADS = 2
CONV_WIDTH = 512
S5_WIDTH = 512
S5_GROUP = 16
S5_GROUPS = 32
S5_STATE = 64
S5_CHUNK = 16
MLA_HEADS = 8
MLA_Q_RANK = 256
MLA_KV_RANK = 128
MLA_NOPE = 64
MLA_ROPE = 32
MLA_V = 64
N_EXPERTS = 8

TM_PROJ = 512
TM_EVEN_IN = 256
TM_FFN = 512
FFN_CHUNK = 2816
T_MOE = 1024
ROW_MOE = 128
RT_GMM = 512
GMM_STAGE_PIECES = 16
GMM_STAGE_SLOTS = 6
SEG_ALIGN = 16
SEG_SIZES = tuple(T_MOE >> k for k in range(T_MOE.bit_length()) if (T_MOE >> k) >= SEG_ALIGN)
TQ_ATTN = 512


def _params(sem, vmem_mib):
    return pltpu.CompilerParams(dimension_semantics=sem, vmem_limit_bytes=vmem_mib * MIB)


def _dot(a, b):
    return jnp.dot(a, b, preferred_element_type=F32)


def _dot_nt(a, b):
    return lax.dot_general(a, b, (((1,), (1,)), ((), ())), preferred_element_type=F32)


def _split(a):
    hi = a.astype(BF16)
    lo = (a - hi.astype(F32)).astype(BF16)
    return hi, lo


def _silu(x):
    return x * jax.nn.sigmoid(x)


def _layer_norm(r, g, b):
    mu = jnp.mean(r, axis=-1, keepdims=True)
    d = r - mu
    var = jnp.mean(d * d, axis=-1, keepdims=True)
    return d * lax.rsqrt(var + LN_EPS) * g + b


def _rms(x, g, n):
    ms = jnp.sum(x * x, axis=-1, keepdims=True) * (1.0 / n)
    return x * lax.rsqrt(ms + RMS_EPS) * g


def _rope(x, cos, sa, sb, q):
    w = x.shape[-1]
    return x * cos + pltpu.roll(x, w - q, 1) * sa + pltpu.roll(x, q, 1) * sb


def _ada_kernel(c_ref, w_ref, b_ref, o_ref):
    c = c_ref[...]
    a_hi, a_lo = _split(_silu(c))
    w_hi, w_lo = _split(w_ref[0])
    o_ref[0] = _dot(a_hi, w_hi) + _dot(a_lo, w_hi) + _dot(a_hi, w_lo) + b_ref[0]


def _modulation(cond, ada_w, ada_b):
    depth, d, d6 = ada_w.shape
    r = cond.shape[0]
    out = pl.pallas_call(
        _ada_kernel,
        out_shape=jax.ShapeDtypeStruct((depth, d6 // d, r, d), F32),
        grid=(depth, d6 // d),
        in_specs=[pl.BlockSpec((r, d), lambda l, j: (0, 0)),
                  pl.BlockSpec((1, d, d), lambda l, j: (l, 0, j)),
                  pl.BlockSpec((1, 1, d), lambda l, j: (l, 0, j))],
        out_specs=pl.BlockSpec((None, 1, r, d), lambda l, j: (l, j, 0, 0)),
        compiler_params=_params(("arbitrary", "arbitrary"), 40),
        name="ada_modulation",
    )(cond, ada_w, ada_b.reshape(depth, 1, d6))
    return out.reshape(depth, d6 // d, r, 1, d)


def _mod_operand(m, row_fn):
    arr, layer, part = m
    return arr, pl.BlockSpec((None, None, 1, 1, arr.shape[-1]), lambda *a: (layer, part, row_fn(*a), 0, 0))


def _load_x(x_refs, n_ctx_tiles):
    if len(x_refs) == 1:
        return x_refs[0][...]
    return _pick_pass(pl.program_id(0), n_ctx_tiles, *x_refs)


def _x_specs(x, tm, tok):
    if not isinstance(x, tuple):
        return (x,), [pl.BlockSpec((tm, x.shape[1]), tok)], 0
    n_ctx, n_lat = x[0].shape[0] // tm, x[1].shape[0] // tm
    return x, list(_pass_specs((tm, x[0].shape[1]), n_ctx, n_lat)), n_ctx


def _even_in_kernel(rows_ref, rblk_ref, *refs, n_x, n_ctx_tiles):
    (sc_ref, sh_ref, wc_ref, wq_ref, wkv_ref, qg_ref, kg_ref, cos_ref, sa_ref, sb_ref,
     conv_ref, q_ref, k_ref, v_ref, kn_ref, vf_ref) = refs[n_x:]
    del rows_ref, rblk_ref
    h = (_load_x(refs[:n_x], n_ctx_tiles) * (1.0 + sc_ref[0]) + sh_ref[0]).astype(BF16)
    conv_ref[...] = _dot(h, wc_ref[...])
    cos, sa, sb = cos_ref[...], sa_ref[...], sb_ref[...]
    quarter = HEAD_DIM // 4
    q = _dot(h, wq_ref[...])
    for hd in range(ATTN_HEADS):
        sl = slice(hd * LANE, (hd + 1) * LANE)
        qn = _rms(q[:, sl], qg_ref[...], HEAD_DIM)
        q_ref[:, sl] = (_rope(qn, cos, sa, sb, quarter) * (HEAD_DIM ** -0.5)).astype(BF16)
    kv = _dot(h, wkv_ref[...])
    kw = ATTN_KV_HEADS * LANE
    for hd in range(ATTN_KV_HEADS):
        sl = slice(hd * LANE, (hd + 1) * LANE)
        kn = _rms(kv[:, sl], kg_ref[...], HEAD_DIM)
        kn_ref[:, sl] = kn
        k_ref[:, sl] = _rope(kn, cos, sa, sb, quarter).astype(BF16)
    v = kv[:, kw:]
    vf_ref[...] = v
    v_ref[...] = v.astype(BF16)


def _even_in(x, sc, sh, wc, wq, wkv, qg, kg, tables, rows, rblk):
    tm = TM_EVEN_IN
    d = wc.shape[0]
    n = rows.shape[0] * tm
    cw, qw, kvw = wc.shape[1], wq.shape[1], wkv.shape[1]
    kw = kvw // 2
    cos, sa, sb = tables
    tok = lambda i, rows, rblk: (i, 0)
    const = lambda i, rows, rblk: (0, 0)
    tab = lambda i, rows, rblk: (rblk[i], 0)
    xs, x_specs, n_ctx = _x_specs(x, tm, tok)
    row = lambda i, rows, rblk: rows[i]
    (sc, sc_spec), (sh, sh_spec) = _mod_operand(sc, row), _mod_operand(sh, row)
    return pl.pallas_call(
        functools.partial(_even_in_kernel, n_x=len(xs), n_ctx_tiles=n_ctx),
        out_shape=(jax.ShapeDtypeStruct((n, cw), F32), jax.ShapeDtypeStruct((n, qw), BF16),
                   jax.ShapeDtypeStruct((n, kw), BF16), jax.ShapeDtypeStruct((n, kw), BF16),
                   jax.ShapeDtypeStruct((n, kw), F32), jax.ShapeDtypeStruct((n, kw), F32)),
        grid_spec=pltpu.PrefetchScalarGridSpec(
            num_scalar_prefetch=2, grid=(n // tm,),
            in_specs=x_specs + [
                      sc_spec, sh_spec,
                      pl.BlockSpec((d, cw), const), pl.BlockSpec((d, qw), const), pl.BlockSpec((d, kvw), const),
                      pl.BlockSpec((1, LANE), const), pl.BlockSpec((1, LANE), const),
                      pl.BlockSpec((tm, LANE), tab), pl.BlockSpec((tm, LANE), tab), pl.BlockSpec((tm, LANE), tab)],
            out_specs=[pl.BlockSpec((tm, cw), tok), pl.BlockSpec((tm, qw), tok),
                       pl.BlockSpec((tm, kw), tok), pl.BlockSpec((tm, kw), tok),
                       pl.BlockSpec((tm, kw), tok), pl.BlockSpec((tm, kw), tok)]),
        compiler_params=_params(("arbitrary",), 40),
        name="even_in_proj",
    )(rows, rblk, *xs, sc, sh, wc, wq, wkv, qg, kg, cos, sa, sb)


def _attn_kernel(*refs, n_heads, group, n_seg):
    q_ref = refs[0]
    seg = refs[1:1 + 2 * n_seg]
    o_ref = refs[1 + 2 * n_seg]
    for hd in range(n_heads):
        sl = slice(hd * LANE, (hd + 1) * LANE)
        ks = slice((hd // group) * LANE, (hd // group + 1) * LANE)
        qh = q_ref[:, sl]
        scores = [_dot_nt(qh, seg[2 * s][:, ks]) for s in range(n_seg)]
        m = jnp.max(scores[0], axis=-1, keepdims=True)
        for s in range(1, n_seg):
            m = jnp.maximum(m, jnp.max(scores[s], axis=-1, keepdims=True))
        den = None
        acc = None
        for s in range(n_seg):
            p = jnp.exp(scores[s] - m)
            ps = jnp.sum(p, axis=-1, keepdims=True)
            pv = _dot(p.astype(BF16), seg[2 * s + 1][:, ks])
            den = ps if den is None else den + ps
            acc = pv if acc is None else acc + pv
        o_ref[:, sl] = (acc * (1.0 / den)).astype(BF16)


def _attention(q, segments, n_heads, group, b, s, row0):
    n, qw = q.shape
    assert n % s == 0 and row0 % s == 0
    b0 = row0 // s
    tq = min(TQ_ATTN, s)
    in_specs = [pl.BlockSpec((None, tq, qw), lambda i, j: (i + b0, j, 0))]
    args = [q.reshape(n // s, s, qw)]
    for k, v, own in segments:
        kw = k.shape[-1]
        if own:
            k, v = k.reshape(n // s, s, kw), v.reshape(n // s, s, kw)
            kmap = lambda i, j: (i + b0, 0, 0)
        else:
            kmap = lambda i, j: (i, 0, 0)
        t = k.shape[1]
        in_specs += [pl.BlockSpec((None, t, kw), kmap, pipeline_mode=pl.Buffered(1)),
                     pl.BlockSpec((None, t, kw), kmap, pipeline_mode=pl.Buffered(1))]
        args += [k, v]
    return pl.pallas_call(
        functools.partial(_attn_kernel, n_heads=n_heads, group=group, n_seg=len(segments)),
        out_shape=jax.ShapeDtypeStruct((b, s, qw), BF16),
        grid=(b, s // tq),
        in_specs=in_specs,
        out_specs=pl.BlockSpec((None, tq, qw), lambda i, j: (i, j, 0)),
        compiler_params=_params(("arbitrary", "arbitrary"), 56),
        name="attention",
    )(*args)


def _pick_pass(i, n_ctx_tiles, ctx_ref, lat_ref):
    return jnp.where(i < n_ctx_tiles, ctx_ref[...], lat_ref[...])


def _pass_specs(block, n_ctx_tiles, n_lat_tiles):
    lead = (0,) * (len(block) - 2)

    def ctx(i, *_):
        return lead + (jnp.minimum(i, n_ctx_tiles - 1), 0)

    def lat(i, *_):
        return lead + (jnp.clip(i - n_ctx_tiles, 0, n_lat_tiles - 1), 0)

    return pl.BlockSpec(block, ctx), pl.BlockSpec(block, lat)


def _even_out_kernel(first_ref, last_ref, rows_ref, c_ref, cp_ref, cn_ref, ap_ref, as_ref, *refs,
                     alpha, n_ctx_tiles, ctx_seq, n_x):
    g_ref, cw_ref, w_ref, lg_ref, lb_ref, o_ref = refs[n_x:]
    del rows_ref
    i = pl.program_id(0)
    attn = _pick_pass(i, n_ctx_tiles, ap_ref, as_ref)
    tm = c_ref.shape[0]
    cwid = CONV_WIDTH
    c = c_ref[...]
    gate_b, z = c[:, :cwid], c[:, cwid:2 * cwid] * c[:, 2 * cwid:]
    cp = cp_ref[...]
    cn = cn_ref[...]
    zp = cp[SUBLANE - 1:SUBLANE, cwid:2 * cwid] * cp[SUBLANE - 1:SUBLANE, 2 * cwid:]
    zn = cn[0:1, cwid:2 * cwid] * cn[0:1, 2 * cwid:]
    zp = zp * (1 - first_ref[i]).astype(F32)
    zn = zn * (1 - last_ref[i]).astype(F32)
    row = lax.broadcasted_iota(jnp.int32, (tm, cwid), 0)
    z_prev = jnp.where(row == 0, zp, pltpu.roll(z, 1, 0))
    z_next = jnp.where(row == tm - 1, zn, pltpu.roll(z, tm - 1, 0))
    for k in range(1, tm // ctx_seq):
        edge = jnp.where(i < n_ctx_tiles, k * ctx_seq, -1)
        z_prev = jnp.where(row == edge, 0.0, z_prev)
        z_next = jnp.where(row == edge - 1, 0.0, z_next)
    cw = cw_ref[...]
    y = gate_b * (cw[0:1] * z_prev + cw[1:2] * z + cw[2:3] * z_next)
    out = _dot(y.astype(BF16), w_ref[:cwid, :]) + _dot(attn, w_ref[cwid:, :])
    x = _load_x(refs[:n_x], n_ctx_tiles)
    o_ref[...] = _layer_norm(alpha * x + g_ref[0] * out, lg_ref[...], lb_ref[...])


def _even_out(conv, attn_p, attn_s, x, g1, conv_w, w_out, ln_g, ln_b, first, last, rows, alpha, ctx_seq):
    tm = TM_PROJ
    n, d = conv.shape[0], w_out.shape[1]
    cw3 = conv.shape[1]
    aw = attn_p.shape[1]
    hb = tm // SUBLANE
    nblk8 = n // SUBLANE
    n_ctx, n_lat = attn_p.shape[0] // tm, attn_s.shape[0] // tm
    ap_spec, as_spec = _pass_specs((tm, aw), n_ctx, n_lat)
    tok = lambda i, f, l, r: (i, 0)
    const = lambda i, f, l, r: (0, 0)
    xs, x_specs, _ = _x_specs(x, tm, tok)
    g1, g1_spec = _mod_operand(g1, lambda i, f, l, r: r[i])
    return pl.pallas_call(
        functools.partial(_even_out_kernel, alpha=alpha, n_ctx_tiles=n_ctx, ctx_seq=ctx_seq, n_x=len(xs)),
        out_shape=jax.ShapeDtypeStruct((n, d), F32),
        grid_spec=pltpu.PrefetchScalarGridSpec(
            num_scalar_prefetch=3, grid=(n // tm,),
            in_specs=[pl.BlockSpec((tm, cw3), tok),
                      pl.BlockSpec((SUBLANE, cw3), lambda i, f, l, r: (jnp.maximum(i * hb - 1, 0), 0)),
                      pl.BlockSpec((SUBLANE, cw3), lambda i, f, l, r: (jnp.minimum((i + 1) * hb, nblk8 - 1), 0)),
                      ap_spec, as_spec] + x_specs + [
                      g1_spec,
                      pl.BlockSpec((SUBLANE, CONV_WIDTH), const),
                      pl.BlockSpec((CONV_WIDTH + aw, d), const),
                      pl.BlockSpec((1, d), const), pl.BlockSpec((1, d), const)],
            out_specs=pl.BlockSpec((tm, d), tok)),
        compiler_params=_params(("arbitrary",), 40),
        name="even_out_proj",
    )(first, last, rows, conv, conv, conv, attn_p, attn_s, *xs, g1, conv_w, w_out, ln_g, ln_b)


def _ffn_kernel(rows_ref, x_ref, sc_ref, sh_ref, g_ref, wg_ref, wu_ref, wd_ref, lg_ref, lb_ref, o_ref,
                h_s, acc_s, *, alpha):
    del rows_ref
    f = pl.program_id(1)

    @pl.when(f == 0)
    def _():
        h_s[...] = (x_ref[...] * (1.0 + sc_ref[0]) + sh_ref[0]).astype(BF16)

    h = h_s[...]
    a = (_silu(_dot(h, wg_ref[...])) * _dot(h, wu_ref[...])).astype(BF16)
    y = _dot(a, wd_ref[...])

    @pl.when(f == 0)
    def _():
        acc_s[...] = y

    @pl.when(f > 0)
    def _():
        acc_s[...] += y

    @pl.when(f == pl.num_programs(1) - 1)
    def _():
        o_ref[...] = _layer_norm(alpha * x_ref[...] + g_ref[0] * acc_s[...], lg_ref[...], lb_ref[...])


def _ffn(x, sc, sh, g2, wg, wu, wd, layer, ln_g, ln_b, rows, alpha):
    n, d = x.shape
    tm = TM_FFN
    ff = wg.shape[-1]
    fc = FFN_CHUNK
    tok = lambda i, f, r: (i, 0)
    const = lambda i, f, r: (0, 0)
    wmode = dict(pipeline_mode=pl.Buffered(1)) if fc == ff else {}
    row = lambda i, f, r: r[i]
    (sc, sc_spec), (sh, sh_spec), (g2, g2_spec) = (_mod_operand(m, row) for m in (sc, sh, g2))
    return pl.pallas_call(
        functools.partial(_ffn_kernel, alpha=alpha),
        out_shape=jax.ShapeDtypeStruct((n, d), F32),
        grid_spec=pltpu.PrefetchScalarGridSpec(
            num_scalar_prefetch=1, grid=(n // tm, ff // fc),
            in_specs=[pl.BlockSpec((tm, d), tok),
                      sc_spec, sh_spec, g2_spec,
                      pl.BlockSpec((None, d, fc), lambda i, f, r: (layer, 0, f), **wmode),
                      pl.BlockSpec((None, d, fc), lambda i, f, r: (layer, 0, f), **wmode),
                      pl.BlockSpec((None, fc, d), lambda i, f, r: (layer, f, 0), **wmode),
                      pl.BlockSpec((1, d), const), pl.BlockSpec((1, d), const)],
            out_specs=pl.BlockSpec((tm, d), tok),
            scratch_shapes=[pltpu.VMEM((tm, d), BF16), pltpu.VMEM((tm, d), F32)]),
        compiler_params=_params(("arbitrary", "arbitrary"), 56),
        name="ffn_swiglu",
    )(rows, x, sc, sh, g2, wg, wu, wd, ln_g, ln_b)


def _odd_in_kernel(rows_ref, x_ref, sc_ref, sh_ref, w_ref, o_ref, u_ref, u_s):
    del rows_ref
    h = (x_ref[...] * (1.0 + sc_ref[0]) + sh_ref[0]).astype(BF16)
    proj = _dot(h, w_ref[...])
    o_ref[...] = proj
    n_tile = S5_WIDTH // LANE
    gpt = LANE // S5_GROUP
    for j in range(n_tile):
        u_s[j] = proj[:, j * LANE:(j + 1) * LANE]
    n_chunk = u_s.shape[1] // S5_CHUNK
    steps = [[u_s[j, pl.ds(s, n_chunk, stride=S5_CHUNK), :] for j in range(n_tile)] for s in range(S5_CHUNK)]
    for g in range(S5_GROUPS):
        sl = slice((g % gpt) * S5_GROUP, (g % gpt + 1) * S5_GROUP)
        u_ref[g] = jnp.concatenate([st[g // gpt][:, sl] for st in steps], axis=1).astype(BF16)


def _odd_in(x, sc, sh, w, rows, n_ctx_tiles, ctx_chunk0, lat_chunk0):
    n, d = x.shape
    tm = TM_PROJ
    nw = w.shape[1]
    cpt = tm // S5_CHUNK
    u_rows = n // S5_CHUNK
    tok = lambda i, r: (i, 0)
    (sc, sc_spec), (sh, sh_spec) = (_mod_operand(m, lambda i, r: r[i]) for m in (sc, sh))
    ublk = lambda i, r: (0, jnp.where(i < n_ctx_tiles, i + ctx_chunk0 // cpt, i - n_ctx_tiles + lat_chunk0 // cpt), 0)
    return pl.pallas_call(
        _odd_in_kernel,
        out_shape=(jax.ShapeDtypeStruct((n, nw), F32),
                   jax.ShapeDtypeStruct((S5_GROUPS, u_rows, S5_CHUNK * S5_GROUP), BF16)),
        grid_spec=pltpu.PrefetchScalarGridSpec(
            num_scalar_prefetch=1, grid=(n // tm,),
            in_specs=[pl.BlockSpec((tm, d), tok), sc_spec, sh_spec,
                      pl.BlockSpec((d, nw), lambda i, r: (0, 0))],
            out_specs=[pl.BlockSpec((tm, nw), tok), pl.BlockSpec((S5_GROUPS, cpt, S5_CHUNK * S5_GROUP), ublk)],
            scratch_shapes=[pltpu.VMEM((S5_WIDTH // LANE, tm, LANE), F32)]),
        compiler_params=_params(("arbitrary",), 32),
        name="odd_in_proj",
    )(rows, x, sc, sh, w)


def _s5_kernel(u_ref, zin_ref, t_ref, zout_ref, al_ref, h0_ref, y_ref, fin_ref, z_s, hf_s, hb_s, *, nb, nc):
    u = u_ref[0]
    z = _dot(u, zin_ref[0])
    for part in range(2):
        for b in range(nb):
            z_s[part, pl.ds(b, nc, stride=nb), :] = z[b * nc:(b + 1) * nc, part * LANE:(part + 1) * LANE]
    ar = al_ref[0, 0:1, :]
    ai = al_ref[0, 1:2, :]
    fwd = lax.broadcasted_iota(jnp.int32, (nb, LANE), 1) < S5_STATE
    aligned = (lambda r: pl.multiple_of(r, SUBLANE)) if nb % SUBLANE == 0 else (lambda r: r)

    def step(k, carry):
        re, im = carry
        rf = aligned(k * nb)
        rb = aligned((nc - 1 - k) * nb)
        hf_s[0, pl.ds(rf, nb), :] = re
        hf_s[1, pl.ds(rf, nb), :] = im
        hb_s[0, pl.ds(rb, nb), :] = re
        hb_s[1, pl.ds(rb, nb), :] = im
        zr = jnp.where(fwd, z_s[0, pl.ds(rf, nb), :], z_s[0, pl.ds(rb, nb), :])
        zi = jnp.where(fwd, z_s[1, pl.ds(rf, nb), :], z_s[1, pl.ds(rb, nb), :])
        return ar * re - ai * im + zr, ar * im + ai * re + zi

    h0 = h0_ref[0]
    re, im = lax.fori_loop(0, nc, step, (h0[:, :LANE], h0[:, LANE:]))
    fin_ref[0, :, :LANE] = re
    fin_ref[0, :, LANE:] = im
    m = u.shape[0]
    is_fwd = lax.broadcasted_iota(jnp.int32, (m, LANE), 1) < S5_STATE
    halves = []
    for part in range(2):
        hf_s[part] = jnp.where(is_fwd, hf_s[part], hb_s[part])
        halves.append(jnp.concatenate([hf_s[part, pl.ds(b, nc, stride=nb), :] for b in range(nb)], axis=0))
    h_in = jnp.concatenate(halves, axis=1).astype(BF16)
    y_ref[0] = _dot(u, t_ref[0]) + _dot(h_in, zout_ref[0])


def _s5(u, mats, layer, h0, nb, nc, row_block):
    zin, tmat, zout, al = mats
    g, _, w = u.shape
    m = nb * nc
    blk = lambda i: (i, 0, 0)
    lblk = lambda i: (layer, i, 0, 0)
    return pl.pallas_call(
        functools.partial(_s5_kernel, nb=nb, nc=nc),
        out_shape=(jax.ShapeDtypeStruct((g, m, w), F32), jax.ShapeDtypeStruct((g, nb, w), F32)),
        grid=(g,),
        in_specs=[pl.BlockSpec((1, m, w), lambda i: (i, row_block, 0)),
                  pl.BlockSpec((None, 1, w, w), lblk), pl.BlockSpec((None, 1, w, w), lblk),
                  pl.BlockSpec((None, 1, w, w), lblk), pl.BlockSpec((None, 1, 2, LANE), lblk),
                  pl.BlockSpec((1, nb, w), blk)],
        out_specs=(pl.BlockSpec((1, m, w), blk), pl.BlockSpec((1, nb, w), blk)),
        scratch_shapes=[pltpu.VMEM((2, m, LANE), F32), pltpu.VMEM((2, m, LANE), F32), pltpu.VMEM((2, m, LANE), F32)],
        compiler_params=_params(("arbitrary",), 32),
        name="s5_scan",
    )(u, zin, tmat, zout, al, h0)


def _s5_matrices(a_re, a_im, log_dt, b_re, b_im, c_re, c_im):
    hp = lax.Precision.HIGHEST
    L = S5_CHUNK
    dt = jnp.exp(log_dt)[..., None]
    lam_re, lam_im = a_re * dt, a_im * dt

    def power(k):
        k = k[:, None, None, None]
        mag = jnp.exp(lam_re * k)
        return mag * jnp.cos(lam_im * k), mag * jnp.sin(lam_im * k)

    ab_re, ab_im = power(jnp.ones((1,), F32))
    ab_re, ab_im = ab_re[0], ab_im[0]
    num_re, num_im = ab_re - 1.0, ab_im
    den = a_re * a_re + a_im * a_im
    f_re = (num_re * a_re + num_im * a_im) / den
    f_im = (num_im * a_re - num_re * a_im) / den
    bb_re = f_re[..., None] * b_re - f_im[..., None] * b_im
    bb_im = f_re[..., None] * b_im + f_im[..., None] * b_re

    ks = jnp.arange(L + 1, dtype=F32)
    pw_re, pw_im = power(ks)

    def zin_dir(d, exps):
        pr, pi = pw_re[exps, d], pw_im[exps, d]
        w_re = pr[..., None] * bb_re[d][None] - pi[..., None] * bb_im[d][None]
        w_im = pr[..., None] * bb_im[d][None] + pi[..., None] * bb_re[d][None]
        to = lambda w: jnp.transpose(w, (1, 0, 3, 2)).reshape(S5_GROUPS, L * S5_GROUP, S5_STATE)
        return to(w_re), to(w_im)

    steps = np.arange(L)
    zf_re, zf_im = zin_dir(0, L - 1 - steps)
    zb_re, zb_im = zin_dir(1, steps)
    zin = jnp.concatenate([zf_re, zb_re, zf_im, zb_im], axis=-1)

    def zout_dir(d, exps):
        pr, pi = pw_re[exps, d], pw_im[exps, d]
        cr, ci = c_re[d], c_im[d]
        e_re = cr[None] * pr[:, :, None, :] - ci[None] * pi[:, :, None, :]
        e_im = cr[None] * pi[:, :, None, :] + ci[None] * pr[:, :, None, :]
        to = lambda e: jnp.transpose(e, (1, 3, 0, 2)).reshape(S5_GROUPS, S5_STATE, L * S5_GROUP)
        return to(e_re), to(-e_im)

    of_re, of_im = zout_dir(0, steps + 1)
    ob_re, ob_im = zout_dir(1, L - steps)
    zout = jnp.concatenate([of_re, ob_re, of_im, ob_im], axis=1)

    def taps(d):
        pr, pi = pw_re[:L, d], pw_im[:L, d]
        m_re = pr[..., None] * bb_re[d][None] - pi[..., None] * bb_im[d][None]
        m_im = pr[..., None] * bb_im[d][None] + pi[..., None] * bb_re[d][None]
        return (jnp.einsum('gpn,lgnq->lgpq', c_re[d], m_re, precision=hp)
                - jnp.einsum('gpn,lgnq->lgpq', c_im[d], m_im, precision=hp))

    kf, kb = taps(0), taps(1)
    pad4 = ((0, 0),) * 3
    tm = jnp.stack([jnp.pad(kf[:L - s], ((s, 0),) + pad4) + jnp.pad(kb[:s + 1][::-1], ((0, L - 1 - s),) + pad4)
                    for s in range(L)])
    tmat = jnp.transpose(tm, (2, 0, 4, 1, 3)).reshape(S5_GROUPS, L * S5_GROUP, L * S5_GROUP)

    al = jnp.stack([jnp.concatenate([pw_re[L, 0], pw_re[L, 1]], axis=-1),
                    jnp.concatenate([pw_im[L, 0], pw_im[L, 1]], axis=-1)], axis=1)
    return zin.astype(BF16), tmat.astype(BF16), zout.astype(BF16), al


def _mla_kv(ckv, kpe, wk_ref, wv_ref, pl_ref, k_ref, v_ref):
    cb = ckv.astype(BF16)
    k_ref[...] = (_dot(cb, wk_ref[...]) + _dot(kpe.astype(BF16), pl_ref[...])).astype(BF16)
    v_ref[...] = _dot(cb, wv_ref[...]).astype(BF16)


def _mla_prep_kernel(rblk_ref, p_ref, qg_ref, kvg_ref, wuq_ref, wk_ref, wv_ref, pl_ref, cos_ref, sa_ref, sb_ref,
                     q_ref, k_ref, v_ref, ckv_ref):
    del rblk_ref
    pr = p_ref[...]
    cos, sa, sb = cos_ref[...], sa_ref[...], sb_ref[...]
    quarter = MLA_ROPE // 4
    scale = (MLA_NOPE + MLA_ROPE) ** -0.5
    qn = _rms(pr[:, :MLA_Q_RANK], qg_ref[...], MLA_Q_RANK).astype(BF16)
    q = _dot(qn, wuq_ref[...])
    for hd in range(MLA_HEADS):
        sl = slice(hd * LANE, (hd + 1) * LANE)
        q_ref[:, sl] = (_rope(q[:, sl], cos, sa, sb, quarter) * scale).astype(BF16)
    ckv = _rms(pr[:, MLA_Q_RANK:MLA_Q_RANK + MLA_KV_RANK], kvg_ref[...], MLA_KV_RANK)
    ckv_ref[...] = ckv
    kpe = _rope(pr[:, MLA_Q_RANK + MLA_KV_RANK:], cos, sa, sb, quarter)
    _mla_kv(ckv, kpe, wk_ref, wv_ref, pl_ref, k_ref, v_ref)


def _mla_prep(proj, qg, kvg, wuq, wk, wv, place, tables, rblk):
    n = proj.shape[0]
    tm = TM_PROJ
    hw = MLA_HEADS * LANE
    pw = MLA_Q_RANK + MLA_KV_RANK + LANE
    cos, sa, sb = tables
    tok = lambda i, r: (i, 0)
    const = lambda i, r: (0, 0)
    tab = lambda i, r: (r[i], 0)
    return pl.pallas_call(
        _mla_prep_kernel,
        out_shape=(jax.ShapeDtypeStruct((n, hw), BF16), jax.ShapeDtypeStruct((n, hw), BF16),
                   jax.ShapeDtypeStruct((n, hw), BF16), jax.ShapeDtypeStruct((n, MLA_KV_RANK), F32)),
        grid_spec=pltpu.PrefetchScalarGridSpec(
            num_scalar_prefetch=1, grid=(n // tm,),
            in_specs=[pl.BlockSpec((tm, pw), lambda i, r: (i, 1)),
                      pl.BlockSpec((1, MLA_Q_RANK), const), pl.BlockSpec((1, MLA_KV_RANK), const),
                      pl.BlockSpec((MLA_Q_RANK, hw), const), pl.BlockSpec((MLA_KV_RANK, hw), const),
                      pl.BlockSpec((MLA_KV_RANK, hw), const), pl.BlockSpec((LANE, hw), const),
                      pl.BlockSpec((tm, LANE), tab), pl.BlockSpec((tm, LANE), tab), pl.BlockSpec((tm, LANE), tab)],
            out_specs=[pl.BlockSpec((tm, hw), tok), pl.BlockSpec((tm, hw), tok), pl.BlockSpec((tm, hw), tok),
                       pl.BlockSpec((tm, MLA_KV_RANK), tok)]),
        compiler_params=_params(("arbitrary",), 32),
        name="mla_prep",
    )(rblk, proj, qg, kvg, wuq, wk, wv, place, cos, sa, sb)


def _mla_cache_kernel(c_ref, p_ref, wk_ref, wv_ref, pl_ref, k_ref, v_ref):
    _mla_kv(c_ref[...], p_ref[...], wk_ref, wv_ref, pl_ref, k_ref, v_ref)


def _mla_cache(ckv, kpe, wk, wv, place):
    n = ckv.shape[0]
    tm = min(TM_PROJ, n)
    assert n % tm == 0
    hw = MLA_HEADS * LANE
    tok = lambda i: (i, 0)
    const = lambda i: (0, 0)
    return pl.pallas_call(
        _mla_cache_kernel,
        out_shape=(jax.ShapeDtypeStruct((n, hw), BF16), jax.ShapeDtypeStruct((n, hw), BF16)),
        grid=(n // tm,),
        in_specs=[pl.BlockSpec((tm, MLA_KV_RANK), tok), pl.BlockSpec((tm, LANE), tok),
                  pl.BlockSpec((MLA_KV_RANK, hw), const), pl.BlockSpec((MLA_KV_RANK, hw), const),
                  pl.BlockSpec((LANE, hw), const)],
        out_specs=[pl.BlockSpec((tm, hw), tok), pl.BlockSpec((tm, hw), tok)],
        compiler_params=_params(("arbitrary",), 32),
        name="mla_cache_kv",
    )(ckv, kpe, wk, wv, place)


def _gelu_tanh(x):
    return 0.5 * x * (1.0 + jnp.tanh(math.sqrt(2.0 / math.pi) * (x + 0.044715 * (x * x * x))))


def _odd_out_kernel(rows_ref, u_ref, yp_ref, ys_ref, ap_ref, as_ref, x_ref, g_ref, d_ref, wglu_ref, bglu_ref, w_ref,
                    lg_ref, lb_ref, o_ref, y_s, *, alpha, n_ctx_tiles):
    del rows_ref
    i = pl.program_id(0)
    yg = _pick_pass(i, n_ctx_tiles, yp_ref, ys_ref)
    n_chunk = yg.shape[1]
    n_tile = S5_WIDTH // LANE
    gpt = LANE // S5_GROUP
    for s in range(S5_CHUNK):
        sl = slice(s * S5_GROUP, (s + 1) * S5_GROUP)
        for j in range(n_tile):
            y_s[j, pl.ds(s, n_chunk, stride=S5_CHUNK), :] = jnp.concatenate(
                [yg[g][:, sl] for g in range(j * gpt, (j + 1) * gpt)], axis=1)
    y_ssm = jnp.concatenate([y_s[j] for j in range(n_tile)], axis=1)
    y = _gelu_tanh(u_ref[...] * d_ref[...] + y_ssm)
    y = y * jax.nn.sigmoid(_dot(y.astype(BF16), wglu_ref[...]) + bglu_ref[...])
    attn = _pick_pass(i, n_ctx_tiles, ap_ref, as_ref)
    out = _dot(y.astype(BF16), w_ref[:S5_WIDTH, :]) + _dot(attn, w_ref[S5_WIDTH:, :])
    o_ref[...] = _layer_norm(alpha * x_ref[...] + g_ref[0] * out, lg_ref[...], lb_ref[...])


def _odd_out(proj, y_p, y_s, attn_p, attn_s, x, g1, s5_d, w_glu, b_glu, w_out, ln_g, ln_b, rows, alpha):
    n, d = x.shape
    tm = TM_PROJ
    aw = attn_p.shape[1]
    cpt = tm // S5_CHUNK
    n_ctx, n_lat = attn_p.shape[0] // tm, attn_s.shape[0] // tm
    ap_spec, as_spec = _pass_specs((tm, aw), n_ctx, n_lat)
    yp_spec, ys_spec = _pass_specs((S5_GROUPS, cpt, S5_CHUNK * S5_GROUP), n_ctx, n_lat)
    tok = lambda i, r: (i, 0)
    const = lambda i, r: (0, 0)
    g1, g1_spec = _mod_operand(g1, lambda i, r: r[i])
    return pl.pallas_call(
        functools.partial(_odd_out_kernel, alpha=alpha, n_ctx_tiles=n_ctx),
        out_shape=jax.ShapeDtypeStruct((n, d), F32),
        grid_spec=pltpu.PrefetchScalarGridSpec(
            num_scalar_prefetch=1, grid=(n // tm,),
            in_specs=[pl.BlockSpec((tm, S5_WIDTH), tok), yp_spec, ys_spec,
                      ap_spec, as_spec, pl.BlockSpec((tm, d), tok),
                      g1_spec,
                      pl.BlockSpec((1, S5_WIDTH), const), pl.BlockSpec((S5_WIDTH, S5_WIDTH), const),
                      pl.BlockSpec((1, S5_WIDTH), const), pl.BlockSpec((S5_WIDTH + aw, d), const),
                      pl.BlockSpec((1, d), const), pl.BlockSpec((1, d), const)],
            out_specs=pl.BlockSpec((tm, d), tok),
            scratch_shapes=[pltpu.VMEM((S5_WIDTH // LANE, tm, LANE), F32)]),
        compiler_params=_params(("arbitrary",), 40),
        name="odd_out_proj",
    )(rows, proj, y_p, y_s, attn_p, attn_s, x, g1, s5_d, w_glu, b_glu, w_out, ln_g, ln_b)


def _router_kernel(rows_ref, x_ref, sc_ref, sh_ref, rt_ref, hb_ref, pos_ref, gate_ref, pt_ref, cnt_ref, tri_s):
    del rows_ref
    t = x_ref.shape[0]
    ne = N_EXPERTS

    @pl.when(pl.program_id(0) == 0)
    def _():
        before = lax.broadcasted_iota(jnp.int32, (t, t), 0) < lax.broadcasted_iota(jnp.int32, (t, t), 1)
        tri_s[...] = jnp.where(before, 1.0, 0.0).astype(BF16)

    h = x_ref[...] * (1.0 + sc_ref[0]) + sh_ref[0]
    h_hi, h_lo = _split(h)
    hb_ref[...] = h_hi
    r_hi, r_lo = _split(rt_ref[...])
    logits = _dot_nt(r_hi, h_hi) + _dot_nt(r_lo, h_hi) + _dot_nt(r_hi, h_lo)
    eid = lax.broadcasted_iota(jnp.int32, (ne, t), 0).astype(F32)
    m0 = jnp.max(logits, axis=0, keepdims=True)
    i0 = jnp.min(jnp.where(logits == m0, eid, float(ne)), axis=0, keepdims=True)
    rest = jnp.where(eid == i0, -jnp.inf, logits)
    m1 = jnp.max(rest, axis=0, keepdims=True)
    i1 = jnp.min(jnp.where(rest == m1, eid, float(ne)), axis=0, keepdims=True)
    ex = jnp.exp(m1 - m0)
    g0 = 1.0 / (1.0 + ex)
    g1 = ex / (1.0 + ex)
    sel0 = eid == i0
    sel1 = eid == i1
    member = jnp.where(sel0, 1.0, jnp.where(sel1, 1.0, 0.0))
    gate = jnp.where(sel0, g0, jnp.where(sel1, g1, 0.0))
    rank = _dot(member.astype(BF16), tri_s[...])
    pos = jnp.where(member > 0.0, rank, -1.0)
    pos_ref[0] = pos.astype(jnp.int32)
    gate_ref[0] = gate
    cnt = jnp.sum(member, axis=1, keepdims=True)
    cnt_ref[0] = jnp.broadcast_to(cnt, (ne, LANE)).astype(jnp.int32)
    packed = jnp.concatenate([pos, gate, jnp.zeros((LANE - 2 * ne, t), F32)], axis=0)
    pt_ref[...] = packed.T


def _router(x, sc, sh, router_t, rows):
    n, d = x.shape
    t = T_MOE
    nb = n // t
    ne = N_EXPERTS
    tok = lambda i, r: (i, 0)
    (sc, sc_spec), (sh, sh_spec) = (_mod_operand(m, lambda i, r: r[i]) for m in (sc, sh))
    blk3 = lambda i, r: (i, 0, 0)
    return pl.pallas_call(
        _router_kernel,
        out_shape=(jax.ShapeDtypeStruct((n, d), BF16), jax.ShapeDtypeStruct((nb, ne, t), jnp.int32),
                   jax.ShapeDtypeStruct((nb, ne, t), F32),
                   jax.ShapeDtypeStruct((n, LANE), F32), jax.ShapeDtypeStruct((nb, ne, LANE), jnp.int32)),
        grid_spec=pltpu.PrefetchScalarGridSpec(
            num_scalar_prefetch=1, grid=(nb,),
            in_specs=[pl.BlockSpec((t, d), tok), sc_spec, sh_spec,
                      pl.BlockSpec((ne, d), lambda i, r: (0, 0))],
            out_specs=[pl.BlockSpec((t, d), tok), pl.BlockSpec((1, ne, t), blk3), pl.BlockSpec((1, ne, t), blk3),
                       pl.BlockSpec((t, LANE), tok), pl.BlockSpec((1, ne, LANE), blk3)],
            scratch_shapes=[pltpu.VMEM((t, t), BF16)]),
        compiler_params=_params(("arbitrary",), 48),
        name="moe_router",
    )(rows, x, sc, sh, router_t)


def _segment_copies(rows, src, dst, src0, dst0, sems, slot):
    out = []
    for k, sz in enumerate(SEG_SIZES):
        off = (rows // (2 * sz)) * (2 * sz)
        s0 = pl.multiple_of(src0 + off, SEG_ALIGN)
        d0 = pl.multiple_of(dst0 + off, SEG_ALIGN)
        sem = sems.at[slot * len(SEG_SIZES) + k]
        out.append(((rows & sz) != 0, pltpu.make_async_copy(src.at[pl.ds(s0, sz)], dst.at[pl.ds(d0, sz)], sem)))
    return out


def _run_copies(copies):
    for pred, cp in copies:
        pl.when(pred)(cp.start)
    for pred, cp in copies:
        pl.when(pred)(cp.wait)


def _moe_gather_kernel(cnt_ref, base_ref, h_ref, pos_ref, gate_ref, xs_in, gs_in, xs_ref, gs_ref,
                       xe_s, ge_s, sem_x, sem_g):
    del xs_in, gs_in
    b = pl.program_id(0)
    t = h_ref.shape[0]
    rt = ROW_MOE
    ne = N_EXPERTS
    buf = b % 2

    def block_copies(blk, buf_):
        out, off = [], 0
        for e in range(ne):
            rows = ((cnt_ref[blk * ne + e] + SEG_ALIGN - 1) // SEG_ALIGN) * SEG_ALIGN
            base = base_ref[blk * ne + e]
            out.append((off, _segment_copies(rows, xe_s.at[buf_], xs_ref, off, base, sem_x, buf_ * ne + e)
                        + _segment_copies(rows, ge_s.at[buf_], gs_ref, off, base, sem_g, buf_ * ne + e)))
            off = off + rows
        return out

    def wait_all(segs):
        for _, seg in segs:
            for pred, cp in seg:
                pl.when(pred)(cp.wait)

    mine = block_copies(b, buf)
    xe, ge = xe_s.at[buf], ge_s.at[buf]
    for e in range(ne):
        cnt = cnt_ref[b * ne + e]
        pos_row = pos_ref[0, e:e + 1, :]
        gate_row = gate_ref[0, e:e + 1, :]
        off = mine[e][0]

        def gather(r, c, off=off, pos_row=pos_row, gate_row=gate_row):
            r0 = r * rt
            hit = (lax.broadcasted_iota(jnp.int32, (rt, t), 0) + r0) == pos_row
            onehot = jnp.where(hit, 1.0, 0.0).astype(BF16)
            dst = pl.multiple_of(off + r0, SEG_ALIGN)
            xe[pl.ds(dst, rt), :] = _dot(onehot, h_ref[...]).astype(BF16)
            g = jnp.sum(jnp.where(hit, gate_row, 0.0), axis=1, keepdims=True)
            ge[pl.ds(dst, rt), :] = jnp.broadcast_to(g, (rt, LANE))
            return c

        lax.fori_loop(0, (cnt + rt - 1) // rt, gather, 0)
        for pred, cp in mine[e][1]:
            pl.when(pred)(cp.start)

    @pl.when(b > 0)
    def _():
        wait_all(block_copies(b - 1, 1 - buf))

    @pl.when(b == pl.num_programs(0) - 1)
    def _():
        wait_all(mine)


def _moe_gather(h, pos, gate, counts, base, n_rows):
    n, d = h.shape
    t = T_MOE
    nb = n // t
    ne = N_EXPERTS
    tok = lambda b, c, o: (b, 0)
    blk3 = lambda b, c, o: (b, 0, 0)
    any_spec = pl.BlockSpec(memory_space=pl.ANY)
    xs0 = jnp.zeros((n_rows, d), BF16)
    gs0 = jnp.zeros((n_rows, LANE), F32)
    cap = -(-(2 * t + ne * (SEG_ALIGN - 1) + ROW_MOE - 1) // ROW_MOE) * ROW_MOE
    return pl.pallas_call(
        _moe_gather_kernel,
        out_shape=(jax.ShapeDtypeStruct((n_rows, d), BF16), jax.ShapeDtypeStruct((n_rows, LANE), F32)),
        grid_spec=pltpu.PrefetchScalarGridSpec(
            num_scalar_prefetch=2, grid=(nb,),
            in_specs=[pl.BlockSpec((t, d), tok), pl.BlockSpec((1, ne, t), blk3), pl.BlockSpec((1, ne, t), blk3),
                      any_spec, any_spec],
            out_specs=[any_spec, any_spec],
            scratch_shapes=[pltpu.VMEM((2, cap, d), BF16), pltpu.VMEM((2, cap, LANE), F32),
                            pltpu.SemaphoreType.DMA((2 * ne * len(SEG_SIZES),)),
                            pltpu.SemaphoreType.DMA((2 * ne * len(SEG_SIZES),))]),
        input_output_aliases={5: 0, 6: 1},
        compiler_params=_params(("arbitrary",), 32),
        name="moe_gather",
    )(counts, base, h, pos, gate, xs0, gs0)


def _stream_cast(src, dst):
    r, c = dst.shape
    rows = r // GMM_STAGE_PIECES
    slots = GMM_STAGE_SLOTS

    def run(stage, sems):
        def piece(k):
            return pltpu.make_async_copy(src.at[pl.ds(k * rows, rows)], stage.at[k % slots], sems.at[k % slots])

        for k in range(slots - 1):
            piece(k).start()
        for k in range(GMM_STAGE_PIECES):
            if k + slots - 1 < GMM_STAGE_PIECES:
                piece(k + slots - 1).start()
            piece(k).wait()
            dst[k * rows:(k + 1) * rows, :] = stage[k % slots].astype(BF16)

    pl.run_scoped(run, pltpu.VMEM((slots, rows, c), F32), pltpu.SemaphoreType.DMA((slots,)))


def _moe_gmm_kernel(te_ref, valid_ref, x_ref, gs_ref, wg_hbm, wu_hbm, wd_hbm, y_ref, wg_s, wu_s, wd_s, *, layer):
    i = pl.program_id(0)
    valid = valid_ref[i] != 0
    e = te_ref[i]
    switch = valid & ((i == 0) | (e != te_ref[jnp.maximum(i - 1, 0)]))

    @pl.when(switch)
    def _():
        _stream_cast(wg_hbm.at[layer, e], wg_s)
        _stream_cast(wu_hbm.at[layer, e], wu_s)
        _stream_cast(wd_hbm.at[layer, e], wd_s)

    @pl.when(valid)
    def _():
        x = x_ref[...]
        a = (_silu(_dot(x, wg_s[...])) * _dot(x, wu_s[...])).astype(BF16)
        y_ref[...] = (_dot(a, wd_s[...]) * gs_ref[:, 0:1]).astype(BF16)

    @pl.when(jnp.logical_not(valid))
    def _():
        y_ref[...] = jnp.zeros_like(y_ref)


def _moe_gmm(xs, gs, tile_expert, tile_valid, wg, wu, wd, layer):
    n_rows, d = xs.shape
    rt = RT_GMM
    ff = wg.shape[-1]
    tok = lambda i, te, tv: (i, 0)
    any_spec = pl.BlockSpec(memory_space=pl.ANY)
    return pl.pallas_call(
        functools.partial(_moe_gmm_kernel, layer=layer),
        out_shape=jax.ShapeDtypeStruct((n_rows, d), BF16),
        grid_spec=pltpu.PrefetchScalarGridSpec(
            num_scalar_prefetch=2, grid=(n_rows // rt,),
            in_specs=[pl.BlockSpec((rt, d), tok), pl.BlockSpec((rt, LANE), tok), any_spec, any_spec, any_spec],
            out_specs=pl.BlockSpec((rt, d), tok),
            scratch_shapes=[pltpu.VMEM((d, ff), BF16), pltpu.VMEM((d, ff), BF16), pltpu.VMEM((ff, d), BF16)]),
        compiler_params=_params(("arbitrary",), 56),
        name="moe_gmm",
    )(tile_expert, tile_valid, xs, gs, wg, wu, wd)


def _moe_combine_kernel(cnt_ref, base_ref, rows_ref, ys_ref, pt_ref, x_ref, g_ref, lg_ref, lb_ref, o_ref,
                        ycat_s, sems, *, alpha, block0):
    del rows_ref
    t = x_ref.shape[0]
    cap = ycat_s.shape[1]
    ne = N_EXPERTS
    i = pl.program_id(0)
    slot = i % 2

    def block_copies(blk, slot_):
        copies, offs, off = [], [], 0
        for e in range(ne):
            cnt = cnt_ref[blk * ne + e]
            rows = ((cnt + SEG_ALIGN - 1) // SEG_ALIGN) * SEG_ALIGN
            copies += _segment_copies(rows, ys_ref, ycat_s.at[slot_], base_ref[blk * ne + e], off, sems,
                                      slot_ * ne + e)
            offs.append(off)
            off = off + rows
        return copies, offs

    def start(copies):
        for pred, cp in copies:
            pl.when(pred)(cp.start)

    @pl.when(i == 0)
    def _():
        ycat_s[...] = jnp.zeros_like(ycat_s)
        start(block_copies(block0, 0)[0])

    @pl.when(i + 1 < pl.num_programs(0))
    def _():
        start(block_copies(i + 1 + block0, 1 - slot)[0])

    copies, offs = block_copies(i + block0, slot)
    for pred, cp in copies:
        pl.when(pred)(cp.wait)

    pt = pt_ref[...]
    lane = lax.broadcasted_iota(jnp.int32, (1, LANE), 1)
    off_lane = jnp.zeros((1, LANE), F32)
    for e in range(1, ne):
        off_lane = jnp.where(lane == e, offs[e].astype(F32), off_lane)
    chosen = (pt >= 0.0) & (lax.broadcasted_iota(jnp.int32, (t, LANE), 1) < ne)
    row_of = pt + off_lane
    s_lo = jnp.min(jnp.where(chosen, row_of, float(cap)), axis=1, keepdims=True)
    s_hi = jnp.max(jnp.where(chosen, row_of, -1.0), axis=1, keepdims=True)
    n_piece = 3
    wp = cap // n_piece
    f = None
    for j in range(n_piece):
        col = (lax.broadcasted_iota(jnp.int32, (t, wp), 1) + j * wp).astype(F32)
        sel = jnp.where(col == s_lo, 1.0, jnp.where(col == s_hi, 1.0, 0.0)).astype(BF16)
        part = _dot(sel, ycat_s.at[slot].at[pl.ds(j * wp, wp)][...])
        f = part if f is None else f + part
    o_ref[...] = _layer_norm(alpha * x_ref[...] + g_ref[0] * f, lg_ref[...], lb_ref[...])


def _moe_combine(ys, pt, x, g2, ln_g, ln_b, counts, base, rows, alpha, block0=0, n_blocks=None):
    n, d = x.shape
    t = T_MOE
    ne = N_EXPERTS
    n_blocks = n // t if n_blocks is None else n_blocks
    cap = -(-(2 * t + ne * (SEG_ALIGN - 1)) // (6 * LANE)) * (6 * LANE)
    tok = lambda b, c, o, r: (b + block0, 0)
    const = lambda b, c, o, r: (0, 0)
    g2, g2_spec = _mod_operand(g2, lambda b, c, o, r: r[b + block0])
    return pl.pallas_call(
        functools.partial(_moe_combine_kernel, alpha=alpha, block0=block0),
        out_shape=jax.ShapeDtypeStruct((n_blocks * t, d), F32),
        grid_spec=pltpu.PrefetchScalarGridSpec(
            num_scalar_prefetch=3, grid=(n_blocks,),
            in_specs=[pl.BlockSpec(memory_space=pl.ANY), pl.BlockSpec((t, LANE), tok), pl.BlockSpec((t, d), tok),
                      g2_spec,
                      pl.BlockSpec((1, d), const), pl.BlockSpec((1, d), const)],
            out_specs=pl.BlockSpec((t, d), lambda b, c, o, r: (b, 0)),
            scratch_shapes=[pltpu.VMEM((2, cap, d), BF16), pltpu.SemaphoreType.DMA((2 * ne * len(SEG_SIZES),))]),
        compiler_params=_params(("arbitrary",), 48),
        name="moe_combine",
    )(counts, base, rows, ys, pt, x, g2, ln_g, ln_b)


def _moe_layout(cnt, n_tiles):
    cnt16 = (cnt + SEG_ALIGN - 1) // SEG_ALIGN * SEG_ALIGN
    rows_e = jnp.sum(cnt16, axis=0)
    rows_e = (rows_e + RT_GMM - 1) // RT_GMM * RT_GMM
    ends = jnp.cumsum(rows_e)
    base = (ends - rows_e)[None, :] + jnp.cumsum(cnt16, axis=0) - cnt16
    tile_end = ends // RT_GMM
    idx = jnp.arange(n_tiles, dtype=jnp.int32)
    valid = idx < tile_end[-1]
    te = jnp.sum((idx[:, None] >= tile_end[None, :]).astype(jnp.int32), axis=1)
    last = jnp.sum(((tile_end[-1] - 1) >= tile_end).astype(jnp.int32))
    te = jnp.where(valid, te, last)
    return base.reshape(-1).astype(jnp.int32), te.astype(jnp.int32), valid.astype(jnp.int32)


def _tile_rows(n, p, ss, tm):
    starts = np.arange(n // tm) * tm
    return jnp.asarray(np.where(starts < p, 0, 1 + (starts - p) // ss).astype(np.int32))


def _tile_rope_blocks(n, p, ss, tm):
    starts = np.arange(n // tm) * tm
    return jnp.asarray(np.where(starts < p, 0, 1 + ((starts - p) % ss) // tm).astype(np.int32))


def _tile_edges(n, p, sp, ss, tm):
    starts = np.arange(n // tm) * tm
    pos = np.where(starts < p, starts % sp, (starts - p) % ss)
    seq = np.where(starts < p, sp, ss)
    return jnp.asarray((pos == 0).astype(np.int32)), jnp.asarray(((pos + tm) % seq == 0).astype(np.int32))


def _rope_tables(n_tokens, rot_dim, lane0, tm):
    rows = n_tokens // GRID_W
    row_pos = jnp.repeat(jnp.arange(rows, dtype=F32), GRID_W)
    col_pos = jnp.tile(jnp.arange(GRID_W, dtype=F32), rows)
    half = rot_dim // 2
    qtr = half // 2
    inv_freq = ROPE_THETA ** (-jnp.arange(0, half, 2, dtype=F32) / half)
    ang_r = row_pos[:, None] * inv_freq
    ang_c = col_pos[:, None] * inv_freq
    ang = jnp.concatenate([ang_r, ang_r, ang_c, ang_c], axis=-1)
    cos, sin = jnp.cos(ang), jnp.sin(ang)
    first = ((np.arange(rot_dim) % half) < qtr).astype(np.float32)
    sa = -sin * first
    sb = sin * (1.0 - first)

    def place(tbl, fill):
        full = jnp.full((n_tokens, LANE), fill, F32).at[:, lane0:lane0 + rot_dim].set(tbl)
        return jnp.concatenate([jnp.full((tm, LANE), fill, F32), full], axis=0)

    return place(cos, 1.0), place(sa, 0.0), place(sb, 0.0)


def _pad_heads(w, n_heads, width):
    lead = w.shape[:-1]
    w = w.reshape(lead + (n_heads, width))
    w = jnp.pad(w, [(0, 0)] * len(lead) + [(0, 0), (0, LANE - width)])
    return w.reshape(lead + (n_heads * LANE,))


def _pad_head_rows(w, n_heads, width):
    d = w.shape[-1]
    w = w.reshape(n_heads, width, d)
    return jnp.pad(w, ((0, 0), (0, LANE - width), (0, 0))).reshape(n_heads * LANE, d)


def kernel(x_prompt, x_sample, cache_attn_k, cache_attn_v, cache_mla_ckv, cache_mla_kpe, state_ssm_re, state_ssm_im, c, c_ctx, ada_w, ada_b, ln_g, ln_b, ev_w_in, ev_conv_w, ev_q_gain, ev_k_gain, ev_w_out, ffn_w_gate, ffn_w_up, ffn_w_down, od_w_in, s5_a_re, s5_a_im, s5_log_dt, s5_b_re, s5_b_im, s5_c_re, s5_c_im, s5_d, s5_w_glu, s5_b_glu, mla_q_gain, mla_w_uq, mla_kv_gain, mla_w_ukv, od_w_out, moe_router, moe_w_gate, moe_w_up, moe_w_down):
    bp, sp, d = x_prompt.shape
    bs, ss, _ = x_sample.shape
    depth = ada_w.shape[0]
    alpha = (2 * depth) ** 0.25
    p = bp * sp
    n = p + bs * ss
    past = cache_attn_k.shape[2]
    for tm in (TM_PROJ, TM_EVEN_IN, TM_FFN, T_MOE):
        assert p % tm == 0 and ss % tm == 0 and (sp % tm == 0 or tm % sp == 0)
    assert sp % S5_CHUNK == 0 and ss % S5_CHUNK == 0

    x = (x_prompt.reshape(p, d), x_sample.reshape(bs * ss, d))

    nrow = -(-(1 + bs) // SUBLANE) * SUBLANE
    cond = jnp.zeros((nrow, d), F32).at[0].set(c_ctx).at[1:1 + bs].set(c)
    mod = _modulation(cond, ada_w, ada_b)

    def mod_part(l, k):
        return (mod, l, k)

    rows_proj = _tile_rows(n, p, ss, TM_PROJ)
    rows_ffn = _tile_rows(n, p, ss, TM_FFN)
    rows_moe = _tile_rows(n, p, ss, T_MOE)
    rblk = _tile_rope_blocks(n, p, ss, TM_PROJ)
    first, last = _tile_edges(n, p, sp, ss, TM_PROJ)
    rows_even = _tile_rows(n, p, ss, TM_EVEN_IN)
    rblk_even = _tile_rope_blocks(n, p, ss, TM_EVEN_IN)
    tables_even = _rope_tables(ss, HEAD_DIM, 0, TM_EVEN_IN)
    tables_mla = _rope_tables(ss, MLA_ROPE, MLA_NOPE, TM_PROJ)

    s5_mats = jax.vmap(_s5_matrices)(s5_a_re, s5_a_im, s5_log_dt, s5_b_re, s5_b_im, s5_c_re, s5_c_im)
    ffn_wg, ffn_wu, ffn_wd = ffn_w_gate.astype(BF16), ffn_w_up.astype(BF16), ffn_w_down.astype(BF16)
    out_k, out_v, out_ckv, out_kpe, out_sre, out_sim = [], [], [], [], [], []
    cw = CONV_WIDTH
    hq = ATTN_HEADS * HEAD_DIM
    hkv = ATTN_KV_HEADS * HEAD_DIM
    for l in range(depth):
        i = l // 2
        sh1, sc1, g1, sh2, sc2, g2 = [mod_part(l, k) for k in range(6)]
        lg = ln_g[l].reshape(2, 1, d)
        lb = ln_b[l].reshape(2, 1, d)
        if l % 2 == 0:
            w_in = ev_w_in[i]
            wc = w_in[:, :3 * cw].astype(BF16)
            wq = _pad_heads(w_in[:, 3 * cw:3 * cw + hq], ATTN_HEADS, HEAD_DIM).astype(BF16)
            wkv = jnp.concatenate([_pad_heads(w_in[:, 3 * cw + hq:3 * cw + hq + hkv], ATTN_KV_HEADS, HEAD_DIM),
                                   _pad_heads(w_in[:, 3 * cw + hq + hkv:], ATTN_KV_HEADS, HEAD_DIM)],
                                  axis=1).astype(BF16)
            qg = jnp.pad(ev_q_gain[i], (0, LANE - HEAD_DIM)).reshape(1, LANE)
            kg = jnp.pad(ev_k_gain[i], (0, LANE - HEAD_DIM)).reshape(1, LANE)
            conv, q, k, v, kn, vf = _even_in(x, sc1, sh1, wc, wq, wkv, qg, kg, tables_even, rows_even, rblk_even)
            kvw = ATTN_KV_HEADS * LANE
            out_k.append(kn[:p].reshape(bp, sp, ATTN_KV_HEADS, LANE)[..., :HEAD_DIM])
            out_v.append(vf[:p].reshape(bp, sp, ATTN_KV_HEADS, LANE)[..., :HEAD_DIM])
            group = ATTN_HEADS // ATTN_KV_HEADS
            qw = ATTN_HEADS * LANE
            a_p = _attention(q, [(k, v, True)], ATTN_HEADS, group, bp, sp, 0)
            kc = _pad_heads(cache_attn_k[:, i].reshape(bs, past, hkv), ATTN_KV_HEADS, HEAD_DIM).astype(BF16)
            vc = _pad_heads(cache_attn_v[:, i].reshape(bs, past, hkv), ATTN_KV_HEADS, HEAD_DIM).astype(BF16)
            a_s = _attention(q, [(kc, vc, False), (k, v, True)], ATTN_HEADS, group, bs, ss, p)
            w_out = jnp.concatenate([ev_w_out[i][:cw], _pad_head_rows(ev_w_out[i][cw:], ATTN_HEADS, HEAD_DIM)],
                                    axis=0).astype(BF16)
            conv_w = jnp.pad(ev_conv_w[i], ((0, SUBLANE - ev_conv_w.shape[1]), (0, 0)))
            x = _even_out(conv, a_p.reshape(p, qw), a_s.reshape(bs * ss, qw), x, g1, conv_w, w_out, lg[0], lb[0],
                          first, last, rows_proj, alpha, sp)
            x = _ffn(x, sc2, sh2, g2, ffn_wg, ffn_wu, ffn_wd, i, lg[1], lb[1], rows_ffn, alpha)
        else:
            q_end = S5_WIDTH + MLA_Q_RANK
            kv_end = q_end + MLA_KV_RANK
            w_in = od_w_in[i]
            w_in = jnp.concatenate([w_in[:, :kv_end], jnp.zeros((d, MLA_NOPE), F32), w_in[:, kv_end:],
                                    jnp.zeros((d, LANE - MLA_NOPE - MLA_ROPE), F32)], axis=1).astype(BF16)
            m_ctx, m_lat = p // S5_CHUNK, bs * ss // S5_CHUNK
            ctx_chunk0, lat_chunk0 = (m_lat, 0) if m_lat >= m_ctx else (0, m_ctx)
            assert ctx_chunk0 % m_ctx == 0 and lat_chunk0 % m_lat == 0
            proj, u = _odd_in(x, sc1, sh1, w_in, rows_proj, p // TM_PROJ, ctx_chunk0, lat_chunk0)
            out_kpe.append(proj[:p, kv_end + MLA_NOPE:kv_end + MLA_NOPE + MLA_ROPE].reshape(bp, sp, MLA_ROPE))

            h0_p = jnp.zeros((S5_GROUPS, bp, 4 * S5_STATE), F32)
            y_p, fin_p = _s5(u, s5_mats, i, h0_p, bp, sp // S5_CHUNK, ctx_chunk0 // m_ctx)

            def pack_state(re, im):
                to = lambda a: a.transpose(2, 0, 1, 3).reshape(S5_GROUPS, a.shape[0], 2 * S5_STATE)
                return jnp.concatenate([to(re), to(im)], axis=-1)

            y_s, _ = _s5(u, s5_mats, i, pack_state(state_ssm_re[:, i], state_ssm_im[:, i]), bs, ss // S5_CHUNK,
                         lat_chunk0 // m_lat)
            fin = fin_p.reshape(S5_GROUPS, bp, 2, 2, S5_STATE)
            out_sre.append(fin[:, :, 0].transpose(1, 2, 0, 3))
            out_sim.append(fin[:, :, 1].transpose(1, 2, 0, 3))

            qk = MLA_NOPE + MLA_ROPE
            wuq = _pad_heads(mla_w_uq[i], MLA_HEADS, qk).astype(BF16)
            wukv = mla_w_ukv[i].reshape(MLA_KV_RANK, MLA_HEADS, MLA_NOPE + MLA_V)
            wk = _pad_heads(wukv[..., :MLA_NOPE].reshape(MLA_KV_RANK, -1), MLA_HEADS, MLA_NOPE).astype(BF16)
            wv = _pad_heads(wukv[..., MLA_NOPE:].reshape(MLA_KV_RANK, -1), MLA_HEADS, MLA_V).astype(BF16)
            place = np.zeros((LANE, MLA_HEADS * LANE), np.float32)
            for hd in range(MLA_HEADS):
                for j in range(MLA_ROPE):
                    place[MLA_NOPE + j, hd * LANE + MLA_NOPE + j] = 1.0
            place = jnp.asarray(place, BF16)
            q, k, v, ckv = _mla_prep(proj, mla_q_gain[i].reshape(1, -1), mla_kv_gain[i].reshape(1, -1),
                                     wuq, wk, wv, place, tables_mla, rblk)
            out_ckv.append(ckv[:p].reshape(bp, sp, MLA_KV_RANK))
            hw = MLA_HEADS * LANE
            a_p = _attention(q, [(k, v, True)], MLA_HEADS, 1, bp, sp, 0)
            kpe_c = jnp.pad(cache_mla_kpe[:, i].reshape(bs * past, MLA_ROPE),
                            ((0, 0), (MLA_NOPE, LANE - MLA_NOPE - MLA_ROPE)))
            kc, vc = _mla_cache(cache_mla_ckv[:, i].reshape(bs * past, MLA_KV_RANK), kpe_c, wk, wv, place)
            a_s = _attention(q, [(kc.reshape(bs, past, hw), vc.reshape(bs, past, hw), False), (k, v, True)],
                             MLA_HEADS, 1, bs, ss, p)
            w_out = jnp.concatenate([od_w_out[i][:S5_WIDTH], _pad_head_rows(od_w_out[i][S5_WIDTH:], MLA_HEADS, MLA_V)],
                                    axis=0).astype(BF16)
            x = _odd_out(proj, y_p, y_s, a_p.reshape(p, hw), a_s.reshape(bs * ss, hw), x, g1, s5_d[i].reshape(1, -1),
                         s5_w_glu[i].astype(BF16), s5_b_glu[i].reshape(1, -1), w_out, lg[0], lb[0], rows_proj, alpha)

            hb, pos, gate, pt, cnt = _router(x, sc2, sh2, moe_router[i].T, rows_moe)
            cnt = cnt[:, :, 0]
            nblk = n // T_MOE
            n_tiles = -(-(2 * n + nblk * N_EXPERTS * (SEG_ALIGN - 1)) // RT_GMM) + N_EXPERTS
            base, tile_expert, tile_valid = _moe_layout(cnt, n_tiles)
            counts = cnt.reshape(-1)
            xs, gs = _moe_gather(hb, pos, gate, counts, base, n_tiles * RT_GMM)
            ys = _moe_gmm(xs, gs, tile_expert, tile_valid, moe_w_gate, moe_w_up, moe_w_down, i)
            if l == depth - 1:
                nb_ctx = p // T_MOE
                x = tuple(_moe_combine(ys, pt, x, g2, lg[1], lb[1], counts, base, rows_moe, alpha, b0, nbk)
                          for b0, nbk in ((0, nb_ctx), (nb_ctx, nblk - nb_ctx)))
            else:
                x = _moe_combine(ys, pt, x, g2, lg[1], lb[1], counts, base, rows_moe, alpha)

    if not isinstance(x, tuple):
        x = (x[:p], x[p:])
    y_prompt = x[0].reshape(bp, sp, d)
    y_sample = x[1].reshape(bs, ss, d)
    return (y_prompt, y_sample, jnp.stack(out_k, axis=1), jnp.stack(out_v, axis=1),
            jnp.stack(out_ckv, axis=1), jnp.stack(out_kpe, axis=1),
            jnp.stack(out_sre, axis=1), jnp.stack(out_sim, axis=1))
```

```python
import functools
import math

import jax
import jax.numpy as jnp
import numpy as np
from jax import lax
from jax.experimental import pallas as pl
from jax.experimental.pallas import tpu as pltpu

F32 = jnp.float32
BF16 = jnp.bfloat16

LANE = 128
SUBLANE = 8
MIB = 1024 * 1024

GRID_W = 64
ROPE_THETA = 10000.0
LN_EPS = 1e-6
RMS_EPS = 1e-6
HEAD_DIM = 64
ATTN_HEADS = 8
ATTN_KV_HEADS = 2
CONV_WIDTH = 512
S5_WIDTH = 512
S5_GROUP = 16
S5_GROUPS = 32
S5_STATE = 64
S5_CHUNK = 16
MLA_HEADS = 8
MLA_Q_RANK = 256
MLA_KV_RANK = 128
MLA_NOPE = 64
MLA_ROPE = 32
MLA_V = 64
N_EXPERTS = 8

TM_PROJ = 512
TM_EVEN_IN = 256
TM_FFN = 512
FFN_CHUNK = 2816
T_MOE = 1024
ROW_MOE = 128
RT_GMM = 256
GMM_STAGE_PIECES = 16
GMM_STAGE_SLOTS = 6
SEG_ALIGN = 16
SEG_SIZES = tuple(T_MOE >> k for k in range(T_MOE.bit_length()) if (T_MOE >> k) >= SEG_ALIGN)
TQ_ATTN = 512


def _params(sem, vmem_mib):
    return pltpu.CompilerParams(dimension_semantics=sem, vmem_limit_bytes=vmem_mib * MIB)


def _dot(a, b):
    return jnp.dot(a, b, preferred_element_type=F32)


def _dot_nt(a, b):
    return lax.dot_general(a, b, (((1,), (1,)), ((), ())), preferred_element_type=F32)


def _split(a):
    hi = a.astype(BF16)
    lo = (a - hi.astype(F32)).astype(BF16)
    return hi, lo


def _silu(x):
    return x * jax.nn.sigmoid(x)


def _layer_norm(r, g, b):
    mu = jnp.mean(r, axis=-1, keepdims=True)
    d = r - mu
    var = jnp.mean(d * d, axis=-1, keepdims=True)
    return d * lax.rsqrt(var + LN_EPS) * g + b


def _rms(x, g, n):
    ms = jnp.sum(x * x, axis=-1, keepdims=True) * (1.0 / n)
    return x * lax.rsqrt(ms + RMS_EPS) * g


def _rope(x, cos, sa, sb, q):
    w = x.shape[-1]
    return x * cos + pltpu.roll(x, w - q, 1) * sa + pltpu.roll(x, q, 1) * sb


def _ada_kernel(c_ref, w_ref, b_ref, o_ref):
    c = c_ref[...]
    a_hi, a_lo = _split(_silu(c))
    w_hi, w_lo = _split(w_ref[0])
    o_ref[0] = _dot(a_hi, w_hi) + _dot(a_lo, w_hi) + _dot(a_hi, w_lo) + b_ref[0]


def _modulation(cond, ada_w, ada_b):
    depth, d, d6 = ada_w.shape
    r = cond.shape[0]
    out = pl.pallas_call(
        _ada_kernel,
        out_shape=jax.ShapeDtypeStruct((depth, d6 // d, r, d), F32),
        grid=(depth, d6 // d),
        in_specs=[pl.BlockSpec((r, d), lambda l, j: (0, 0)),
                  pl.BlockSpec((1, d, d), lambda l, j: (l, 0, j)),
                  pl.BlockSpec((1, 1, d), lambda l, j: (l, 0, j))],
        out_specs=pl.BlockSpec((None, 1, r, d), lambda l, j: (l, j, 0, 0)),
        compiler_params=_params(("arbitrary", "arbitrary"), 40),
        name="ada_modulation",
    )(cond, ada_w, ada_b.reshape(depth, 1, d6))
    return out.reshape(depth, d6 // d, r, 1, d)


def _mod_operand(m, row_fn):
    arr, layer, part = m
    return arr, pl.BlockSpec((None, None, 1, 1, arr.shape[-1]), lambda *a: (layer, part, row_fn(*a), 0, 0))


def _load_x(x_refs, n_ctx_tiles):
    if len(x_refs) == 1:
        return x_refs[0][...]
    return _pick_pass(pl.program_id(0), n_ctx_tiles, *x_refs)


def _x_specs(x, tm, tok):
    if not isinstance(x, tuple):
        return (x,), [pl.BlockSpec((tm, x.shape[1]), tok)], 0
    n_ctx, n_lat = x[0].shape[0] // tm, x[1].shape[0] // tm
    return x, list(_pass_specs((tm, x[0].shape[1]), n_ctx, n_lat)), n_ctx


def _even_in_kernel(rows_ref, rblk_ref, *refs, n_x, n_ctx_tiles):
    (sc_ref, sh_ref, wc_ref, wq_ref, wkv_ref, qg_ref, kg_ref, cos_ref, sa_ref, sb_ref,
     conv_ref, q_ref, k_ref, v_ref, kn_ref, vf_ref) = refs[n_x:]
    del rows_ref, rblk_ref
    h = (_load_x(refs[:n_x], n_ctx_tiles) * (1.0 + sc_ref[0]) + sh_ref[0]).astype(BF16)
    conv_ref[...] = _dot(h, wc_ref[...])
    cos, sa, sb = cos_ref[...], sa_ref[...], sb_ref[...]
    quarter = HEAD_DIM // 4
    q = _dot(h, wq_ref[...])
    for hd in range(ATTN_HEADS):
        sl = slice(hd * LANE, (hd + 1) * LANE)
        qn = _rms(q[:, sl], qg_ref[...], HEAD_DIM)
        q_ref[:, sl] = (_rope(qn, cos, sa, sb, quarter) * (HEAD_DIM ** -0.5)).astype(BF16)
    kv = _dot(h, wkv_ref[...])
    kw = ATTN_KV_HEADS * LANE
    for hd in range(ATTN_KV_HEADS):
        sl = slice(hd * LANE, (hd + 1) * LANE)
        kn = _rms(kv[:, sl], kg_ref[...], HEAD_DIM)
        kn_ref[:, sl] = kn
        k_ref[:, sl] = _rope(kn, cos, sa, sb, quarter).astype(BF16)
    v = kv[:, kw:]
    vf_ref[...] = v
    v_ref[...] = v.astype(BF16)


def _even_in(x, sc, sh, wc, wq, wkv, qg, kg, tables, rows, rblk):
    tm = TM_EVEN_IN
    d = wc.shape[0]
    n = rows.shape[0] * tm
    cw, qw, kvw = wc.shape[1], wq.shape[1], wkv.shape[1]
    kw = kvw // 2
    cos, sa, sb = tables
    tok = lambda i, rows, rblk: (i, 0)
    const = lambda i, rows, rblk: (0, 0)
    tab = lambda i, rows, rblk: (rblk[i], 0)
    xs, x_specs, n_ctx = _x_specs(x, tm, tok)
    row = lambda i, rows, rblk: rows[i]
    (sc, sc_spec), (sh, sh_spec) = _mod_operand(sc, row), _mod_operand(sh, row)
    return pl.pallas_call(
        functools.partial(_even_in_kernel, n_x=len(xs), n_ctx_tiles=n_ctx),
        out_shape=(jax.ShapeDtypeStruct((n, cw), F32), jax.ShapeDtypeStruct((n, qw), BF16),
                   jax.ShapeDtypeStruct((n, kw), BF16), jax.ShapeDtypeStruct((n, kw), BF16),
                   jax.ShapeDtypeStruct((n, kw), F32), jax.ShapeDtypeStruct((n, kw), F32)),
        grid_spec=pltpu.PrefetchScalarGridSpec(
            num_scalar_prefetch=2, grid=(n // tm,),
            in_specs=x_specs + [
                      sc_spec, sh_spec,
                      pl.BlockSpec((d, cw), const), pl.BlockSpec((d, qw), const), pl.BlockSpec((d, kvw), const),
                      pl.BlockSpec((1, LANE), const), pl.BlockSpec((1, LANE), const),
                      pl.BlockSpec((tm, LANE), tab), pl.BlockSpec((tm, LANE), tab), pl.BlockSpec((tm, LANE), tab)],
            out_specs=[pl.BlockSpec((tm, cw), tok), pl.BlockSpec((tm, qw), tok),
                       pl.BlockSpec((tm, kw), tok), pl.BlockSpec((tm, kw), tok),
                       pl.BlockSpec((tm, kw), tok), pl.BlockSpec((tm, kw), tok)]),
        compiler_params=_params(("arbitrary",), 40),
        name="even_in_proj",
    )(rows, rblk, *xs, sc, sh, wc, wq, wkv, qg, kg, cos, sa, sb)


def _attn_kernel(*refs, n_heads, group, n_seg):
    q_ref = refs[0]
    seg = refs[1:1 + 2 * n_seg]
    o_ref = refs[1 + 2 * n_seg]
    for hd in range(n_heads):
        sl = slice(hd * LANE, (hd + 1) * LANE)
        ks = slice((hd // group) * LANE, (hd // group + 1) * LANE)
        qh = q_ref[:, sl]
        scores = [_dot_nt(qh, seg[2 * s][:, ks]) for s in range(n_seg)]
        m = jnp.max(scores[0], axis=-1, keepdims=True)
        for s in range(1, n_seg):
            m = jnp.maximum(m, jnp.max(scores[s], axis=-1, keepdims=True))
        den = None
        acc = None
        for s in range(n_seg):
            p = jnp.exp(scores[s] - m)
            ps = jnp.sum(p, axis=-1, keepdims=True)
            pv = _dot(p.astype(BF16), seg[2 * s + 1][:, ks])
            den = ps if den is None else den + ps
            acc = pv if acc is None else acc + pv
        o_ref[:, sl] = (acc * (1.0 / den)).astype(BF16)


def _attention(q, segments, n_heads, group, b, s, row0):
    n, qw = q.shape
    assert n % s == 0 and row0 % s == 0
    b0 = row0 // s
    tq = min(TQ_ATTN, s)
    in_specs = [pl.BlockSpec((None, tq, qw), lambda i, j: (i + b0, j, 0))]
    args = [q.reshape(n // s, s, qw)]
    for k, v, own in segments:
        kw = k.shape[-1]
        if own:
            k, v = k.reshape(n // s, s, kw), v.reshape(n // s, s, kw)
            kmap = lambda i, j: (i + b0, 0, 0)
        else:
            kmap = lambda i, j: (i, 0, 0)
        t = k.shape[1]
        in_specs += [pl.BlockSpec((None, t, kw), kmap, pipeline_mode=pl.Buffered(1)),
                     pl.BlockSpec((None, t, kw), kmap, pipeline_mode=pl.Buffered(1))]
        args += [k, v]
    return pl.pallas_call(
        functools.partial(_attn_kernel, n_heads=n_heads, group=group, n_seg=len(segments)),
        out_shape=jax.ShapeDtypeStruct((b, s, qw), BF16),
        grid=(b, s // tq),
        in_specs=in_specs,
        out_specs=pl.BlockSpec((None, tq, qw), lambda i, j: (i, j, 0)),
        compiler_params=_params(("arbitrary", "arbitrary"), 56),
        name="attention",
    )(*args)


def _pick_pass(i, n_ctx_tiles, ctx_ref, lat_ref):
    return jnp.where(i < n_ctx_tiles, ctx_ref[...], lat_ref[...])


def _pass_specs(block, n_ctx_tiles, n_lat_tiles):
    lead = (0,) * (len(block) - 2)

    def ctx(i, *_):
        return lead + (jnp.minimum(i, n_ctx_tiles - 1), 0)

    def lat(i, *_):
        return lead + (jnp.clip(i - n_ctx_tiles, 0, n_lat_tiles - 1), 0)

    return pl.BlockSpec(block, ctx), pl.BlockSpec(block, lat)


def _even_out_kernel(first_ref, last_ref, rows_ref, c_ref, cp_ref, cn_ref, ap_ref, as_ref, *refs,
                     alpha, n_ctx_tiles, ctx_seq, n_x):
    g_ref, cw_ref, w_ref, lg_ref, lb_ref, o_ref = refs[n_x:]
    del rows_ref
    i = pl.program_id(0)
    attn = _pick_pass(i, n_ctx_tiles, ap_ref, as_ref)
    tm = c_ref.shape[0]
    cwid = CONV_WIDTH
    c = c_ref[...]
    gate_b, z = c[:, :cwid], c[:, cwid:2 * cwid] * c[:, 2 * cwid:]
    cp = cp_ref[...]
    cn = cn_ref[...]
    zp = cp[SUBLANE - 1:SUBLANE, cwid:2 * cwid] * cp[SUBLANE - 1:SUBLANE, 2 * cwid:]
    zn = cn[0:1, cwid:2 * cwid] * cn[0:1, 2 * cwid:]
    zp = zp * (1 - first_ref[i]).astype(F32)
    zn = zn * (1 - last_ref[i]).astype(F32)
    row = lax.broadcasted_iota(jnp.int32, (tm, cwid), 0)
    z_prev = jnp.where(row == 0, zp, pltpu.roll(z, 1, 0))
    z_next = jnp.where(row == tm - 1, zn, pltpu.roll(z, tm - 1, 0))
    for k in range(1, tm // ctx_seq):
        edge = jnp.where(i < n_ctx_tiles, k * ctx_seq, -1)
        z_prev = jnp.where(row == edge, 0.0, z_prev)
        z_next = jnp.where(row == edge - 1, 0.0, z_next)
    cw = cw_ref[...]
    y = gate_b * (cw[0:1] * z_prev + cw[1:2] * z + cw[2:3] * z_next)
    out = _dot(y.astype(BF16), w_ref[:cwid, :]) + _dot(attn, w_ref[cwid:, :])
    x = _load_x(refs[:n_x], n_ctx_tiles)
    o_ref[...] = _layer_norm(alpha * x + g_ref[0] * out, lg_ref[...], lb_ref[...])


def _even_out(conv, attn_p, attn_s, x, g1, conv_w, w_out, ln_g, ln_b, first, last, rows, alpha, ctx_seq):
    tm = TM_PROJ
    n, d = conv.shape[0], w_out.shape[1]
    cw3 = conv.shape[1]
    aw = attn_p.shape[1]
    hb = tm // SUBLANE
    nblk8 = n // SUBLANE
    n_ctx, n_lat = attn_p.shape[0] // tm, attn_s.shape[0] // tm
    ap_spec, as_spec = _pass_specs((tm, aw), n_ctx, n_lat)
    tok = lambda i, f, l, r: (i, 0)
    const = lambda i, f, l, r: (0, 0)
    xs, x_specs, _ = _x_specs(x, tm, tok)
    g1, g1_spec = _mod_operand(g1, lambda i, f, l, r: r[i])
    return pl.pallas_call(
        functools.partial(_even_out_kernel, alpha=alpha, n_ctx_tiles=n_ctx, ctx_seq=ctx_seq, n_x=len(xs)),
        out_shape=jax.ShapeDtypeStruct((n, d), F32),
        grid_spec=pltpu.PrefetchScalarGridSpec(
            num_scalar_prefetch=3, grid=(n // tm,),
            in_specs=[pl.BlockSpec((tm, cw3), tok),
                      pl.BlockSpec((SUBLANE, cw3), lambda i, f, l, r: (jnp.maximum(i * hb - 1, 0), 0)),
                      pl.BlockSpec((SUBLANE, cw3), lambda i, f, l, r: (jnp.minimum((i + 1) * hb, nblk8 - 1), 0)),
                      ap_spec, as_spec] + x_specs + [
                      g1_spec,
                      pl.BlockSpec((SUBLANE, CONV_WIDTH), const),
                      pl.BlockSpec((CONV_WIDTH + aw, d), const),
                      pl.BlockSpec((1, d), const), pl.BlockSpec((1, d), const)],
            out_specs=pl.BlockSpec((tm, d), tok)),
        compiler_params=_params(("arbitrary",), 40),
        name="even_out_proj",
    )(first, last, rows, conv, conv, conv, attn_p, attn_s, *xs, g1, conv_w, w_out, ln_g, ln_b)


def _ffn_kernel(rows_ref, x_ref, sc_ref, sh_ref, g_ref, wg_ref, wu_ref, wd_ref, lg_ref, lb_ref, o_ref,
                h_s, acc_s, *, alpha):
    del rows_ref
    f = pl.program_id(1)

    @pl.when(f == 0)
    def _():
        h_s[...] = (x_ref[...] * (1.0 + sc_ref[0]) + sh_ref[0]).astype(BF16)

    h = h_s[...]
    a = (_silu(_dot(h, wg_ref[...])) * _dot(h, wu_ref[...])).astype(BF16)
    y = _dot(a, wd_ref[...])

    @pl.when(f == 0)
    def _():
        acc_s[...] = y

    @pl.when(f > 0)
    def _():
        acc_s[...] += y

    @pl.when(f == pl.num_programs(1) - 1)
    def _():
        o_ref[...] = _layer_norm(alpha * x_ref[...] + g_ref[0] * acc_s[...], lg_ref[...], lb_ref[...])


def _ffn(x, sc, sh, g2, wg, wu, wd, layer, ln_g, ln_b, rows, alpha):
    n, d = x.shape
    tm = TM_FFN
    ff = wg.shape[-1]
    fc = FFN_CHUNK
    tok = lambda i, f, r: (i, 0)
    const = lambda i, f, r: (0, 0)
    wmode = dict(pipeline_mode=pl.Buffered(1)) if fc == ff else {}
    row = lambda i, f, r: r[i]
    (sc, sc_spec), (sh, sh_spec), (g2, g2_spec) = (_mod_operand(m, row) for m in (sc, sh, g2))
    return pl.pallas_call(
        functools.partial(_ffn_kernel, alpha=alpha),
        out_shape=jax.ShapeDtypeStruct((n, d), F32),
        grid_spec=pltpu.PrefetchScalarGridSpec(
            num_scalar_prefetch=1, grid=(n // tm, ff // fc),
            in_specs=[pl.BlockSpec((tm, d), tok),
                      sc_spec, sh_spec, g2_spec,
                      pl.BlockSpec((None, d, fc), lambda i, f, r: (layer, 0, f), **wmode),
                      pl.BlockSpec((None, d, fc), lambda i, f, r: (layer, 0, f), **wmode),
                      pl.BlockSpec((None, fc, d), lambda i, f, r: (layer, f, 0), **wmode),
                      pl.BlockSpec((1, d), const), pl.BlockSpec((1, d), const)],
            out_specs=pl.BlockSpec((tm, d), tok),
            scratch_shapes=[pltpu.VMEM((tm, d), BF16), pltpu.VMEM((tm, d), F32)]),
        compiler_params=_params(("arbitrary", "arbitrary"), 56),
        name="ffn_swiglu",
    )(rows, x, sc, sh, g2, wg, wu, wd, ln_g, ln_b)


def _odd_in_kernel(rows_ref, x_ref, sc_ref, sh_ref, w_ref, o_ref, u_ref, u_s):
    del rows_ref
    h = (x_ref[...] * (1.0 + sc_ref[0]) + sh_ref[0]).astype(BF16)
    proj = _dot(h, w_ref[...])
    o_ref[...] = proj
    n_tile = S5_WIDTH // LANE
    gpt = LANE // S5_GROUP
    for j in range(n_tile):
        u_s[j] = proj[:, j * LANE:(j + 1) * LANE]
    n_chunk = u_s.shape[1] // S5_CHUNK
    steps = [[u_s[j, pl.ds(s, n_chunk, stride=S5_CHUNK), :] for j in range(n_tile)] for s in range(S5_CHUNK)]
    for g in range(S5_GROUPS):
        sl = slice((g % gpt) * S5_GROUP, (g % gpt + 1) * S5_GROUP)
        u_ref[g] = jnp.concatenate([st[g // gpt][:, sl] for st in steps], axis=1).astype(BF16)


def _odd_in(x, sc, sh, w, rows, n_ctx_tiles, ctx_chunk0, lat_chunk0):
    n, d = x.shape
    tm = TM_PROJ
    nw = w.shape[1]
    cpt = tm // S5_CHUNK
    u_rows = n // S5_CHUNK
    tok = lambda i, r: (i, 0)
    (sc, sc_spec), (sh, sh_spec) = (_mod_operand(m, lambda i, r: r[i]) for m in (sc, sh))
    ublk = lambda i, r: (0, jnp.where(i < n_ctx_tiles, i + ctx_chunk0 // cpt, i - n_ctx_tiles + lat_chunk0 // cpt), 0)
    return pl.pallas_call(
        _odd_in_kernel,
        out_shape=(jax.ShapeDtypeStruct((n, nw), F32),
                   jax.ShapeDtypeStruct((S5_GROUPS, u_rows, S5_CHUNK * S5_GROUP), BF16)),
        grid_spec=pltpu.PrefetchScalarGridSpec(
            num_scalar_prefetch=1, grid=(n // tm,),
            in_specs=[pl.BlockSpec((tm, d), tok), sc_spec, sh_spec,
                      pl.BlockSpec((d, nw), lambda i, r: (0, 0))],
            out_specs=[pl.BlockSpec((tm, nw), tok), pl.BlockSpec((S5_GROUPS, cpt, S5_CHUNK * S5_GROUP), ublk)],
            scratch_shapes=[pltpu.VMEM((S5_WIDTH // LANE, tm, LANE), F32)]),
        compiler_params=_params(("arbitrary",), 32),
        name="odd_in_proj",
    )(rows, x, sc, sh, w)


def _s5_kernel(u_ref, zin_ref, t_ref, zout_ref, al_ref, h0_ref, y_ref, fin_ref, z_s, hf_s, hb_s, *, nb, nc):
    u = u_ref[0]
    z = _dot(u, zin_ref[0])
    for part in range(2):
        for b in range(nb):
            z_s[part, pl.ds(b, nc, stride=nb), :] = z[b * nc:(b + 1) * nc, part * LANE:(part + 1) * LANE]
    ar = al_ref[0, 0:1, :]
    ai = al_ref[0, 1:2, :]
    fwd = lax.broadcasted_iota(jnp.int32, (nb, LANE), 1) < S5_STATE
    aligned = (lambda r: pl.multiple_of(r, SUBLANE)) if nb % SUBLANE == 0 else (lambda r: r)

    def step(k, carry):
        re, im = carry
        rf = aligned(k * nb)
        rb = aligned((nc - 1 - k) * nb)
        hf_s[0, pl.ds(rf, nb), :] = re
        hf_s[1, pl.ds(rf, nb), :] = im
        hb_s[0, pl.ds(rb, nb), :] = re
        hb_s[1, pl.ds(rb, nb), :] = im
        zr = jnp.where(fwd, z_s[0, pl.ds(rf, nb), :], z_s[0, pl.ds(rb, nb), :])
        zi = jnp.where(fwd, z_s[1, pl.ds(rf, nb), :], z_s[1, pl.ds(rb, nb), :])
        return ar * re - ai * im + zr, ar * im + ai * re + zi

    h0 = h0_ref[0]
    re, im = lax.fori_loop(0, nc, step, (h0[:, :LANE], h0[:, LANE:]))
    fin_ref[0, :, :LANE] = re
    fin_ref[0, :, LANE:] = im
    m = u.shape[0]
    is_fwd = lax.broadcasted_iota(jnp.int32, (m, LANE), 1) < S5_STATE
    halves = []
    for part in range(2):
        hf_s[part] = jnp.where(is_fwd, hf_s[part], hb_s[part])
        halves.append(jnp.concatenate([hf_s[part, pl.ds(b, nc, stride=nb), :] for b in range(nb)], axis=0))
    h_in = jnp.concatenate(halves, axis=1).astype(BF16)
    y_ref[0] = _dot(u, t_ref[0]) + _dot(h_in, zout_ref[0])


def _s5(u, mats, layer, h0, nb, nc, row_block):
    zin, tmat, zout, al = mats
    g, _, w = u.shape
    m = nb * nc
    blk = lambda i: (i, 0, 0)
    lblk = lambda i: (layer, i, 0, 0)
    return pl.pallas_call(
        functools.partial(_s5_kernel, nb=nb, nc=nc),
        out_shape=(jax.ShapeDtypeStruct((g, m, w), F32), jax.ShapeDtypeStruct((g, nb, w), F32)),
        grid=(g,),
        in_specs=[pl.BlockSpec((1, m, w), lambda i: (i, row_block, 0)),
                  pl.BlockSpec((None, 1, w, w), lblk), pl.BlockSpec((None, 1, w, w), lblk),
                  pl.BlockSpec((None, 1, w, w), lblk), pl.BlockSpec((None, 1, 2, LANE), lblk),
                  pl.BlockSpec((1, nb, w), blk)],
        out_specs=(pl.BlockSpec((1, m, w), blk), pl.BlockSpec((1, nb, w), blk)),
        scratch_shapes=[pltpu.VMEM((2, m, LANE), F32), pltpu.VMEM((2, m, LANE), F32), pltpu.VMEM((2, m, LANE), F32)],
        compiler_params=_params(("arbitrary",), 32),
        name="s5_scan",
    )(u, zin, tmat, zout, al, h0)


def _s5_matrices(a_re, a_im, log_dt, b_re, b_im, c_re, c_im):
    hp = lax.Precision.HIGHEST
    L = S5_CHUNK
    dt = jnp.exp(log_dt)[..., None]
    lam_re, lam_im = a_re * dt, a_im * dt

    def power(k):
        k = k[:, None, None, None]
        mag = jnp.exp(lam_re * k)
        return mag * jnp.cos(lam_im * k), mag * jnp.sin(lam_im * k)

    ab_re, ab_im = power(jnp.ones((1,), F32))
    ab_re, ab_im = ab_re[0], ab_im[0]
    num_re, num_im = ab_re - 1.0, ab_im
    den = a_re * a_re + a_im * a_im
    f_re = (num_re * a_re + num_im * a_im) / den
    f_im = (num_im * a_re - num_re * a_im) / den
    bb_re = f_re[..., None] * b_re - f_im[..., None] * b_im
    bb_im = f_re[..., None] * b_im + f_im[..., None] * b_re

    ks = jnp.arange(L + 1, dtype=F32)
    pw_re, pw_im = power(ks)

    def zin_dir(d, exps):
        pr, pi = pw_re[exps, d], pw_im[exps, d]
        w_re = pr[..., None] * bb_re[d][None] - pi[..., None] * bb_im[d][None]
        w_im = pr[..., None] * bb_im[d][None] + pi[..., None] * bb_re[d][None]
        to = lambda w: jnp.transpose(w, (1, 0, 3, 2)).reshape(S5_GROUPS, L * S5_GROUP, S5_STATE)
        return to(w_re), to(w_im)

    steps = np.arange(L)
    zf_re, zf_im = zin_dir(0, L - 1 - steps)
    zb_re, zb_im = zin_dir(1, steps)
    zin = jnp.concatenate([zf_re, zb_re, zf_im, zb_im], axis=-1)

    def zout_dir(d, exps):
        pr, pi = pw_re[exps, d], pw_im[exps, d]
        cr, ci = c_re[d], c_im[d]
        e_re = cr[None] * pr[:, :, None, :] - ci[None] * pi[:, :, None, :]
        e_im = cr[None] * pi[:, :, None, :] + ci[None] * pr[:, :, None, :]
        to = lambda e: jnp.transpose(e, (1, 3, 0, 2)).reshape(S5_GROUPS, S5_STATE, L * S5_GROUP)
        return to(e_re), to(-e_im)

    of_re, of_im = zout_dir(0, steps + 1)
    ob_re, ob_im = zout_dir(1, L - steps)
    zout = jnp.concatenate([of_re, ob_re, of_im, ob_im], axis=1)

    def taps(d):
        pr, pi = pw_re[:L, d], pw_im[:L, d]
        m_re = pr[..., None] * bb_re[d][None] - pi[..., None] * bb_im[d][None]
        m_im = pr[..., None] * bb_im[d][None] + pi[..., None] * bb_re[d][None]
        return (jnp.einsum('gpn,lgnq->lgpq', c_re[d], m_re, precision=hp)
                - jnp.einsum('gpn,lgnq->lgpq', c_im[d], m_im, precision=hp))

    kf, kb = taps(0), taps(1)
    pad4 = ((0, 0),) * 3
    tm = jnp.stack([jnp.pad(kf[:L - s], ((s, 0),) + pad4) + jnp.pad(kb[:s + 1][::-1], ((0, L - 1 - s),) + pad4)
                    for s in range(L)])
    tmat = jnp.transpose(tm, (2, 0, 4, 1, 3)).reshape(S5_GROUPS, L * S5_GROUP, L * S5_GROUP)

    al = jnp.stack([jnp.concatenate([pw_re[L, 0], pw_re[L, 1]], axis=-1),
                    jnp.concatenate([pw_im[L, 0], pw_im[L, 1]], axis=-1)], axis=1)
    return zin.astype(BF16), tmat.astype(BF16), zout.astype(BF16), al


def _mla_kv(ckv, kpe, wk_ref, wv_ref, pl_ref, k_ref, v_ref):
    cb = ckv.astype(BF16)
    k_ref[...] = (_dot(cb, wk_ref[...]) + _dot(kpe.astype(BF16), pl_ref[...])).astype(BF16)
    v_ref[...] = _dot(cb, wv_ref[...]).astype(BF16)


def _mla_prep_kernel(rblk_ref, p_ref, qg_ref, kvg_ref, wuq_ref, wk_ref, wv_ref, pl_ref, cos_ref, sa_ref, sb_ref,
                     q_ref, k_ref, v_ref, ckv_ref):
    del rblk_ref
    pr = p_ref[...]
    cos, sa, sb = cos_ref[...], sa_ref[...], sb_ref[...]
    quarter = MLA_ROPE // 4
    scale = (MLA_NOPE + MLA_ROPE) ** -0.5
    qn = _rms(pr[:, :MLA_Q_RANK], qg_ref[...], MLA_Q_RANK).astype(BF16)
    q = _dot(qn, wuq_ref[...])
    for hd in range(MLA_HEADS):
        sl = slice(hd * LANE, (hd + 1) * LANE)
        q_ref[:, sl] = (_rope(q[:, sl], cos, sa, sb, quarter) * scale).astype(BF16)
    ckv = _rms(pr[:, MLA_Q_RANK:MLA_Q_RANK + MLA_KV_RANK], kvg_ref[...], MLA_KV_RANK)
    ckv_ref[...] = ckv
    kpe = _rope(pr[:, MLA_Q_RANK + MLA_KV_RANK:], cos, sa, sb, quarter)
    _mla_kv(ckv, kpe, wk_ref, wv_ref, pl_ref, k_ref, v_ref)


def _mla_prep(proj, qg, kvg, wuq, wk, wv, place, tables, rblk):
    n = proj.shape[0]
    tm = TM_PROJ
    hw = MLA_HEADS * LANE
    pw = MLA_Q_RANK + MLA_KV_RANK + LANE
    cos, sa, sb = tables
    tok = lambda i, r: (i, 0)
    const = lambda i, r: (0, 0)
    tab = lambda i, r: (r[i], 0)
    return pl.pallas_call(
        _mla_prep_kernel,
        out_shape=(jax.ShapeDtypeStruct((n, hw), BF16), jax.ShapeDtypeStruct((n, hw), BF16),
                   jax.ShapeDtypeStruct((n, hw), BF16), jax.ShapeDtypeStruct((n, MLA_KV_RANK), F32)),
        grid_spec=pltpu.PrefetchScalarGridSpec(
            num_scalar_prefetch=1, grid=(n // tm,),
            in_specs=[pl.BlockSpec((tm, pw), lambda i, r: (i, 1)),
                      pl.BlockSpec((1, MLA_Q_RANK), const), pl.BlockSpec((1, MLA_KV_RANK), const),
                      pl.BlockSpec((MLA_Q_RANK, hw), const), pl.BlockSpec((MLA_KV_RANK, hw), const),
                      pl.BlockSpec((MLA_KV_RANK, hw), const), pl.BlockSpec((LANE, hw), const),
                      pl.BlockSpec((tm, LANE), tab), pl.BlockSpec((tm, LANE), tab), pl.BlockSpec((tm, LANE), tab)],
            out_specs=[pl.BlockSpec((tm, hw), tok), pl.BlockSpec((tm, hw), tok), pl.BlockSpec((tm, hw), tok),
                       pl.BlockSpec((tm, MLA_KV_RANK), tok)]),
        compiler_params=_params(("arbitrary",), 32),
        name="mla_prep",
    )(rblk, proj, qg, kvg, wuq, wk, wv, place, cos, sa, sb)


def _mla_cache_kernel(c_ref, p_ref, wk_ref, wv_ref, pl_ref, k_ref, v_ref):
    _mla_kv(c_ref[...], p_ref[...], wk_ref, wv_ref, pl_ref, k_ref, v_ref)


def _mla_cache(ckv, kpe, wk, wv, place):
    n = ckv.shape[0]
    tm = min(TM_PROJ, n)
    assert n % tm == 0
    hw = MLA_HEADS * LANE
    tok = lambda i: (i, 0)
    const = lambda i: (0, 0)
    return pl.pallas_call(
        _mla_cache_kernel,
        out_shape=(jax.ShapeDtypeStruct((n, hw), BF16), jax.ShapeDtypeStruct((n, hw), BF16)),
        grid=(n // tm,),
        in_specs=[pl.BlockSpec((tm, MLA_KV_RANK), tok), pl.BlockSpec((tm, LANE), tok),
                  pl.BlockSpec((MLA_KV_RANK, hw), const), pl.BlockSpec((MLA_KV_RANK, hw), const),
                  pl.BlockSpec((LANE, hw), const)],
        out_specs=[pl.BlockSpec((tm, hw), tok), pl.BlockSpec((tm, hw), tok)],
        compiler_params=_params(("arbitrary",), 32),
        name="mla_cache_kv",
    )(ckv, kpe, wk, wv, place)


def _gelu_tanh(x):
    return 0.5 * x * (1.0 + jnp.tanh(math.sqrt(2.0 / math.pi) * (x + 0.044715 * (x * x * x))))


def _odd_out_kernel(rows_ref, u_ref, yp_ref, ys_ref, ap_ref, as_ref, x_ref, g_ref, d_ref, wglu_ref, bglu_ref, w_ref,
                    lg_ref, lb_ref, o_ref, y_s, *, alpha, n_ctx_tiles):
    del rows_ref
    i = pl.program_id(0)
    yg = _pick_pass(i, n_ctx_tiles, yp_ref, ys_ref)
    n_chunk = yg.shape[1]
    n_tile = S5_WIDTH // LANE
    gpt = LANE // S5_GROUP
    for s in range(S5_CHUNK):
        sl = slice(s * S5_GROUP, (s + 1) * S5_GROUP)
        for j in range(n_tile):
            y_s[j, pl.ds(s, n_chunk, stride=S5_CHUNK), :] = jnp.concatenate(
                [yg[g][:, sl] for g in range(j * gpt, (j + 1) * gpt)], axis=1)
    y_ssm = jnp.concatenate([y_s[j] for j in range(n_tile)], axis=1)
    y = _gelu_tanh(u_ref[...] * d_ref[...] + y_ssm)
    y = y * jax.nn.sigmoid(_dot(y.astype(BF16), wglu_ref[...]) + bglu_ref[...])
    attn = _pick_pass(i, n_ctx_tiles, ap_ref, as_ref)
    out = _dot(y.astype(BF16), w_ref[:S5_WIDTH, :]) + _dot(attn, w_ref[S5_WIDTH:, :])
    o_ref[...] = _layer_norm(alpha * x_ref[...] + g_ref[0] * out, lg_ref[...], lb_ref[...])


def _odd_out(proj, y_p, y_s, attn_p, attn_s, x, g1, s5_d, w_glu, b_glu, w_out, ln_g, ln_b, rows, alpha):
    n, d = x.shape
    tm = TM_PROJ
    aw = attn_p.shape[1]
    cpt = tm // S5_CHUNK
    n_ctx, n_lat = attn_p.shape[0] // tm, attn_s.shape[0] // tm
    ap_spec, as_spec = _pass_specs((tm, aw), n_ctx, n_lat)
    yp_spec, ys_spec = _pass_specs((S5_GROUPS, cpt, S5_CHUNK * S5_GROUP), n_ctx, n_lat)
    tok = lambda i, r: (i, 0)
    const = lambda i, r: (0, 0)
    g1, g1_spec = _mod_operand(g1, lambda i, r: r[i])
    return pl.pallas_call(
        functools.partial(_odd_out_kernel, alpha=alpha, n_ctx_tiles=n_ctx),
        out_shape=jax.ShapeDtypeStruct((n, d), F32),
        grid_spec=pltpu.PrefetchScalarGridSpec(
            num_scalar_prefetch=1, grid=(n // tm,),
            in_specs=[pl.BlockSpec((tm, S5_WIDTH), tok), yp_spec, ys_spec,
                      ap_spec, as_spec, pl.BlockSpec((tm, d), tok),
                      g1_spec,
                      pl.BlockSpec((1, S5_WIDTH), const), pl.BlockSpec((S5_WIDTH, S5_WIDTH), const),
                      pl.BlockSpec((1, S5_WIDTH), const), pl.BlockSpec((S5_WIDTH + aw, d), const),
                      pl.BlockSpec((1, d), const), pl.BlockSpec((1, d), const)],
            out_specs=pl.BlockSpec((tm, d), tok),
            scratch_shapes=[pltpu.VMEM((S5_WIDTH // LANE, tm, LANE), F32)]),
        compiler_params=_params(("arbitrary",), 40),
        name="odd_out_proj",
    )(rows, proj, y_p, y_s, attn_p, attn_s, x, g1, s5_d, w_glu, b_glu, w_out, ln_g, ln_b)


def _router_kernel(rows_ref, x_ref, sc_ref, sh_ref, rt_ref, hb_ref, pos_ref, gate_ref, pt_ref, cnt_ref, tri_s):
    del rows_ref
    t = x_ref.shape[0]
    ne = N_EXPERTS

    @pl.when(pl.program_id(0) == 0)
    def _():
        before = lax.broadcasted_iota(jnp.int32, (t, t), 0) < lax.broadcasted_iota(jnp.int32, (t, t), 1)
        tri_s[...] = jnp.where(before, 1.0, 0.0).astype(BF16)

    h = x_ref[...] * (1.0 + sc_ref[0]) + sh_ref[0]
    h_hi, h_lo = _split(h)
    hb_ref[...] = h_hi
    r_hi, r_lo = _split(rt_ref[...])
    logits = _dot_nt(r_hi, h_hi) + _dot_nt(r_lo, h_hi) + _dot_nt(r_hi, h_lo)
    eid = lax.broadcasted_iota(jnp.int32, (ne, t), 0).astype(F32)
    m0 = jnp.max(logits, axis=0, keepdims=True)
    i0 = jnp.min(jnp.where(logits == m0, eid, float(ne)), axis=0, keepdims=True)
    rest = jnp.where(eid == i0, -jnp.inf, logits)
    m1 = jnp.max(rest, axis=0, keepdims=True)
    i1 = jnp.min(jnp.where(rest == m1, eid, float(ne)), axis=0, keepdims=True)
    ex = jnp.exp(m1 - m0)
    g0 = 1.0 / (1.0 + ex)
    g1 = ex / (1.0 + ex)
    sel0 = eid == i0
    sel1 = eid == i1
    member = jnp.where(sel0, 1.0, jnp.where(sel1, 1.0, 0.0))
    gate = jnp.where(sel0, g0, jnp.where(sel1, g1, 0.0))
    rank = _dot(member.astype(BF16), tri_s[...])
    pos = jnp.where(member > 0.0, rank, -1.0)
    pos_ref[0] = pos.astype(jnp.int32)
    gate_ref[0] = gate
    cnt = jnp.sum(member, axis=1, keepdims=True)
    cnt_ref[0] = jnp.broadcast_to(cnt, (ne, LANE)).astype(jnp.int32)
    packed = jnp.concatenate([pos, gate, jnp.zeros((LANE - 2 * ne, t), F32)], axis=0)
    pt_ref[...] = packed.T


def _router(x, sc, sh, router_t, rows):
    n, d = x.shape
    t = T_MOE
    nb = n // t
    ne = N_EXPERTS
    tok = lambda i, r: (i, 0)
    (sc, sc_spec), (sh, sh_spec) = (_mod_operand(m, lambda i, r: r[i]) for m in (sc, sh))
    blk3 = lambda i, r: (i, 0, 0)
    return pl.pallas_call(
        _router_kernel,
        out_shape=(jax.ShapeDtypeStruct((n, d), BF16), jax.ShapeDtypeStruct((nb, ne, t), jnp.int32),
                   jax.ShapeDtypeStruct((nb, ne, t), F32),
                   jax.ShapeDtypeStruct((n, LANE), F32), jax.ShapeDtypeStruct((nb, ne, LANE), jnp.int32)),
        grid_spec=pltpu.PrefetchScalarGridSpec(
            num_scalar_prefetch=1, grid=(nb,),
            in_specs=[pl.BlockSpec((t, d), tok), sc_spec, sh_spec,
                      pl.BlockSpec((ne, d), lambda i, r: (0, 0))],
            out_specs=[pl.BlockSpec((t, d), tok), pl.BlockSpec((1, ne, t), blk3), pl.BlockSpec((1, ne, t), blk3),
                       pl.BlockSpec((t, LANE), tok), pl.BlockSpec((1, ne, LANE), blk3)],
            scratch_shapes=[pltpu.VMEM((t, t), BF16)]),
        compiler_params=_params(("arbitrary",), 48),
        name="moe_router",
    )(rows, x, sc, sh, router_t)


def _segment_copies(rows, src, dst, src0, dst0, sems, slot):
    out = []
    for k, sz in enumerate(SEG_SIZES):
        off = (rows // (2 * sz)) * (2 * sz)
        s0 = pl.multiple_of(src0 + off, SEG_ALIGN)
        d0 = pl.multiple_of(dst0 + off, SEG_ALIGN)
        sem = sems.at[slot * len(SEG_SIZES) + k]
        out.append(((rows & sz) != 0, pltpu.make_async_copy(src.at[pl.ds(s0, sz)], dst.at[pl.ds(d0, sz)], sem)))
    return out


def _run_copies(copies):
    for pred, cp in copies:
        pl.when(pred)(cp.start)
    for pred, cp in copies:
        pl.when(pred)(cp.wait)


def _moe_gather_kernel(cnt_ref, base_ref, h_ref, pos_ref, gate_ref, xs_in, gs_in, xs_ref, gs_ref,
                       xe_s, ge_s, sem_x, sem_g):
    del xs_in, gs_in
    b = pl.program_id(0)
    t = h_ref.shape[0]
    rt = ROW_MOE
    ne = N_EXPERTS
    buf = b % 2

    def block_copies(blk, buf_):
        out, off = [], 0
        for e in range(ne):
            rows = ((cnt_ref[blk * ne + e] + SEG_ALIGN - 1) // SEG_ALIGN) * SEG_ALIGN
            base = base_ref[blk * ne + e]
            out.append((off, _segment_copies(rows, xe_s.at[buf_], xs_ref, off, base, sem_x, buf_ * ne + e)
                        + _segment_copies(rows, ge_s.at[buf_], gs_ref, off, base, sem_g, buf_ * ne + e)))
            off = off + rows
        return out

    def wait_all(segs):
        for _, seg in segs:
            for pred, cp in seg:
                pl.when(pred)(cp.wait)

    mine = block_copies(b, buf)
    xe, ge = xe_s.at[buf], ge_s.at[buf]
    for e in range(ne):
        cnt = cnt_ref[b * ne + e]
        pos_row = pos_ref[0, e:e + 1, :]
        gate_row = gate_ref[0, e:e + 1, :]
        off = mine[e][0]

        def gather(r, c, off=off, pos_row=pos_row, gate_row=gate_row):
            r0 = r * rt
            hit = (lax.broadcasted_iota(jnp.int32, (rt, t), 0) + r0) == pos_row
            onehot = jnp.where(hit, 1.0, 0.0).astype(BF16)
            dst = pl.multiple_of(off + r0, SEG_ALIGN)
            xe[pl.ds(dst, rt), :] = _dot(onehot, h_ref[...]).astype(BF16)
            g = jnp.sum(jnp.where(hit, gate_row, 0.0), axis=1, keepdims=True)
            ge[pl.ds(dst, rt), :] = jnp.broadcast_to(g, (rt, LANE))
            return c

        lax.fori_loop(0, (cnt + rt - 1) // rt, gather, 0)
        for pred, cp in mine[e][1]:
            pl.when(pred)(cp.start)

    @pl.when(b > 0)
    def _():
        wait_all(block_copies(b - 1, 1 - buf))

    @pl.when(b == pl.num_programs(0) - 1)
    def _():
        wait_all(mine)


def _moe_gather(h, pos, gate, counts, base, n_rows):
    n, d = h.shape
    t = T_MOE
    nb = n // t
    ne = N_EXPERTS
    tok = lambda b, c, o: (b, 0)
    blk3 = lambda b, c, o: (b, 0, 0)
    any_spec = pl.BlockSpec(memory_space=pl.ANY)
    xs0 = jnp.zeros((n_rows, d), BF16)
    gs0 = jnp.zeros((n_rows, LANE), F32)
    cap = -(-(2 * t + ne * (SEG_ALIGN - 1) + ROW_MOE - 1) // ROW_MOE) * ROW_MOE
    return pl.pallas_call(
        _moe_gather_kernel,
        out_shape=(jax.ShapeDtypeStruct((n_rows, d), BF16), jax.ShapeDtypeStruct((n_rows, LANE), F32)),
        grid_spec=pltpu.PrefetchScalarGridSpec(
            num_scalar_prefetch=2, grid=(nb,),
            in_specs=[pl.BlockSpec((t, d), tok), pl.BlockSpec((1, ne, t), blk3), pl.BlockSpec((1, ne, t), blk3),
                      any_spec, any_spec],
            out_specs=[any_spec, any_spec],
            scratch_shapes=[pltpu.VMEM((2, cap, d), BF16), pltpu.VMEM((2, cap, LANE), F32),
                            pltpu.SemaphoreType.DMA((2 * ne * len(SEG_SIZES),)),
                            pltpu.SemaphoreType.DMA((2 * ne * len(SEG_SIZES),))]),
        input_output_aliases={5: 0, 6: 1},
        compiler_params=_params(("arbitrary",), 32),
        name="moe_gather",
    )(counts, base, h, pos, gate, xs0, gs0)


def _stream_cast(src, dst):
    r, c = dst.shape
    rows = r // GMM_STAGE_PIECES
    slots = GMM_STAGE_SLOTS

    def run(stage, sems):
        def piece(k):
            return pltpu.make_async_copy(src.at[pl.ds(k * rows, rows)], stage.at[k % slots], sems.at[k % slots])

        for k in range(slots - 1):
            piece(k).start()
        for k in range(GMM_STAGE_PIECES):
            if k + slots - 1 < GMM_STAGE_PIECES:
                piece(k + slots - 1).start()
            piece(k).wait()
            dst[k * rows:(k + 1) * rows, :] = stage[k % slots].astype(BF16)

    pl.run_scoped(run, pltpu.VMEM((slots, rows, c), F32), pltpu.SemaphoreType.DMA((slots,)))


def _moe_gmm_kernel(te_ref, valid_ref, x_ref, gs_ref, wg_hbm, wu_hbm, wd_hbm, y_ref, wg_s, wu_s, wd_s, *, layer):
    i = pl.program_id(0)
    valid = valid_ref[i] != 0
    e = te_ref[i]
    switch = valid & ((i == 0) | (e != te_ref[jnp.maximum(i - 1, 0)]))

    @pl.when(switch)
    def _():
        _stream_cast(wg_hbm.at[layer, e], wg_s)
        _stream_cast(wu_hbm.at[layer, e], wu_s)
        _stream_cast(wd_hbm.at[layer, e], wd_s)

    @pl.when(valid)
    def _():
        x = x_ref[...]
        a = (_silu(_dot(x, wg_s[...])) * _dot(x, wu_s[...])).astype(BF16)
        y_ref[...] = (_dot(a, wd_s[...]) * gs_ref[:, 0:1]).astype(BF16)

    @pl.when(jnp.logical_not(valid))
    def _():
        y_ref[...] = jnp.zeros_like(y_ref)


def _moe_gmm(xs, gs, tile_expert, tile_valid, wg, wu, wd, layer):
    n_rows, d = xs.shape
    rt = RT_GMM
    ff = wg.shape[-1]
    tok = lambda i, te, tv: (i, 0)
    any_spec = pl.BlockSpec(memory_space=pl.ANY)
    return pl.pallas_call(
        functools.partial(_moe_gmm_kernel, layer=layer),
        out_shape=jax.ShapeDtypeStruct((n_rows, d), BF16),
        grid_spec=pltpu.PrefetchScalarGridSpec(
            num_scalar_prefetch=2, grid=(n_rows // rt,),
            in_specs=[pl.BlockSpec((rt, d), tok), pl.BlockSpec((rt, LANE), tok), any_spec, any_spec, any_spec],
            out_specs=pl.BlockSpec((rt, d), tok),
            scratch_shapes=[pltpu.VMEM((d, ff), BF16), pltpu.VMEM((d, ff), BF16), pltpu.VMEM((ff, d), BF16)]),
        compiler_params=_params(("arbitrary",), 56),
        name="moe_gmm",
    )(tile_expert, tile_valid, xs, gs, wg, wu, wd)


def _moe_combine_kernel(cnt_ref, base_ref, rows_ref, ys_ref, pt_ref, x_ref, g_ref, lg_ref, lb_ref, o_ref,
                        ycat_s, sems, *, alpha, block0):
    del rows_ref
    t = x_ref.shape[0]
    cap = ycat_s.shape[1]
    ne = N_EXPERTS
    i = pl.program_id(0)
    slot = i % 2

    def block_copies(blk, slot_):
        copies, offs, off = [], [], 0
        for e in range(ne):
            cnt = cnt_ref[blk * ne + e]
            rows = ((cnt + SEG_ALIGN - 1) // SEG_ALIGN) * SEG_ALIGN
            copies += _segment_copies(rows, ys_ref, ycat_s.at[slot_], base_ref[blk * ne + e], off, sems,
                                      slot_ * ne + e)
            offs.append(off)
            off = off + rows
        return copies, offs

    def start(copies):
        for pred, cp in copies:
            pl.when(pred)(cp.start)

    @pl.when(i == 0)
    def _():
        ycat_s[...] = jnp.zeros_like(ycat_s)
        start(block_copies(block0, 0)[0])

    @pl.when(i + 1 < pl.num_programs(0))
    def _():
        start(block_copies(i + 1 + block0, 1 - slot)[0])

    copies, offs = block_copies(i + block0, slot)
    for pred, cp in copies:
        pl.when(pred)(cp.wait)

    pt = pt_ref[...]
    lane = lax.broadcasted_iota(jnp.int32, (1, LANE), 1)
    off_lane = jnp.zeros((1, LANE), F32)
    for e in range(1, ne):
        off_lane = jnp.where(lane == e, offs[e].astype(F32), off_lane)
    chosen = (pt >= 0.0) & (lax.broadcasted_iota(jnp.int32, (t, LANE), 1) < ne)
    row_of = pt + off_lane
    s_lo = jnp.min(jnp.where(chosen, row_of, float(cap)), axis=1, keepdims=True)
    s_hi = jnp.max(jnp.where(chosen, row_of, -1.0), axis=1, keepdims=True)
    n_piece = 3
    wp = cap // n_piece
    f = None
    for j in range(n_piece):
        col = (lax.broadcasted_iota(jnp.int32, (t, wp), 1) + j * wp).astype(F32)
        sel = jnp.where(col == s_lo, 1.0, jnp.where(col == s_hi, 1.0, 0.0)).astype(BF16)
        part = _dot(sel, ycat_s.at[slot].at[pl.ds(j * wp, wp)][...])
        f = part if f is None else f + part
    o_ref[...] = _layer_norm(alpha * x_ref[...] + g_ref[0] * f, lg_ref[...], lb_ref[...])


def _moe_combine(ys, pt, x, g2, ln_g, ln_b, counts, base, rows, alpha, block0=0, n_blocks=None):
    n, d = x.shape
    t = T_MOE
    ne = N_EXPERTS
    n_blocks = n // t if n_blocks is None else n_blocks
    cap = -(-(2 * t + ne * (SEG_ALIGN - 1)) // (6 * LANE)) * (6 * LANE)
    tok = lambda b, c, o, r: (b + block0, 0)
    const = lambda b, c, o, r: (0, 0)
    g2, g2_spec = _mod_operand(g2, lambda b, c, o, r: r[b + block0])
    return pl.pallas_call(
        functools.partial(_moe_combine_kernel, alpha=alpha, block0=block0),
        out_shape=jax.ShapeDtypeStruct((n_blocks * t, d), F32),
        grid_spec=pltpu.PrefetchScalarGridSpec(
            num_scalar_prefetch=3, grid=(n_blocks,),
            in_specs=[pl.BlockSpec(memory_space=pl.ANY), pl.BlockSpec((t, LANE), tok), pl.BlockSpec((t, d), tok),
                      g2_spec,
                      pl.BlockSpec((1, d), const), pl.BlockSpec((1, d), const)],
            out_specs=pl.BlockSpec((t, d), lambda b, c, o, r: (b, 0)),
            scratch_shapes=[pltpu.VMEM((2, cap, d), BF16), pltpu.SemaphoreType.DMA((2 * ne * len(SEG_SIZES),))]),
        compiler_params=_params(("arbitrary",), 48),
        name="moe_combine",
    )(counts, base, rows, ys, pt, x, g2, ln_g, ln_b)


def _moe_layout(cnt, n_tiles):
    cnt16 = (cnt + SEG_ALIGN - 1) // SEG_ALIGN * SEG_ALIGN
    rows_e = jnp.sum(cnt16, axis=0)
    rows_e = (rows_e + RT_GMM - 1) // RT_GMM * RT_GMM
    ends = jnp.cumsum(rows_e)
    base = (ends - rows_e)[None, :] + jnp.cumsum(cnt16, axis=0) - cnt16
    tile_end = ends // RT_GMM
    idx = jnp.arange(n_tiles, dtype=jnp.int32)
    valid = idx < tile_end[-1]
    te = jnp.sum((idx[:, None] >= tile_end[None, :]).astype(jnp.int32), axis=1)
    last = jnp.sum(((tile_end[-1] - 1) >= tile_end).astype(jnp.int32))
    te = jnp.where(valid, te, last)
    return base.reshape(-1).astype(jnp.int32), te.astype(jnp.int32), valid.astype(jnp.int32)


def _tile_rows(n, p, ss, tm):
    starts = np.arange(n // tm) * tm
    return jnp.asarray(np.where(starts < p, 0, 1 + (starts - p) // ss).astype(np.int32))


def _tile_rope_blocks(n, p, ss, tm):
    starts = np.arange(n // tm) * tm
    return jnp.asarray(np.where(starts < p, 0, 1 + ((starts - p) % ss) // tm).astype(np.int32))


def _tile_edges(n, p, sp, ss, tm):
    starts = np.arange(n // tm) * tm
    pos = np.where(starts < p, starts % sp, (starts - p) % ss)
    seq = np.where(starts < p, sp, ss)
    return jnp.asarray((pos == 0).astype(np.int32)), jnp.asarray(((pos + tm) % seq == 0).astype(np.int32))


def _rope_tables(n_tokens, rot_dim, lane0, tm):
    rows = n_tokens // GRID_W
    row_pos = jnp.repeat(jnp.arange(rows, dtype=F32), GRID_W)
    col_pos = jnp.tile(jnp.arange(GRID_W, dtype=F32), rows)
    half = rot_dim // 2
    qtr = half // 2
    inv_freq = ROPE_THETA ** (-jnp.arange(0, half, 2, dtype=F32) / half)
    ang_r = row_pos[:, None] * inv_freq
    ang_c = col_pos[:, None] * inv_freq
    ang = jnp.concatenate([ang_r, ang_r, ang_c, ang_c], axis=-1)
    cos, sin = jnp.cos(ang), jnp.sin(ang)
    first = ((np.arange(rot_dim) % half) < qtr).astype(np.float32)
    sa = -sin * first
    sb = sin * (1.0 - first)

    def place(tbl, fill):
        full = jnp.full((n_tokens, LANE), fill, F32).at[:, lane0:lane0 + rot_dim].set(tbl)
        return jnp.concatenate([jnp.full((tm, LANE), fill, F32), full], axis=0)

    return place(cos, 1.0), place(sa, 0.0), place(sb, 0.0)


def _pad_heads(w, n_heads, width):
    lead = w.shape[:-1]
    w = w.reshape(lead + (n_heads, width))
    w = jnp.pad(w, [(0, 0)] * len(lead) + [(0, 0), (0, LANE - width)])
    return w.reshape(lead + (n_heads * LANE,))


def _pad_head_rows(w, n_heads, width):
    d = w.shape[-1]
    w = w.reshape(n_heads, width, d)
    return jnp.pad(w, ((0, 0), (0, LANE - width), (0, 0))).reshape(n_heads * LANE, d)


def kernel(x_prompt, x_sample, cache_attn_k, cache_attn_v, cache_mla_ckv, cache_mla_kpe, state_ssm_re, state_ssm_im, c, c_ctx, ada_w, ada_b, ln_g, ln_b, ev_w_in, ev_conv_w, ev_q_gain, ev_k_gain, ev_w_out, ffn_w_gate, ffn_w_up, ffn_w_down, od_w_in, s5_a_re, s5_a_im, s5_log_dt, s5_b_re, s5_b_im, s5_c_re, s5_c_im, s5_d, s5_w_glu, s5_b_glu, mla_q_gain, mla_w_uq, mla_kv_gain, mla_w_ukv, od_w_out, moe_router, moe_w_gate, moe_w_up, moe_w_down):
    bp, sp, d = x_prompt.shape
    bs, ss, _ = x_sample.shape
    depth = ada_w.shape[0]
    alpha = (2 * depth) ** 0.25
    p = bp * sp
    n = p + bs * ss
    past = cache_attn_k.shape[2]
    for tm in (TM_PROJ, TM_EVEN_IN, TM_FFN, T_MOE):
        assert p % tm == 0 and ss % tm == 0 and (sp % tm == 0 or tm % sp == 0)
    assert sp % S5_CHUNK == 0 and ss % S5_CHUNK == 0

    x = (x_prompt.reshape(p, d), x_sample.reshape(bs * ss, d))

    nrow = -(-(1 + bs) // SUBLANE) * SUBLANE
    cond = jnp.zeros((nrow, d), F32).at[0].set(c_ctx).at[1:1 + bs].set(c)
    mod = _modulation(cond, ada_w, ada_b)

    def mod_part(l, k):
        return (mod, l, k)

    rows_proj = _tile_rows(n, p, ss, TM_PROJ)
    rows_ffn = _tile_rows(n, p, ss, TM_FFN)
    rows_moe = _tile_rows(n, p, ss, T_MOE)
    rblk = _tile_rope_blocks(n, p, ss, TM_PROJ)
    first, last = _tile_edges(n, p, sp, ss, TM_PROJ)
    rows_even = _tile_rows(n, p, ss, TM_EVEN_IN)
    rblk_even = _tile_rope_blocks(n, p, ss, TM_EVEN_IN)
    tables_even = _rope_tables(ss, HEAD_DIM, 0, TM_EVEN_IN)
    tables_mla = _rope_tables(ss, MLA_ROPE, MLA_NOPE, TM_PROJ)

    s5_mats = jax.vmap(_s5_matrices)(s5_a_re, s5_a_im, s5_log_dt, s5_b_re, s5_b_im, s5_c_re, s5_c_im)
    ffn_wg, ffn_wu, ffn_wd = ffn_w_gate.astype(BF16), ffn_w_up.astype(BF16), ffn_w_down.astype(BF16)
    out_k, out_v, out_ckv, out_kpe, out_sre, out_sim = [], [], [], [], [], []
    cw = CONV_WIDTH
    hq = ATTN_HEADS * HEAD_DIM
    hkv = ATTN_KV_HEADS * HEAD_DIM
    for l in range(depth):
        i = l // 2
        sh1, sc1, g1, sh2, sc2, g2 = [mod_part(l, k) for k in range(6)]
        lg = ln_g[l].reshape(2, 1, d)
        lb = ln_b[l].reshape(2, 1, d)
        if l % 2 == 0:
            w_in = ev_w_in[i]
            wc = w_in[:, :3 * cw].astype(BF16)
            wq = _pad_heads(w_in[:, 3 * cw:3 * cw + hq], ATTN_HEADS, HEAD_DIM).astype(BF16)
            wkv = jnp.concatenate([_pad_heads(w_in[:, 3 * cw + hq:3 * cw + hq + hkv], ATTN_KV_HEADS, HEAD_DIM),
                                   _pad_heads(w_in[:, 3 * cw + hq + hkv:], ATTN_KV_HEADS, HEAD_DIM)],
                                  axis=1).astype(BF16)
            qg = jnp.pad(ev_q_gain[i], (0, LANE - HEAD_DIM)).reshape(1, LANE)
            kg = jnp.pad(ev_k_gain[i], (0, LANE - HEAD_DIM)).reshape(1, LANE)
            conv, q, k, v, kn, vf = _even_in(x, sc1, sh1, wc, wq, wkv, qg, kg, tables_even, rows_even, rblk_even)
            kvw = ATTN_KV_HEADS * LANE
            out_k.append(kn[:p].reshape(bp, sp, ATTN_KV_HEADS, LANE)[..., :HEAD_DIM])
            out_v.append(vf[:p].reshape(bp, sp, ATTN_KV_HEADS, LANE)[..., :HEAD_DIM])
            group = ATTN_HEADS // ATTN_KV_HEADS
            qw = ATTN_HEADS * LANE
            a_p = _attention(q, [(k, v, True)], ATTN_HEADS, group, bp, sp, 0)
            kc = _pad_heads(cache_attn_k[:, i].reshape(bs, past, hkv), ATTN_KV_HEADS, HEAD_DIM).astype(BF16)
            vc = _pad_heads(cache_attn_v[:, i].reshape(bs, past, hkv), ATTN_KV_HEADS, HEAD_DIM).astype(BF16)
            a_s = _attention(q, [(kc, vc, False), (k, v, True)], ATTN_HEADS, group, bs, ss, p)
            w_out = jnp.concatenate([ev_w_out[i][:cw], _pad_head_rows(ev_w_out[i][cw:], ATTN_HEADS, HEAD_DIM)],
                                    axis=0).astype(BF16)
            conv_w = jnp.pad(ev_conv_w[i], ((0, SUBLANE - ev_conv_w.shape[1]), (0, 0)))
            x = _even_out(conv, a_p.reshape(p, qw), a_s.reshape(bs * ss, qw), x, g1, conv_w, w_out, lg[0], lb[0],
                          first, last, rows_proj, alpha, sp)
            x = _ffn(x, sc2, sh2, g2, ffn_wg, ffn_wu, ffn_wd, i, lg[1], lb[1], rows_ffn, alpha)
        else:
            q_end = S5_WIDTH + MLA_Q_RANK
            kv_end = q_end + MLA_KV_RANK
            w_in = od_w_in[i]
            w_in = jnp.concatenate([w_in[:, :kv_end], jnp.zeros((d, MLA_NOPE), F32), w_in[:, kv_end:],
                                    jnp.zeros((d, LANE - MLA_NOPE - MLA_ROPE), F32)], axis=1).astype(BF16)
            m_ctx, m_lat = p // S5_CHUNK, bs * ss // S5_CHUNK
            ctx_chunk0, lat_chunk0 = (m_lat, 0) if m_lat >= m_ctx else (0, m_ctx)
            assert ctx_chunk0 % m_ctx == 0 and lat_chunk0 % m_lat == 0
            proj, u = _odd_in(x, sc1, sh1, w_in, rows_proj, p // TM_PROJ, ctx_chunk0, lat_chunk0)
            out_kpe.append(proj[:p, kv_end + MLA_NOPE:kv_end + MLA_NOPE + MLA_ROPE].reshape(bp, sp, MLA_ROPE))

            h0_p = jnp.zeros((S5_GROUPS, bp, 4 * S5_STATE), F32)
            y_p, fin_p = _s5(u, s5_mats, i, h0_p, bp, sp // S5_CHUNK, ctx_chunk0 // m_ctx)

            def pack_state(re, im):
                to = lambda a: a.transpose(2, 0, 1, 3).reshape(S5_GROUPS, a.shape[0], 2 * S5_STATE)
                return jnp.concatenate([to(re), to(im)], axis=-1)

            y_s, _ = _s5(u, s5_mats, i, pack_state(state_ssm_re[:, i], state_ssm_im[:, i]), bs, ss // S5_CHUNK,
                         lat_chunk0 // m_lat)
            fin = fin_p.reshape(S5_GROUPS, bp, 2, 2, S5_STATE)
            out_sre.append(fin[:, :, 0].transpose(1, 2, 0, 3))
            out_sim.append(fin[:, :, 1].transpose(1, 2, 0, 3))

            qk = MLA_NOPE + MLA_ROPE
            wuq = _pad_heads(mla_w_uq[i], MLA_HEADS, qk).astype(BF16)
            wukv = mla_w_ukv[i].reshape(MLA_KV_RANK, MLA_HEADS, MLA_NOPE + MLA_V)
            wk = _pad_heads(wukv[..., :MLA_NOPE].reshape(MLA_KV_RANK, -1), MLA_HEADS, MLA_NOPE).astype(BF16)
            wv = _pad_heads(wukv[..., MLA_NOPE:].reshape(MLA_KV_RANK, -1), MLA_HEADS, MLA_V).astype(BF16)
            place = np.zeros((LANE, MLA_HEADS * LANE), np.float32)
            for hd in range(MLA_HEADS):
                for j in range(MLA_ROPE):
                    place[MLA_NOPE + j, hd * LANE + MLA_NOPE + j] = 1.0
            place = jnp.asarray(place, BF16)
            q, k, v, ckv = _mla_prep(proj, mla_q_gain[i].reshape(1, -1), mla_kv_gain[i].reshape(1, -1),
                                     wuq, wk, wv, place, tables_mla, rblk)
            out_ckv.append(ckv[:p].reshape(bp, sp, MLA_KV_RANK))
            hw = MLA_HEADS * LANE
            a_p = _attention(q, [(k, v, True)], MLA_HEADS, 1, bp, sp, 0)
            kpe_c = jnp.pad(cache_mla_kpe[:, i].reshape(bs * past, MLA_ROPE),
                            ((0, 0), (MLA_NOPE, LANE - MLA_NOPE - MLA_ROPE)))
            kc, vc = _mla_cache(cache_mla_ckv[:, i].reshape(bs * past, MLA_KV_RANK), kpe_c, wk, wv, place)
            a_s = _attention(q, [(kc.reshape(bs, past, hw), vc.reshape(bs, past, hw), False), (k, v, True)],
                             MLA_HEADS, 1, bs, ss, p)
            w_out = jnp.concatenate([od_w_out[i][:S5_WIDTH], _pad_head_rows(od_w_out[i][S5_WIDTH:], MLA_HEADS, MLA_V)],
                                    axis=0).astype(BF16)
            x = _odd_out(proj, y_p, y_s, a_p.reshape(p, hw), a_s.reshape(bs * ss, hw), x, g1, s5_d[i].reshape(1, -1),
                         s5_w_glu[i].astype(BF16), s5_b_glu[i].reshape(1, -1), w_out, lg[0], lb[0], rows_proj, alpha)

            hb, pos, gate, pt, cnt = _router(x, sc2, sh2, moe_router[i].T, rows_moe)
            cnt = cnt[:, :, 0]
            nblk = n // T_MOE
            n_tiles = -(-(2 * n + nblk * N_EXPERTS * (SEG_ALIGN - 1)) // RT_GMM) + N_EXPERTS
            base, tile_expert, tile_valid = _moe_layout(cnt, n_tiles)
            counts = cnt.reshape(-1)
            xs, gs = _moe_gather(hb, pos, gate, counts, base, n_tiles * RT_GMM)
            ys = _moe_gmm(xs, gs, tile_expert, tile_valid, moe_w_gate, moe_w_up, moe_w_down, i)
            if l == depth - 1:
                nb_ctx = p // T_MOE
                x = tuple(_moe_combine(ys, pt, x, g2, lg[1], lb[1], counts, base, rows_moe, alpha, b0, nbk)
                          for b0, nbk in ((0, nb_ctx), (nb_ctx, nblk - nb_ctx)))
            else:
                x = _moe_combine(ys, pt, x, g2, lg[1], lb[1], counts, base, rows_moe, alpha)

    if not isinstance(x, tuple):
        x = (x[:p], x[p:])
    y_prompt = x[0].reshape(bp, sp, d)
    y_sample = x[1].reshape(bs, ss, d)
    return (y_prompt, y_sample, jnp.stack(out_k, axis=1), jnp.stack(out_v, axis=1),
            jnp.stack(out_ckv, axis=1), jnp.stack(out_kpe, axis=1),
            jnp.stack(out_sre, axis=1), jnp.stack(out_sim, axis=1))
```
